```python
import math
import jax, jax.numpy as jnp
from jax import lax
import numpy as np

D_MODEL = 1024
BATCH = 2
SEQ = 8192
DEPTH = 1
DEC_BATCH = 4
DEC_SEQ = 4096
PAST_LEN = 128

HEAD_DIM = 64
ROPE_THETA = 10000.0
BLOCK = 128
A_Q_HEADS = 8
A_KV_HEADS = 2
A_GROUP = A_Q_HEADS // A_KV_HEADS
WINDOW = 128
A_WIDTH = A_Q_HEADS * HEAD_DIM
B_HEADS = 4
B_V_DIM = 2 * HEAD_DIM
B_WIDTH = B_HEADS * B_V_DIM
A_Q_COLS = A_Q_HEADS * HEAD_DIM
A_KV_COLS = A_KV_HEADS * HEAD_DIM
B_QK_COLS = B_HEADS * 2 * HEAD_DIM
B_V_COLS = B_WIDTH
GATE_COLS = 2 * D_MODEL
IN_COLS = A_Q_COLS + 2 * A_KV_COLS + 2 * B_QK_COLS + B_V_COLS + GATE_COLS
N_EXPERTS = 256
TOP_K = 8
N_GROUPS = 8
TOPK_GROUPS = 4
D_EXPERT = 256
ROUTED_SCALE = 2.5
EXPERT_ROWS = 128
DN_ALPHA = (2 * DEPTH) ** 0.25
DN_BETA = (8 * DEPTH) ** -0.25
LN_EPS = 1e-5
RMS_EPS = 1e-5
NEG = -1e30

kernel_name = "hybrid_swa_diffattn_moe_deepnorm_encoder"


def rope(x, pos):
    half = x.shape[-1] // 2
    inv = 1.0 / (ROPE_THETA ** (jnp.arange(half, dtype=jnp.float32) / half))
    ang = pos.astype(jnp.float32)[:, None] * inv[None, :]
    shp = (ang.shape[0],) + (1,) * (x.ndim - 3) + (half,)
    cos = jnp.cos(ang).reshape(shp)
    sin = jnp.sin(ang).reshape(shp)
    xf = x.astype(jnp.float32)
    x1, x2 = xf[..., :half], xf[..., half:]
    return jnp.concatenate([x1 * cos - x2 * sin, x2 * cos + x1 * sin], axis=-1).astype(x.dtype)


def layer_norm(x, g, b):
    xf = x.astype(jnp.float32)
    mu = jnp.mean(xf, axis=-1, keepdims=True)
    var = jnp.mean(jnp.square(xf - mu), axis=-1, keepdims=True)
    y = (xf - mu) * lax.rsqrt(var + LN_EPS) * g.astype(jnp.float32) + b.astype(jnp.float32)
    return y.astype(x.dtype)


def windowed_gqa(q, k, v, sink):
    bsz, s = q.shape[:2]
    nb = s // BLOCK
    qb = q.reshape(bsz, nb, BLOCK, A_KV_HEADS, A_GROUP, HEAD_DIM)

    def band(t):
        tp = jnp.pad(t, ((0, 0), (BLOCK, BLOCK), (0, 0), (0, 0)))
        tp = tp.reshape(bsz, nb + 2, BLOCK, A_KV_HEADS, HEAD_DIM)
        return jnp.concatenate([tp[:, :-2], tp[:, 1:-1], tp[:, 2:]], axis=2)

    kb, vb = band(k), band(v)
    sc = jnp.einsum('bnqhgd,bnkhd->bnhgqk', qb, kb).astype(jnp.float32) * (HEAD_DIM ** -0.5)
    qi = jnp.arange(BLOCK)[:, None]
    kj = jnp.arange(3 * BLOCK)[None, :]
    rel = kj - BLOCK - qi
    kpos = (jnp.arange(nb) * BLOCK)[:, None, None] - BLOCK + kj[None]
    mask = (jnp.abs(rel) <= WINDOW)[None] & (kpos >= 0) & (kpos < s)
    sc = jnp.where(mask[None, :, None, None], sc, NEG)
    snk = jnp.broadcast_to(sink.astype(jnp.float32).reshape(1, 1, A_KV_HEADS, A_GROUP, 1, 1),
                           sc.shape[:-1] + (1,))
    p = jax.nn.softmax(jnp.concatenate([sc, snk], axis=-1), axis=-1)[..., :-1]
    o = jnp.einsum('bnhgqk,bnkhd->bnqhgd', p.astype(v.dtype), vb)
    return o.reshape(bsz, s, A_WIDTH)


def diff_attention(q, k, v, lam, lam_init, subln_g):
    bsz, s = q.shape[:2]
    nb = s // BLOCK
    qb = q.reshape(bsz, nb, BLOCK, B_HEADS, 2, HEAD_DIM).transpose(1, 0, 2, 3, 4, 5)
    scale = HEAD_DIM ** -0.5

    def one_block(qblk):
        sc = jnp.einsum('bqhcd,bkhcd->bhcqk', qblk, k).astype(jnp.float32) * scale
        p = jax.nn.softmax(sc, axis=-1)
        a = p[:, :, 0] - lam * p[:, :, 1]
        return jnp.einsum('bhqk,bkhe->bqhe', a.astype(v.dtype), v)

    o = lax.map(one_block, qb)
    o = o.transpose(1, 0, 2, 3, 4).reshape(bsz, s, B_HEADS, B_V_DIM).astype(jnp.float32)
    o = o * lax.rsqrt(jnp.mean(jnp.square(o), axis=-1, keepdims=True) + RMS_EPS)
    o = o * subln_g.astype(jnp.float32) * (1.0 - lam_init)
    return o.reshape(bsz, s, B_WIDTH).astype(v.dtype)


def moe(x, router_w, router_bias, w_gate, w_up, w_down, ws_gate, ws_up, ws_down):
    bsz, s, d = x.shape
    t = bsz * s
    xf = x.reshape(t, d)
    scores = jax.nn.sigmoid((xf @ router_w).astype(jnp.float32))
    choice = scores + router_bias.astype(jnp.float32)
    grp = choice.reshape(t, N_GROUPS, N_EXPERTS // N_GROUPS)
    grp_score = lax.top_k(grp, 2)[0].sum(-1)
    _, gidx = lax.top_k(grp_score, TOPK_GROUPS)
    gmask = jax.nn.one_hot(gidx, N_GROUPS, dtype=jnp.float32).sum(1)
    emask = jnp.repeat(gmask, N_EXPERTS // N_GROUPS, axis=1) > 0
    _, eidx = lax.top_k(jnp.where(emask, choice, -jnp.inf), TOP_K)
    wts = jnp.take_along_axis(scores, eidx, axis=1)
    wts = wts / jnp.sum(wts, axis=-1, keepdims=True) * ROUTED_SCALE

    n_assign = t * TOP_K
    n_blk = -(-(n_assign + N_EXPERTS * (EXPERT_ROWS - 1)) // EXPERT_ROWS)
    n_rows = n_blk * EXPERT_ROWS
    flat_e = eidx.reshape(-1)
    order = jnp.argsort(flat_e)
    e_sorted = flat_e[order]
    counts = jnp.bincount(flat_e, length=N_EXPERTS)
    padded = ((counts + EXPERT_ROWS - 1) // EXPERT_ROWS) * EXPERT_ROWS
    pend = jnp.cumsum(padded)
    pstart = pend - padded
    start = jnp.cumsum(counts) - counts
    dest = pstart[e_sorted] + (jnp.arange(n_assign) - start[e_sorted])
    buf_tok = jnp.zeros((n_rows,), jnp.int32).at[dest].set((order // TOP_K).astype(jnp.int32))
    buf_w = jnp.zeros((n_rows,), jnp.float32).at[dest].set(wts.reshape(-1)[order])
    blk_expert = jnp.clip(jnp.searchsorted(pend, jnp.arange(n_blk) * EXPERT_ROWS, side='right'),
                          0, N_EXPERTS - 1)
    xin = xf[buf_tok].reshape(n_blk, EXPERT_ROWS, d)

    def expert_block(args):
        e, xb = args
        h = jax.nn.silu(xb @ w_gate[e]) * (xb @ w_up[e])
        return h @ w_down[e]

    out = lax.map(expert_block, (blk_expert, xin)).reshape(n_rows, d)
    routed = jax.ops.segment_sum(out * buf_w[:, None].astype(out.dtype), buf_tok, num_segments=t)
    shared = (jax.nn.silu(xf @ ws_gate) * (xf @ ws_up)) @ ws_down
    return (routed + shared).reshape(bsz, s, d)


def encoder_layer(x, layer_idx, w_in, attn_sink, lambda_q1, lambda_k1, lambda_q2, lambda_k2, subln_g,
                  w_o_a, w_o_b, w_out, ln1_g, ln1_b, router_w, router_bias, w_gate, w_up, w_down,
                  ws_gate, ws_up, ws_down, ln2_g, ln2_b):
    bsz, s, _ = x.shape
    pos = jnp.arange(s)
    u = x @ w_in
    cuts = np.cumsum([A_Q_COLS, A_KV_COLS, A_KV_COLS, B_QK_COLS, B_QK_COLS, B_V_COLS, D_MODEL]).tolist()
    qa, ka, va, qd, kd, vd, ga, gb = jnp.split(u, cuts, axis=-1)
    qa = rope(qa.reshape(bsz, s, A_Q_HEADS, HEAD_DIM), pos)
    ka = rope(ka.reshape(bsz, s, A_KV_HEADS, HEAD_DIM), pos)
    va = va.reshape(bsz, s, A_KV_HEADS, HEAD_DIM)
    o_a = windowed_gqa(qa, ka, va, attn_sink)
    qd = rope(qd.reshape(bsz, s, B_HEADS, 2, HEAD_DIM), pos)
    kd = rope(kd.reshape(bsz, s, B_HEADS, 2, HEAD_DIM), pos)
    vd = vd.reshape(bsz, s, B_HEADS, B_V_DIM)
    lam_init = 0.8 - 0.6 * math.exp(-0.3 * layer_idx)
    lam = (jnp.exp(jnp.sum(lambda_q1.astype(jnp.float32) * lambda_k1.astype(jnp.float32)))
           - jnp.exp(jnp.sum(lambda_q2.astype(jnp.float32) * lambda_k2.astype(jnp.float32))) + lam_init)
    o_b = diff_attention(qd, kd, vd, lam, lam_init, subln_g)
    merged = jax.nn.sigmoid(ga) * (o_a @ w_o_a) + jax.nn.sigmoid(gb) * (o_b @ w_o_b)
    x = layer_norm(DN_ALPHA * x + merged @ w_out, ln1_g, ln1_b)
    f = moe(x, router_w, router_bias, w_gate, w_up, w_down, ws_gate, ws_up, ws_down)
    x = layer_norm(DN_ALPHA * x + f, ln2_g, ln2_b)
    return x


def setup_inputs(seed: int = 0) -> dict:
    key = jax.random.key(seed)
    ks = jax.random.split(key, 26)
    n = lambda k, shp, sc: jax.random.normal(k, shp, jnp.float32) * sc
    L, D = DEPTH, D_MODEL
    return {
        "x_prompt": n(ks[0], (BATCH, SEQ, D), 1.0),
        "x_sample": n(ks[1], (DEC_BATCH, DEC_SEQ, D), 1.0),
        "w_in": n(ks[2], (L, D, IN_COLS), D ** -0.5),
        "attn_sink": n(ks[3], (L, A_Q_HEADS), 0.5),
        "lambda_q1": n(ks[4], (L, HEAD_DIM), 0.1),
        "lambda_k1": n(ks[5], (L, HEAD_DIM), 0.1),
        "lambda_q2": n(ks[6], (L, HEAD_DIM), 0.1),
        "lambda_k2": n(ks[7], (L, HEAD_DIM), 0.1),
        "subln_g": 1.0 + n(ks[8], (L, B_V_DIM), 0.02),
        "w_o_a": n(ks[9], (L, A_WIDTH, D), A_WIDTH ** -0.5 * DN_BETA),
        "w_o_b": n(ks[10], (L, B_WIDTH, D), B_WIDTH ** -0.5 * DN_BETA),
        "w_out": n(ks[11], (L, D, D), D ** -0.5 * DN_BETA),
        "ln1_g": 1.0 + n(ks[12], (L, D), 0.02),
        "ln1_b": n(ks[13], (L, D), 0.02),
        "router_w": n(ks[14], (L, D, N_EXPERTS), D ** -0.5),
        "router_bias": n(ks[15], (L, N_EXPERTS), 0.01),
        "w_gate": n(ks[16], (L, N_EXPERTS, D, D_EXPERT), D ** -0.5),
        "w_up": n(ks[17], (L, N_EXPERTS, D, D_EXPERT), D ** -0.5),
        "w_down": n(ks[18], (L, N_EXPERTS, D_EXPERT, D), D_EXPERT ** -0.5 * DN_BETA),
        "ws_gate": n(ks[19], (L, D, D_EXPERT), D ** -0.5),
        "ws_up": n(ks[20], (L, D, D_EXPERT), D ** -0.5),
        "ws_down": n(ks[21], (L, D_EXPERT, D), D_EXPERT ** -0.5 * DN_BETA),
        "ln2_g": 1.0 + n(ks[22], (L, D), 0.02),
        "ln2_b": n(ks[23], (L, D), 0.02),
    }


def reference(x_prompt, x_sample, w_in, attn_sink, lambda_q1, lambda_k1, lambda_q2, lambda_k2, subln_g,
              w_o_a, w_o_b, w_out, ln1_g, ln1_b, router_w, router_bias, w_gate, w_up, w_down,
              ws_gate, ws_up, ws_down, ln2_g, ln2_b):
    y_prompt = x_prompt
    y_sample = x_sample
    for l in range(DEPTH):
        lp = [p[l] for p in (w_in, attn_sink, lambda_q1, lambda_k1, lambda_q2, lambda_k2, subln_g,
                             w_o_a, w_o_b, w_out, ln1_g, ln1_b, router_w, router_bias, w_gate, w_up,
                             w_down, ws_gate, ws_up, ws_down, ln2_g, ln2_b)]
        y_prompt = encoder_layer(y_prompt, l, *lp)
        y_sample = encoder_layer(y_sample, l, *lp)
    return (y_prompt, y_sample)
```

```python
import functools
import math

import jax
import jax.numpy as jnp
from jax import lax
from jax.experimental import pallas as pl
from jax.experimental.pallas import tpu as pltpu

D_MODEL = 1024
HEAD_DIM = 64
ROPE_THETA = 10000.0
BLOCK = 128
A_Q_HEADS = 8
A_KV_HEADS = 2
WINDOW = 128
B_HEADS = 4
N_EXPERTS = 256
TOP_K = 8
N_GROUPS = 8
TOPK_GROUPS = 4
GROUP_SIZE = N_EXPERTS // N_GROUPS
D_EXPERT = 256
ROUTED_SCALE = 2.5
EXPERT_ROWS = 128
DEPTH = 1
DN_ALPHA = (2 * DEPTH) ** 0.25
LN_EPS = 1e-5
RMS_EPS = 1e-5
NEG = -1e30

LANES = 128
VMEM_LIMIT_BYTES = 56 * 1024 * 1024

F32 = jnp.float32
BF16 = jnp.bfloat16
_NT = (((1,), (1,)), ((), ()))


def _params(sem, vmem=VMEM_LIMIT_BYTES):
    return pltpu.CompilerParams(dimension_semantics=sem, vmem_limit_bytes=vmem)


_C_QA, _C_KA, _C_VA, _C_QD, _C_KD, _C_VD, _C_GA, _C_GB, _C_END = 0, 512, 768, 1024, 1536, 2048, 2560, 3584, 4608


def _prep_w_in(w_in):
    cuts = [0, 512, 640, 768, 1280, 1792, 2304, 3328, 4352]
    qa, ka, va, qd, kd, vd, ga, gb = [w_in[:, cuts[i]:cuts[i + 1]] for i in range(8)]
    dup = lambda w: jnp.concatenate([w[:, :64], w[:, :64], w[:, 64:], w[:, 64:]], axis=1)
    return jnp.concatenate([qa, dup(ka), dup(va), qd, kd, vd, ga, gb], axis=1).astype(BF16)


def _rope_tables(s):
    half = HEAD_DIM // 2
    inv = 1.0 / (ROPE_THETA ** (jnp.arange(half, dtype=F32) / half))
    ang = jnp.arange(s, dtype=F32)[:, None] * inv[None, :]
    cos, sin = jnp.cos(ang), jnp.sin(ang)
    return (jnp.concatenate([cos, cos, cos, cos], axis=1),
            jnp.concatenate([-sin, sin, -sin, sin], axis=1))


def _in_proj_kernel(x_ref, w_ref, cos_ref, sin_ref,
                    qa_ref, ka_ref, va_ref, qd_ref, kd_ref, vd_ref, ga_ref, gb_ref):
    xb = x_ref[...].astype(BF16)
    cos = cos_ref[...]
    sin = sin_ref[...]
    lane = lax.broadcasted_iota(jnp.int32, cos.shape, 1)
    first_half = (lane & (HEAD_DIM // 2)) == 0

    def proj(c0, c1):
        return jnp.dot(xb, w_ref[:, c0:c1], preferred_element_type=F32)

    def rope_store(u, out_ref, scale):
        for j in range(u.shape[1] // LANES):
            uj = u[:, LANES * j:LANES * (j + 1)]
            rot = jnp.where(first_half, pltpu.roll(uj, LANES - 32, 1), pltpu.roll(uj, 32, 1))
            r = uj * cos + rot * sin
            if scale != 1.0:
                r = r * scale
            out_ref[:, LANES * j:LANES * (j + 1)] = r.astype(out_ref.dtype)

    scale = HEAD_DIM ** -0.5
    rope_store(proj(_C_QA, _C_KA), qa_ref, scale)
    rope_store(proj(_C_KA, _C_VA), ka_ref, 1.0)
    va_ref[...] = proj(_C_VA, _C_QD).astype(va_ref.dtype)
    rope_store(proj(_C_QD, _C_KD), qd_ref, scale)
    rope_store(proj(_C_KD, _C_VD), kd_ref, 1.0)
    vd_ref[...] = proj(_C_VD, _C_GA).astype(vd_ref.dtype)
    ga_ref[...] = jax.nn.sigmoid(proj(_C_GA, _C_GB)).astype(ga_ref.dtype)
    gb_ref[...] = jax.nn.sigmoid(proj(_C_GB, _C_END)).astype(gb_ref.dtype)


def _in_proj(x2, w_perm, cos_t, sin_t, seq, tm):
    t = x2.shape[0]
    nseq = seq // tm
    row = lambda i: (i, 0)
    widths = (512, 256, 256, 512, 512, 512, 1024, 1024)
    return pl.pallas_call(
        _in_proj_kernel,
        grid=(t // tm,),
        in_specs=[pl.BlockSpec((tm, D_MODEL), row),
                  pl.BlockSpec((D_MODEL, _C_END), lambda i: (0, 0)),
                  pl.BlockSpec((tm, LANES), lambda i: (i % nseq, 0)),
                  pl.BlockSpec((tm, LANES), lambda i: (i % nseq, 0))],
        out_specs=[pl.BlockSpec((tm, w), row) for w in widths],
        out_shape=[jax.ShapeDtypeStruct((t, w), BF16) for w in widths],
        compiler_params=_params(("arbitrary",)),
        name="in_proj",
    )(x2, w_perm, cos_t, sin_t)


def _attn_a_kernel(sink_ref, q_ref, kp_ref, kc_ref, kn_ref, vp_ref, vc_ref, vn_ref, o_ref, *, seq, tq):
    i = pl.program_id(1)
    kk = jnp.concatenate([kp_ref[...], kc_ref[...], kn_ref[...]], axis=0)
    vv = jnp.concatenate([vp_ref[...], vc_ref[...], vn_ref[...]], axis=0)
    lane = lax.broadcasted_iota(jnp.int32, (kk.shape[0], LANES), 1)
    lo = lane < HEAD_DIM
    zero = jnp.zeros((kk.shape[0], LANES), BF16)
    k_lo = [jnp.where(lo, kk[:, LANES * h:LANES * (h + 1)], zero) for h in range(A_KV_HEADS)]
    k_hi = [jnp.where(lo, zero, kk[:, LANES * h:LANES * (h + 1)]) for h in range(A_KV_HEADS)]
    v_lo = [jnp.where(lo, vv[:, LANES * h:LANES * (h + 1)], zero) for h in range(A_KV_HEADS)]
    v_hi = [jnp.where(lo, zero, vv[:, LANES * h:LANES * (h + 1)]) for h in range(A_KV_HEADS)]

    qi = lax.broadcasted_iota(jnp.int32, (BLOCK, 3 * BLOCK), 0)
    kj = lax.broadcasted_iota(jnp.int32, (BLOCK, 3 * BLOCK), 1)
    band = jnp.abs(kj - BLOCK - qi) <= WINDOW
    for j in range(tq // BLOCK):
        kpos = i * tq + (j - 1) * BLOCK + kj
        mask = band & (kpos >= 0) & (kpos < seq)
        r0, r1 = j * BLOCK, (j + 3) * BLOCK
        for c in range(A_Q_HEADS // 2):
            h = c // 2
            q2 = q_ref[j * BLOCK:(j + 1) * BLOCK, LANES * c:LANES * (c + 1)]
            out = None
            for half, (kx, vx) in enumerate(((k_lo[h], v_lo[h]), (k_hi[h], v_hi[h]))):
                s = lax.dot_general(q2, kx[r0:r1], _NT, preferred_element_type=F32)
                s = jnp.where(mask, s, NEG)
                snk = sink_ref[2 * c + half]
                m = jnp.maximum(jnp.max(s, axis=-1, keepdims=True), snk)
                p = jnp.exp(s - m)
                den = jnp.sum(p, axis=-1, keepdims=True) + jnp.exp(snk - m)
                p = p * (1.0 / den)
                pv = jnp.dot(p.astype(BF16), vx[r0:r1], preferred_element_type=F32)
                out = pv if out is None else out + pv
            o_ref[j * BLOCK:(j + 1) * BLOCK, LANES * c:LANES * (c + 1)] = out.astype(o_ref.dtype)


def _attn_a(qa, ka, va, sink, bsz, seq, tq):
    t = qa.shape[0]
    nq = seq // tq
    nb = seq // BLOCK
    r = tq // BLOCK
    cur = lambda b, i: (b * nq + i, 0)
    prev = lambda b, i: (b * nb + jnp.maximum(i * r - 1, 0), 0)
    nxt = lambda b, i: (b * nb + jnp.minimum((i + 1) * r, nb - 1), 0)
    kv_specs = [pl.BlockSpec((BLOCK, 256), prev), pl.BlockSpec((tq, 256), cur), pl.BlockSpec((BLOCK, 256), nxt)]
    return pl.pallas_call(
        functools.partial(_attn_a_kernel, seq=seq, tq=tq),
        grid=(bsz, nq),
        in_specs=[pl.BlockSpec(memory_space=pltpu.SMEM), pl.BlockSpec((tq, 512), cur)] + kv_specs + kv_specs,
        out_specs=pl.BlockSpec((tq, 512), cur),
        out_shape=jax.ShapeDtypeStruct((t, 512), BF16),
        compiler_params=_params(("arbitrary", "arbitrary")),
        name="attn_a",
    )(sink, qa, ka, ka, ka, va, va, va)


def _attn_b_kernel(lam_ref, g_ref, q_ref, k_ref, v_ref, o_ref, acc_ref, *, seq, tk, lam_init):
    q = q_ref[...]
    tq = q.shape[0]
    lane = lax.broadcasted_iota(jnp.int32, (tk, LANES), 1)
    lo = lane < HEAD_DIM
    zero = jnp.zeros((tk, LANES), BF16)
    acc_ref[...] = jnp.zeros_like(acc_ref)

    def body(kc, carry):
        ks = pl.multiple_of(kc * tk, tk)
        k = k_ref[pl.ds(ks, tk), :]
        v = v_ref[pl.ds(ks, tk), :]
        new = []
        for idx, kx in enumerate((jnp.where(lo, k, zero), jnp.where(lo, zero, k))):
            m_old, l_old = carry[2 * idx], carry[2 * idx + 1]
            s = lax.dot_general(q, kx, _NT, preferred_element_type=F32)
            m_new = jnp.maximum(m_old, jnp.max(s, axis=-1, keepdims=True))
            alpha = jnp.exp(m_old - m_new)
            p = jnp.exp(s - m_new)
            l_new = alpha * l_old + jnp.sum(p, axis=-1, keepdims=True)
            acc_ref[idx] = alpha * acc_ref[idx] + jnp.dot(p.astype(BF16), v, preferred_element_type=F32)
            new += [m_new, l_new]
        return tuple(new)

    minf = jnp.full((tq, 1), -jnp.inf, F32)
    zl = jnp.zeros((tq, 1), F32)
    _, l1, _, l2 = lax.fori_loop(0, seq // tk, body, (minf, zl, minf, zl))

    lp = lam_ref[...]
    lam = (jnp.exp(jnp.sum(lp[0:1] * lp[1:2], axis=-1, keepdims=True))
           - jnp.exp(jnp.sum(lp[2:3] * lp[3:4], axis=-1, keepdims=True)) + lam_init)
    o = acc_ref[0] * (1.0 / l1) - lam * (acc_ref[1] * (1.0 / l2))
    o = o * lax.rsqrt(jnp.mean(o * o, axis=-1, keepdims=True) + RMS_EPS)
    o = o * g_ref[...] * (1.0 - lam_init)
    o_ref[...] = o.astype(o_ref.dtype)


def _attn_b(qd, kd, vd, lam_p, subln_g, bsz, seq, tq, tk, lam_init):
    t = qd.shape[0]
    nq = seq // tq
    return pl.pallas_call(
        functools.partial(_attn_b_kernel, seq=seq, tk=tk, lam_init=lam_init),
        grid=(bsz, B_HEADS, nq),
        in_specs=[pl.BlockSpec((4, HEAD_DIM), lambda b, h, i: (0, 0)),
                  pl.BlockSpec((1, LANES), lambda b, h, i: (0, 0)),
                  pl.BlockSpec((tq, LANES), lambda b, h, i: (b * nq + i, h)),
                  pl.BlockSpec((seq, LANES), lambda b, h, i: (b, h)),
                  pl.BlockSpec((seq, LANES), lambda b, h, i: (b, h))],
        out_specs=pl.BlockSpec((tq, LANES), lambda b, h, i: (b * nq + i, h)),
        out_shape=jax.ShapeDtypeStruct((t, 512), BF16),
        scratch_shapes=[pltpu.VMEM((2, tq, LANES), F32)],
        compiler_params=_params(("arbitrary", "arbitrary", "arbitrary")),
        name="attn_b",
    )(lam_p, subln_g, qd, kd, vd)


def _layer_norm(y, g, b):
    mu = jnp.mean(y, axis=-1, keepdims=True)
    d = y - mu
    var = jnp.mean(d * d, axis=-1, keepdims=True)
    return d * lax.rsqrt(var + LN_EPS) * g + b


def _first_argmax(vals, rowf, big):
    m = jnp.max(vals, axis=0, keepdims=True)
    idx = jnp.min(jnp.where(vals == m, rowf, big), axis=0, keepdims=True)
    return m, idx


def _post_attn_kernel(x_ref, oa_ref, ob_ref, ga_ref, gb_ref, woa_ref, wob_ref, wout_ref, g1_ref, b1_ref,
                      rwt_ref, rb_ref, wsg_ref, wsu_ref, wsd_ref, tri_ref,
                      x1_ref, base_ref, eidx_ref, wts_ref, rank_ref, cnt_ref, cnt_scr):
    i = pl.program_id(0)
    tm = x_ref.shape[0]

    @pl.when(i == 0)
    def _():
        cnt_scr[...] = jnp.zeros_like(cnt_scr)

    a = jnp.dot(oa_ref[...], woa_ref[...], preferred_element_type=F32)
    b = jnp.dot(ob_ref[...], wob_ref[...], preferred_element_type=F32)
    merged = ga_ref[...].astype(F32) * a + gb_ref[...].astype(F32) * b
    mix = jnp.dot(merged.astype(BF16), wout_ref[...], preferred_element_type=F32)
    x1 = _layer_norm(DN_ALPHA * x_ref[...] + mix, g1_ref[...], b1_ref[...])
    x1_ref[...] = x1
    x1b = x1.astype(BF16)

    hg = jnp.dot(x1b, wsg_ref[...], preferred_element_type=F32)
    hu = jnp.dot(x1b, wsu_ref[...], preferred_element_type=F32)
    hs = (hg * jax.nn.sigmoid(hg)) * hu
    shared = jnp.dot(hs.astype(BF16), wsd_ref[...], preferred_element_type=F32)
    base_ref[...] = DN_ALPHA * x1 + shared

    logits = lax.dot_general(rwt_ref[...], x1b, _NT, preferred_element_type=F32)
    scores = jax.nn.sigmoid(logits)
    choice = scores + rb_ref[...]
    ninf = -jnp.inf
    grow = lax.broadcasted_iota(jnp.int32, (GROUP_SIZE, tm), 0).astype(F32)
    gscore = []
    for g in range(N_GROUPS):
        blk = choice[GROUP_SIZE * g:GROUP_SIZE * (g + 1)]
        m1, i1 = _first_argmax(blk, grow, float(GROUP_SIZE))
        m2 = jnp.max(jnp.where(grow == i1, ninf, blk), axis=0, keepdims=True)
        gscore.append(m1 + m2)
    selected = [jnp.zeros((1, tm), F32) for _ in range(N_GROUPS)]
    work = list(gscore)
    for _ in range(TOPK_GROUPS):
        best = work[0]
        for g in range(1, N_GROUPS):
            best = jnp.maximum(best, work[g])
        taken = jnp.zeros((1, tm), F32)
        for g in range(N_GROUPS):
            hit = jnp.where((work[g] == best) & (taken == 0.0), 1.0, 0.0)
            taken = jnp.maximum(taken, hit)
            selected[g] = jnp.maximum(selected[g], hit)
            work[g] = jnp.where(hit > 0.0, ninf, work[g])
    masked = jnp.concatenate(
        [jnp.where(selected[g] > 0.0, choice[GROUP_SIZE * g:GROUP_SIZE * (g + 1)], ninf) for g in range(N_GROUPS)],
        axis=0)

    rowf = lax.broadcasted_iota(jnp.int32, (N_EXPERTS, tm), 0).astype(F32)
    hits, idxs, ws = [], [], []
    for _ in range(TOP_K):
        _, idx = _first_argmax(masked, rowf, float(N_EXPERTS))
        hit = rowf == idx
        hits.append(hit)
        idxs.append(idx)
        ws.append(jnp.sum(jnp.where(hit, scores, 0.0), axis=0, keepdims=True))
        masked = jnp.where(hit, ninf, masked)
    wsum = ws[0]
    for k in range(1, TOP_K):
        wsum = wsum + ws[k]

    member = hits[0]
    for k in range(1, TOP_K):
        member = member | hits[k]
    member_f = jnp.where(member, 1.0, 0.0)
    before = jnp.dot(member_f.astype(BF16), tri_ref[...], preferred_element_type=F32) + cnt_scr[...]
    for k in range(TOP_K):
        eidx_ref[k:k + 1, :] = idxs[k].astype(jnp.int32)
        wts_ref[k:k + 1, :] = ws[k] / wsum * ROUTED_SCALE
        rank_ref[k:k + 1, :] = jnp.sum(jnp.where(hits[k], before, 0.0), axis=0, keepdims=True).astype(jnp.int32)
    cnt_scr[...] = cnt_scr[...] + jnp.sum(member_f, axis=1, keepdims=True)
    cnt_ref[...] = jnp.broadcast_to(cnt_scr[...], cnt_ref.shape).astype(jnp.int32)


def _post_attn(x2, oa, ob, ga, gb, woa, wob, wout, g1, b1, rwt, rb, wsg, wsu, wsd, tm):
    t = x2.shape[0]
    tri = jnp.triu(jnp.ones((tm, tm), F32), k=1).astype(BF16)
    row = lambda i: (i, 0)
    col = lambda i: (0, i)
    full = lambda i: (0, 0)
    wspec = lambda arr: pl.BlockSpec(arr.shape, full)
    weights = (woa, wob, wout, g1, b1, rwt, rb, wsg, wsu, wsd, tri)
    return pl.pallas_call(
        _post_attn_kernel,
        grid=(t // tm,),
        in_specs=[pl.BlockSpec((tm, D_MODEL), row), pl.BlockSpec((tm, 512), row), pl.BlockSpec((tm, 512), row),
                  pl.BlockSpec((tm, D_MODEL), row), pl.BlockSpec((tm, D_MODEL), row)] + [wspec(w) for w in weights],
        out_specs=[pl.BlockSpec((tm, D_MODEL), row), pl.BlockSpec((tm, D_MODEL), row),
                   pl.BlockSpec((TOP_K, tm), col), pl.BlockSpec((TOP_K, tm), col), pl.BlockSpec((TOP_K, tm), col),
                   pl.BlockSpec((N_EXPERTS, LANES), full)],
        out_shape=[jax.ShapeDtypeStruct((t, D_MODEL), F32), jax.ShapeDtypeStruct((t, D_MODEL), F32),
                   jax.ShapeDtypeStruct((TOP_K, t), jnp.int32), jax.ShapeDtypeStruct((TOP_K, t), F32),
                   jax.ShapeDtypeStruct((TOP_K, t), jnp.int32), jax.ShapeDtypeStruct((N_EXPERTS, LANES), jnp.int32)],
        scratch_shapes=[pltpu.VMEM((N_EXPERTS, 1), F32)],
        compiler_params=_params(("arbitrary",)),
        name="post_attn",
    )(x2, oa, ob, ga, gb, *weights)


def _dest_kernel(eidx_ref, rank_ref, pstart_ref, dest_ref):
    tm = eidx_ref.shape[1]
    rows = lax.broadcasted_iota(jnp.int32, (N_EXPERTS, tm), 0)
    pstart = pstart_ref[...]
    for k in range(TOP_K):
        hit = rows == eidx_ref[k:k + 1, :]
        start = jnp.sum(jnp.where(hit, pstart, 0).astype(F32), axis=0, keepdims=True).astype(jnp.int32)
        dest_ref[0, k:k + 1, :] = start + rank_ref[k:k + 1, :]


def _dest(eidx, rank, pstart, tm):
    t = eidx.shape[1]
    return pl.pallas_call(
        _dest_kernel,
        grid=(t // tm,),
        in_specs=[pl.BlockSpec((TOP_K, tm), lambda i: (0, i)), pl.BlockSpec((TOP_K, tm), lambda i: (0, i)),
                  pl.BlockSpec((N_EXPERTS, 1), lambda i: (0, 0))],
        out_specs=pl.BlockSpec((1, TOP_K, tm), lambda i: (i, 0, 0)),
        out_shape=jax.ShapeDtypeStruct((t // tm, TOP_K, tm), jnp.int32),
        compiler_params=_params(("arbitrary",)),
        name="dest",
    )(eidx, rank, pstart)


def _row_copy(src_ref, src_row, dst_ref, dst_row, sem):
    return pltpu.make_async_copy(src_ref.at[pl.ds(src_row, 1), :], dst_ref.at[pl.ds(dst_row, 1), :], sem)


def _dispatch_kernel(dest_ref, x_ref, xs_in_ref, xs_ref, sem):
    del xs_in_ref
    tm = x_ref.shape[0]

    def issue(t, carry):
        for k in range(TOP_K):
            _row_copy(x_ref, t, xs_ref, dest_ref[0, k, t], sem).start()
        return carry

    lax.fori_loop(0, tm, issue, 0)

    def drain(t, carry):
        for k in range(TOP_K):
            _row_copy(x_ref, t, xs_ref, dest_ref[0, k, t], sem).wait()
        return carry

    lax.fori_loop(0, tm, drain, 0)


def _dispatch(dest3, x1, n_rows):
    t = x1.shape[0]
    tm = dest3.shape[2]
    zeros = jnp.zeros((n_rows, D_MODEL), F32)
    return pl.pallas_call(
        _dispatch_kernel,
        grid=(t // tm,),
        in_specs=[pl.BlockSpec((1, TOP_K, tm), lambda i: (i, 0, 0), memory_space=pltpu.SMEM),
                  pl.BlockSpec((tm, D_MODEL), lambda i: (i, 0)),
                  pl.BlockSpec(memory_space=pl.ANY)],
        out_specs=pl.BlockSpec(memory_space=pl.ANY),
        out_shape=jax.ShapeDtypeStruct((n_rows, D_MODEL), F32),
        scratch_shapes=[pltpu.SemaphoreType.DMA],
        input_output_aliases={2: 0},
        compiler_params=_params(("arbitrary",)),
        name="dispatch",
    )(dest3, x1, zeros)


def _experts_kernel(be_ref, nv_ref, xs_ref, wg_ref, wu_ref, wd_ref, o_ref, wg_scr, wu_scr, wd_scr):
    i = pl.program_id(0)
    e = be_ref[i]
    e_prev = be_ref[jnp.maximum(i - 1, 0)]

    @pl.when((i == 0) | (e != e_prev))
    def _():
        wg_scr[...] = wg_ref[0].astype(BF16)
        wu_scr[...] = wu_ref[0].astype(BF16)
        wd_scr[...] = wd_ref[0].astype(BF16)

    @pl.when(nv_ref[i] > 0)
    def _():
        xb = xs_ref[...].astype(BF16)
        hg = jnp.dot(xb, wg_scr[...], preferred_element_type=F32)
        hu = jnp.dot(xb, wu_scr[...], preferred_element_type=F32)
        h = (hg * jax.nn.sigmoid(hg)) * hu
        o_ref[...] = jnp.dot(h.astype(BF16), wd_scr[...], preferred_element_type=F32)

    @pl.when(nv_ref[i] == 0)
    def _():
        o_ref[...] = jnp.zeros_like(o_ref)


def _experts(blk_expert, nvalid, xs, w_gate, w_up, w_down):
    n_rows = xs.shape[0]
    n_blk = n_rows // EXPERT_ROWS
    wmap = lambda i, be, nv: (be[i], 0, 0)
    grid_spec = pltpu.PrefetchScalarGridSpec(
        num_scalar_prefetch=2,
        grid=(n_blk,),
        in_specs=[pl.BlockSpec((EXPERT_ROWS, D_MODEL), lambda i, be, nv: (i, 0)),
                  pl.BlockSpec((1, D_MODEL, D_EXPERT), wmap),
                  pl.BlockSpec((1, D_MODEL, D_EXPERT), wmap),
                  pl.BlockSpec((1, D_EXPERT, D_MODEL), wmap)],
        out_specs=pl.BlockSpec((EXPERT_ROWS, D_MODEL), lambda i, be, nv: (i, 0)),
        scratch_shapes=[pltpu.VMEM((D_MODEL, D_EXPERT), BF16), pltpu.VMEM((D_MODEL, D_EXPERT), BF16),
                        pltpu.VMEM((D_EXPERT, D_MODEL), BF16)])
    return pl.pallas_call(
        _experts_kernel,
        grid_spec=grid_spec,
        out_shape=jax.ShapeDtypeStruct((n_rows, D_MODEL), F32),
        compiler_params=_params(("arbitrary",)),
        name="experts",
    )(blk_expert, nvalid, xs, w_gate, w_up, w_down)


def _combine_kernel(dest_ref, w_ref, base_ref, g2_ref, b2_ref, ys_ref, o_ref, buf, sem):
    tm = base_ref.shape[0]

    def issue(t, carry):
        for k in range(TOP_K):
            _row_copy(ys_ref, dest_ref[0, k, t], buf.at[k], t, sem).start()
        return carry

    lax.fori_loop(0, tm, issue, 0)

    def drain(t, carry):
        for k in range(TOP_K):
            _row_copy(ys_ref, dest_ref[0, k, t], buf.at[k], t, sem).wait()
        return carry

    lax.fori_loop(0, tm, drain, 0)

    w = w_ref[...]
    y = base_ref[...]
    for k in range(TOP_K):
        y = y + buf[k] * w[:, k:k + 1]
    o_ref[...] = _layer_norm(y, g2_ref[...], b2_ref[...])


def _combine(dest3, wts_tk, base, g2, b2, ys):
    t = base.shape[0]
    tm = dest3.shape[2]
    return pl.pallas_call(
        _combine_kernel,
        grid=(t // tm,),
        in_specs=[pl.BlockSpec((1, TOP_K, tm), lambda i: (i, 0, 0), memory_space=pltpu.SMEM),
                  pl.BlockSpec((tm, TOP_K), lambda i: (i, 0)),
                  pl.BlockSpec((tm, D_MODEL), lambda i: (i, 0)),
                  pl.BlockSpec((1, D_MODEL), lambda i: (0, 0)),
                  pl.BlockSpec((1, D_MODEL), lambda i: (0, 0)),
                  pl.BlockSpec(memory_space=pl.ANY)],
        out_specs=pl.BlockSpec((tm, D_MODEL), lambda i: (i, 0)),
        out_shape=jax.ShapeDtypeStruct((t, D_MODEL), F32),
        scratch_shapes=[pltpu.VMEM((TOP_K, tm, D_MODEL), F32), pltpu.SemaphoreType.DMA],
        compiler_params=_params(("arbitrary",)),
        name="combine",
    )(dest3, wts_tk, base, g2, b2, ys)


def _tile(n, pref):
    t = pref
    while n % t:
        t //= 2
    return t


def _encoder_layer(x, layer_idx, wp):
    bsz, seq, _ = x.shape
    t = bsz * seq
    x2 = x.reshape(t, D_MODEL)
    lam_init = 0.8 - 0.6 * math.exp(-0.3 * layer_idx)

    cos_t, sin_t = _rope_tables(seq)
    qa, ka, va, qd, kd, vd, ga, gb = _in_proj(x2, wp["w_in"], cos_t, sin_t, seq, _tile(seq, 256))
    oa = _attn_a(qa, ka, va, wp["sink"], bsz, seq, _tile(seq, 512))
    ob = _attn_b(qd, kd, vd, wp["lam_p"], wp["subln_g"], bsz, seq, _tile(seq, 512), _tile(seq, 1024), lam_init)

    tm = _tile(t, 512)
    x1, base, eidx, wts, rank, cnt = _post_attn(
        x2, oa, ob, ga, gb, wp["w_o_a"], wp["w_o_b"], wp["w_out"], wp["ln1_g"], wp["ln1_b"],
        wp["router_wt"], wp["router_bias"], wp["ws_gate"], wp["ws_up"], wp["ws_down"], tm)

    n_assign = t * TOP_K
    n_blk = -(-(n_assign + N_EXPERTS * (EXPERT_ROWS - 1)) // EXPERT_ROWS)
    counts = cnt[:, 0]
    padded = ((counts + EXPERT_ROWS - 1) // EXPERT_ROWS) * EXPERT_ROWS
    pend = jnp.cumsum(padded)
    pstart = pend - padded
    blk_row0 = jnp.arange(n_blk, dtype=jnp.int32) * EXPERT_ROWS
    blk_expert = jnp.clip(jnp.searchsorted(pend, blk_row0, side="right"), 0, N_EXPERTS - 1).astype(jnp.int32)
    nvalid = jnp.clip(counts[blk_expert] - (blk_row0 - pstart[blk_expert]), 0, EXPERT_ROWS).astype(jnp.int32)

    tmd = _tile(t, 128)
    dest3 = _dest(eidx, rank, pstart.astype(jnp.int32)[:, None], tmd)
    xs = _dispatch(dest3, x1, n_blk * EXPERT_ROWS)
    ys = _experts(blk_expert, nvalid, xs, wp["w_gate"], wp["w_up"], wp["w_down"])
    y = _combine(dest3, wts.T, base, wp["ln2_g"], wp["ln2_b"], ys)
    return y.reshape(bsz, seq, D_MODEL)


def kernel(x_prompt, x_sample, w_in, attn_sink, lambda_q1, lambda_k1, lambda_q2, lambda_k2, subln_g, w_o_a, w_o_b, w_out, ln1_g, ln1_b, router_w, router_bias, w_gate, w_up, w_down, ws_gate, ws_up, ws_down, ln2_g, ln2_b):
    y_prompt, y_sample = x_prompt, x_sample
    for l in range(DEPTH):
        wp = {
            "w_in": _prep_w_in(w_in[l]),
            "sink": attn_sink[l].astype(F32),
            "lam_p": jnp.stack([lambda_q1[l], lambda_k1[l], lambda_q2[l], lambda_k2[l]]).astype(F32),
            "subln_g": subln_g[l].astype(F32)[None, :],
            "w_o_a": w_o_a[l].astype(BF16), "w_o_b": w_o_b[l].astype(BF16), "w_out": w_out[l].astype(BF16),
            "ln1_g": ln1_g[l].astype(F32)[None, :], "ln1_b": ln1_b[l].astype(F32)[None, :],
            "router_wt": router_w[l].T.astype(BF16), "router_bias": router_bias[l].astype(F32)[:, None],
            "w_gate": w_gate[l], "w_up": w_up[l], "w_down": w_down[l],
            "ws_gate": ws_gate[l].astype(BF16), "ws_up": ws_up[l].astype(BF16), "ws_down": ws_down[l].astype(BF16),
            "ln2_g": ln2_g[l].astype(F32)[None, :], "ln2_b": ln2_b[l].astype(F32)[None, :],
        }
        y_prompt = _encoder_layer(y_prompt, l, wp)
        y_sample = _encoder_layer(y_sample, l, wp)
    return (y_prompt, y_sample)
```

```python
import functools
import math

import jax
import jax.numpy as jnp
from jax import lax
from jax.experimental import pallas as pl
from jax.experimental.pallas import tpu as pltpu

D_MODEL = 1024
HEAD_DIM = 64
ROPE_THETA = 10000.0
BLOCK = 128
A_Q_HEADS = 8
A_KV_HEADS = 2
WINDOW = 128
B_HEADS = 4
N_EXPERTS = 256
TOP_K = 8
N_GROUPS = 8
TOPK_GROUPS = 4
GROUP_SIZE = N_EXPERTS // N_GROUPS
D_EXPERT = 256
ROUTED_SCALE = 2.5
EXPERT_ROWS = 256
DEPTH = 1
DN_ALPHA = (2 * DEPTH) ** 0.25
LN_EPS = 1e-5
RMS_EPS = 1e-5
NEG = -1e30
LOG2E = math.log2(math.e)

LANES = 128
SUBLANES = 8
VMEM_LIMIT_BYTES = 56 * 1024 * 1024

F32 = jnp.float32
BF16 = jnp.bfloat16
_NT = (((1,), (1,)), ((), ()))


def _params(sem, vmem=VMEM_LIMIT_BYTES):
    return pltpu.CompilerParams(dimension_semantics=sem, vmem_limit_bytes=vmem)


_C_QA, _C_KA, _C_VA, _C_QD, _C_KD, _C_VD, _C_GA, _C_GB, _C_END = 0, 512, 768, 1024, 1536, 2048, 2560, 3584, 4608


def _prep_w_in(w_in):
    cuts = [0, 512, 640, 768, 1280, 1792, 2304, 3328, 4352]
    qa, ka, va, qd, kd, vd, ga, gb = [w_in[:, cuts[i]:cuts[i + 1]] for i in range(8)]
    dup = lambda w: jnp.concatenate([w[:, :64], w[:, :64], w[:, 64:], w[:, 64:]], axis=1)
    return jnp.concatenate([qa, dup(ka), dup(va), qd, kd, vd, ga, gb], axis=1).astype(BF16)


def _rope_tables(s):
    half = HEAD_DIM // 2
    inv = 1.0 / (ROPE_THETA ** (jnp.arange(half, dtype=F32) / half))
    ang = jnp.arange(s, dtype=F32)[:, None] * inv[None, :]
    cos, sin = jnp.cos(ang), jnp.sin(ang)
    return (jnp.concatenate([cos, cos, cos, cos], axis=1),
            jnp.concatenate([-sin, sin, -sin, sin], axis=1))


def _in_proj_kernel(x_ref, w_ref, cos_ref, sin_ref,
                    qa_ref, ka_ref, va_ref, qd_ref, kd_ref, vd_ref, ga_ref, gb_ref):
    xb = x_ref[...].astype(BF16)
    cos = cos_ref[...]
    sin = sin_ref[...]
    lane = lax.broadcasted_iota(jnp.int32, cos.shape, 1)
    first_half = (lane & (HEAD_DIM // 2)) == 0

    def proj(c0, c1):
        return jnp.dot(xb, w_ref[:, c0:c1], preferred_element_type=F32)

    def rope_store(u, out_ref, scale):
        for j in range(u.shape[1] // LANES):
            uj = u[:, LANES * j:LANES * (j + 1)]
            rot = jnp.where(first_half, pltpu.roll(uj, LANES - 32, 1), pltpu.roll(uj, 32, 1))
            r = uj * cos + rot * sin
            if scale != 1.0:
                r = r * scale
            out_ref[:, LANES * j:LANES * (j + 1)] = r.astype(out_ref.dtype)

    scale = HEAD_DIM ** -0.5
    rope_store(proj(_C_QA, _C_KA), qa_ref, scale)
    rope_store(proj(_C_KA, _C_VA), ka_ref, 1.0)
    va_ref[...] = proj(_C_VA, _C_QD).astype(va_ref.dtype)
    rope_store(proj(_C_QD, _C_KD), qd_ref, scale * LOG2E)
    rope_store(proj(_C_KD, _C_VD), kd_ref, 1.0)
    vd_ref[...] = proj(_C_VD, _C_GA).astype(vd_ref.dtype)
    ga_ref[...] = jax.nn.sigmoid(proj(_C_GA, _C_GB)).astype(ga_ref.dtype)
    gb_ref[...] = jax.nn.sigmoid(proj(_C_GB, _C_END)).astype(gb_ref.dtype)


def _in_proj(x2, w_perm, cos_t, sin_t, seq, tm):
    t = x2.shape[0]
    nseq = seq // tm
    row = lambda i: (i, 0)
    widths = (512, 256, 256, 512, 512, 512, 1024, 1024)
    return pl.pallas_call(
        _in_proj_kernel,
        grid=(t // tm,),
        in_specs=[pl.BlockSpec((tm, D_MODEL), row),
                  pl.BlockSpec((D_MODEL, _C_END), lambda i: (0, 0)),
                  pl.BlockSpec((tm, LANES), lambda i: (i % nseq, 0)),
                  pl.BlockSpec((tm, LANES), lambda i: (i % nseq, 0))],
        out_specs=[pl.BlockSpec((tm, w), row) for w in widths],
        out_shape=[jax.ShapeDtypeStruct((t, w), BF16) for w in widths],
        compiler_params=_params(("arbitrary",)),
        name="in_proj",
    )(x2, w_perm, cos_t, sin_t)


def _attn_a_kernel(sink_ref, q_ref, kp_ref, kc_ref, kn_ref, vp_ref, vc_ref, vn_ref, o_ref, *, seq, tq):
    i = pl.program_id(1)
    kk = jnp.concatenate([kp_ref[...], kc_ref[...], kn_ref[...]], axis=0)
    vv = jnp.concatenate([vp_ref[...], vc_ref[...], vn_ref[...]], axis=0)
    lane = lax.broadcasted_iota(jnp.int32, (kk.shape[0], LANES), 1)
    lo = lane < HEAD_DIM
    zero = jnp.zeros((kk.shape[0], LANES), BF16)
    k_lo = [jnp.where(lo, kk[:, LANES * h:LANES * (h + 1)], zero) for h in range(A_KV_HEADS)]
    k_hi = [jnp.where(lo, zero, kk[:, LANES * h:LANES * (h + 1)]) for h in range(A_KV_HEADS)]
    v_lo = [jnp.where(lo, vv[:, LANES * h:LANES * (h + 1)], zero) for h in range(A_KV_HEADS)]
    v_hi = [jnp.where(lo, zero, vv[:, LANES * h:LANES * (h + 1)]) for h in range(A_KV_HEADS)]

    qi = lax.broadcasted_iota(jnp.int32, (BLOCK, 3 * BLOCK), 0)
    kj = lax.broadcasted_iota(jnp.int32, (BLOCK, 3 * BLOCK), 1)
    band = jnp.abs(kj - BLOCK - qi) <= WINDOW
    for j in range(tq // BLOCK):
        kpos = i * tq + (j - 1) * BLOCK + kj
        mask = band & (kpos >= 0) & (kpos < seq)
        r0, r1 = j * BLOCK, (j + 3) * BLOCK
        for c in range(A_Q_HEADS // 2):
            h = c // 2
            q2 = q_ref[j * BLOCK:(j + 1) * BLOCK, LANES * c:LANES * (c + 1)]
            out = None
            for half, (kx, vx) in enumerate(((k_lo[h], v_lo[h]), (k_hi[h], v_hi[h]))):
                s = lax.dot_general(q2, kx[r0:r1], _NT, preferred_element_type=F32)
                s = jnp.where(mask, s, NEG)
                snk = sink_ref[2 * c + half]
                m = jnp.maximum(jnp.max(s, axis=-1, keepdims=True), snk)
                p = jnp.exp(s - m)
                den = jnp.sum(p, axis=-1, keepdims=True) + jnp.exp(snk - m)
                p = p * (1.0 / den)
                pv = jnp.dot(p.astype(BF16), vx[r0:r1], preferred_element_type=F32)
                out = pv if out is None else out + pv
            o_ref[j * BLOCK:(j + 1) * BLOCK, LANES * c:LANES * (c + 1)] = out.astype(o_ref.dtype)


def _attn_a(qa, ka, va, sink, bsz, seq, tq):
    t = qa.shape[0]
    nq = seq // tq
    nb = seq // BLOCK
    r = tq // BLOCK
    cur = lambda b, i: (b * nq + i, 0)
    prev = lambda b, i: (b * nb + jnp.maximum(i * r - 1, 0), 0)
    nxt = lambda b, i: (b * nb + jnp.minimum((i + 1) * r, nb - 1), 0)
    kv_specs = [pl.BlockSpec((BLOCK, 256), prev), pl.BlockSpec((tq, 256), cur), pl.BlockSpec((BLOCK, 256), nxt)]
    return pl.pallas_call(
        functools.partial(_attn_a_kernel, seq=seq, tq=tq),
        grid=(bsz, nq),
        in_specs=[pl.BlockSpec(memory_space=pltpu.SMEM), pl.BlockSpec((tq, 512), cur)] + kv_specs + kv_specs,
        out_specs=pl.BlockSpec((tq, 512), cur),
        out_shape=jax.ShapeDtypeStruct((t, 512), BF16),
        compiler_params=_params(("arbitrary", "arbitrary")),
        name="attn_a",
    )(sink, qa, ka, ka, ka, va, va, va)


def _attn_b_kernel(lam_ref, g_ref, q_ref, k_ref, v_ref, o_ref, acc_ref, s_ref, *, seq, tk, lam_init):
    q = q_ref[...]
    tq = q.shape[0]
    lane = lax.broadcasted_iota(jnp.int32, (tk, LANES), 1)
    lo = lane < HEAD_DIM
    zero = jnp.zeros((tk, LANES), BF16)
    acc_ref[...] = jnp.zeros_like(acc_ref)
    ones_col = jnp.where(lane == 0, 1.0, 0.0).astype(BF16)

    def scores(kc):
        k = k_ref[pl.ds(pl.multiple_of(kc * tk, tk), tk), :]
        return [lax.dot_general(q, kx, _NT, preferred_element_type=F32)
                for kx in (jnp.where(lo, k, zero), jnp.where(lo, zero, k))]

    def consume(kc, maxes):
        v_aug = jnp.concatenate([v_ref[pl.ds(pl.multiple_of(kc * tk, tk), tk), :], ones_col], axis=1)
        new = []
        for idx in range(2):
            s = s_ref[idx]
            m_new = jnp.maximum(maxes[idx], jnp.max(s, axis=-1, keepdims=True))
            alpha = jnp.exp2(maxes[idx] - m_new)
            p = jnp.exp2(s - m_new)
            acc_ref[idx] = alpha * acc_ref[idx] + jnp.dot(p.astype(BF16), v_aug, preferred_element_type=F32)
            new.append(m_new)
        return tuple(new)

    def body(kc, maxes):
        nxt = scores(kc + 1)
        maxes = consume(kc, maxes)
        s_ref[0], s_ref[1] = nxt
        return maxes

    s_ref[0], s_ref[1] = scores(0)
    minf = jnp.full((tq, 1), -jnp.inf, F32)
    n_chunks = seq // tk
    consume(n_chunks - 1, lax.fori_loop(0, n_chunks - 1, body, (minf, minf)))

    lp = lam_ref[...]
    lam = (jnp.exp(jnp.sum(lp[0:1] * lp[1:2], axis=-1, keepdims=True))
           - jnp.exp(jnp.sum(lp[2:3] * lp[3:4], axis=-1, keepdims=True)) + lam_init)
    acc1, acc2 = acc_ref[0], acc_ref[1]
    l1, l2 = acc1[:, LANES:LANES + 1], acc2[:, LANES:LANES + 1]
    o = acc1[:, :LANES] * (1.0 / l1) - lam * (acc2[:, :LANES] * (1.0 / l2))
    o = o * lax.rsqrt(jnp.mean(o * o, axis=-1, keepdims=True) + RMS_EPS)
    o = o * g_ref[...] * (1.0 - lam_init)
    o_ref[...] = o.astype(o_ref.dtype)


def _attn_b(qd, kd, vd, lam_p, subln_g, bsz, seq, tq, tk, lam_init):
    t = qd.shape[0]
    nq = seq // tq
    return pl.pallas_call(
        functools.partial(_attn_b_kernel, seq=seq, tk=tk, lam_init=lam_init),
        grid=(bsz, B_HEADS, nq),
        in_specs=[pl.BlockSpec((4, HEAD_DIM), lambda b, h, i: (0, 0)),
                  pl.BlockSpec((1, LANES), lambda b, h, i: (0, 0)),
                  pl.BlockSpec((tq, LANES), lambda b, h, i: (b * nq + i, h)),
                  pl.BlockSpec((seq, LANES), lambda b, h, i: (b, h)),
                  pl.BlockSpec((seq, LANES), lambda b, h, i: (b, h))],
        out_specs=pl.BlockSpec((tq, LANES), lambda b, h, i: (b * nq + i, h)),
        out_shape=jax.ShapeDtypeStruct((t, 512), BF16),
        scratch_shapes=[pltpu.VMEM((2, tq, 2 * LANES), F32), pltpu.VMEM((2, tq, tk), F32)],
        compiler_params=_params(("arbitrary", "arbitrary", "arbitrary")),
        name="attn_b",
    )(lam_p, subln_g, qd, kd, vd)


HALF = D_MODEL // 2
_HI_MASK = -65536


def _pack_pair(lo, hi):
    lo_bits = lax.bitcast_convert_type(lo.astype(BF16).astype(F32), jnp.int32)
    hi_bits = lax.bitcast_convert_type(hi.astype(BF16).astype(F32), jnp.int32)
    return lax.shift_right_logical(lo_bits, 16) | hi_bits


def _unpack_pair(w):
    lo = lax.bitcast_convert_type(lax.shift_left(w, 16), F32)
    hi = lax.bitcast_convert_type(w & _HI_MASK, F32)
    return lo, hi


def _layer_norm(y, g, b):
    mu = jnp.mean(y, axis=-1, keepdims=True)
    d = y - mu
    var = jnp.mean(d * d, axis=-1, keepdims=True)
    return d * lax.rsqrt(var + LN_EPS) * g + b


def _first_argmax(vals, rowf, big):
    m = jnp.max(vals, axis=0, keepdims=True)
    idx = jnp.min(jnp.where(vals == m, rowf, big), axis=0, keepdims=True)
    return m, idx


def _post_attn_kernel(x_ref, oa_ref, ob_ref, ga_ref, gb_ref, woa_ref, wob_ref, wout_ref, g1_ref, b1_ref,
                      rwt_ref, rb_ref, wsg_ref, wsu_ref, wsd_ref, tri_ref, cnt_in_ref,
                      x1_ref, base_ref, eidx_ref, wts_ref, rank_ref, cnt_ref, cnt_scr):
    i = pl.program_id(0)
    tm = x_ref.shape[0]

    @pl.when(i == 0)
    def _():
        cnt_scr[...] = cnt_in_ref[...]

    a = jnp.dot(oa_ref[...], woa_ref[...], preferred_element_type=F32)
    b = jnp.dot(ob_ref[...], wob_ref[...], preferred_element_type=F32)
    merged = ga_ref[...].astype(F32) * a + gb_ref[...].astype(F32) * b
    mix = jnp.dot(merged.astype(BF16), wout_ref[...], preferred_element_type=F32)
    x1 = _layer_norm(DN_ALPHA * x_ref[...] + mix, g1_ref[...], b1_ref[...])
    x1_ref[...] = _pack_pair(x1[:, :HALF], x1[:, HALF:])
    x1b = x1.astype(BF16)

    hg = jnp.dot(x1b, wsg_ref[...], preferred_element_type=F32)
    hu = jnp.dot(x1b, wsu_ref[...], preferred_element_type=F32)
    hs = (hg * jax.nn.sigmoid(hg)) * hu
    shared = jnp.dot(hs.astype(BF16), wsd_ref[...], preferred_element_type=F32)
    base_ref[...] = DN_ALPHA * x1 + shared

    logits = lax.dot_general(rwt_ref[...], x1b, _NT, preferred_element_type=F32)
    scores = jax.nn.sigmoid(logits)
    choice = scores + rb_ref[...]
    ninf = -jnp.inf
    grow = lax.broadcasted_iota(jnp.int32, (GROUP_SIZE, tm), 0).astype(F32)
    gscore = []
    for g in range(N_GROUPS):
        blk = choice[GROUP_SIZE * g:GROUP_SIZE * (g + 1)]
        m1, i1 = _first_argmax(blk, grow, float(GROUP_SIZE))
        m2 = jnp.max(jnp.where(grow == i1, ninf, blk), axis=0, keepdims=True)
        gscore.append(m1 + m2)
    selected = [jnp.zeros((1, tm), F32) for _ in range(N_GROUPS)]
    work = list(gscore)
    for _ in range(TOPK_GROUPS):
        best = work[0]
        for g in range(1, N_GROUPS):
            best = jnp.maximum(best, work[g])
        taken = jnp.zeros((1, tm), F32)
        for g in range(N_GROUPS):
            hit = jnp.where((work[g] == best) & (taken == 0.0), 1.0, 0.0)
            taken = jnp.maximum(taken, hit)
            selected[g] = jnp.maximum(selected[g], hit)
            work[g] = jnp.where(hit > 0.0, ninf, work[g])
    masked = jnp.concatenate(
        [jnp.where(selected[g] > 0.0, choice[GROUP_SIZE * g:GROUP_SIZE * (g + 1)], ninf) for g in range(N_GROUPS)],
        axis=0)

    rowf = lax.broadcasted_iota(jnp.int32, (N_EXPERTS, tm), 0).astype(F32)
    hits, idxs, ws = [], [], []
    for _ in range(TOP_K):
        _, idx = _first_argmax(masked, rowf, float(N_EXPERTS))
        hit = rowf == idx
        hits.append(hit)
        idxs.append(idx)
        ws.append(jnp.sum(jnp.where(hit, scores, 0.0), axis=0, keepdims=True))
        masked = jnp.where(hit, ninf, masked)
    wsum = ws[0]
    for k in range(1, TOP_K):
        wsum = wsum + ws[k]

    member = hits[0]
    for k in range(1, TOP_K):
        member = member | hits[k]
    member_f = jnp.where(member, 1.0, 0.0)
    before = jnp.dot(member_f.astype(BF16), tri_ref[...], preferred_element_type=F32) + cnt_scr[...]
    for k in range(TOP_K):
        eidx_ref[k:k + 1, :] = idxs[k].astype(jnp.int32)
        wts_ref[k:k + 1, :] = ws[k] / wsum * ROUTED_SCALE
        rank_ref[k:k + 1, :] = jnp.sum(jnp.where(hits[k], before, 0.0), axis=0, keepdims=True).astype(jnp.int32)
    cnt_scr[...] = cnt_scr[...] + jnp.sum(member_f, axis=1, keepdims=True)
    cnt_ref[...] = jnp.broadcast_to(cnt_scr[...], cnt_ref.shape).astype(jnp.int32)


def _post_attn(x2, oa, ob, ga, gb, woa, wob, wout, g1, b1, rwt, rb, wsg, wsu, wsd, cnt_in, tm):
    t = x2.shape[0]
    tri = jnp.triu(jnp.ones((tm, tm), F32), k=1).astype(BF16)
    row = lambda i: (i, 0)
    col = lambda i: (0, i)
    full = lambda i: (0, 0)
    wspec = lambda arr: pl.BlockSpec(arr.shape, full)
    weights = (woa, wob, wout, g1, b1, rwt, rb, wsg, wsu, wsd, tri, cnt_in)
    return pl.pallas_call(
        _post_attn_kernel,
        grid=(t // tm,),
        in_specs=[pl.BlockSpec((tm, D_MODEL), row), pl.BlockSpec((tm, 512), row), pl.BlockSpec((tm, 512), row),
                  pl.BlockSpec((tm, D_MODEL), row), pl.BlockSpec((tm, D_MODEL), row)] + [wspec(w) for w in weights],
        out_specs=[pl.BlockSpec((tm, HALF), row), pl.BlockSpec((tm, D_MODEL), row),
                   pl.BlockSpec((TOP_K, tm), col), pl.BlockSpec((TOP_K, tm), col), pl.BlockSpec((TOP_K, tm), col),
                   pl.BlockSpec((N_EXPERTS, LANES), full)],
        out_shape=[jax.ShapeDtypeStruct((t, HALF), jnp.int32), jax.ShapeDtypeStruct((t, D_MODEL), F32),
                   jax.ShapeDtypeStruct((TOP_K, t), jnp.int32), jax.ShapeDtypeStruct((TOP_K, t), F32),
                   jax.ShapeDtypeStruct((TOP_K, t), jnp.int32), jax.ShapeDtypeStruct((N_EXPERTS, LANES), jnp.int32)],
        scratch_shapes=[pltpu.VMEM((N_EXPERTS, 1), F32)],
        compiler_params=_params(("arbitrary",)),
        name="post_attn",
    )(x2, oa, ob, ga, gb, *weights)


def _dest_kernel(eidx_ref, rank_ref, pstart_ref, dest_ref):
    tm = eidx_ref.shape[1]
    rows = lax.broadcasted_iota(jnp.int32, (N_EXPERTS, tm), 0)
    pstart = pstart_ref[...]
    for k in range(TOP_K):
        hit = rows == eidx_ref[k:k + 1, :]
        start = jnp.sum(jnp.where(hit, pstart, 0).astype(F32), axis=0, keepdims=True).astype(jnp.int32)
        dest_ref[0, k:k + 1, :] = start + rank_ref[k:k + 1, :]


def _dest(eidx, rank, pstart, tm):
    t = eidx.shape[1]
    return pl.pallas_call(
        _dest_kernel,
        grid=(t // tm,),
        in_specs=[pl.BlockSpec((TOP_K, tm), lambda i: (0, i)), pl.BlockSpec((TOP_K, tm), lambda i: (0, i)),
                  pl.BlockSpec((N_EXPERTS, 1), lambda i: (0, 0))],
        out_specs=pl.BlockSpec((1, TOP_K, tm), lambda i: (i, 0, 0)),
        out_shape=jax.ShapeDtypeStruct((t // tm, TOP_K, tm), jnp.int32),
        compiler_params=_params(("arbitrary",)),
        name="dest",
    )(eidx, rank, pstart)


def _row_copy(src_ref, src_row, dst_ref, dst_row, sem):
    return pltpu.make_async_copy(src_ref.at[pl.ds(src_row, 1), :], dst_ref.at[pl.ds(dst_row, 1), :], sem)


def _dispatch_kernel(pad_ref, dest_ref, *refs, tiles):
    n_groups = len(tiles)
    x_refs, xs_ref, zbuf, sem = refs[:n_groups], refs[n_groups], refs[n_groups + 1], refs[n_groups + 2]
    i = pl.program_id(0)
    n_rows = xs_ref.shape[0]
    tm = dest_ref.shape[2]

    def zero_copy(e):
        r0 = jnp.minimum(lax.shift_left(lax.shift_right_logical(pad_ref[e], 3), 3), n_rows - EXPERT_ROWS)
        return pltpu.make_async_copy(zbuf, xs_ref.at[pl.ds(pl.multiple_of(r0, SUBLANES), EXPERT_ROWS), :], sem)

    def zero_start(e, carry):
        zero_copy(e).start()
        return carry

    def zero_wait(e, carry):
        zero_copy(e).wait()
        return carry

    @pl.when(i == 0)
    def _():
        zbuf[...] = jnp.zeros_like(zbuf)
        lax.fori_loop(0, N_EXPERTS, zero_start, 0)
        lax.fori_loop(0, N_EXPERTS, zero_wait, 0)

    def scatter(x_ref):
        def issue(t, carry):
            for k in range(TOP_K):
                _row_copy(x_ref, t, xs_ref, dest_ref[0, k, t], sem).start()
            return carry

        def drain(t, carry):
            for k in range(TOP_K):
                _row_copy(x_ref, t, xs_ref, dest_ref[0, k, t], sem).wait()
            return carry

        lax.fori_loop(0, tm, issue, 0)
        lax.fori_loop(0, tm, drain, 0)

    first = 0
    for g in range(n_groups):
        pl.when((i >= first) & (i < first + tiles[g]))(functools.partial(scatter, x_refs[g]))
        first += tiles[g]


def _dispatch(pad_row0, dest3, x1ps, n_rows):
    tm = dest3.shape[2]
    tiles = tuple(x.shape[0] // tm for x in x1ps)
    firsts = [sum(tiles[:g]) for g in range(len(tiles))]

    def x_spec(g):
        return pl.BlockSpec((tm, HALF), lambda i: (jnp.clip(i - firsts[g], 0, tiles[g] - 1), 0))

    return pl.pallas_call(
        functools.partial(_dispatch_kernel, tiles=tiles),
        grid=(sum(tiles),),
        in_specs=[pl.BlockSpec(memory_space=pltpu.SMEM),
                  pl.BlockSpec((1, TOP_K, tm), lambda i: (i, 0, 0), memory_space=pltpu.SMEM)]
                 + [x_spec(g) for g in range(len(tiles))],
        out_specs=pl.BlockSpec(memory_space=pl.ANY),
        out_shape=jax.ShapeDtypeStruct((n_rows, HALF), jnp.int32),
        scratch_shapes=[pltpu.VMEM((EXPERT_ROWS, HALF), jnp.int32), pltpu.SemaphoreType.DMA],
        compiler_params=_params(("arbitrary",)),
        name="dispatch",
    )(pad_row0, dest3, *x1ps)


def _experts_kernel(be_ref, nv_ref, xs_ref, wg_ref, wu_ref, wd_ref, o_ref, wg_scr, wu_scr, wd_scr):
    i = pl.program_id(0)
    e = be_ref[i]
    e_prev = be_ref[jnp.maximum(i - 1, 0)]

    @pl.when((i == 0) | (e != e_prev))
    def _():
        wg_scr[...] = wg_ref[0].astype(BF16)
        wu_scr[...] = wu_ref[0].astype(BF16)
        wd_scr[...] = wd_ref[0].astype(BF16)

    @pl.when(nv_ref[i] > 0)
    def _():
        lo, hi = _unpack_pair(xs_ref[...])
        lo, hi = lo.astype(BF16), hi.astype(BF16)
        hg = (jnp.dot(lo, wg_scr[:HALF], preferred_element_type=F32)
              + jnp.dot(hi, wg_scr[HALF:], preferred_element_type=F32))
        hu = (jnp.dot(lo, wu_scr[:HALF], preferred_element_type=F32)
              + jnp.dot(hi, wu_scr[HALF:], preferred_element_type=F32))
        h = ((hg * jax.nn.sigmoid(hg)) * hu).astype(BF16)
        o_ref[...] = _pack_pair(jnp.dot(h, wd_scr[:, :HALF], preferred_element_type=F32),
                                jnp.dot(h, wd_scr[:, HALF:], preferred_element_type=F32))

    @pl.when(nv_ref[i] == 0)
    def _():
        o_ref[...] = jnp.zeros_like(o_ref)


def _experts(blk_expert, nvalid, xs, w_gate, w_up, w_down):
    n_rows = xs.shape[0]
    n_blk = n_rows // EXPERT_ROWS
    wmap = lambda i, be, nv: (be[i], 0, 0)
    xmap = lambda i, be, nv: (jnp.where(nv[i] > 0, i, 0), 0)
    grid_spec = pltpu.PrefetchScalarGridSpec(
        num_scalar_prefetch=2,
        grid=(n_blk,),
        in_specs=[pl.BlockSpec((EXPERT_ROWS, HALF), xmap),
                  pl.BlockSpec((1, D_MODEL, D_EXPERT), wmap),
                  pl.BlockSpec((1, D_MODEL, D_EXPERT), wmap),
                  pl.BlockSpec((1, D_EXPERT, D_MODEL), wmap)],
        out_specs=pl.BlockSpec((EXPERT_ROWS, HALF), lambda i, be, nv: (i, 0)),
        scratch_shapes=[pltpu.VMEM((D_MODEL, D_EXPERT), BF16), pltpu.VMEM((D_MODEL, D_EXPERT), BF16),
                        pltpu.VMEM((D_EXPERT, D_MODEL), BF16)])
    return pl.pallas_call(
        _experts_kernel,
        grid_spec=grid_spec,
        out_shape=jax.ShapeDtypeStruct((n_rows, HALF), jnp.int32),
        compiler_params=_params(("arbitrary",)),
        name="experts",
    )(blk_expert, nvalid, xs, w_gate, w_up, w_down)


def _combine_kernel(dest_ref, w_ref, base_ref, g2_ref, b2_ref, ys_ref, o_ref, buf, sem):
    tm = base_ref.shape[0]

    def issue(t, carry):
        for k in range(TOP_K):
            _row_copy(ys_ref, dest_ref[0, k, t], buf.at[k], t, sem).start()
        return carry

    lax.fori_loop(0, tm, issue, 0)

    def drain(t, carry):
        for k in range(TOP_K):
            _row_copy(ys_ref, dest_ref[0, k, t], buf.at[k], t, sem).wait()
        return carry

    lax.fori_loop(0, tm, drain, 0)

    w = w_ref[...]
    y_lo = base_ref[:, :HALF]
    y_hi = base_ref[:, HALF:]
    for k in range(TOP_K):
        lo, hi = _unpack_pair(buf[k])
        y_lo = y_lo + lo * w[:, k:k + 1]
        y_hi = y_hi + hi * w[:, k:k + 1]
    o_ref[...] = _layer_norm(jnp.concatenate([y_lo, y_hi], axis=1), g2_ref[...], b2_ref[...])


def _combine(dest3, wts_tk, base, g2, b2, ys):
    t = base.shape[0]
    tm = dest3.shape[2]
    return pl.pallas_call(
        _combine_kernel,
        grid=(t // tm,),
        in_specs=[pl.BlockSpec((1, TOP_K, tm), lambda i: (i, 0, 0), memory_space=pltpu.SMEM),
                  pl.BlockSpec((tm, TOP_K), lambda i: (i, 0)),
                  pl.BlockSpec((tm, D_MODEL), lambda i: (i, 0)),
                  pl.BlockSpec((1, D_MODEL), lambda i: (0, 0)),
                  pl.BlockSpec((1, D_MODEL), lambda i: (0, 0)),
                  pl.BlockSpec(memory_space=pl.ANY)],
        out_specs=pl.BlockSpec((tm, D_MODEL), lambda i: (i, 0)),
        out_shape=jax.ShapeDtypeStruct((t, D_MODEL), F32),
        scratch_shapes=[pltpu.VMEM((TOP_K, tm, HALF), jnp.int32), pltpu.SemaphoreType.DMA],
        compiler_params=_params(("arbitrary",)),
        name="combine",
    )(dest3, wts_tk, base, g2, b2, ys)


def _tile(n, pref):
    t = pref
    while n % t:
        t //= 2
    return t


def _token_mixers(x, layer_idx, wp, cnt_in):
    bsz, seq, _ = x.shape
    t = bsz * seq
    x2 = x.reshape(t, D_MODEL)
    lam_init = 0.8 - 0.6 * math.exp(-0.3 * layer_idx)

    cos_t, sin_t = _rope_tables(seq)
    qa, ka, va, qd, kd, vd, ga, gb = _in_proj(x2, wp["w_in"], cos_t, sin_t, seq, _tile(seq, 256))
    oa = _attn_a(qa, ka, va, wp["sink"], bsz, seq, _tile(seq, 512))
    ob = _attn_b(qd, kd, vd, wp["lam_p"], wp["subln_g"], bsz, seq, _tile(seq, 512), _tile(seq, 1024), lam_init)
    return _post_attn(
        x2, oa, ob, ga, gb, wp["w_o_a"], wp["w_o_b"], wp["w_out"], wp["ln1_g"], wp["ln1_b"],
        wp["router_wt"], wp["router_bias"], wp["ws_gate"], wp["ws_up"], wp["ws_down"], cnt_in, _tile(t, 512))


def _encoder_layer(xs_in, layer_idx, wp):
    cnt_in = jnp.zeros((N_EXPERTS, 1), F32)
    routed = []
    for x in xs_in:
        x1, base, eidx, wts, rank, cnt = _token_mixers(x, layer_idx, wp, cnt_in)
        cnt_in = cnt[:, :1].astype(F32)
        routed.append((x1, base, eidx, wts, rank))

    n_assign = sum(x.shape[0] * x.shape[1] for x in xs_in) * TOP_K
    n_blk = -(-(n_assign + N_EXPERTS * (EXPERT_ROWS - 1)) // EXPERT_ROWS)
    counts = cnt[:, 0]
    padded = ((counts + EXPERT_ROWS - 1) // EXPERT_ROWS) * EXPERT_ROWS
    pend = jnp.cumsum(padded)
    pstart = pend - padded
    blk_row0 = jnp.arange(n_blk, dtype=jnp.int32) * EXPERT_ROWS
    blk_expert = jnp.clip(jnp.searchsorted(pend, blk_row0, side="right"), 0, N_EXPERTS - 1).astype(jnp.int32)
    nvalid = jnp.clip(counts[blk_expert] - (blk_row0 - pstart[blk_expert]), 0, EXPERT_ROWS).astype(jnp.int32)
    pstart_col = pstart.astype(jnp.int32)[:, None]

    dests = [_dest(eidx, rank, pstart_col, 128) for _, _, eidx, _, rank in routed]
    xs = _dispatch((pstart + counts).astype(jnp.int32), jnp.concatenate(dests, axis=0),
                   [r[0] for r in routed], n_blk * EXPERT_ROWS)
    ys = _experts(blk_expert, nvalid, xs, wp["w_gate"], wp["w_up"], wp["w_down"])
    outs = []
    for x, dest3, (x1, base, eidx, wts, rank) in zip(xs_in, dests, routed):
        y = _combine(dest3, wts.T, base, wp["ln2_g"], wp["ln2_b"], ys)
        outs.append(y.reshape(x.shape))
    return outs


def kernel(x_prompt, x_sample, w_in, attn_sink, lambda_q1, lambda_k1, lambda_q2, lambda_k2, subln_g, w_o_a, w_o_b, w_out, ln1_g, ln1_b, router_w, router_bias, w_gate, w_up, w_down, ws_gate, ws_up, ws_down, ln2_g, ln2_b):
    y_prompt, y_sample = x_prompt, x_sample
    for l in range(DEPTH):
        wp = {
            "w_in": _prep_w_in(w_in[l]),
            "sink": attn_sink[l].astype(F32),
            "lam_p": jnp.stack([lambda_q1[l], lambda_k1[l], lambda_q2[l], lambda_k2[l]]).astype(F32),
            "subln_g": subln_g[l].astype(F32)[None, :],
            "w_o_a": w_o_a[l].astype(BF16), "w_o_b": w_o_b[l].astype(BF16), "w_out": w_out[l].astype(BF16),
            "ln1_g": ln1_g[l].astype(F32)[None, :], "ln1_b": ln1_b[l].astype(F32)[None, :],
            "router_wt": router_w[l].T.astype(BF16), "router_bias": router_bias[l].astype(F32)[:, None],
            "w_gate": w_gate[l], "w_up": w_up[l], "w_down": w_down[l],
            "ws_gate": ws_gate[l].astype(BF16), "ws_up": ws_up[l].astype(BF16), "ws_down": ws_down[l].astype(BF16),
            "ln2_g": ln2_g[l].astype(F32)[None, :], "ln2_b": ln2_b[l].astype(F32)[None, :],
        }
        y_prompt, y_sample = _encoder_layer((y_prompt, y_sample), l, wp)
    return (y_prompt, y_sample)
```

```python
import functools
import math

import jax
import jax.numpy as jnp
from jax import lax
from jax.experimental import pallas as pl
from jax.experimental.pallas import tpu as pltpu

D_MODEL = 1024
HEAD_DIM = 64
ROPE_THETA = 10000.0
BLOCK = 128
A_Q_HEADS = 8
A_KV_HEADS = 2
WINDOW = 128
B_HEADS = 4
N_EXPERTS = 256
TOP_K = 8
N_GROUPS = 8
TOPK_GROUPS = 4
GROUP_SIZE = N_EXPERTS // N_GROUPS
D_EXPERT = 256
ROUTED_SCALE = 2.5
EXPERT_ROWS = 256
DEPTH = 1
DN_ALPHA = (2 * DEPTH) ** 0.25
LN_EPS = 1e-5
RMS_EPS = 1e-5
NEG = -1e30
LOG2E = math.log2(math.e)

LANES = 128
SUBLANES = 8
VMEM_LIMIT_BYTES = 56 * 1024 * 1024

F32 = jnp.float32
BF16 = jnp.bfloat16
_NT = (((1,), (1,)), ((), ()))


def _params(sem, vmem=VMEM_LIMIT_BYTES):
    return pltpu.CompilerParams(dimension_semantics=sem, vmem_limit_bytes=vmem)


_C_QA, _C_KA, _C_VA, _C_QD, _C_KD, _C_VD, _C_GA, _C_GB, _C_END = 0, 512, 768, 1024, 1536, 2048, 2560, 3584, 4608


def _prep_w_in(w_in):
    cuts = [0, 512, 640, 768, 1280, 1792, 2304, 3328, 4352]
    qa, ka, va, qd, kd, vd, ga, gb = [w_in[:, cuts[i]:cuts[i + 1]] for i in range(8)]
    dup = lambda w: jnp.concatenate([w[:, :64], w[:, :64], w[:, 64:], w[:, 64:]], axis=1)
    return jnp.concatenate([qa, dup(ka), dup(va), qd, kd, vd, ga, gb], axis=1).astype(BF16)


def _rope_tables(s):
    half = HEAD_DIM // 2
    inv = 1.0 / (ROPE_THETA ** (jnp.arange(half, dtype=F32) / half))
    ang = jnp.arange(s, dtype=F32)[:, None] * inv[None, :]
    cos, sin = jnp.cos(ang), jnp.sin(ang)
    return (jnp.concatenate([cos, cos, cos, cos], axis=1),
            jnp.concatenate([-sin, sin, -sin, sin], axis=1))


def _in_proj_kernel(x_ref, w_ref, cos_ref, sin_ref,
                    qa_ref, ka_ref, va_ref, qd_ref, kd_ref, vd_ref, ga_ref, gb_ref):
    xb = x_ref[...].astype(BF16)
    cos = cos_ref[...]
    sin = sin_ref[...]
    lane = lax.broadcasted_iota(jnp.int32, cos.shape, 1)
    first_half = (lane & (HEAD_DIM // 2)) == 0

    def proj(c0, c1):
        return jnp.dot(xb, w_ref[:, c0:c1], preferred_element_type=F32)

    def rope_store(u, out_ref, scale):
        for j in range(u.shape[1] // LANES):
            uj = u[:, LANES * j:LANES * (j + 1)]
            rot = jnp.where(first_half, pltpu.roll(uj, LANES - 32, 1), pltpu.roll(uj, 32, 1))
            r = uj * cos + rot * sin
            if scale != 1.0:
                r = r * scale
            out_ref[:, LANES * j:LANES * (j + 1)] = r.astype(out_ref.dtype)

    scale = HEAD_DIM ** -0.5
    rope_store(proj(_C_QA, _C_KA), qa_ref, scale)
    rope_store(proj(_C_KA, _C_VA), ka_ref, 1.0)
    va_ref[...] = proj(_C_VA, _C_QD).astype(va_ref.dtype)
    rope_store(proj(_C_QD, _C_KD), qd_ref, scale * LOG2E)
    rope_store(proj(_C_KD, _C_VD), kd_ref, 1.0)
    vd_ref[...] = proj(_C_VD, _C_GA).astype(vd_ref.dtype)
    ga_ref[...] = jax.nn.sigmoid(proj(_C_GA, _C_GB)).astype(ga_ref.dtype)
    gb_ref[...] = jax.nn.sigmoid(proj(_C_GB, _C_END)).astype(gb_ref.dtype)


def _in_proj(x2, w_perm, cos_t, sin_t, seq, tm):
    t = x2.shape[0]
    nseq = seq // tm
    row = lambda i: (i, 0)
    widths = (512, 256, 256, 512, 512, 512, 1024, 1024)
    return pl.pallas_call(
        _in_proj_kernel,
        grid=(t // tm,),
        in_specs=[pl.BlockSpec((tm, D_MODEL), row),
                  pl.BlockSpec((D_MODEL, _C_END), lambda i: (0, 0)),
                  pl.BlockSpec((tm, LANES), lambda i: (i % nseq, 0)),
                  pl.BlockSpec((tm, LANES), lambda i: (i % nseq, 0))],
        out_specs=[pl.BlockSpec((tm, w), row) for w in widths],
        out_shape=[jax.ShapeDtypeStruct((t, w), BF16) for w in widths],
        compiler_params=_params(("arbitrary",)),
        name="in_proj",
    )(x2, w_perm, cos_t, sin_t)


def _attn_a_kernel(sink_ref, q_ref, kp_ref, kc_ref, kn_ref, vp_ref, vc_ref, vn_ref, o_ref, *, seq, tq):
    i = pl.program_id(1)
    kk = jnp.concatenate([kp_ref[...], kc_ref[...], kn_ref[...]], axis=0)
    vv = jnp.concatenate([vp_ref[...], vc_ref[...], vn_ref[...]], axis=0)
    lane = lax.broadcasted_iota(jnp.int32, (kk.shape[0], LANES), 1)
    lo = lane < HEAD_DIM
    zero = jnp.zeros((kk.shape[0], LANES), BF16)
    k_lo = [jnp.where(lo, kk[:, LANES * h:LANES * (h + 1)], zero) for h in range(A_KV_HEADS)]
    k_hi = [jnp.where(lo, zero, kk[:, LANES * h:LANES * (h + 1)]) for h in range(A_KV_HEADS)]
    v_lo = [jnp.where(lo, vv[:, LANES * h:LANES * (h + 1)], zero) for h in range(A_KV_HEADS)]
    v_hi = [jnp.where(lo, zero, vv[:, LANES * h:LANES * (h + 1)]) for h in range(A_KV_HEADS)]

    qi = lax.broadcasted_iota(jnp.int32, (BLOCK, 3 * BLOCK), 0)
    kj = lax.broadcasted_iota(jnp.int32, (BLOCK, 3 * BLOCK), 1)
    band = jnp.abs(kj - BLOCK - qi) <= WINDOW
    for j in range(tq // BLOCK):
        kpos = i * tq + (j - 1) * BLOCK + kj
        mask = band & (kpos >= 0) & (kpos < seq)
        r0, r1 = j * BLOCK, (j + 3) * BLOCK
        for c in range(A_Q_HEADS // 2):
            h = c // 2
            q2 = q_ref[j * BLOCK:(j + 1) * BLOCK, LANES * c:LANES * (c + 1)]
            out = None
            for half, (kx, vx) in enumerate(((k_lo[h], v_lo[h]), (k_hi[h], v_hi[h]))):
                s = lax.dot_general(q2, kx[r0:r1], _NT, preferred_element_type=F32)
                s = jnp.where(mask, s, NEG)
                snk = sink_ref[2 * c + half]
                m = jnp.maximum(jnp.max(s, axis=-1, keepdims=True), snk)
                p = jnp.exp(s - m)
                den = jnp.sum(p, axis=-1, keepdims=True) + jnp.exp(snk - m)
                p = p * (1.0 / den)
                pv = jnp.dot(p.astype(BF16), vx[r0:r1], preferred_element_type=F32)
                out = pv if out is None else out + pv
            o_ref[j * BLOCK:(j + 1) * BLOCK, LANES * c:LANES * (c + 1)] = out.astype(o_ref.dtype)


def _attn_a(qa, ka, va, sink, bsz, seq, tq):
    t = qa.shape[0]
    nq = seq // tq
    nb = seq // BLOCK
    r = tq // BLOCK
    cur = lambda b, i: (b * nq + i, 0)
    prev = lambda b, i: (b * nb + jnp.maximum(i * r - 1, 0), 0)
    nxt = lambda b, i: (b * nb + jnp.minimum((i + 1) * r, nb - 1), 0)
    kv_specs = [pl.BlockSpec((BLOCK, 256), prev), pl.BlockSpec((tq, 256), cur), pl.BlockSpec((BLOCK, 256), nxt)]
    return pl.pallas_call(
        functools.partial(_attn_a_kernel, seq=seq, tq=tq),
        grid=(bsz, nq),
        in_specs=[pl.BlockSpec(memory_space=pltpu.SMEM), pl.BlockSpec((tq, 512), cur)] + kv_specs + kv_specs,
        out_specs=pl.BlockSpec((tq, 512), cur),
        out_shape=jax.ShapeDtypeStruct((t, 512), BF16),
        compiler_params=_params(("arbitrary", "arbitrary")),
        name="attn_a",
    )(sink, qa, ka, ka, ka, va, va, va)


def _attn_b_kernel(lam_ref, g_ref, q_ref, k_ref, v_ref, o_ref, acc_ref, s_ref, *, seq, tk, lam_init):
    q = q_ref[...]
    tq = q.shape[0]
    lane = lax.broadcasted_iota(jnp.int32, (tk, LANES), 1)
    lo = lane < HEAD_DIM
    zero = jnp.zeros((tk, LANES), BF16)
    acc_ref[...] = jnp.zeros_like(acc_ref)
    ones_col = jnp.where(lane == 0, 1.0, 0.0).astype(BF16)

    def scores(kc):
        k = k_ref[pl.ds(pl.multiple_of(kc * tk, tk), tk), :]
        return [lax.dot_general(q, kx, _NT, preferred_element_type=F32)
                for kx in (jnp.where(lo, k, zero), jnp.where(lo, zero, k))]

    def consume(kc, maxes):
        v_aug = jnp.concatenate([v_ref[pl.ds(pl.multiple_of(kc * tk, tk), tk), :], ones_col], axis=1)
        new = []
        for idx in range(2):
            s = s_ref[idx]
            m_new = jnp.maximum(maxes[idx], jnp.max(s, axis=-1, keepdims=True))
            alpha = jnp.exp2(maxes[idx] - m_new)
            p = jnp.exp2(s - m_new)
            acc_ref[idx] = alpha * acc_ref[idx] + jnp.dot(p.astype(BF16), v_aug, preferred_element_type=F32)
            new.append(m_new)
        return tuple(new)

    def body(kc, maxes):
        nxt = scores(kc + 1)
        maxes = consume(kc, maxes)
        s_ref[0], s_ref[1] = nxt
        return maxes

    s_ref[0], s_ref[1] = scores(0)
    minf = jnp.full((tq, 1), -jnp.inf, F32)
    n_chunks = seq // tk
    consume(n_chunks - 1, lax.fori_loop(0, n_chunks - 1, body, (minf, minf)))

    lp = lam_ref[...]
    lam = (jnp.exp(jnp.sum(lp[0:1] * lp[1:2], axis=-1, keepdims=True))
           - jnp.exp(jnp.sum(lp[2:3] * lp[3:4], axis=-1, keepdims=True)) + lam_init)
    acc1, acc2 = acc_ref[0], acc_ref[1]
    l1, l2 = acc1[:, LANES:LANES + 1], acc2[:, LANES:LANES + 1]
    o = acc1[:, :LANES] * (1.0 / l1) - lam * (acc2[:, :LANES] * (1.0 / l2))
    o = o * lax.rsqrt(jnp.mean(o * o, axis=-1, keepdims=True) + RMS_EPS)
    o = o * g_ref[...] * (1.0 - lam_init)
    o_ref[...] = o.astype(o_ref.dtype)


def _attn_b(qd, kd, vd, lam_p, subln_g, bsz, seq, tq, tk, lam_init):
    t = qd.shape[0]
    nq = seq // tq
    return pl.pallas_call(
        functools.partial(_attn_b_kernel, seq=seq, tk=tk, lam_init=lam_init),
        grid=(bsz, B_HEADS, nq),
        in_specs=[pl.BlockSpec((4, HEAD_DIM), lambda b, h, i: (0, 0)),
                  pl.BlockSpec((1, LANES), lambda b, h, i: (0, 0)),
                  pl.BlockSpec((tq, LANES), lambda b, h, i: (b * nq + i, h)),
                  pl.BlockSpec((seq, LANES), lambda b, h, i: (b, h)),
                  pl.BlockSpec((seq, LANES), lambda b, h, i: (b, h))],
        out_specs=pl.BlockSpec((tq, LANES), lambda b, h, i: (b * nq + i, h)),
        out_shape=jax.ShapeDtypeStruct((t, 512), BF16),
        scratch_shapes=[pltpu.VMEM((2, tq, 2 * LANES), F32), pltpu.VMEM((2, tq, tk), F32)],
        compiler_params=_params(("arbitrary", "arbitrary", "arbitrary")),
        name="attn_b",
    )(lam_p, subln_g, qd, kd, vd)


HALF = D_MODEL // 2
_HI_MASK = -65536


def _pack_pair(lo, hi):
    lo_bits = lax.bitcast_convert_type(lo.astype(BF16).astype(F32), jnp.int32)
    hi_bits = lax.bitcast_convert_type(hi.astype(BF16).astype(F32), jnp.int32)
    return lax.shift_right_logical(lo_bits, 16) | hi_bits


def _unpack_pair(w):
    lo = lax.bitcast_convert_type(lax.shift_left(w, 16), F32)
    hi = lax.bitcast_convert_type(w & _HI_MASK, F32)
    return lo, hi


def _layer_norm(y, g, b):
    mu = jnp.mean(y, axis=-1, keepdims=True)
    d = y - mu
    var = jnp.mean(d * d, axis=-1, keepdims=True)
    return d * lax.rsqrt(var + LN_EPS) * g + b


def _first_argmax(vals, rowf, big):
    m = jnp.max(vals, axis=0, keepdims=True)
    idx = jnp.min(jnp.where(vals == m, rowf, big), axis=0, keepdims=True)
    return m, idx


def _post_attn_kernel(x_ref, oa_ref, ob_ref, ga_ref, gb_ref, woa_ref, wob_ref, wout_ref, g1_ref, b1_ref,
                      rwt_ref, rb_ref, wsg_ref, wsu_ref, wsd_ref, tri_ref, cnt_in_ref,
                      x1_ref, base_ref, eidx_ref, wts_ref, rank_ref, cnt_ref, cnt_scr):
    i = pl.program_id(0)
    tm = x_ref.shape[0]

    @pl.when(i == 0)
    def _():
        cnt_scr[...] = cnt_in_ref[...]

    a = jnp.dot(oa_ref[...], woa_ref[...], preferred_element_type=F32)
    b = jnp.dot(ob_ref[...], wob_ref[...], preferred_element_type=F32)
    merged = ga_ref[...].astype(F32) * a + gb_ref[...].astype(F32) * b
    mix = jnp.dot(merged.astype(BF16), wout_ref[...], preferred_element_type=F32)
    x1 = _layer_norm(DN_ALPHA * x_ref[...] + mix, g1_ref[...], b1_ref[...])
    x1_ref[...] = _pack_pair(x1[:, :HALF], x1[:, HALF:])
    x1b = x1.astype(BF16)

    hg = jnp.dot(x1b, wsg_ref[...], preferred_element_type=F32)
    hu = jnp.dot(x1b, wsu_ref[...], preferred_element_type=F32)
    hs = (hg * jax.nn.sigmoid(hg)) * hu
    shared = jnp.dot(hs.astype(BF16), wsd_ref[...], preferred_element_type=F32)
    base_ref[...] = DN_ALPHA * x1 + shared

    logits = lax.dot_general(rwt_ref[...], x1b, _NT, preferred_element_type=F32)
    scores = jax.nn.sigmoid(logits)
    choice = scores + rb_ref[...]
    ninf = -jnp.inf
    grow = lax.broadcasted_iota(jnp.int32, (GROUP_SIZE, tm), 0).astype(F32)
    gscore = []
    for g in range(N_GROUPS):
        blk = choice[GROUP_SIZE * g:GROUP_SIZE * (g + 1)]
        m1, i1 = _first_argmax(blk, grow, float(GROUP_SIZE))
        m2 = jnp.max(jnp.where(grow == i1, ninf, blk), axis=0, keepdims=True)
        gscore.append(m1 + m2)
    selected = [jnp.zeros((1, tm), F32) for _ in range(N_GROUPS)]
    work = list(gscore)
    for _ in range(TOPK_GROUPS):
        best = work[0]
        for g in range(1, N_GROUPS):
            best = jnp.maximum(best, work[g])
        taken = jnp.zeros((1, tm), F32)
        for g in range(N_GROUPS):
            hit = jnp.where((work[g] == best) & (taken == 0.0), 1.0, 0.0)
            taken = jnp.maximum(taken, hit)
            selected[g] = jnp.maximum(selected[g], hit)
            work[g] = jnp.where(hit > 0.0, ninf, work[g])
    masked = jnp.concatenate(
        [jnp.where(selected[g] > 0.0, choice[GROUP_SIZE * g:GROUP_SIZE * (g + 1)], ninf) for g in range(N_GROUPS)],
        axis=0)

    rowf = lax.broadcasted_iota(jnp.int32, (N_EXPERTS, tm), 0).astype(F32)
    hits, idxs, ws = [], [], []
    for _ in range(TOP_K):
        _, idx = _first_argmax(masked, rowf, float(N_EXPERTS))
        hit = rowf == idx
        hits.append(hit)
        idxs.append(idx)
        ws.append(jnp.sum(jnp.where(hit, scores, 0.0), axis=0, keepdims=True))
        masked = jnp.where(hit, ninf, masked)
    wsum = ws[0]
    for k in range(1, TOP_K):
        wsum = wsum + ws[k]

    member = hits[0]
    for k in range(1, TOP_K):
        member = member | hits[k]
    member_f = jnp.where(member, 1.0, 0.0)
    before = jnp.dot(member_f.astype(BF16), tri_ref[...], preferred_element_type=F32) + cnt_scr[...]
    for k in range(TOP_K):
        eidx_ref[k:k + 1, :] = idxs[k].astype(jnp.int32)
        wts_ref[k:k + 1, :] = ws[k] / wsum * ROUTED_SCALE
        rank_ref[k:k + 1, :] = jnp.sum(jnp.where(hits[k], before, 0.0), axis=0, keepdims=True).astype(jnp.int32)
    cnt_scr[...] = cnt_scr[...] + jnp.sum(member_f, axis=1, keepdims=True)
    cnt_ref[...] = jnp.broadcast_to(cnt_scr[...], cnt_ref.shape).astype(jnp.int32)


def _post_attn(x2, oa, ob, ga, gb, woa, wob, wout, g1, b1, rwt, rb, wsg, wsu, wsd, cnt_in, tm):
    t = x2.shape[0]
    tri = jnp.triu(jnp.ones((tm, tm), F32), k=1).astype(BF16)
    row = lambda i: (i, 0)
    col = lambda i: (0, i)
    full = lambda i: (0, 0)
    wspec = lambda arr: pl.BlockSpec(arr.shape, full)
    weights = (woa, wob, wout, g1, b1, rwt, rb, wsg, wsu, wsd, tri, cnt_in)
    return pl.pallas_call(
        _post_attn_kernel,
        grid=(t // tm,),
        in_specs=[pl.BlockSpec((tm, D_MODEL), row), pl.BlockSpec((tm, 512), row), pl.BlockSpec((tm, 512), row),
                  pl.BlockSpec((tm, D_MODEL), row), pl.BlockSpec((tm, D_MODEL), row)] + [wspec(w) for w in weights],
        out_specs=[pl.BlockSpec((tm, HALF), row), pl.BlockSpec((tm, D_MODEL), row),
                   pl.BlockSpec((TOP_K, tm), col), pl.BlockSpec((TOP_K, tm), col), pl.BlockSpec((TOP_K, tm), col),
                   pl.BlockSpec((N_EXPERTS, LANES), full)],
        out_shape=[jax.ShapeDtypeStruct((t, HALF), jnp.int32), jax.ShapeDtypeStruct((t, D_MODEL), F32),
                   jax.ShapeDtypeStruct((TOP_K, t), jnp.int32), jax.ShapeDtypeStruct((TOP_K, t), F32),
                   jax.ShapeDtypeStruct((TOP_K, t), jnp.int32), jax.ShapeDtypeStruct((N_EXPERTS, LANES), jnp.int32)],
        scratch_shapes=[pltpu.VMEM((N_EXPERTS, 1), F32)],
        compiler_params=_params(("arbitrary",)),
        name="post_attn",
    )(x2, oa, ob, ga, gb, *weights)


def _dest_kernel(eidx_ref, rank_ref, pstart_ref, dest_ref):
    tm = eidx_ref.shape[1]
    rows = lax.broadcasted_iota(jnp.int32, (N_EXPERTS, tm), 0)
    pstart = pstart_ref[...]
    for k in range(TOP_K):
        hit = rows == eidx_ref[k:k + 1, :]
        start = jnp.sum(jnp.where(hit, pstart, 0).astype(F32), axis=0, keepdims=True).astype(jnp.int32)
        dest_ref[0, k:k + 1, :] = start + rank_ref[k:k + 1, :]


def _dest(eidx, rank, pstart, tm):
    t = eidx.shape[1]
    return pl.pallas_call(
        _dest_kernel,
        grid=(t // tm,),
        in_specs=[pl.BlockSpec((TOP_K, tm), lambda i: (0, i)), pl.BlockSpec((TOP_K, tm), lambda i: (0, i)),
                  pl.BlockSpec((N_EXPERTS, 1), lambda i: (0, 0))],
        out_specs=pl.BlockSpec((1, TOP_K, tm), lambda i: (i, 0, 0)),
        out_shape=jax.ShapeDtypeStruct((t // tm, TOP_K, tm), jnp.int32),
        compiler_params=_params(("arbitrary",)),
        name="dest",
    )(eidx, rank, pstart)


def _row_copy(src_ref, src_row, dst_ref, dst_row, sem):
    return pltpu.make_async_copy(src_ref.at[pl.ds(src_row, 1), :], dst_ref.at[pl.ds(dst_row, 1), :], sem)


def _dispatch_kernel(pad_ref, dest_ref, *refs, tiles):
    n_groups = len(tiles)
    x_refs, xs_ref, zbuf, sem = refs[:n_groups], refs[n_groups], refs[n_groups + 1], refs[n_groups + 2]
    i = pl.program_id(0)
    n_rows = xs_ref.shape[0]
    tm = dest_ref.shape[2]

    def zero_copy(e):
        r0 = jnp.minimum(lax.shift_left(lax.shift_right_logical(pad_ref[e], 3), 3), n_rows - EXPERT_ROWS)
        return pltpu.make_async_copy(zbuf, xs_ref.at[pl.ds(pl.multiple_of(r0, SUBLANES), EXPERT_ROWS), :], sem)

    def zero_start(e, carry):
        zero_copy(e).start()
        return carry

    def zero_wait(e, carry):
        zero_copy(e).wait()
        return carry

    @pl.when(i == 0)
    def _():
        zbuf[...] = jnp.zeros_like(zbuf)
        lax.fori_loop(0, N_EXPERTS, zero_start, 0)
        lax.fori_loop(0, N_EXPERTS, zero_wait, 0)

    def scatter(x_ref):
        def issue(t, carry):
            for k in range(TOP_K):
                _row_copy(x_ref, t, xs_ref, dest_ref[0, k, t], sem).start(priority=k % 2)
            return carry

        def drain(t, carry):
            for k in range(TOP_K):
                _row_copy(x_ref, t, xs_ref, dest_ref[0, k, t], sem).wait()
            return carry

        lax.fori_loop(0, tm, issue, 0)
        lax.fori_loop(0, tm, drain, 0)

    first = 0
    for g in range(n_groups):
        pl.when((i >= first) & (i < first + tiles[g]))(functools.partial(scatter, x_refs[g]))
        first += tiles[g]


def _dispatch(pad_row0, dest3, x1ps, n_rows):
    tm = dest3.shape[2]
    tiles = tuple(x.shape[0] // tm for x in x1ps)
    firsts = [sum(tiles[:g]) for g in range(len(tiles))]

    def x_spec(g):
        return pl.BlockSpec((tm, HALF), lambda i: (jnp.clip(i - firsts[g], 0, tiles[g] - 1), 0))

    return pl.pallas_call(
        functools.partial(_dispatch_kernel, tiles=tiles),
        grid=(sum(tiles),),
        in_specs=[pl.BlockSpec(memory_space=pltpu.SMEM),
                  pl.BlockSpec((1, TOP_K, tm), lambda i: (i, 0, 0), memory_space=pltpu.SMEM)]
                 + [x_spec(g) for g in range(len(tiles))],
        out_specs=pl.BlockSpec(memory_space=pl.ANY),
        out_shape=jax.ShapeDtypeStruct((n_rows, HALF), jnp.int32),
        scratch_shapes=[pltpu.VMEM((EXPERT_ROWS, HALF), jnp.int32), pltpu.SemaphoreType.DMA],
        compiler_params=_params(("arbitrary",)),
        name="dispatch",
    )(pad_row0, dest3, *x1ps)


def _experts_kernel(cb_ref, xs_ref, wg_ref, wu_ref, wd_ref, ys_ref,
                    wg_scr, wu_scr, wd_scr, xbuf, ybuf, sem_in, sem_out):
    e = pl.program_id(0)
    g0, g1, total = cb_ref[e], cb_ref[e + 1], cb_ref[N_EXPERTS]

    def rows(g):
        return pl.ds(pl.multiple_of(g * EXPERT_ROWS, EXPERT_ROWS), EXPERT_ROWS)

    def x_copy(g):
        return pltpu.make_async_copy(xs_ref.at[rows(g), :], xbuf.at[g % 2], sem_in.at[g % 2])

    def y_copy(g):
        return pltpu.make_async_copy(ybuf.at[g % 2], ys_ref.at[rows(g), :], sem_out.at[g % 2])

    @pl.when((e == 0) & (total > 0))
    def _():
        x_copy(0).start()

    @pl.when(g1 > g0)
    def _():
        wg_scr[...] = wg_ref[0].astype(BF16)
        wu_scr[...] = wu_ref[0].astype(BF16)
        wd_scr[...] = wd_ref[0].astype(BF16)

    def block(g, carry):
        slot = g % 2

        @pl.when(g + 1 < total)
        def _():
            x_copy(g + 1).start()

        x_copy(g).wait()

        @pl.when(g >= 2)
        def _():
            y_copy(g - 2).wait()

        lo, hi = _unpack_pair(xbuf[slot])
        lo, hi = lo.astype(BF16), hi.astype(BF16)
        hg = (jnp.dot(lo, wg_scr[:HALF], preferred_element_type=F32)
              + jnp.dot(hi, wg_scr[HALF:], preferred_element_type=F32))
        hu = (jnp.dot(lo, wu_scr[:HALF], preferred_element_type=F32)
              + jnp.dot(hi, wu_scr[HALF:], preferred_element_type=F32))
        h = ((hg * jax.nn.sigmoid(hg)) * hu).astype(BF16)
        ybuf[slot] = _pack_pair(jnp.dot(h, wd_scr[:, :HALF], preferred_element_type=F32),
                                jnp.dot(h, wd_scr[:, HALF:], preferred_element_type=F32))
        y_copy(g).start()
        return carry

    lax.fori_loop(g0, g1, block, 0)

    @pl.when(e == N_EXPERTS - 1)
    def _():
        @pl.when(total >= 2)
        def _():
            y_copy(total - 2).wait()

        @pl.when(total >= 1)
        def _():
            y_copy(total - 1).wait()


def _experts(cum_blocks, xs, w_gate, w_up, w_down):
    n_rows = xs.shape[0]
    wmap = lambda e, cb: (e, 0, 0)
    grid_spec = pltpu.PrefetchScalarGridSpec(
        num_scalar_prefetch=1,
        grid=(N_EXPERTS,),
        in_specs=[pl.BlockSpec(memory_space=pl.ANY),
                  pl.BlockSpec((1, D_MODEL, D_EXPERT), wmap),
                  pl.BlockSpec((1, D_MODEL, D_EXPERT), wmap),
                  pl.BlockSpec((1, D_EXPERT, D_MODEL), wmap)],
        out_specs=pl.BlockSpec(memory_space=pl.ANY),
        scratch_shapes=[pltpu.VMEM((D_MODEL, D_EXPERT), BF16), pltpu.VMEM((D_MODEL, D_EXPERT), BF16),
                        pltpu.VMEM((D_EXPERT, D_MODEL), BF16),
                        pltpu.VMEM((2, EXPERT_ROWS, HALF), jnp.int32), pltpu.VMEM((2, EXPERT_ROWS, HALF), jnp.int32),
                        pltpu.SemaphoreType.DMA((2,)), pltpu.SemaphoreType.DMA((2,))])
    return pl.pallas_call(
        _experts_kernel,
        grid_spec=grid_spec,
        out_shape=jax.ShapeDtypeStruct((n_rows, HALF), jnp.int32),
        compiler_params=_params(("arbitrary",)),
        name="experts",
    )(cum_blocks, xs, w_gate, w_up, w_down)


def _combine_kernel(dest_ref, w_ref, base_ref, g2_ref, b2_ref, ys_ref, o_ref, buf, sem):
    tm = base_ref.shape[0]

    def issue(t, carry):
        for k in range(TOP_K):
            _row_copy(ys_ref, dest_ref[0, k, t], buf.at[k], t, sem).start(priority=k % 2)
        return carry

    lax.fori_loop(0, tm, issue, 0)

    def drain(t, carry):
        for k in range(TOP_K):
            _row_copy(ys_ref, dest_ref[0, k, t], buf.at[k], t, sem).wait()
        return carry

    lax.fori_loop(0, tm, drain, 0)

    w = w_ref[...]
    y_lo = base_ref[:, :HALF]
    y_hi = base_ref[:, HALF:]
    for k in range(TOP_K):
        lo, hi = _unpack_pair(buf[k])
        y_lo = y_lo + lo * w[:, k:k + 1]
        y_hi = y_hi + hi * w[:, k:k + 1]
    o_ref[...] = _layer_norm(jnp.concatenate([y_lo, y_hi], axis=1), g2_ref[...], b2_ref[...])


def _combine(dest3, wts_tk, base, g2, b2, ys):
    t = base.shape[0]
    tm = dest3.shape[2]
    return pl.pallas_call(
        _combine_kernel,
        grid=(t // tm,),
        in_specs=[pl.BlockSpec((1, TOP_K, tm), lambda i: (i, 0, 0), memory_space=pltpu.SMEM),
                  pl.BlockSpec((tm, TOP_K), lambda i: (i, 0)),
                  pl.BlockSpec((tm, D_MODEL), lambda i: (i, 0)),
                  pl.BlockSpec((1, D_MODEL), lambda i: (0, 0)),
                  pl.BlockSpec((1, D_MODEL), lambda i: (0, 0)),
                  pl.BlockSpec(memory_space=pl.ANY)],
        out_specs=pl.BlockSpec((tm, D_MODEL), lambda i: (i, 0)),
        out_shape=jax.ShapeDtypeStruct((t, D_MODEL), F32),
        scratch_shapes=[pltpu.VMEM((TOP_K, tm, HALF), jnp.int32), pltpu.SemaphoreType.DMA],
        compiler_params=_params(("arbitrary",)),
        name="combine",
    )(dest3, wts_tk, base, g2, b2, ys)


def _tile(n, pref):
    t = pref
    while n % t:
        t //= 2
    return t


def _token_mixers(x, layer_idx, wp, cnt_in):
    bsz, seq, _ = x.shape
    t = bsz * seq
    x2 = x.reshape(t, D_MODEL)
    lam_init = 0.8 - 0.6 * math.exp(-0.3 * layer_idx)

    cos_t, sin_t = _rope_tables(seq)
    qa, ka, va, qd, kd, vd, ga, gb = _in_proj(x2, wp["w_in"], cos_t, sin_t, seq, _tile(seq, 256))
    oa = _attn_a(qa, ka, va, wp["sink"], bsz, seq, _tile(seq, 512))
    ob = _attn_b(qd, kd, vd, wp["lam_p"], wp["subln_g"], bsz, seq, _tile(seq, 512), _tile(seq, 1024), lam_init)
    return _post_attn(
        x2, oa, ob, ga, gb, wp["w_o_a"], wp["w_o_b"], wp["w_out"], wp["ln1_g"], wp["ln1_b"],
        wp["router_wt"], wp["router_bias"], wp["ws_gate"], wp["ws_up"], wp["ws_down"], cnt_in, _tile(t, 512))


def _encoder_layer(xs_in, layer_idx, wp):
    cnt_in = jnp.zeros((N_EXPERTS, 1), F32)
    routed = []
    for x in xs_in:
        x1, base, eidx, wts, rank, cnt = _token_mixers(x, layer_idx, wp, cnt_in)
        cnt_in = cnt[:, :1].astype(F32)
        routed.append((x1, base, eidx, wts, rank))

    n_assign = sum(x.shape[0] * x.shape[1] for x in xs_in) * TOP_K
    n_blk = -(-(n_assign + N_EXPERTS * (EXPERT_ROWS - 1)) // EXPERT_ROWS)
    counts = cnt[:, 0]
    padded = ((counts + EXPERT_ROWS - 1) // EXPERT_ROWS) * EXPERT_ROWS
    pend = jnp.cumsum(padded)
    pstart = pend - padded
    cum_blocks = jnp.concatenate([jnp.zeros((1,), jnp.int32), (pend // EXPERT_ROWS).astype(jnp.int32)])
    pstart_col = pstart.astype(jnp.int32)[:, None]

    dests = [_dest(eidx, rank, pstart_col, 128) for _, _, eidx, _, rank in routed]
    xs = _dispatch((pstart + counts).astype(jnp.int32), jnp.concatenate(dests, axis=0),
                   [r[0] for r in routed], n_blk * EXPERT_ROWS)
    ys = _experts(cum_blocks, xs, wp["w_gate"], wp["w_up"], wp["w_down"])
    outs = []
    for x, dest3, (x1, base, eidx, wts, rank) in zip(xs_in, dests, routed):
        y = _combine(dest3, wts.T, base, wp["ln2_g"], wp["ln2_b"], ys)
        outs.append(y.reshape(x.shape))
    return outs


def kernel(x_prompt, x_sample, w_in, attn_sink, lambda_q1, lambda_k1, lambda_q2, lambda_k2, subln_g, w_o_a, w_o_b, w_out, ln1_g, ln1_b, router_w, router_bias, w_gate, w_up, w_down, ws_gate, ws_up, ws_down, ln2_g, ln2_b):
    y_prompt, y_sample = x_prompt, x_sample
    for l in range(DEPTH):
        wp = {
            "w_in": _prep_w_in(w_in[l]),
            "sink": attn_sink[l].astype(F32),
            "lam_p": jnp.stack([lambda_q1[l], lambda_k1[l], lambda_q2[l], lambda_k2[l]]).astype(F32),
            "subln_g": subln_g[l].astype(F32)[None, :],
            "w_o_a": w_o_a[l].astype(BF16), "w_o_b": w_o_b[l].astype(BF16), "w_out": w_out[l].astype(BF16),
            "ln1_g": ln1_g[l].astype(F32)[None, :], "ln1_b": ln1_b[l].astype(F32)[None, :],
            "router_wt": router_w[l].T.astype(BF16), "router_bias": router_bias[l].astype(F32)[:, None],
            "w_gate": w_gate[l], "w_up": w_up[l], "w_down": w_down[l],
            "ws_gate": ws_gate[l].astype(BF16), "ws_up": ws_up[l].astype(BF16), "ws_down": ws_down[l].astype(BF16),
            "ln2_g": ln2_g[l].astype(F32)[None, :], "ln2_b": ln2_b[l].astype(F32)[None, :],
        }
        y_prompt, y_sample = _encoder_layer((y_prompt, y_sample), l, wp)
    return (y_prompt, y_sample)
```

```python
import functools
import math

import jax
import jax.numpy as jnp
from jax import lax
from jax.experimental import pallas as pl
from jax.experimental.pallas import tpu as pltpu
from jax.experimental.pallas import tpu_sc as plsc

D_MODEL = 1024
HEAD_DIM = 64
ROPE_THETA = 10000.0
BLOCK = 128
A_Q_HEADS = 8
A_KV_HEADS = 2
WINDOW = 128
B_HEADS = 4
N_EXPERTS = 256
TOP_K = 8
N_GROUPS = 8
TOPK_GROUPS = 4
GROUP_SIZE = N_EXPERTS // N_GROUPS
D_EXPERT = 256
ROUTED_SCALE = 2.5
EXPERT_ROWS = 256
EXPERT_SLOTS = 4
DEPTH = 1
DN_ALPHA = (2 * DEPTH) ** 0.25
LN_EPS = 1e-5
RMS_EPS = 1e-5
NEG = -1e30
LOG2E = math.log2(math.e)

LANES = 128
SUBLANES = 8
SC_CORES = 2
SC_SUBCORES = 16
SC_WINDOW = 128
SC_SPLIT = 2
VMEM_LIMIT_BYTES = 56 * 1024 * 1024

F32 = jnp.float32
BF16 = jnp.bfloat16
_NT = (((1,), (1,)), ((), ()))


def _params(sem, vmem=VMEM_LIMIT_BYTES):
    return pltpu.CompilerParams(dimension_semantics=sem, vmem_limit_bytes=vmem)


_C_QA, _C_KA, _C_VA, _C_QD, _C_KD, _C_VD, _C_GA, _C_GB, _C_END = 0, 512, 768, 1024, 1536, 2048, 2560, 3584, 4608


def _prep_w_in(w_in):
    cuts = [0, 512, 640, 768, 1280, 1792, 2304, 3328, 4352]
    qa, ka, va, qd, kd, vd, ga, gb = [w_in[:, cuts[i]:cuts[i + 1]] for i in range(8)]
    dup = lambda w: jnp.concatenate([w[:, :64], w[:, :64], w[:, 64:], w[:, 64:]], axis=1)
    return jnp.concatenate([qa, dup(ka), dup(va), qd, kd, vd, ga, gb], axis=1).astype(BF16)


def _rope_tables(s):
    half = HEAD_DIM // 2
    inv = 1.0 / (ROPE_THETA ** (jnp.arange(half, dtype=F32) / half))
    ang = jnp.arange(s, dtype=F32)[:, None] * inv[None, :]
    cos, sin = jnp.cos(ang), jnp.sin(ang)
    return (jnp.concatenate([cos, cos, cos, cos], axis=1),
            jnp.concatenate([-sin, sin, -sin, sin], axis=1))


def _in_proj_kernel(x_ref, w_ref, cos_ref, sin_ref,
                    qa_ref, ka_ref, va_ref, qd_ref, kd_ref, vd_ref, ga_ref, gb_ref):
    xb = x_ref[...].astype(BF16)
    cos = cos_ref[...]
    sin = sin_ref[...]
    lane = lax.broadcasted_iota(jnp.int32, cos.shape, 1)
    first_half = (lane & (HEAD_DIM // 2)) == 0

    def proj(c0, c1):
        return jnp.dot(xb, w_ref[:, c0:c1], preferred_element_type=F32)

    def rope_store(u, out_ref, scale):
        for j in range(u.shape[1] // LANES):
            uj = u[:, LANES * j:LANES * (j + 1)]
            rot = jnp.where(first_half, pltpu.roll(uj, LANES - 32, 1), pltpu.roll(uj, 32, 1))
            r = uj * cos + rot * sin
            if scale != 1.0:
                r = r * scale
            out_ref[:, LANES * j:LANES * (j + 1)] = r.astype(out_ref.dtype)

    scale = HEAD_DIM ** -0.5
    rope_store(proj(_C_QA, _C_KA), qa_ref, scale)
    rope_store(proj(_C_KA, _C_VA), ka_ref, 1.0)
    va_ref[...] = proj(_C_VA, _C_QD).astype(va_ref.dtype)
    rope_store(proj(_C_QD, _C_KD), qd_ref, scale * LOG2E)
    rope_store(proj(_C_KD, _C_VD), kd_ref, 1.0)
    vd_ref[...] = proj(_C_VD, _C_GA).astype(vd_ref.dtype)
    ga_ref[...] = jax.nn.sigmoid(proj(_C_GA, _C_GB)).astype(ga_ref.dtype)
    gb_ref[...] = jax.nn.sigmoid(proj(_C_GB, _C_END)).astype(gb_ref.dtype)


def _in_proj(x2, w_perm, cos_t, sin_t, seq, tm):
    t = x2.shape[0]
    nseq = seq // tm
    row = lambda i: (i, 0)
    widths = (512, 256, 256, 512, 512, 512, 1024, 1024)
    return pl.pallas_call(
        _in_proj_kernel,
        grid=(t // tm,),
        in_specs=[pl.BlockSpec((tm, D_MODEL), row),
                  pl.BlockSpec((D_MODEL, _C_END), lambda i: (0, 0)),
                  pl.BlockSpec((tm, LANES), lambda i: (i % nseq, 0)),
                  pl.BlockSpec((tm, LANES), lambda i: (i % nseq, 0))],
        out_specs=[pl.BlockSpec((tm, w), row) for w in widths],
        out_shape=[jax.ShapeDtypeStruct((t, w), BF16) for w in widths],
        compiler_params=_params(("arbitrary",)),
        name="in_proj",
    )(x2, w_perm, cos_t, sin_t)


def _attn_a_kernel(sink_ref, q_ref, kp_ref, kc_ref, kn_ref, vp_ref, vc_ref, vn_ref, o_ref, *, seq, tq):
    i = pl.program_id(1)
    kk = jnp.concatenate([kp_ref[...], kc_ref[...], kn_ref[...]], axis=0)
    vv = jnp.concatenate([vp_ref[...], vc_ref[...], vn_ref[...]], axis=0)
    lane = lax.broadcasted_iota(jnp.int32, (kk.shape[0], LANES), 1)
    lo = lane < HEAD_DIM
    zero = jnp.zeros((kk.shape[0], LANES), BF16)
    k_lo = [jnp.where(lo, kk[:, LANES * h:LANES * (h + 1)], zero) for h in range(A_KV_HEADS)]
    k_hi = [jnp.where(lo, zero, kk[:, LANES * h:LANES * (h + 1)]) for h in range(A_KV_HEADS)]
    v_lo = [jnp.where(lo, vv[:, LANES * h:LANES * (h + 1)], zero) for h in range(A_KV_HEADS)]
    v_hi = [jnp.where(lo, zero, vv[:, LANES * h:LANES * (h + 1)]) for h in range(A_KV_HEADS)]

    qi = lax.broadcasted_iota(jnp.int32, (BLOCK, 3 * BLOCK), 0)
    kj = lax.broadcasted_iota(jnp.int32, (BLOCK, 3 * BLOCK), 1)
    band = jnp.abs(kj - BLOCK - qi) <= WINDOW
    for j in range(tq // BLOCK):
        kpos = i * tq + (j - 1) * BLOCK + kj
        mask = band & (kpos >= 0) & (kpos < seq)
        r0, r1 = j * BLOCK, (j + 3) * BLOCK
        for c in range(A_Q_HEADS // 2):
            h = c // 2
            q2 = q_ref[j * BLOCK:(j + 1) * BLOCK, LANES * c:LANES * (c + 1)]
            out = None
            for half, (kx, vx) in enumerate(((k_lo[h], v_lo[h]), (k_hi[h], v_hi[h]))):
                s = lax.dot_general(q2, kx[r0:r1], _NT, preferred_element_type=F32)
                s = jnp.where(mask, s, NEG)
                snk = sink_ref[2 * c + half]
                m = jnp.maximum(jnp.max(s, axis=-1, keepdims=True), snk)
                p = jnp.exp(s - m)
                den = jnp.sum(p, axis=-1, keepdims=True) + jnp.exp(snk - m)
                p = p * (1.0 / den)
                pv = jnp.dot(p.astype(BF16), vx[r0:r1], preferred_element_type=F32)
                out = pv if out is None else out + pv
            o_ref[j * BLOCK:(j + 1) * BLOCK, LANES * c:LANES * (c + 1)] = out.astype(o_ref.dtype)


def _attn_a(qa, ka, va, sink, bsz, seq, tq):
    t = qa.shape[0]
    nq = seq // tq
    nb = seq // BLOCK
    r = tq // BLOCK
    cur = lambda b, i: (b * nq + i, 0)
    prev = lambda b, i: (b * nb + jnp.maximum(i * r - 1, 0), 0)
    nxt = lambda b, i: (b * nb + jnp.minimum((i + 1) * r, nb - 1), 0)
    kv_specs = [pl.BlockSpec((BLOCK, 256), prev), pl.BlockSpec((tq, 256), cur), pl.BlockSpec((BLOCK, 256), nxt)]
    return pl.pallas_call(
        functools.partial(_attn_a_kernel, seq=seq, tq=tq),
        grid=(bsz, nq),
        in_specs=[pl.BlockSpec(memory_space=pltpu.SMEM), pl.BlockSpec((tq, 512), cur)] + kv_specs + kv_specs,
        out_specs=pl.BlockSpec((tq, 512), cur),
        out_shape=jax.ShapeDtypeStruct((t, 512), BF16),
        compiler_params=_params(("arbitrary", "arbitrary")),
        name="attn_a",
    )(sink, qa, ka, ka, ka, va, va, va)


def _attn_b_kernel(lam_ref, g_ref, q_ref, k_ref, v_ref, o_ref, acc_ref, s_ref, *, seq, tk, lam_init):
    q = q_ref[...]
    tq = q.shape[0]
    lane = lax.broadcasted_iota(jnp.int32, (tk, LANES), 1)
    lo = lane < HEAD_DIM
    zero = jnp.zeros((tk, LANES), BF16)
    acc_ref[...] = jnp.zeros_like(acc_ref)
    ones_col = jnp.where(lane == 0, 1.0, 0.0).astype(BF16)

    def scores(kc):
        k = k_ref[pl.ds(pl.multiple_of(kc * tk, tk), tk), :]
        return [lax.dot_general(q, kx, _NT, preferred_element_type=F32)
                for kx in (jnp.where(lo, k, zero), jnp.where(lo, zero, k))]

    def consume(kc, maxes):
        v_aug = jnp.concatenate([v_ref[pl.ds(pl.multiple_of(kc * tk, tk), tk), :], ones_col], axis=1)
        new = []
        for idx in range(2):
            s = s_ref[idx]
            m_new = jnp.maximum(maxes[idx], jnp.max(s, axis=-1, keepdims=True))
            alpha = jnp.exp2(maxes[idx] - m_new)
            p = jnp.exp2(s - m_new)
            acc_ref[idx] = alpha * acc_ref[idx] + jnp.dot(p.astype(BF16), v_aug, preferred_element_type=F32)
            new.append(m_new)
        return tuple(new)

    def body(kc, maxes):
        nxt = scores(kc + 1)
        maxes = consume(kc, maxes)
        s_ref[0], s_ref[1] = nxt
        return maxes

    s_ref[0], s_ref[1] = scores(0)
    minf = jnp.full((tq, 1), -jnp.inf, F32)
    n_chunks = seq // tk
    consume(n_chunks - 1, lax.fori_loop(0, n_chunks - 1, body, (minf, minf)))

    lp = lam_ref[...]
    lam = (jnp.exp(jnp.sum(lp[0:1] * lp[1:2], axis=-1, keepdims=True))
           - jnp.exp(jnp.sum(lp[2:3] * lp[3:4], axis=-1, keepdims=True)) + lam_init)
    acc1, acc2 = acc_ref[0], acc_ref[1]
    l1, l2 = acc1[:, LANES:LANES + 1], acc2[:, LANES:LANES + 1]
    o = acc1[:, :LANES] * (1.0 / l1) - lam * (acc2[:, :LANES] * (1.0 / l2))
    o = o * lax.rsqrt(jnp.mean(o * o, axis=-1, keepdims=True) + RMS_EPS)
    o = o * g_ref[...] * (1.0 - lam_init)
    o_ref[...] = o.astype(o_ref.dtype)


def _attn_b(qd, kd, vd, lam_p, subln_g, bsz, seq, tq, tk, lam_init):
    t = qd.shape[0]
    nq = seq // tq
    return pl.pallas_call(
        functools.partial(_attn_b_kernel, seq=seq, tk=tk, lam_init=lam_init),
        grid=(bsz, B_HEADS, nq),
        in_specs=[pl.BlockSpec((4, HEAD_DIM), lambda b, h, i: (0, 0)),
                  pl.BlockSpec((1, LANES), lambda b, h, i: (0, 0)),
                  pl.BlockSpec((tq, LANES), lambda b, h, i: (b * nq + i, h)),
                  pl.BlockSpec((seq, LANES), lambda b, h, i: (b, h)),
                  pl.BlockSpec((seq, LANES), lambda b, h, i: (b, h))],
        out_specs=pl.BlockSpec((tq, LANES), lambda b, h, i: (b * nq + i, h)),
        out_shape=jax.ShapeDtypeStruct((t, 512), BF16),
        scratch_shapes=[pltpu.VMEM((2, tq, 2 * LANES), F32), pltpu.VMEM((2, tq, tk), F32)],
        compiler_params=_params(("arbitrary", "arbitrary", "arbitrary")),
        name="attn_b",
    )(lam_p, subln_g, qd, kd, vd)


HALF = D_MODEL // 2
_HI_MASK = -65536


def _pack_pair(lo, hi):
    lo_bits = lax.bitcast_convert_type(lo.astype(BF16).astype(F32), jnp.int32)
    hi_bits = lax.bitcast_convert_type(hi.astype(BF16).astype(F32), jnp.int32)
    return lax.shift_right_logical(lo_bits, 16) | hi_bits


def _unpack_pair(w):
    lo = lax.bitcast_convert_type(lax.shift_left(w, 16), F32)
    hi = lax.bitcast_convert_type(w & _HI_MASK, F32)
    return lo, hi


def _layer_norm(y, g, b):
    mu = jnp.mean(y, axis=-1, keepdims=True)
    d = y - mu
    var = jnp.mean(d * d, axis=-1, keepdims=True)
    return d * lax.rsqrt(var + LN_EPS) * g + b


def _first_argmax(vals, rowf, big):
    m = jnp.max(vals, axis=0, keepdims=True)
    idx = jnp.min(jnp.where(vals == m, rowf, big), axis=0, keepdims=True)
    return m, idx


def _post_attn_kernel(x_ref, oa_ref, ob_ref, ga_ref, gb_ref, woa_ref, wob_ref, wout_ref, g1_ref, b1_ref,
                      rwt_ref, rb_ref, wsg_ref, wsu_ref, wsd_ref, tri_ref, cnt_in_ref,
                      x1_ref, base_ref, eidx_ref, wts_ref, rank_ref, cnt_ref, cnt_scr):
    i = pl.program_id(0)
    tm = x_ref.shape[0]

    @pl.when(i == 0)
    def _():
        cnt_scr[...] = cnt_in_ref[...]

    a = jnp.dot(oa_ref[...], woa_ref[...], preferred_element_type=F32)
    b = jnp.dot(ob_ref[...], wob_ref[...], preferred_element_type=F32)
    merged = ga_ref[...].astype(F32) * a + gb_ref[...].astype(F32) * b
    mix = jnp.dot(merged.astype(BF16), wout_ref[...], preferred_element_type=F32)
    x1 = _layer_norm(DN_ALPHA * x_ref[...] + mix, g1_ref[...], b1_ref[...])
    x1_ref[...] = _pack_pair(x1[:, :HALF], x1[:, HALF:])
    x1b = x1.astype(BF16)

    hg = jnp.dot(x1b, wsg_ref[...], preferred_element_type=F32)
    hu = jnp.dot(x1b, wsu_ref[...], preferred_element_type=F32)
    hs = (hg * jax.nn.sigmoid(hg)) * hu
    shared = jnp.dot(hs.astype(BF16), wsd_ref[...], preferred_element_type=F32)
    base_ref[...] = DN_ALPHA * x1 + shared

    logits = lax.dot_general(rwt_ref[...], x1b, _NT, preferred_element_type=F32)
    scores = jax.nn.sigmoid(logits)
    choice = scores + rb_ref[...]
    ninf = -jnp.inf
    grow = lax.broadcasted_iota(jnp.int32, (GROUP_SIZE, tm), 0).astype(F32)
    gscore = []
    for g in range(N_GROUPS):
        blk = choice[GROUP_SIZE * g:GROUP_SIZE * (g + 1)]
        m1, i1 = _first_argmax(blk, grow, float(GROUP_SIZE))
        m2 = jnp.max(jnp.where(grow == i1, ninf, blk), axis=0, keepdims=True)
        gscore.append(m1 + m2)
    selected = [jnp.zeros((1, tm), F32) for _ in range(N_GROUPS)]
    work = list(gscore)
    for _ in range(TOPK_GROUPS):
        best = work[0]
        for g in range(1, N_GROUPS):
            best = jnp.maximum(best, work[g])
        taken = jnp.zeros((1, tm), F32)
        for g in range(N_GROUPS):
            hit = jnp.where((work[g] == best) & (taken == 0.0), 1.0, 0.0)
            taken = jnp.maximum(taken, hit)
            selected[g] = jnp.maximum(selected[g], hit)
            work[g] = jnp.where(hit > 0.0, ninf, work[g])
    masked = jnp.concatenate(
        [jnp.where(selected[g] > 0.0, choice[GROUP_SIZE * g:GROUP_SIZE * (g + 1)], ninf) for g in range(N_GROUPS)],
        axis=0)

    rowf = lax.broadcasted_iota(jnp.int32, (N_EXPERTS, tm), 0).astype(F32)
    hits, idxs, ws = [], [], []
    for _ in range(TOP_K):
        _, idx = _first_argmax(masked, rowf, float(N_EXPERTS))
        hit = rowf == idx
        hits.append(hit)
        idxs.append(idx)
        ws.append(jnp.sum(jnp.where(hit, scores, 0.0), axis=0, keepdims=True))
        masked = jnp.where(hit, ninf, masked)
    wsum = ws[0]
    for k in range(1, TOP_K):
        wsum = wsum + ws[k]

    member = hits[0]
    for k in range(1, TOP_K):
        member = member | hits[k]
    member_f = jnp.where(member, 1.0, 0.0)
    before = jnp.dot(member_f.astype(BF16), tri_ref[...], preferred_element_type=F32) + cnt_scr[...]
    for k in range(TOP_K):
        eidx_ref[k:k + 1, :] = idxs[k].astype(jnp.int32)
        wts_ref[k:k + 1, :] = ws[k] / wsum * ROUTED_SCALE
        rank_ref[k:k + 1, :] = jnp.sum(jnp.where(hits[k], before, 0.0), axis=0, keepdims=True).astype(jnp.int32)
    cnt_scr[...] = cnt_scr[...] + jnp.sum(member_f, axis=1, keepdims=True)
    cnt_ref[...] = jnp.broadcast_to(cnt_scr[...], cnt_ref.shape).astype(jnp.int32)


def _post_attn(x2, oa, ob, ga, gb, woa, wob, wout, g1, b1, rwt, rb, wsg, wsu, wsd, cnt_in, tm):
    t = x2.shape[0]
    tri = jnp.triu(jnp.ones((tm, tm), F32), k=1).astype(BF16)
    row = lambda i: (i, 0)
    col = lambda i: (0, i)
    full = lambda i: (0, 0)
    wspec = lambda arr: pl.BlockSpec(arr.shape, full)
    weights = (woa, wob, wout, g1, b1, rwt, rb, wsg, wsu, wsd, tri, cnt_in)
    return pl.pallas_call(
        _post_attn_kernel,
        grid=(t // tm,),
        in_specs=[pl.BlockSpec((tm, D_MODEL), row), pl.BlockSpec((tm, 512), row), pl.BlockSpec((tm, 512), row),
                  pl.BlockSpec((tm, D_MODEL), row), pl.BlockSpec((tm, D_MODEL), row)] + [wspec(w) for w in weights],
        out_specs=[pl.BlockSpec((tm, HALF), row), pl.BlockSpec((tm, D_MODEL), row),
                   pl.BlockSpec((TOP_K, tm), col), pl.BlockSpec((TOP_K, tm), col), pl.BlockSpec((TOP_K, tm), col),
                   pl.BlockSpec((N_EXPERTS, LANES), full)],
        out_shape=[jax.ShapeDtypeStruct((t, HALF), jnp.int32), jax.ShapeDtypeStruct((t, D_MODEL), F32),
                   jax.ShapeDtypeStruct((TOP_K, t), jnp.int32), jax.ShapeDtypeStruct((TOP_K, t), F32),
                   jax.ShapeDtypeStruct((TOP_K, t), jnp.int32), jax.ShapeDtypeStruct((N_EXPERTS, LANES), jnp.int32)],
        scratch_shapes=[pltpu.VMEM((N_EXPERTS, 1), F32)],
        compiler_params=_params(("arbitrary",)),
        name="post_attn",
    )(x2, oa, ob, ga, gb, *weights)


def _dest_kernel(eidx_ref, rank_ref, pstart_ref, dest_ref, dest_kt_ref):
    tm = eidx_ref.shape[1]
    rows = lax.broadcasted_iota(jnp.int32, (N_EXPERTS, tm), 0)
    pstart = pstart_ref[...]
    for k in range(TOP_K):
        hit = rows == eidx_ref[k:k + 1, :]
        start = jnp.sum(jnp.where(hit, pstart, 0).astype(F32), axis=0, keepdims=True).astype(jnp.int32)
        dest = start + rank_ref[k:k + 1, :]
        dest_ref[0, k:k + 1, :] = dest
        dest_kt_ref[k:k + 1, :] = dest


def _dest(eidx, rank, pstart, tm):
    t = eidx.shape[1]
    return pl.pallas_call(
        _dest_kernel,
        grid=(t // tm,),
        in_specs=[pl.BlockSpec((TOP_K, tm), lambda i: (0, i)), pl.BlockSpec((TOP_K, tm), lambda i: (0, i)),
                  pl.BlockSpec((N_EXPERTS, 1), lambda i: (0, 0))],
        out_specs=[pl.BlockSpec((1, TOP_K, tm), lambda i: (i, 0, 0)), pl.BlockSpec((TOP_K, tm), lambda i: (0, i))],
        out_shape=[jax.ShapeDtypeStruct((t // tm, TOP_K, tm), jnp.int32),
                   jax.ShapeDtypeStruct((TOP_K, t), jnp.int32)],
        compiler_params=_params(("arbitrary",)),
        name="dest",
    )(eidx, rank, pstart)


def _row_copy(src_ref, src_row, dst_ref, dst_row, sem):
    return pltpu.make_async_copy(src_ref.at[pl.ds(src_row, 1), :], dst_ref.at[pl.ds(dst_row, 1), :], sem)


def _dispatch_kernel(pad_ref, dest_ref, *refs, tiles):
    n_groups = len(tiles)
    x_refs, xs_ref, zbuf, sem = refs[:n_groups], refs[n_groups], refs[n_groups + 1], refs[n_groups + 2]
    i = pl.program_id(0)
    n_rows = xs_ref.shape[0]
    tm = dest_ref.shape[2]

    def zero_copy(e):
        r0 = jnp.minimum(lax.shift_left(lax.shift_right_logical(pad_ref[e], 3), 3), n_rows - EXPERT_ROWS)
        return pltpu.make_async_copy(zbuf, xs_ref.at[pl.ds(pl.multiple_of(r0, SUBLANES), EXPERT_ROWS), :], sem)

    def zero_start(e, carry):
        zero_copy(e).start()
        return carry

    def zero_wait(e, carry):
        zero_copy(e).wait()
        return carry

    @pl.when(i == 0)
    def _():
        zbuf[...] = jnp.zeros_like(zbuf)
        lax.fori_loop(0, N_EXPERTS, zero_start, 0)
        lax.fori_loop(0, N_EXPERTS, zero_wait, 0)

    def scatter(x_ref):
        def issue(t, carry):
            for k in range(TOP_K):
                _row_copy(x_ref, t, xs_ref, dest_ref[0, k, t], sem).start(priority=k % 2)
            return carry

        def drain(t, carry):
            for k in range(TOP_K):
                _row_copy(x_ref, t, xs_ref, dest_ref[0, k, t], sem).wait()
            return carry

        lax.fori_loop(0, tm, issue, 0)
        lax.fori_loop(0, tm, drain, 0)

    first = 0
    for g in range(n_groups):
        pl.when((i >= first) & (i < first + tiles[g]))(functools.partial(scatter, x_refs[g]))
        first += tiles[g]


def _dispatch(pad_row0, dest3, x1ps, n_rows):
    tm = dest3.shape[2]
    tiles = tuple(x.shape[0] // tm for x in x1ps)
    firsts = [sum(tiles[:g]) for g in range(len(tiles))]

    def x_spec(g):
        return pl.BlockSpec((tm, HALF), lambda i: (jnp.clip(i - firsts[g], 0, tiles[g] - 1), 0))

    return pl.pallas_call(
        functools.partial(_dispatch_kernel, tiles=tiles),
        grid=(sum(tiles),),
        in_specs=[pl.BlockSpec(memory_space=pltpu.SMEM),
                  pl.BlockSpec((1, TOP_K, tm), lambda i: (i, 0, 0), memory_space=pltpu.SMEM)]
                 + [x_spec(g) for g in range(len(tiles))],
        out_specs=pl.BlockSpec(memory_space=pl.ANY),
        out_shape=jax.ShapeDtypeStruct((n_rows, HALF), jnp.int32),
        scratch_shapes=[pltpu.VMEM((EXPERT_ROWS, HALF), jnp.int32), pltpu.SemaphoreType.DMA],
        compiler_params=_params(("arbitrary",)),
        name="dispatch",
    )(pad_row0, dest3, *x1ps)


def _experts_kernel(cb_ref, xs_ref, wg_ref, wu_ref, wd_ref, ys_ref,
                    wg_scr, wu_scr, wd_scr, xbuf, ybuf, sem_in, sem_out):
    e = pl.program_id(0)
    g0, g1, total = cb_ref[e], cb_ref[e + 1], cb_ref[N_EXPERTS]

    def rows(g):
        return pl.ds(pl.multiple_of(g * EXPERT_ROWS, EXPERT_ROWS), EXPERT_ROWS)

    def x_copy(g):
        s = g % EXPERT_SLOTS
        return pltpu.make_async_copy(xs_ref.at[rows(g), :], xbuf.at[s], sem_in.at[s])

    def y_copy(g):
        s = g % EXPERT_SLOTS
        return pltpu.make_async_copy(ybuf.at[s], ys_ref.at[rows(g), :], sem_out.at[s])

    @pl.when(e == 0)
    def _():
        for g in range(EXPERT_SLOTS - 1):
            pl.when(g < total)(lambda g=g: x_copy(g).start())

    @pl.when(g1 > g0)
    def _():
        wg_scr[...] = wg_ref[0].astype(BF16)
        wu_scr[...] = wu_ref[0].astype(BF16)
        wd_scr[...] = wd_ref[0].astype(BF16)

    def block(g, carry):
        slot = g % EXPERT_SLOTS

        @pl.when(g + EXPERT_SLOTS - 1 < total)
        def _():
            x_copy(g + EXPERT_SLOTS - 1).start()

        x_copy(g).wait()

        @pl.when(g >= EXPERT_SLOTS)
        def _():
            y_copy(g - EXPERT_SLOTS).wait()

        lo, hi = _unpack_pair(xbuf[slot])
        lo, hi = lo.astype(BF16), hi.astype(BF16)
        hg = (jnp.dot(lo, wg_scr[:HALF], preferred_element_type=F32)
              + jnp.dot(hi, wg_scr[HALF:], preferred_element_type=F32))
        hu = (jnp.dot(lo, wu_scr[:HALF], preferred_element_type=F32)
              + jnp.dot(hi, wu_scr[HALF:], preferred_element_type=F32))
        h = ((hg * jax.nn.sigmoid(hg)) * hu).astype(BF16)
        ybuf[slot] = _pack_pair(jnp.dot(h, wd_scr[:, :HALF], preferred_element_type=F32),
                                jnp.dot(h, wd_scr[:, HALF:], preferred_element_type=F32))
        y_copy(g).start()
        return carry

    lax.fori_loop(g0, g1, block, 0)

    @pl.when(e == N_EXPERTS - 1)
    def _():
        for back in range(EXPERT_SLOTS, 0, -1):
            pl.when(total >= back)(lambda back=back: y_copy(total - back).wait())


def _experts(cum_blocks, xs, w_gate, w_up, w_down):
    n_rows = xs.shape[0]
    wmap = lambda e, cb: (e, 0, 0)
    grid_spec = pltpu.PrefetchScalarGridSpec(
        num_scalar_prefetch=1,
        grid=(N_EXPERTS,),
        in_specs=[pl.BlockSpec(memory_space=pl.ANY),
                  pl.BlockSpec((1, D_MODEL, D_EXPERT), wmap),
                  pl.BlockSpec((1, D_MODEL, D_EXPERT), wmap),
                  pl.BlockSpec((1, D_EXPERT, D_MODEL), wmap)],
        out_specs=pl.BlockSpec(memory_space=pl.ANY),
        scratch_shapes=[pltpu.VMEM((D_MODEL, D_EXPERT), BF16), pltpu.VMEM((D_MODEL, D_EXPERT), BF16),
                        pltpu.VMEM((D_EXPERT, D_MODEL), BF16),
                        pltpu.VMEM((EXPERT_SLOTS, EXPERT_ROWS, HALF), jnp.int32),
                        pltpu.VMEM((EXPERT_SLOTS, EXPERT_ROWS, HALF), jnp.int32),
                        pltpu.SemaphoreType.DMA((EXPERT_SLOTS,)), pltpu.SemaphoreType.DMA((EXPERT_SLOTS,))])
    return pl.pallas_call(
        _experts_kernel,
        grid_spec=grid_spec,
        out_shape=jax.ShapeDtypeStruct((n_rows, HALF), jnp.int32),
        compiler_params=_params(("arbitrary",)),
        name="experts",
    )(cum_blocks, xs, w_gate, w_up, w_down)


def _sc_gather_rows(table, idx):
    n_rows, d_row = table.shape
    table = table.reshape(n_rows * SC_SPLIT, d_row // SC_SPLIT)
    idx = (idx[:, None] * SC_SPLIT + jnp.arange(SC_SPLIT, dtype=idx.dtype)[None, :]).reshape(-1)
    n, d = idx.shape[0], table.shape[1]
    mesh = plsc.VectorSubcoreMesh(core_axis_name="core", subcore_axis_name="subcore",
                                  num_cores=SC_CORES, num_subcores=SC_SUBCORES)

    @functools.partial(pl.kernel, out_type=jax.ShapeDtypeStruct((n, d), table.dtype), mesh=mesh, scratch_types=[])
    def gather(table_hbm, idx_hbm, out_hbm):
        def window(idx_vmem, out_vmem):
            pltpu.sync_copy(table_hbm.at[idx_vmem.at[0]], out_vmem)

        pltpu.emit_pipeline(
            window,
            grid=(n // SC_WINDOW,),
            in_specs=[pl.BlockSpec((1, SC_WINDOW), lambda i: (0, i))],
            out_specs=[pl.BlockSpec((SC_WINDOW, d), lambda i: (i, 0))],
            core_axis_name=("core", "subcore"),
            dimension_semantics=(pltpu.PARALLEL,),
        )(idx_hbm, out_hbm)

    return gather(table, idx.reshape(1, n)).reshape(n // SC_SPLIT, d_row)


def _combine_kernel(w_ref, base_ref, g2_ref, b2_ref, rows_ref, o_ref):
    w = w_ref[...]
    y_lo = base_ref[:, :HALF]
    y_hi = base_ref[:, HALF:]
    for k in range(TOP_K):
        lo, hi = _unpack_pair(rows_ref[k])
        y_lo = y_lo + lo * w[:, k:k + 1]
        y_hi = y_hi + hi * w[:, k:k + 1]
    o_ref[...] = _layer_norm(jnp.concatenate([y_lo, y_hi], axis=1), g2_ref[...], b2_ref[...])


def _combine(wts_tk, base, g2, b2, rows, tm):
    t = base.shape[0]
    return pl.pallas_call(
        _combine_kernel,
        grid=(t // tm,),
        in_specs=[pl.BlockSpec((tm, TOP_K), lambda i: (i, 0)),
                  pl.BlockSpec((tm, D_MODEL), lambda i: (i, 0)),
                  pl.BlockSpec((1, D_MODEL), lambda i: (0, 0)),
                  pl.BlockSpec((1, D_MODEL), lambda i: (0, 0)),
                  pl.BlockSpec((TOP_K, tm, HALF), lambda i: (0, i, 0))],
        out_specs=pl.BlockSpec((tm, D_MODEL), lambda i: (i, 0)),
        out_shape=jax.ShapeDtypeStruct((t, D_MODEL), F32),
        compiler_params=_params(("arbitrary",)),
        name="combine",
    )(wts_tk, base, g2, b2, rows)


def _tile(n, pref):
    t = pref
    while n % t:
        t //= 2
    return t


def _token_mixers(x, layer_idx, wp, cnt_in):
    bsz, seq, _ = x.shape
    t = bsz * seq
    x2 = x.reshape(t, D_MODEL)
    lam_init = 0.8 - 0.6 * math.exp(-0.3 * layer_idx)

    cos_t, sin_t = _rope_tables(seq)
    qa, ka, va, qd, kd, vd, ga, gb = _in_proj(x2, wp["w_in"], cos_t, sin_t, seq, _tile(seq, 256))
    oa = _attn_a(qa, ka, va, wp["sink"], bsz, seq, _tile(seq, 512))
    ob = _attn_b(qd, kd, vd, wp["lam_p"], wp["subln_g"], bsz, seq, _tile(seq, 512), _tile(seq, 1024), lam_init)
    return _post_attn(
        x2, oa, ob, ga, gb, wp["w_o_a"], wp["w_o_b"], wp["w_out"], wp["ln1_g"], wp["ln1_b"],
        wp["router_wt"], wp["router_bias"], wp["ws_gate"], wp["ws_up"], wp["ws_down"], cnt_in, _tile(t, 512))


def _encoder_layer(xs_in, layer_idx, wp):
    cnt_in = jnp.zeros((N_EXPERTS, 1), F32)
    routed = []
    for x in xs_in:
        x1, base, eidx, wts, rank, cnt = _token_mixers(x, layer_idx, wp, cnt_in)
        cnt_in = cnt[:, :1].astype(F32)
        routed.append((x1, base, eidx, wts, rank))

    n_assign = sum(x.shape[0] * x.shape[1] for x in xs_in) * TOP_K
    n_blk = -(-(n_assign + N_EXPERTS * (EXPERT_ROWS - 1)) // EXPERT_ROWS)
    counts = cnt[:, 0]
    padded = ((counts + EXPERT_ROWS - 1) // EXPERT_ROWS) * EXPERT_ROWS
    pend = jnp.cumsum(padded)
    pstart = pend - padded
    cum_blocks = jnp.concatenate([jnp.zeros((1,), jnp.int32), (pend // EXPERT_ROWS).astype(jnp.int32)])
    pstart_col = pstart.astype(jnp.int32)[:, None]

    dests = [_dest(eidx, rank, pstart_col, 128) for _, _, eidx, _, rank in routed]
    xs = _dispatch((pstart + counts).astype(jnp.int32), jnp.concatenate([d[0] for d in dests], axis=0),
                   [r[0] for r in routed], n_blk * EXPERT_ROWS)
    ys = _experts(cum_blocks, xs, wp["w_gate"], wp["w_up"], wp["w_down"])
    outs = []
    for x, (_, dest_kt), (x1, base, eidx, wts, rank) in zip(xs_in, dests, routed):
        t = base.shape[0]
        rows = _sc_gather_rows(ys, dest_kt.reshape(-1)).reshape(TOP_K, t, HALF)
        y = _combine(wts.T, base, wp["ln2_g"], wp["ln2_b"], rows, _tile(t, 256))
        outs.append(y.reshape(x.shape))
    return outs


def kernel(x_prompt, x_sample, w_in, attn_sink, lambda_q1, lambda_k1, lambda_q2, lambda_k2, subln_g, w_o_a, w_o_b, w_out, ln1_g, ln1_b, router_w, router_bias, w_gate, w_up, w_down, ws_gate, ws_up, ws_down, ln2_g, ln2_b):
    y_prompt, y_sample = x_prompt, x_sample
    for l in range(DEPTH):
        wp = {
            "w_in": _prep_w_in(w_in[l]),
            "sink": attn_sink[l].astype(F32),
            "lam_p": jnp.stack([lambda_q1[l], lambda_k1[l], lambda_q2[l], lambda_k2[l]]).astype(F32),
            "subln_g": subln_g[l].astype(F32)[None, :],
            "w_o_a": w_o_a[l].astype(BF16), "w_o_b": w_o_b[l].astype(BF16), "w_out": w_out[l].astype(BF16),
            "ln1_g": ln1_g[l].astype(F32)[None, :], "ln1_b": ln1_b[l].astype(F32)[None, :],
            "router_wt": router_w[l].T.astype(BF16), "router_bias": router_bias[l].astype(F32)[:, None],
            "w_gate": w_gate[l], "w_up": w_up[l], "w_down": w_down[l],
            "ws_gate": ws_gate[l].astype(BF16), "ws_up": ws_up[l].astype(BF16), "ws_down": ws_down[l].astype(BF16),
            "ln2_g": ln2_g[l].astype(F32)[None, :], "ln2_b": ln2_b[l].astype(F32)[None, :],
        }
        y_prompt, y_sample = _encoder_layer((y_prompt, y_sample), l, wp)
    return (y_prompt, y_sample)
```

```python
import functools
import math

import jax
import jax.numpy as jnp
from jax import lax
from jax.experimental import pallas as pl
from jax.experimental.pallas import tpu as pltpu
from jax.experimental.pallas import tpu_sc as plsc

D_MODEL = 1024
HEAD_DIM = 64
ROPE_THETA = 10000.0
BLOCK = 128
A_Q_HEADS = 8
A_KV_HEADS = 2
WINDOW = 128
B_HEADS = 4
N_EXPERTS = 256
TOP_K = 8
N_GROUPS = 8
TOPK_GROUPS = 4
GROUP_SIZE = N_EXPERTS // N_GROUPS
D_EXPERT = 256
ROUTED_SCALE = 2.5
EXPERT_ROWS = 256
EXPERT_SLOTS = 4
DEPTH = 1
DN_ALPHA = (2 * DEPTH) ** 0.25
LN_EPS = 1e-5
RMS_EPS = 1e-5
NEG = -1e30
LOG2E = math.log2(math.e)

LANES = 128
SUBLANES = 8
SC_CORES = 2
SC_SUBCORES = 16
SC_WINDOW = 128
SC_SPLIT = 2
PIECE = D_MODEL // 2 // SC_SPLIT
VMEM_LIMIT_BYTES = 56 * 1024 * 1024

F32 = jnp.float32
BF16 = jnp.bfloat16
_NT = (((1,), (1,)), ((), ()))


def _params(sem, vmem=VMEM_LIMIT_BYTES):
    return pltpu.CompilerParams(dimension_semantics=sem, vmem_limit_bytes=vmem)


_C_QA, _C_KA, _C_VA, _C_QD, _C_KD, _C_VD, _C_GA, _C_GB, _C_END = 0, 512, 768, 1024, 1536, 2048, 2560, 3584, 4608


def _prep_w_in(w_in):
    cuts = [0, 512, 640, 768, 1280, 1792, 2304, 3328, 4352]
    qa, ka, va, qd, kd, vd, ga, gb = [w_in[:, cuts[i]:cuts[i + 1]] for i in range(8)]
    dup = lambda w: jnp.concatenate([w[:, :64], w[:, :64], w[:, 64:], w[:, 64:]], axis=1)
    return jnp.concatenate([qa, dup(ka), dup(va), qd, kd, vd, ga, gb], axis=1).astype(BF16)


def _rope_tables(s):
    half = HEAD_DIM // 2
    inv = 1.0 / (ROPE_THETA ** (jnp.arange(half, dtype=F32) / half))
    ang = jnp.arange(s, dtype=F32)[:, None] * inv[None, :]
    cos, sin = jnp.cos(ang), jnp.sin(ang)
    return (jnp.concatenate([cos, cos, cos, cos], axis=1),
            jnp.concatenate([-sin, sin, -sin, sin], axis=1))


def _in_proj_kernel(x_ref, w_ref, cos_ref, sin_ref,
                    qa_ref, ka_ref, va_ref, qd_ref, kd_ref, vd_ref, ga_ref, gb_ref):
    xb = x_ref[...].astype(BF16)
    cos = cos_ref[...]
    sin = sin_ref[...]
    lane = lax.broadcasted_iota(jnp.int32, cos.shape, 1)
    first_half = (lane & (HEAD_DIM // 2)) == 0

    def proj(c0, c1):
        return jnp.dot(xb, w_ref[:, c0:c1], preferred_element_type=F32)

    def rope_store(u, out_ref, scale):
        for j in range(u.shape[1] // LANES):
            uj = u[:, LANES * j:LANES * (j + 1)]
            rot = jnp.where(first_half, pltpu.roll(uj, LANES - 32, 1), pltpu.roll(uj, 32, 1))
            r = uj * cos + rot * sin
            if scale != 1.0:
                r = r * scale
            out_ref[:, LANES * j:LANES * (j + 1)] = r.astype(out_ref.dtype)

    scale = HEAD_DIM ** -0.5
    rope_store(proj(_C_QA, _C_KA), qa_ref, scale)
    rope_store(proj(_C_KA, _C_VA), ka_ref, 1.0)
    va_ref[...] = proj(_C_VA, _C_QD).astype(va_ref.dtype)
    rope_store(proj(_C_QD, _C_KD), qd_ref, scale * LOG2E)
    rope_store(proj(_C_KD, _C_VD), kd_ref, 1.0)
    vd_ref[...] = proj(_C_VD, _C_GA).astype(vd_ref.dtype)
    ga_ref[...] = jax.nn.sigmoid(proj(_C_GA, _C_GB)).astype(ga_ref.dtype)
    gb_ref[...] = jax.nn.sigmoid(proj(_C_GB, _C_END)).astype(gb_ref.dtype)


def _in_proj(x2, w_perm, cos_t, sin_t, seq, tm):
    t = x2.shape[0]
    nseq = seq // tm
    row = lambda i: (i, 0)
    widths = (512, 256, 256, 512, 512, 512, 1024, 1024)
    return pl.pallas_call(
        _in_proj_kernel,
        grid=(t // tm,),
        in_specs=[pl.BlockSpec((tm, D_MODEL), row),
                  pl.BlockSpec((D_MODEL, _C_END), lambda i: (0, 0)),
                  pl.BlockSpec((tm, LANES), lambda i: (i % nseq, 0)),
                  pl.BlockSpec((tm, LANES), lambda i: (i % nseq, 0))],
        out_specs=[pl.BlockSpec((tm, w), row) for w in widths],
        out_shape=[jax.ShapeDtypeStruct((t, w), BF16) for w in widths],
        compiler_params=_params(("arbitrary",)),
        name="in_proj",
    )(x2, w_perm, cos_t, sin_t)


def _attn_a_kernel(sink_ref, q_ref, kp_ref, kc_ref, kn_ref, vp_ref, vc_ref, vn_ref, o_ref, *, seq, tq):
    i = pl.program_id(1)
    kk = jnp.concatenate([kp_ref[...], kc_ref[...], kn_ref[...]], axis=0)
    vv = jnp.concatenate([vp_ref[...], vc_ref[...], vn_ref[...]], axis=0)
    lane = lax.broadcasted_iota(jnp.int32, (kk.shape[0], LANES), 1)
    lo = lane < HEAD_DIM
    zero = jnp.zeros((kk.shape[0], LANES), BF16)
    k_lo = [jnp.where(lo, kk[:, LANES * h:LANES * (h + 1)], zero) for h in range(A_KV_HEADS)]
    k_hi = [jnp.where(lo, zero, kk[:, LANES * h:LANES * (h + 1)]) for h in range(A_KV_HEADS)]
    v_lo = [jnp.where(lo, vv[:, LANES * h:LANES * (h + 1)], zero) for h in range(A_KV_HEADS)]
    v_hi = [jnp.where(lo, zero, vv[:, LANES * h:LANES * (h + 1)]) for h in range(A_KV_HEADS)]

    qi = lax.broadcasted_iota(jnp.int32, (BLOCK, 3 * BLOCK), 0)
    kj = lax.broadcasted_iota(jnp.int32, (BLOCK, 3 * BLOCK), 1)
    band = jnp.abs(kj - BLOCK - qi) <= WINDOW
    for j in range(tq // BLOCK):
        kpos = i * tq + (j - 1) * BLOCK + kj
        mask = band & (kpos >= 0) & (kpos < seq)
        r0, r1 = j * BLOCK, (j + 3) * BLOCK
        for c in range(A_Q_HEADS // 2):
            h = c // 2
            q2 = q_ref[j * BLOCK:(j + 1) * BLOCK, LANES * c:LANES * (c + 1)]
            out = None
            for half, (kx, vx) in enumerate(((k_lo[h], v_lo[h]), (k_hi[h], v_hi[h]))):
                s = lax.dot_general(q2, kx[r0:r1], _NT, preferred_element_type=F32)
                s = jnp.where(mask, s, NEG)
                snk = sink_ref[2 * c + half]
                m = jnp.maximum(jnp.max(s, axis=-1, keepdims=True), snk)
                p = jnp.exp(s - m)
                den = jnp.sum(p, axis=-1, keepdims=True) + jnp.exp(snk - m)
                p = p * (1.0 / den)
                pv = jnp.dot(p.astype(BF16), vx[r0:r1], preferred_element_type=F32)
                out = pv if out is None else out + pv
            o_ref[j * BLOCK:(j + 1) * BLOCK, LANES * c:LANES * (c + 1)] = out.astype(o_ref.dtype)


def _attn_a(qa, ka, va, sink, bsz, seq, tq):
    t = qa.shape[0]
    nq = seq // tq
    nb = seq // BLOCK
    r = tq // BLOCK
    cur = lambda b, i: (b * nq + i, 0)
    prev = lambda b, i: (b * nb + jnp.maximum(i * r - 1, 0), 0)
    nxt = lambda b, i: (b * nb + jnp.minimum((i + 1) * r, nb - 1), 0)
    kv_specs = [pl.BlockSpec((BLOCK, 256), prev), pl.BlockSpec((tq, 256), cur), pl.BlockSpec((BLOCK, 256), nxt)]
    return pl.pallas_call(
        functools.partial(_attn_a_kernel, seq=seq, tq=tq),
        grid=(bsz, nq),
        in_specs=[pl.BlockSpec(memory_space=pltpu.SMEM), pl.BlockSpec((tq, 512), cur)] + kv_specs + kv_specs,
        out_specs=pl.BlockSpec((tq, 512), cur),
        out_shape=jax.ShapeDtypeStruct((t, 512), BF16),
        compiler_params=_params(("arbitrary", "arbitrary")),
        name="attn_a",
    )(sink, qa, ka, ka, ka, va, va, va)


def _attn_b_kernel(lam_ref, g_ref, q_ref, k_ref, v_ref, o_ref, acc_ref, s_ref, *, seq, tk, lam_init):
    q = q_ref[...]
    tq = q.shape[0]
    lane = lax.broadcasted_iota(jnp.int32, (tk, LANES), 1)
    lo = lane < HEAD_DIM
    zero = jnp.zeros((tk, LANES), BF16)
    acc_ref[...] = jnp.zeros_like(acc_ref)
    ones_col = jnp.where(lane == 0, 1.0, 0.0).astype(BF16)

    def scores(kc):
        k = k_ref[pl.ds(pl.multiple_of(kc * tk, tk), tk), :]
        return [lax.dot_general(q, kx, _NT, preferred_element_type=F32)
                for kx in (jnp.where(lo, k, zero), jnp.where(lo, zero, k))]

    def consume(kc, maxes):
        v_aug = jnp.concatenate([v_ref[pl.ds(pl.multiple_of(kc * tk, tk), tk), :], ones_col], axis=1)
        new = []
        for idx in range(2):
            s = s_ref[idx]
            m_new = jnp.maximum(maxes[idx], jnp.max(s, axis=-1, keepdims=True))
            alpha = jnp.exp2(maxes[idx] - m_new)
            p = jnp.exp2(s - m_new)
            acc_ref[idx] = alpha * acc_ref[idx] + jnp.dot(p.astype(BF16), v_aug, preferred_element_type=F32)
            new.append(m_new)
        return tuple(new)

    def body(kc, maxes):
        nxt = scores(kc + 1)
        maxes = consume(kc, maxes)
        s_ref[0], s_ref[1] = nxt
        return maxes

    s_ref[0], s_ref[1] = scores(0)
    minf = jnp.full((tq, 1), -jnp.inf, F32)
    n_chunks = seq // tk
    consume(n_chunks - 1, lax.fori_loop(0, n_chunks - 1, body, (minf, minf)))

    lp = lam_ref[...]
    lam = (jnp.exp(jnp.sum(lp[0:1] * lp[1:2], axis=-1, keepdims=True))
           - jnp.exp(jnp.sum(lp[2:3] * lp[3:4], axis=-1, keepdims=True)) + lam_init)
    acc1, acc2 = acc_ref[0], acc_ref[1]
    l1, l2 = acc1[:, LANES:LANES + 1], acc2[:, LANES:LANES + 1]
    o = acc1[:, :LANES] * (1.0 / l1) - lam * (acc2[:, :LANES] * (1.0 / l2))
    o = o * lax.rsqrt(jnp.mean(o * o, axis=-1, keepdims=True) + RMS_EPS)
    o = o * g_ref[...] * (1.0 - lam_init)
    o_ref[...] = o.astype(o_ref.dtype)


def _attn_b(qd, kd, vd, lam_p, subln_g, bsz, seq, tq, tk, lam_init):
    t = qd.shape[0]
    nq = seq // tq
    return pl.pallas_call(
        functools.partial(_attn_b_kernel, seq=seq, tk=tk, lam_init=lam_init),
        grid=(bsz, B_HEADS, nq),
        in_specs=[pl.BlockSpec((4, HEAD_DIM), lambda b, h, i: (0, 0)),
                  pl.BlockSpec((1, LANES), lambda b, h, i: (0, 0)),
                  pl.BlockSpec((tq, LANES), lambda b, h, i: (b * nq + i, h)),
                  pl.BlockSpec((seq, LANES), lambda b, h, i: (b, h)),
                  pl.BlockSpec((seq, LANES), lambda b, h, i: (b, h))],
        out_specs=pl.BlockSpec((tq, LANES), lambda b, h, i: (b * nq + i, h)),
        out_shape=jax.ShapeDtypeStruct((t, 512), BF16),
        scratch_shapes=[pltpu.VMEM((2, tq, 2 * LANES), F32), pltpu.VMEM((2, tq, tk), F32)],
        compiler_params=_params(("arbitrary", "arbitrary", "arbitrary")),
        name="attn_b",
    )(lam_p, subln_g, qd, kd, vd)


HALF = D_MODEL // 2
_HI_MASK = -65536


def _pack_pair(lo, hi):
    lo_bits = lax.bitcast_convert_type(lo.astype(BF16).astype(F32), jnp.int32)
    hi_bits = lax.bitcast_convert_type(hi.astype(BF16).astype(F32), jnp.int32)
    return lax.shift_right_logical(lo_bits, 16) | hi_bits


def _unpack_pair(w):
    lo = lax.bitcast_convert_type(lax.shift_left(w, 16), F32)
    hi = lax.bitcast_convert_type(w & _HI_MASK, F32)
    return lo, hi


def _layer_norm(y, g, b):
    mu = jnp.mean(y, axis=-1, keepdims=True)
    d = y - mu
    var = jnp.mean(d * d, axis=-1, keepdims=True)
    return d * lax.rsqrt(var + LN_EPS) * g + b


def _first_argmax(vals, rowf, big):
    m = jnp.max(vals, axis=0, keepdims=True)
    idx = jnp.min(jnp.where(vals == m, rowf, big), axis=0, keepdims=True)
    return m, idx


def _post_attn_kernel(x_ref, oa_ref, ob_ref, ga_ref, gb_ref, woa_ref, wob_ref, wout_ref, g1_ref, b1_ref,
                      rwt_ref, rb_ref, wsg_ref, wsu_ref, wsd_ref, tri_ref, cnt_in_ref,
                      x1_ref, base_ref, eidx_ref, wts_ref, rank_ref, cnt_ref, cnt_scr):
    i = pl.program_id(0)
    tm = x_ref.shape[0]

    @pl.when(i == 0)
    def _():
        cnt_scr[...] = cnt_in_ref[...]

    a = jnp.dot(oa_ref[...], woa_ref[...], preferred_element_type=F32)
    b = jnp.dot(ob_ref[...], wob_ref[...], preferred_element_type=F32)
    merged = ga_ref[...].astype(F32) * a + gb_ref[...].astype(F32) * b
    mix = jnp.dot(merged.astype(BF16), wout_ref[...], preferred_element_type=F32)
    x1 = _layer_norm(DN_ALPHA * x_ref[...] + mix, g1_ref[...], b1_ref[...])
    x1_ref[...] = _pack_pair(x1[:, :HALF], x1[:, HALF:])
    x1b = x1.astype(BF16)

    hg = jnp.dot(x1b, wsg_ref[...], preferred_element_type=F32)
    hu = jnp.dot(x1b, wsu_ref[...], preferred_element_type=F32)
    hs = (hg * jax.nn.sigmoid(hg)) * hu
    shared = jnp.dot(hs.astype(BF16), wsd_ref[...], preferred_element_type=F32)
    base_ref[...] = DN_ALPHA * x1 + shared

    logits = lax.dot_general(rwt_ref[...], x1b, _NT, preferred_element_type=F32)
    scores = jax.nn.sigmoid(logits)
    choice = scores + rb_ref[...]
    ninf = -jnp.inf
    grow = lax.broadcasted_iota(jnp.int32, (GROUP_SIZE, tm), 0).astype(F32)
    gscore = []
    for g in range(N_GROUPS):
        blk = choice[GROUP_SIZE * g:GROUP_SIZE * (g + 1)]
        m1, i1 = _first_argmax(blk, grow, float(GROUP_SIZE))
        m2 = jnp.max(jnp.where(grow == i1, ninf, blk), axis=0, keepdims=True)
        gscore.append(m1 + m2)
    selected = [jnp.zeros((1, tm), F32) for _ in range(N_GROUPS)]
    work = list(gscore)
    for _ in range(TOPK_GROUPS):
        best = work[0]
        for g in range(1, N_GROUPS):
            best = jnp.maximum(best, work[g])
        taken = jnp.zeros((1, tm), F32)
        for g in range(N_GROUPS):
            hit = jnp.where((work[g] == best) & (taken == 0.0), 1.0, 0.0)
            taken = jnp.maximum(taken, hit)
            selected[g] = jnp.maximum(selected[g], hit)
            work[g] = jnp.where(hit > 0.0, ninf, work[g])
    masked = jnp.concatenate(
        [jnp.where(selected[g] > 0.0, choice[GROUP_SIZE * g:GROUP_SIZE * (g + 1)], ninf) for g in range(N_GROUPS)],
        axis=0)

    rowf = lax.broadcasted_iota(jnp.int32, (N_EXPERTS, tm), 0).astype(F32)
    hits, idxs, ws = [], [], []
    for _ in range(TOP_K):
        _, idx = _first_argmax(masked, rowf, float(N_EXPERTS))
        hit = rowf == idx
        hits.append(hit)
        idxs.append(idx)
        ws.append(jnp.sum(jnp.where(hit, scores, 0.0), axis=0, keepdims=True))
        masked = jnp.where(hit, ninf, masked)
    wsum = ws[0]
    for k in range(1, TOP_K):
        wsum = wsum + ws[k]

    member = hits[0]
    for k in range(1, TOP_K):
        member = member | hits[k]
    member_f = jnp.where(member, 1.0, 0.0)
    before = jnp.dot(member_f.astype(BF16), tri_ref[...], preferred_element_type=F32) + cnt_scr[...]
    for k in range(TOP_K):
        eidx_ref[k:k + 1, :] = idxs[k].astype(jnp.int32)
        wts_ref[k:k + 1, :] = ws[k] / wsum * ROUTED_SCALE
        rank_ref[k:k + 1, :] = jnp.sum(jnp.where(hits[k], before, 0.0), axis=0, keepdims=True).astype(jnp.int32)
    cnt_scr[...] = cnt_scr[...] + jnp.sum(member_f, axis=1, keepdims=True)
    cnt_ref[...] = jnp.broadcast_to(cnt_scr[...], cnt_ref.shape).astype(jnp.int32)


def _post_attn(x2, oa, ob, ga, gb, woa, wob, wout, g1, b1, rwt, rb, wsg, wsu, wsd, cnt_in, tm):
    t = x2.shape[0]
    tri = jnp.triu(jnp.ones((tm, tm), F32), k=1).astype(BF16)
    row = lambda i: (i, 0)
    col = lambda i: (0, i)
    full = lambda i: (0, 0)
    wspec = lambda arr: pl.BlockSpec(arr.shape, full)
    weights = (woa, wob, wout, g1, b1, rwt, rb, wsg, wsu, wsd, tri, cnt_in)
    return pl.pallas_call(
        _post_attn_kernel,
        grid=(t // tm,),
        in_specs=[pl.BlockSpec((tm, D_MODEL), row), pl.BlockSpec((tm, 512), row), pl.BlockSpec((tm, 512), row),
                  pl.BlockSpec((tm, D_MODEL), row), pl.BlockSpec((tm, D_MODEL), row)] + [wspec(w) for w in weights],
        out_specs=[pl.BlockSpec((tm, HALF), row), pl.BlockSpec((tm, D_MODEL), row),
                   pl.BlockSpec((TOP_K, tm), col), pl.BlockSpec((TOP_K, tm), col), pl.BlockSpec((TOP_K, tm), col),
                   pl.BlockSpec((N_EXPERTS, LANES), full)],
        out_shape=[jax.ShapeDtypeStruct((t, HALF), jnp.int32), jax.ShapeDtypeStruct((t, D_MODEL), F32),
                   jax.ShapeDtypeStruct((TOP_K, t), jnp.int32), jax.ShapeDtypeStruct((TOP_K, t), F32),
                   jax.ShapeDtypeStruct((TOP_K, t), jnp.int32), jax.ShapeDtypeStruct((N_EXPERTS, LANES), jnp.int32)],
        scratch_shapes=[pltpu.VMEM((N_EXPERTS, 1), F32)],
        compiler_params=_params(("arbitrary",)),
        name="post_attn",
    )(x2, oa, ob, ga, gb, *weights)


def _dest_kernel(eidx_ref, rank_ref, pstart_ref, dest_ref, dest_kt_ref):
    tm = eidx_ref.shape[1]
    rows = lax.broadcasted_iota(jnp.int32, (N_EXPERTS, tm), 0)
    pstart = pstart_ref[...]
    for k in range(TOP_K):
        hit = rows == eidx_ref[k:k + 1, :]
        start = jnp.sum(jnp.where(hit, pstart, 0).astype(F32), axis=0, keepdims=True).astype(jnp.int32)
        dest = start + rank_ref[k:k + 1, :]
        dest_ref[0, k:k + 1, :] = dest
        dest_kt_ref[k:k + 1, :] = dest


def _dest(eidx, rank, pstart, tm):
    t = eidx.shape[1]
    return pl.pallas_call(
        _dest_kernel,
        grid=(t // tm,),
        in_specs=[pl.BlockSpec((TOP_K, tm), lambda i: (0, i)), pl.BlockSpec((TOP_K, tm), lambda i: (0, i)),
                  pl.BlockSpec((N_EXPERTS, 1), lambda i: (0, 0))],
        out_specs=[pl.BlockSpec((1, TOP_K, tm), lambda i: (i, 0, 0)), pl.BlockSpec((TOP_K, tm), lambda i: (0, i))],
        out_shape=[jax.ShapeDtypeStruct((t // tm, TOP_K, tm), jnp.int32),
                   jax.ShapeDtypeStruct((TOP_K, t), jnp.int32)],
        compiler_params=_params(("arbitrary",)),
        name="dest",
    )(eidx, rank, pstart)


def _row_copy(src_ref, src_row, dst_ref, dst_row, sem):
    return pltpu.make_async_copy(src_ref.at[pl.ds(src_row, 1), :], dst_ref.at[pl.ds(dst_row, 1), :], sem)


def _dispatch_kernel(pad_ref, dest_ref, *refs, tiles):
    n_groups = len(tiles)
    x_refs, xs_ref, zbuf, sem = refs[:n_groups], refs[n_groups], refs[n_groups + 1], refs[n_groups + 2]
    i = pl.program_id(0)
    n_rows = xs_ref.shape[0]
    tm = dest_ref.shape[2]

    def zero_copy(e):
        r0 = jnp.minimum(lax.shift_left(lax.shift_right_logical(pad_ref[e], 3), 3), n_rows - EXPERT_ROWS)
        return pltpu.make_async_copy(zbuf, xs_ref.at[pl.ds(pl.multiple_of(r0, SUBLANES), EXPERT_ROWS), :], sem)

    def zero_start(e, carry):
        zero_copy(e).start()
        return carry

    def zero_wait(e, carry):
        zero_copy(e).wait()
        return carry

    @pl.when(i == 0)
    def _():
        zbuf[...] = jnp.zeros_like(zbuf)
        lax.fori_loop(0, N_EXPERTS, zero_start, 0)
        lax.fori_loop(0, N_EXPERTS, zero_wait, 0)

    def scatter(x_ref):
        def issue(t, carry):
            for k in range(TOP_K):
                _row_copy(x_ref, t, xs_ref, dest_ref[0, k, t], sem).start(priority=k % 2)
            return carry

        def drain(t, carry):
            for k in range(TOP_K):
                _row_copy(x_ref, t, xs_ref, dest_ref[0, k, t], sem).wait()
            return carry

        lax.fori_loop(0, tm, issue, 0)
        lax.fori_loop(0, tm, drain, 0)

    first = 0
    for g in range(n_groups):
        pl.when((i >= first) & (i < first + tiles[g]))(functools.partial(scatter, x_refs[g]))
        first += tiles[g]


def _dispatch(pad_row0, dest3, x1ps, n_rows):
    tm = dest3.shape[2]
    tiles = tuple(x.shape[0] // tm for x in x1ps)
    firsts = [sum(tiles[:g]) for g in range(len(tiles))]

    def x_spec(g):
        return pl.BlockSpec((tm, HALF), lambda i: (jnp.clip(i - firsts[g], 0, tiles[g] - 1), 0))

    return pl.pallas_call(
        functools.partial(_dispatch_kernel, tiles=tiles),
        grid=(sum(tiles),),
        in_specs=[pl.BlockSpec(memory_space=pltpu.SMEM),
                  pl.BlockSpec((1, TOP_K, tm), lambda i: (i, 0, 0), memory_space=pltpu.SMEM)]
                 + [x_spec(g) for g in range(len(tiles))],
        out_specs=pl.BlockSpec(memory_space=pl.ANY),
        out_shape=jax.ShapeDtypeStruct((n_rows, HALF), jnp.int32),
        scratch_shapes=[pltpu.VMEM((EXPERT_ROWS, HALF), jnp.int32), pltpu.SemaphoreType.DMA],
        compiler_params=_params(("arbitrary",)),
        name="dispatch",
    )(pad_row0, dest3, *x1ps)


def _experts_kernel(cb_ref, xs_ref, wg_ref, wu_ref, wd_ref, ys_ref,
                    wg_scr, wu_scr, wd_scr, xbuf, ybuf, sem_in, sem_out):
    e = pl.program_id(0)
    g0, g1, total = cb_ref[e], cb_ref[e + 1], cb_ref[N_EXPERTS]

    def rows(g):
        return pl.ds(pl.multiple_of(g * EXPERT_ROWS, EXPERT_ROWS), EXPERT_ROWS)

    def x_copy(g):
        s = g % EXPERT_SLOTS
        return pltpu.make_async_copy(xs_ref.at[rows(g), :], xbuf.at[s], sem_in.at[s])

    def y_copy(g):
        s = g % EXPERT_SLOTS
        return pltpu.make_async_copy(ybuf.at[s], ys_ref.at[:, rows(g), :], sem_out.at[s])

    @pl.when(e == 0)
    def _():
        for g in range(EXPERT_SLOTS - 1):
            pl.when(g < total)(lambda g=g: x_copy(g).start())

    @pl.when(g1 > g0)
    def _():
        wg_scr[...] = wg_ref[0].astype(BF16)
        wu_scr[...] = wu_ref[0].astype(BF16)
        wd_scr[...] = wd_ref[0].astype(BF16)

    def block(g, carry):
        slot = g % EXPERT_SLOTS

        @pl.when(g + EXPERT_SLOTS - 1 < total)
        def _():
            x_copy(g + EXPERT_SLOTS - 1).start()

        x_copy(g).wait()

        @pl.when(g >= EXPERT_SLOTS)
        def _():
            y_copy(g - EXPERT_SLOTS).wait()

        lo, hi = _unpack_pair(xbuf[slot])
        lo, hi = lo.astype(BF16), hi.astype(BF16)
        hg = (jnp.dot(lo, wg_scr[:HALF], preferred_element_type=F32)
              + jnp.dot(hi, wg_scr[HALF:], preferred_element_type=F32))
        hu = (jnp.dot(lo, wu_scr[:HALF], preferred_element_type=F32)
              + jnp.dot(hi, wu_scr[HALF:], preferred_element_type=F32))
        h = ((hg * jax.nn.sigmoid(hg)) * hu).astype(BF16)
        y = _pack_pair(jnp.dot(h, wd_scr[:, :HALF], preferred_element_type=F32),
                       jnp.dot(h, wd_scr[:, HALF:], preferred_element_type=F32))
        for piece in range(SC_SPLIT):
            ybuf[slot, piece] = y[:, piece * PIECE:(piece + 1) * PIECE]
        y_copy(g).start()
        return carry

    lax.fori_loop(g0, g1, block, 0)

    @pl.when(e == N_EXPERTS - 1)
    def _():
        for back in range(EXPERT_SLOTS, 0, -1):
            pl.when(total >= back)(lambda back=back: y_copy(total - back).wait())


def _experts(cum_blocks, xs, w_gate, w_up, w_down):
    n_rows = xs.shape[0]
    wmap = lambda e, cb: (e, 0, 0)
    grid_spec = pltpu.PrefetchScalarGridSpec(
        num_scalar_prefetch=1,
        grid=(N_EXPERTS,),
        in_specs=[pl.BlockSpec(memory_space=pl.ANY),
                  pl.BlockSpec((1, D_MODEL, D_EXPERT), wmap),
                  pl.BlockSpec((1, D_MODEL, D_EXPERT), wmap),
                  pl.BlockSpec((1, D_EXPERT, D_MODEL), wmap)],
        out_specs=pl.BlockSpec(memory_space=pl.ANY),
        scratch_shapes=[pltpu.VMEM((D_MODEL, D_EXPERT), BF16), pltpu.VMEM((D_MODEL, D_EXPERT), BF16),
                        pltpu.VMEM((D_EXPERT, D_MODEL), BF16),
                        pltpu.VMEM((EXPERT_SLOTS, EXPERT_ROWS, HALF), jnp.int32),
                        pltpu.VMEM((EXPERT_SLOTS, SC_SPLIT, EXPERT_ROWS, PIECE), jnp.int32),
                        pltpu.SemaphoreType.DMA((EXPERT_SLOTS,)), pltpu.SemaphoreType.DMA((EXPERT_SLOTS,))])
    return pl.pallas_call(
        _experts_kernel,
        grid_spec=grid_spec,
        out_shape=jax.ShapeDtypeStruct((SC_SPLIT, n_rows, PIECE), jnp.int32),
        compiler_params=_params(("arbitrary",)),
        name="experts",
    )(cum_blocks, xs, w_gate, w_up, w_down)


def _sc_gather_rows(table, idx):
    n, d = idx.shape[0], table.shape[2]
    mesh = plsc.VectorSubcoreMesh(core_axis_name="core", subcore_axis_name="subcore",
                                  num_cores=SC_CORES, num_subcores=SC_SUBCORES)

    @functools.partial(pl.kernel, out_type=jax.ShapeDtypeStruct((SC_SPLIT, n, d), table.dtype), mesh=mesh,
                       scratch_types=[])
    def gather(table_hbm, idx_hbm, out_hbm):
        for piece in range(SC_SPLIT):
            def window(idx_vmem, out_vmem, piece=piece):
                pltpu.sync_copy(table_hbm.at[piece].at[idx_vmem.at[0]], out_vmem)

            pltpu.emit_pipeline(
                window,
                grid=(n // SC_WINDOW,),
                in_specs=[pl.BlockSpec((1, SC_WINDOW), lambda i: (0, i))],
                out_specs=[pl.BlockSpec((SC_WINDOW, d), lambda i: (i, 0))],
                core_axis_name=("core", "subcore"),
                dimension_semantics=(pltpu.PARALLEL,),
            )(idx_hbm, out_hbm.at[piece])

    return gather(table, idx.reshape(1, n))


def _combine_kernel(w_ref, base_ref, g2_ref, b2_ref, rows_ref, o_ref):
    w = w_ref[...]
    acc = [[base_ref[:, half * HALF + p * PIECE:half * HALF + (p + 1) * PIECE] for p in range(SC_SPLIT)]
           for half in range(2)]
    for k in range(TOP_K):
        wk = w[:, k:k + 1]
        for p in range(SC_SPLIT):
            lo, hi = _unpack_pair(rows_ref[p, k])
            acc[0][p] = acc[0][p] + lo * wk
            acc[1][p] = acc[1][p] + hi * wk
    o_ref[...] = _layer_norm(jnp.concatenate(acc[0] + acc[1], axis=1), g2_ref[...], b2_ref[...])


def _combine(wts_tk, base, g2, b2, rows, tm):
    t = base.shape[0]
    return pl.pallas_call(
        _combine_kernel,
        grid=(t // tm,),
        in_specs=[pl.BlockSpec((tm, TOP_K), lambda i: (i, 0)),
                  pl.BlockSpec((tm, D_MODEL), lambda i: (i, 0)),
                  pl.BlockSpec((1, D_MODEL), lambda i: (0, 0)),
                  pl.BlockSpec((1, D_MODEL), lambda i: (0, 0)),
                  pl.BlockSpec((SC_SPLIT, TOP_K, tm, PIECE), lambda i: (0, 0, i, 0))],
        out_specs=pl.BlockSpec((tm, D_MODEL), lambda i: (i, 0)),
        out_shape=jax.ShapeDtypeStruct((t, D_MODEL), F32),
        compiler_params=_params(("arbitrary",)),
        name="combine",
    )(wts_tk, base, g2, b2, rows)


def _tile(n, pref):
    t = pref
    while n % t:
        t //= 2
    return t


def _token_mixers(x, layer_idx, wp, cnt_in):
    bsz, seq, _ = x.shape
    t = bsz * seq
    x2 = x.reshape(t, D_MODEL)
    lam_init = 0.8 - 0.6 * math.exp(-0.3 * layer_idx)

    cos_t, sin_t = _rope_tables(seq)
    qa, ka, va, qd, kd, vd, ga, gb = _in_proj(x2, wp["w_in"], cos_t, sin_t, seq, _tile(seq, 256))
    oa = _attn_a(qa, ka, va, wp["sink"], bsz, seq, _tile(seq, 512))
    ob = _attn_b(qd, kd, vd, wp["lam_p"], wp["subln_g"], bsz, seq, _tile(seq, 512), _tile(seq, 1024), lam_init)
    return _post_attn(
        x2, oa, ob, ga, gb, wp["w_o_a"], wp["w_o_b"], wp["w_out"], wp["ln1_g"], wp["ln1_b"],
        wp["router_wt"], wp["router_bias"], wp["ws_gate"], wp["ws_up"], wp["ws_down"], cnt_in, _tile(t, 512))


def _encoder_layer(xs_in, layer_idx, wp):
    cnt_in = jnp.zeros((N_EXPERTS, 1), F32)
    routed = []
    for x in xs_in:
        x1, base, eidx, wts, rank, cnt = _token_mixers(x, layer_idx, wp, cnt_in)
        cnt_in = cnt[:, :1].astype(F32)
        routed.append((x1, base, eidx, wts, rank))

    n_assign = sum(x.shape[0] * x.shape[1] for x in xs_in) * TOP_K
    n_blk = -(-(n_assign + N_EXPERTS * (EXPERT_ROWS - 1)) // EXPERT_ROWS)
    counts = cnt[:, 0]
    padded = ((counts + EXPERT_ROWS - 1) // EXPERT_ROWS) * EXPERT_ROWS
    pend = jnp.cumsum(padded)
    pstart = pend - padded
    cum_blocks = jnp.concatenate([jnp.zeros((1,), jnp.int32), (pend // EXPERT_ROWS).astype(jnp.int32)])
    pstart_col = pstart.astype(jnp.int32)[:, None]

    dests = [_dest(eidx, rank, pstart_col, 128) for _, _, eidx, _, rank in routed]
    xs = _dispatch((pstart + counts).astype(jnp.int32), jnp.concatenate([d[0] for d in dests], axis=0),
                   [r[0] for r in routed], n_blk * EXPERT_ROWS)
    ys = _experts(cum_blocks, xs, wp["w_gate"], wp["w_up"], wp["w_down"])
    outs = []
    for x, (_, dest_kt), (x1, base, eidx, wts, rank) in zip(xs_in, dests, routed):
        t = base.shape[0]
        rows = _sc_gather_rows(ys, dest_kt.reshape(-1)).reshape(SC_SPLIT, TOP_K, t, PIECE)
        y = _combine(wts.T, base, wp["ln2_g"], wp["ln2_b"], rows, _tile(t, 256))
        outs.append(y.reshape(x.shape))
    return outs


def kernel(x_prompt, x_sample, w_in, attn_sink, lambda_q1, lambda_k1, lambda_q2, lambda_k2, subln_g, w_o_a, w_o_b, w_out, ln1_g, ln1_b, router_w, router_bias, w_gate, w_up, w_down, ws_gate, ws_up, ws_down, ln2_g, ln2_b):
    y_prompt, y_sample = x_prompt, x_sample
    for l in range(DEPTH):
        wp = {
            "w_in": _prep_w_in(w_in[l]),
            "sink": attn_sink[l].astype(F32),
            "lam_p": jnp.stack([lambda_q1[l], lambda_k1[l], lambda_q2[l], lambda_k2[l]]).astype(F32),
            "subln_g": subln_g[l].astype(F32)[None, :],
            "w_o_a": w_o_a[l].astype(BF16), "w_o_b": w_o_b[l].astype(BF16), "w_out": w_out[l].astype(BF16),
            "ln1_g": ln1_g[l].astype(F32)[None, :], "ln1_b": ln1_b[l].astype(F32)[None, :],
            "router_wt": router_w[l].T.astype(BF16), "router_bias": router_bias[l].astype(F32)[:, None],
            "w_gate": w_gate[l], "w_up": w_up[l], "w_down": w_down[l],
            "ws_gate": ws_gate[l].astype(BF16), "ws_up": ws_up[l].astype(BF16), "ws_down": ws_down[l].astype(BF16),
            "ln2_g": ln2_g[l].astype(F32)[None, :], "ln2_b": ln2_b[l].astype(F32)[None, :],
        }
        y_prompt, y_sample = _encoder_layer((y_prompt, y_sample), l, wp)
    return (y_prompt, y_sample)
```

```python
import functools
import math

import jax
import jax.numpy as jnp
from jax import lax
from jax.experimental import pallas as pl
from jax.experimental.pallas import tpu as pltpu
from jax.experimental.pallas import tpu_sc as plsc

D_MODEL = 1024
HEAD_DIM = 64
ROPE_THETA = 10000.0
BLOCK = 128
A_Q_HEADS = 8
A_KV_HEADS = 2
WINDOW = 128
B_HEADS = 4
N_EXPERTS = 256
TOP_K = 8
N_GROUPS = 8
TOPK_GROUPS = 4
GROUP_SIZE = N_EXPERTS // N_GROUPS
D_EXPERT = 256
ROUTED_SCALE = 2.5
EXPERT_ROWS = 256
EXPERT_SLOTS = 4
DEPTH = 1
DN_ALPHA = (2 * DEPTH) ** 0.25
LN_EPS = 1e-5
RMS_EPS = 1e-5
NEG = -1e30
LOG2E = math.log2(math.e)

LANES = 128
SC_CORES = 2
SC_SUBCORES = 16
SC_WINDOW = 128
SC_SPLIT = 2
PIECE = D_MODEL // 2 // SC_SPLIT
VMEM_LIMIT_BYTES = 56 * 1024 * 1024

F32 = jnp.float32
BF16 = jnp.bfloat16
_NT = (((1,), (1,)), ((), ()))


def _params(sem, vmem=VMEM_LIMIT_BYTES):
    return pltpu.CompilerParams(dimension_semantics=sem, vmem_limit_bytes=vmem)


_C_QA, _C_KA, _C_VA, _C_QD, _C_KD, _C_VD, _C_GA, _C_GB, _C_END = 0, 512, 768, 1024, 1536, 2048, 2560, 3584, 4608


def _prep_w_in(w_in):
    cuts = [0, 512, 640, 768, 1280, 1792, 2304, 3328, 4352]
    qa, ka, va, qd, kd, vd, ga, gb = [w_in[:, cuts[i]:cuts[i + 1]] for i in range(8)]
    dup = lambda w: jnp.concatenate([w[:, :64], w[:, :64], w[:, 64:], w[:, 64:]], axis=1)
    return jnp.concatenate([qa, dup(ka), dup(va), qd, kd, vd, ga, gb], axis=1).astype(BF16)


def _rope_tables(s):
    half = HEAD_DIM // 2
    inv = 1.0 / (ROPE_THETA ** (jnp.arange(half, dtype=F32) / half))
    ang = jnp.arange(s, dtype=F32)[:, None] * inv[None, :]
    cos, sin = jnp.cos(ang), jnp.sin(ang)
    return (jnp.concatenate([cos, cos, cos, cos], axis=1),
            jnp.concatenate([-sin, sin, -sin, sin], axis=1))


def _in_proj_kernel(x_ref, w_ref, cos_ref, sin_ref,
                    qa_ref, ka_ref, va_ref, qd_ref, kd_ref, vd_ref, ga_ref, gb_ref):
    xb = x_ref[...].astype(BF16)
    cos = cos_ref[...]
    sin = sin_ref[...]
    lane = lax.broadcasted_iota(jnp.int32, cos.shape, 1)
    first_half = (lane & (HEAD_DIM // 2)) == 0

    def proj(c0, c1):
        return jnp.dot(xb, w_ref[:, c0:c1], preferred_element_type=F32)

    def rope_store(u, out_ref, scale):
        for j in range(u.shape[1] // LANES):
            uj = u[:, LANES * j:LANES * (j + 1)]
            rot = jnp.where(first_half, pltpu.roll(uj, LANES - 32, 1), pltpu.roll(uj, 32, 1))
            r = uj * cos + rot * sin
            if scale != 1.0:
                r = r * scale
            out_ref[:, LANES * j:LANES * (j + 1)] = r.astype(out_ref.dtype)

    scale = HEAD_DIM ** -0.5
    rope_store(proj(_C_QA, _C_KA), qa_ref, scale)
    rope_store(proj(_C_KA, _C_VA), ka_ref, 1.0)
    va_ref[...] = proj(_C_VA, _C_QD).astype(va_ref.dtype)
    rope_store(proj(_C_QD, _C_KD), qd_ref, scale * LOG2E)
    rope_store(proj(_C_KD, _C_VD), kd_ref, 1.0)
    vd_ref[...] = proj(_C_VD, _C_GA).astype(vd_ref.dtype)
    ga_ref[...] = jax.nn.sigmoid(proj(_C_GA, _C_GB)).astype(ga_ref.dtype)
    gb_ref[...] = jax.nn.sigmoid(proj(_C_GB, _C_END)).astype(gb_ref.dtype)


def _in_proj(x2, w_perm, cos_t, sin_t, seq, tm):
    t = x2.shape[0]
    nseq = seq // tm
    row = lambda i: (i, 0)
    widths = (512, 256, 256, 512, 512, 512, 1024, 1024)
    return pl.pallas_call(
        _in_proj_kernel,
        grid=(t // tm,),
        in_specs=[pl.BlockSpec((tm, D_MODEL), row),
                  pl.BlockSpec((D_MODEL, _C_END), lambda i: (0, 0)),
                  pl.BlockSpec((tm, LANES), lambda i: (i % nseq, 0)),
                  pl.BlockSpec((tm, LANES), lambda i: (i % nseq, 0))],
        out_specs=[pl.BlockSpec((tm, w), row) for w in widths],
        out_shape=[jax.ShapeDtypeStruct((t, w), BF16) for w in widths],
        compiler_params=_params(("arbitrary",)),
        name="in_proj",
    )(x2, w_perm, cos_t, sin_t)


def _attn_a_kernel(sink_ref, q_ref, kp_ref, kc_ref, kn_ref, vp_ref, vc_ref, vn_ref, o_ref, *, seq, tq):
    i = pl.program_id(1)
    kk = jnp.concatenate([kp_ref[...], kc_ref[...], kn_ref[...]], axis=0)
    vv = jnp.concatenate([vp_ref[...], vc_ref[...], vn_ref[...]], axis=0)
    lane = lax.broadcasted_iota(jnp.int32, (kk.shape[0], LANES), 1)
    lo = lane < HEAD_DIM
    zero = jnp.zeros((kk.shape[0], LANES), BF16)
    k_lo = [jnp.where(lo, kk[:, LANES * h:LANES * (h + 1)], zero) for h in range(A_KV_HEADS)]
    k_hi = [jnp.where(lo, zero, kk[:, LANES * h:LANES * (h + 1)]) for h in range(A_KV_HEADS)]
    v_lo = [jnp.where(lo, vv[:, LANES * h:LANES * (h + 1)], zero) for h in range(A_KV_HEADS)]
    v_hi = [jnp.where(lo, zero, vv[:, LANES * h:LANES * (h + 1)]) for h in range(A_KV_HEADS)]

    qi = lax.broadcasted_iota(jnp.int32, (BLOCK, 3 * BLOCK), 0)
    kj = lax.broadcasted_iota(jnp.int32, (BLOCK, 3 * BLOCK), 1)
    band = jnp.abs(kj - BLOCK - qi) <= WINDOW
    for j in range(tq // BLOCK):
        kpos = i * tq + (j - 1) * BLOCK + kj
        mask = band & (kpos >= 0) & (kpos < seq)
        r0, r1 = j * BLOCK, (j + 3) * BLOCK
        for c in range(A_Q_HEADS // 2):
            h = c // 2
            q2 = q_ref[j * BLOCK:(j + 1) * BLOCK, LANES * c:LANES * (c + 1)]
            out = None
            for half, (kx, vx) in enumerate(((k_lo[h], v_lo[h]), (k_hi[h], v_hi[h]))):
                s = lax.dot_general(q2, kx[r0:r1], _NT, preferred_element_type=F32)
                s = jnp.where(mask, s, NEG)
                snk = sink_ref[2 * c + half]
                m = jnp.maximum(jnp.max(s, axis=-1, keepdims=True), snk)
                p = jnp.exp(s - m)
                den = jnp.sum(p, axis=-1, keepdims=True) + jnp.exp(snk - m)
                p = p * (1.0 / den)
                pv = jnp.dot(p.astype(BF16), vx[r0:r1], preferred_element_type=F32)
                out = pv if out is None else out + pv
            o_ref[j * BLOCK:(j + 1) * BLOCK, LANES * c:LANES * (c + 1)] = out.astype(o_ref.dtype)


def _attn_a(qa, ka, va, sink, bsz, seq, tq):
    t = qa.shape[0]
    nq = seq // tq
    nb = seq // BLOCK
    r = tq // BLOCK
    cur = lambda b, i: (b * nq + i, 0)
    prev = lambda b, i: (b * nb + jnp.maximum(i * r - 1, 0), 0)
    nxt = lambda b, i: (b * nb + jnp.minimum((i + 1) * r, nb - 1), 0)
    kv_specs = [pl.BlockSpec((BLOCK, 256), prev), pl.BlockSpec((tq, 256), cur), pl.BlockSpec((BLOCK, 256), nxt)]
    return pl.pallas_call(
        functools.partial(_attn_a_kernel, seq=seq, tq=tq),
        grid=(bsz, nq),
        in_specs=[pl.BlockSpec(memory_space=pltpu.SMEM), pl.BlockSpec((tq, 512), cur)] + kv_specs + kv_specs,
        out_specs=pl.BlockSpec((tq, 512), cur),
        out_shape=jax.ShapeDtypeStruct((t, 512), BF16),
        compiler_params=_params(("arbitrary", "arbitrary")),
        name="attn_a",
    )(sink, qa, ka, ka, ka, va, va, va)


def _attn_b_kernel(lam_ref, g_ref, q_ref, k_ref, v_ref, o_ref, acc_ref, s_ref, *, seq, tk, lam_init):
    q = q_ref[...]
    tq = q.shape[0]
    lane = lax.broadcasted_iota(jnp.int32, (tk, LANES), 1)
    lo = lane < HEAD_DIM
    zero = jnp.zeros((tk, LANES), BF16)
    acc_ref[...] = jnp.zeros_like(acc_ref)
    ones_col = jnp.where(lane == 0, 1.0, 0.0).astype(BF16)

    def scores(kc):
        k = k_ref[pl.ds(pl.multiple_of(kc * tk, tk), tk), :]
        return [lax.dot_general(q, kx, _NT, preferred_element_type=F32)
                for kx in (jnp.where(lo, k, zero), jnp.where(lo, zero, k))]

    def consume(kc, maxes):
        v_aug = jnp.concatenate([v_ref[pl.ds(pl.multiple_of(kc * tk, tk), tk), :], ones_col], axis=1)
        new = []
        for idx in range(2):
            s = s_ref[idx]
            m_new = jnp.maximum(maxes[idx], jnp.max(s, axis=-1, keepdims=True))
            alpha = jnp.exp2(maxes[idx] - m_new)
            p = jnp.exp2(s - m_new)
            acc_ref[idx] = alpha * acc_ref[idx] + jnp.dot(p.astype(BF16), v_aug, preferred_element_type=F32)
            new.append(m_new)
        return tuple(new)

    def body(kc, maxes):
        nxt = scores(kc + 1)
        maxes = consume(kc, maxes)
        s_ref[0], s_ref[1] = nxt
        return maxes

    s_ref[0], s_ref[1] = scores(0)
    minf = jnp.full((tq, 1), -jnp.inf, F32)
    n_chunks = seq // tk
    consume(n_chunks - 1, lax.fori_loop(0, n_chunks - 1, body, (minf, minf)))

    lp = lam_ref[...]
    lam = (jnp.exp(jnp.sum(lp[0:1] * lp[1:2], axis=-1, keepdims=True))
           - jnp.exp(jnp.sum(lp[2:3] * lp[3:4], axis=-1, keepdims=True)) + lam_init)
    acc1, acc2 = acc_ref[0], acc_ref[1]
    l1, l2 = acc1[:, LANES:LANES + 1], acc2[:, LANES:LANES + 1]
    o = acc1[:, :LANES] * (1.0 / l1) - lam * (acc2[:, :LANES] * (1.0 / l2))
    o = o * lax.rsqrt(jnp.mean(o * o, axis=-1, keepdims=True) + RMS_EPS)
    o = o * g_ref[...] * (1.0 - lam_init)
    o_ref[...] = o.astype(o_ref.dtype)


def _attn_b(qd, kd, vd, lam_p, subln_g, bsz, seq, tq, tk, lam_init):
    t = qd.shape[0]
    nq = seq // tq
    return pl.pallas_call(
        functools.partial(_attn_b_kernel, seq=seq, tk=tk, lam_init=lam_init),
        grid=(bsz, B_HEADS, nq),
        in_specs=[pl.BlockSpec((4, HEAD_DIM), lambda b, h, i: (0, 0)),
                  pl.BlockSpec((1, LANES), lambda b, h, i: (0, 0)),
                  pl.BlockSpec((tq, LANES), lambda b, h, i: (b * nq + i, h)),
                  pl.BlockSpec((seq, LANES), lambda b, h, i: (b, h)),
                  pl.BlockSpec((seq, LANES), lambda b, h, i: (b, h))],
        out_specs=pl.BlockSpec((tq, LANES), lambda b, h, i: (b * nq + i, h)),
        out_shape=jax.ShapeDtypeStruct((t, 512), BF16),
        scratch_shapes=[pltpu.VMEM((2, tq, 2 * LANES), F32), pltpu.VMEM((2, tq, tk), F32)],
        compiler_params=_params(("arbitrary", "arbitrary", "arbitrary")),
        name="attn_b",
    )(lam_p, subln_g, qd, kd, vd)


HALF = D_MODEL // 2
_HI_MASK = -65536


def _pack_pair(lo, hi):
    lo_bits = lax.bitcast_convert_type(lo.astype(BF16).astype(F32), jnp.int32)
    hi_bits = lax.bitcast_convert_type(hi.astype(BF16).astype(F32), jnp.int32)
    return lax.shift_right_logical(lo_bits, 16) | hi_bits


def _unpack_pair(w):
    lo = lax.bitcast_convert_type(lax.shift_left(w, 16), F32)
    hi = lax.bitcast_convert_type(w & _HI_MASK, F32)
    return lo, hi


def _layer_norm(y, g, b):
    mu = jnp.mean(y, axis=-1, keepdims=True)
    d = y - mu
    var = jnp.mean(d * d, axis=-1, keepdims=True)
    return d * lax.rsqrt(var + LN_EPS) * g + b


def _first_argmax(vals, rowf, big):
    m = jnp.max(vals, axis=0, keepdims=True)
    idx = jnp.min(jnp.where(vals == m, rowf, big), axis=0, keepdims=True)
    return m, idx


def _post_attn_kernel(x_ref, oa_ref, ob_ref, ga_ref, gb_ref, woa_ref, wob_ref, wout_ref, g1_ref, b1_ref,
                      rwt_ref, rb_ref, wsg_ref, wsu_ref, wsd_ref, tri_ref, cnt_in_ref,
                      x1_ref, base_ref, eidx_ref, wts_ref, rank_ref, cnt_ref, cnt_scr):
    i = pl.program_id(0)
    tm = x_ref.shape[0]

    @pl.when(i == 0)
    def _():
        cnt_scr[...] = cnt_in_ref[...]

    a = jnp.dot(oa_ref[...], woa_ref[...], preferred_element_type=F32)
    b = jnp.dot(ob_ref[...], wob_ref[...], preferred_element_type=F32)
    merged = ga_ref[...].astype(F32) * a + gb_ref[...].astype(F32) * b
    mix = jnp.dot(merged.astype(BF16), wout_ref[...], preferred_element_type=F32)
    x1 = _layer_norm(DN_ALPHA * x_ref[...] + mix, g1_ref[...], b1_ref[...])
    x1p = _pack_pair(x1[:, :HALF], x1[:, HALF:])
    for piece in range(SC_SPLIT):
        x1_ref[piece] = x1p[:, piece * PIECE:(piece + 1) * PIECE]
    x1b = x1.astype(BF16)

    hg = jnp.dot(x1b, wsg_ref[...], preferred_element_type=F32)
    hu = jnp.dot(x1b, wsu_ref[...], preferred_element_type=F32)
    hs = (hg * jax.nn.sigmoid(hg)) * hu
    shared = jnp.dot(hs.astype(BF16), wsd_ref[...], preferred_element_type=F32)
    base_ref[...] = DN_ALPHA * x1 + shared

    logits = lax.dot_general(rwt_ref[...], x1b, _NT, preferred_element_type=F32)
    scores = jax.nn.sigmoid(logits)
    choice = scores + rb_ref[...]
    ninf = -jnp.inf
    grow = lax.broadcasted_iota(jnp.int32, (GROUP_SIZE, tm), 0).astype(F32)
    gscore = []
    for g in range(N_GROUPS):
        blk = choice[GROUP_SIZE * g:GROUP_SIZE * (g + 1)]
        m1, i1 = _first_argmax(blk, grow, float(GROUP_SIZE))
        m2 = jnp.max(jnp.where(grow == i1, ninf, blk), axis=0, keepdims=True)
        gscore.append(m1 + m2)
    selected = [jnp.zeros((1, tm), F32) for _ in range(N_GROUPS)]
    work = list(gscore)
    for _ in range(TOPK_GROUPS):
        best = work[0]
        for g in range(1, N_GROUPS):
            best = jnp.maximum(best, work[g])
        taken = jnp.zeros((1, tm), F32)
        for g in range(N_GROUPS):
            hit = jnp.where((work[g] == best) & (taken == 0.0), 1.0, 0.0)
            taken = jnp.maximum(taken, hit)
            selected[g] = jnp.maximum(selected[g], hit)
            work[g] = jnp.where(hit > 0.0, ninf, work[g])
    masked = jnp.concatenate(
        [jnp.where(selected[g] > 0.0, choice[GROUP_SIZE * g:GROUP_SIZE * (g + 1)], ninf) for g in range(N_GROUPS)],
        axis=0)

    rowf = lax.broadcasted_iota(jnp.int32, (N_EXPERTS, tm), 0).astype(F32)
    hits, idxs, ws = [], [], []
    for _ in range(TOP_K):
        _, idx = _first_argmax(masked, rowf, float(N_EXPERTS))
        hit = rowf == idx
        hits.append(hit)
        idxs.append(idx)
        ws.append(jnp.sum(jnp.where(hit, scores, 0.0), axis=0, keepdims=True))
        masked = jnp.where(hit, ninf, masked)
    wsum = ws[0]
    for k in range(1, TOP_K):
        wsum = wsum + ws[k]

    member = hits[0]
    for k in range(1, TOP_K):
        member = member | hits[k]
    member_f = jnp.where(member, 1.0, 0.0)
    before = jnp.dot(member_f.astype(BF16), tri_ref[...], preferred_element_type=F32) + cnt_scr[...]
    for k in range(TOP_K):
        eidx_ref[k:k + 1, :] = idxs[k].astype(jnp.int32)
        wts_ref[k:k + 1, :] = ws[k] / wsum * ROUTED_SCALE
        rank_ref[k:k + 1, :] = jnp.sum(jnp.where(hits[k], before, 0.0), axis=0, keepdims=True).astype(jnp.int32)
    cnt_scr[...] = cnt_scr[...] + jnp.sum(member_f, axis=1, keepdims=True)
    cnt_ref[...] = jnp.broadcast_to(cnt_scr[...], cnt_ref.shape).astype(jnp.int32)


def _post_attn(x2, oa, ob, ga, gb, woa, wob, wout, g1, b1, rwt, rb, wsg, wsu, wsd, cnt_in, tm):
    t = x2.shape[0]
    tri = jnp.triu(jnp.ones((tm, tm), F32), k=1).astype(BF16)
    row = lambda i: (i, 0)
    col = lambda i: (0, i)
    full = lambda i: (0, 0)
    wspec = lambda arr: pl.BlockSpec(arr.shape, full)
    weights = (woa, wob, wout, g1, b1, rwt, rb, wsg, wsu, wsd, tri, cnt_in)
    return pl.pallas_call(
        _post_attn_kernel,
        grid=(t // tm,),
        in_specs=[pl.BlockSpec((tm, D_MODEL), row), pl.BlockSpec((tm, 512), row), pl.BlockSpec((tm, 512), row),
                  pl.BlockSpec((tm, D_MODEL), row), pl.BlockSpec((tm, D_MODEL), row)] + [wspec(w) for w in weights],
        out_specs=[pl.BlockSpec((SC_SPLIT, tm, PIECE), lambda i: (0, i, 0)), pl.BlockSpec((tm, D_MODEL), row),
                   pl.BlockSpec((TOP_K, tm), col), pl.BlockSpec((TOP_K, tm), col), pl.BlockSpec((TOP_K, tm), col),
                   pl.BlockSpec((N_EXPERTS, LANES), full)],
        out_shape=[jax.ShapeDtypeStruct((SC_SPLIT, t, PIECE), jnp.int32), jax.ShapeDtypeStruct((t, D_MODEL), F32),
                   jax.ShapeDtypeStruct((TOP_K, t), jnp.int32), jax.ShapeDtypeStruct((TOP_K, t), F32),
                   jax.ShapeDtypeStruct((TOP_K, t), jnp.int32), jax.ShapeDtypeStruct((N_EXPERTS, LANES), jnp.int32)],
        scratch_shapes=[pltpu.VMEM((N_EXPERTS, 1), F32)],
        compiler_params=_params(("arbitrary",)),
        name="post_attn",
    )(x2, oa, ob, ga, gb, *weights)


def _dest_kernel(eidx_ref, rank_ref, pstart_ref, dest_ref):
    tm = eidx_ref.shape[1]
    rows = lax.broadcasted_iota(jnp.int32, (N_EXPERTS, tm), 0)
    pstart = pstart_ref[...]
    for k in range(TOP_K):
        hit = rows == eidx_ref[k:k + 1, :]
        start = jnp.sum(jnp.where(hit, pstart, 0).astype(F32), axis=0, keepdims=True).astype(jnp.int32)
        dest_ref[k:k + 1, :] = start + rank_ref[k:k + 1, :]


def _dest(eidx, rank, pstart, tm):
    t = eidx.shape[1]
    return pl.pallas_call(
        _dest_kernel,
        grid=(t // tm,),
        in_specs=[pl.BlockSpec((TOP_K, tm), lambda i: (0, i)), pl.BlockSpec((TOP_K, tm), lambda i: (0, i)),
                  pl.BlockSpec((N_EXPERTS, 1), lambda i: (0, 0))],
        out_specs=pl.BlockSpec((TOP_K, tm), lambda i: (0, i)),
        out_shape=jax.ShapeDtypeStruct((TOP_K, t), jnp.int32),
        compiler_params=_params(("arbitrary",)),
        name="dest",
    )(eidx, rank, pstart)


def _sc_mesh():
    return plsc.VectorSubcoreMesh(core_axis_name="core", subcore_axis_name="subcore",
                                  num_cores=SC_CORES, num_subcores=SC_SUBCORES)


def _sc_scatter_rows(xs_groups, dests, n_rows):
    n_groups = len(xs_groups)

    @functools.partial(pl.kernel, out_type=jax.ShapeDtypeStruct((SC_SPLIT, n_rows, PIECE), jnp.int32),
                       mesh=_sc_mesh(), scratch_types=[])
    def scatter(*refs):
        out_hbm = refs[2 * n_groups]
        for g in range(n_groups):
            x_hbm, idx_hbm = refs[2 * g], refs[2 * g + 1]
            windows = x_hbm.shape[1] // SC_WINDOW
            for piece in range(SC_SPLIT):
                def window(x_vmem, idx_vmem, piece=piece):
                    pltpu.sync_copy(x_vmem, out_hbm.at[piece].at[idx_vmem.at[0]])

                pltpu.emit_pipeline(
                    window,
                    grid=(TOP_K * windows,),
                    in_specs=[pl.BlockSpec((SC_WINDOW, PIECE), lambda i, windows=windows: (i % windows, 0)),
                              pl.BlockSpec((1, SC_WINDOW), lambda i: (0, i))],
                    out_specs=[],
                    core_axis_name=("core", "subcore"),
                    dimension_semantics=(pltpu.PARALLEL,),
                )(x_hbm.at[piece], idx_hbm)

    args = []
    for x, d in zip(xs_groups, dests):
        args += [x, d.reshape(1, -1)]
    return scatter(*args)


def _experts_kernel(cb_ref, cnt_ref, xs_ref, wg_ref, wu_ref, wd_ref, ys_ref,
                    wg_scr, wu_scr, wd_scr, xbuf, ybuf, sem_in, sem_out):
    e = pl.program_id(0)
    g0, g1, total = cb_ref[e], cb_ref[e + 1], cb_ref[N_EXPERTS]

    def rows(g):
        return pl.ds(pl.multiple_of(g * EXPERT_ROWS, EXPERT_ROWS), EXPERT_ROWS)

    def x_copy(g):
        s = g % EXPERT_SLOTS
        return pltpu.make_async_copy(xs_ref.at[:, rows(g), :], xbuf.at[s], sem_in.at[s])

    def y_copy(g):
        s = g % EXPERT_SLOTS
        return pltpu.make_async_copy(ybuf.at[s], ys_ref.at[:, rows(g), :], sem_out.at[s])

    @pl.when(e == 0)
    def _():
        for g in range(EXPERT_SLOTS - 1):
            pl.when(g < total)(lambda g=g: x_copy(g).start())

    @pl.when(g1 > g0)
    def _():
        wg_scr[...] = wg_ref[0].astype(BF16)
        wu_scr[...] = wu_ref[0].astype(BF16)
        wd_scr[...] = wd_ref[0].astype(BF16)

    def block(g, carry):
        slot = g % EXPERT_SLOTS

        @pl.when(g + EXPERT_SLOTS - 1 < total)
        def _():
            x_copy(g + EXPERT_SLOTS - 1).start()

        x_copy(g).wait()

        @pl.when(g >= EXPERT_SLOTS)
        def _():
            y_copy(g - EXPERT_SLOTS).wait()

        valid = lax.broadcasted_iota(jnp.int32, (EXPERT_ROWS, PIECE), 0) < cnt_ref[e] - (g - g0) * EXPERT_ROWS
        hg = hu = None
        for piece in range(SC_SPLIT):
            for part, col0 in zip(_unpack_pair(xbuf[slot, piece]), (piece * PIECE, HALF + piece * PIECE)):
                xb = jnp.where(valid, part, 0.0).astype(BF16)
                pg = jnp.dot(xb, wg_scr[col0:col0 + PIECE], preferred_element_type=F32)
                pu = jnp.dot(xb, wu_scr[col0:col0 + PIECE], preferred_element_type=F32)
                hg = pg if hg is None else hg + pg
                hu = pu if hu is None else hu + pu
        h = ((hg * jax.nn.sigmoid(hg)) * hu).astype(BF16)
        y = _pack_pair(jnp.dot(h, wd_scr[:, :HALF], preferred_element_type=F32),
                       jnp.dot(h, wd_scr[:, HALF:], preferred_element_type=F32))
        for piece in range(SC_SPLIT):
            ybuf[slot, piece] = y[:, piece * PIECE:(piece + 1) * PIECE]
        y_copy(g).start()
        return carry

    lax.fori_loop(g0, g1, block, 0)

    @pl.when(e == N_EXPERTS - 1)
    def _():
        for back in range(EXPERT_SLOTS, 0, -1):
            pl.when(total >= back)(lambda back=back: y_copy(total - back).wait())


def _experts(cum_blocks, counts, xs, w_gate, w_up, w_down):
    n_rows = xs.shape[1]
    wmap = lambda e, cb, cnt: (e, 0, 0)
    grid_spec = pltpu.PrefetchScalarGridSpec(
        num_scalar_prefetch=2,
        grid=(N_EXPERTS,),
        in_specs=[pl.BlockSpec(memory_space=pl.ANY),
                  pl.BlockSpec((1, D_MODEL, D_EXPERT), wmap),
                  pl.BlockSpec((1, D_MODEL, D_EXPERT), wmap),
                  pl.BlockSpec((1, D_EXPERT, D_MODEL), wmap)],
        out_specs=pl.BlockSpec(memory_space=pl.ANY),
        scratch_shapes=[pltpu.VMEM((D_MODEL, D_EXPERT), BF16), pltpu.VMEM((D_MODEL, D_EXPERT), BF16),
                        pltpu.VMEM((D_EXPERT, D_MODEL), BF16),
                        pltpu.VMEM((EXPERT_SLOTS, SC_SPLIT, EXPERT_ROWS, PIECE), jnp.int32),
                        pltpu.VMEM((EXPERT_SLOTS, SC_SPLIT, EXPERT_ROWS, PIECE), jnp.int32),
                        pltpu.SemaphoreType.DMA((EXPERT_SLOTS,)), pltpu.SemaphoreType.DMA((EXPERT_SLOTS,))])
    return pl.pallas_call(
        _experts_kernel,
        grid_spec=grid_spec,
        out_shape=jax.ShapeDtypeStruct((SC_SPLIT, n_rows, PIECE), jnp.int32),
        compiler_params=_params(("arbitrary",)),
        name="experts",
    )(cum_blocks, counts, xs, w_gate, w_up, w_down)


def _sc_gather_rows(table, idx):
    n, d = idx.shape[0], table.shape[2]

    @functools.partial(pl.kernel, out_type=jax.ShapeDtypeStruct((SC_SPLIT, n, d), table.dtype), mesh=_sc_mesh(),
                       scratch_types=[])
    def gather(table_hbm, idx_hbm, out_hbm):
        for piece in range(SC_SPLIT):
            def window(idx_vmem, out_vmem, piece=piece):
                pltpu.sync_copy(table_hbm.at[piece].at[idx_vmem.at[0]], out_vmem)

            pltpu.emit_pipeline(
                window,
                grid=(n // SC_WINDOW,),
                in_specs=[pl.BlockSpec((1, SC_WINDOW), lambda i: (0, i))],
                out_specs=[pl.BlockSpec((SC_WINDOW, d), lambda i: (i, 0))],
                core_axis_name=("core", "subcore"),
                dimension_semantics=(pltpu.PARALLEL,),
            )(idx_hbm, out_hbm.at[piece])

    return gather(table, idx.reshape(1, n))


def _combine_kernel(w_ref, base_ref, g2_ref, b2_ref, rows_ref, o_ref):
    w = w_ref[...]
    acc = [[base_ref[:, half * HALF + p * PIECE:half * HALF + (p + 1) * PIECE] for p in range(SC_SPLIT)]
           for half in range(2)]
    for k in range(TOP_K):
        wk = w[:, k:k + 1]
        for p in range(SC_SPLIT):
            lo, hi = _unpack_pair(rows_ref[p, k])
            acc[0][p] = acc[0][p] + lo * wk
            acc[1][p] = acc[1][p] + hi * wk
    o_ref[...] = _layer_norm(jnp.concatenate(acc[0] + acc[1], axis=1), g2_ref[...], b2_ref[...])


def _combine(wts_tk, base, g2, b2, rows, tm):
    t = base.shape[0]
    return pl.pallas_call(
        _combine_kernel,
        grid=(t // tm,),
        in_specs=[pl.BlockSpec((tm, TOP_K), lambda i: (i, 0)),
                  pl.BlockSpec((tm, D_MODEL), lambda i: (i, 0)),
                  pl.BlockSpec((1, D_MODEL), lambda i: (0, 0)),
                  pl.BlockSpec((1, D_MODEL), lambda i: (0, 0)),
                  pl.BlockSpec((SC_SPLIT, TOP_K, tm, PIECE), lambda i: (0, 0, i, 0))],
        out_specs=pl.BlockSpec((tm, D_MODEL), lambda i: (i, 0)),
        out_shape=jax.ShapeDtypeStruct((t, D_MODEL), F32),
        compiler_params=_params(("arbitrary",)),
        name="combine",
    )(wts_tk, base, g2, b2, rows)


def _tile(n, pref):
    t = pref
    while n % t:
        t //= 2
    return t


def _token_mixers(x, layer_idx, wp, cnt_in):
    bsz, seq, _ = x.shape
    t = bsz * seq
    x2 = x.reshape(t, D_MODEL)
    lam_init = 0.8 - 0.6 * math.exp(-0.3 * layer_idx)

    cos_t, sin_t = _rope_tables(seq)
    qa, ka, va, qd, kd, vd, ga, gb = _in_proj(x2, wp["w_in"], cos_t, sin_t, seq, _tile(seq, 256))
    oa = _attn_a(qa, ka, va, wp["sink"], bsz, seq, _tile(seq, 512))
    ob = _attn_b(qd, kd, vd, wp["lam_p"], wp["subln_g"], bsz, seq, _tile(seq, 512), _tile(seq, 1024), lam_init)
    return _post_attn(
        x2, oa, ob, ga, gb, wp["w_o_a"], wp["w_o_b"], wp["w_out"], wp["ln1_g"], wp["ln1_b"],
        wp["router_wt"], wp["router_bias"], wp["ws_gate"], wp["ws_up"], wp["ws_down"], cnt_in, _tile(t, 512))


def _encoder_layer(xs_in, layer_idx, wp):
    cnt_in = jnp.zeros((N_EXPERTS, 1), F32)
    routed = []
    for x in xs_in:
        x1, base, eidx, wts, rank, cnt = _token_mixers(x, layer_idx, wp, cnt_in)
        cnt_in = cnt[:, :1].astype(F32)
        routed.append((x1, base, eidx, wts, rank))

    n_assign = sum(x.shape[0] * x.shape[1] for x in xs_in) * TOP_K
    n_blk = -(-(n_assign + N_EXPERTS * (EXPERT_ROWS - 1)) // EXPERT_ROWS)
    counts = cnt[:, 0]
    padded = ((counts + EXPERT_ROWS - 1) // EXPERT_ROWS) * EXPERT_ROWS
    pend = jnp.cumsum(padded)
    pstart = pend - padded
    cum_blocks = jnp.concatenate([jnp.zeros((1,), jnp.int32), (pend // EXPERT_ROWS).astype(jnp.int32)])
    pstart_col = pstart.astype(jnp.int32)[:, None]

    dests = [_dest(eidx, rank, pstart_col, 128) for _, _, eidx, _, rank in routed]
    xs = _sc_scatter_rows([r[0] for r in routed], dests, n_blk * EXPERT_ROWS)
    ys = _experts(cum_blocks, counts.astype(jnp.int32), xs, wp["w_gate"], wp["w_up"], wp["w_down"])
    outs = []
    for x, dest, (x1, base, eidx, wts, rank) in zip(xs_in, dests, routed):
        t = base.shape[0]
        rows = _sc_gather_rows(ys, dest.reshape(-1)).reshape(SC_SPLIT, TOP_K, t, PIECE)
        y = _combine(wts.T, base, wp["ln2_g"], wp["ln2_b"], rows, _tile(t, 256))
        outs.append(y.reshape(x.shape))
    return outs


def kernel(x_prompt, x_sample, w_in, attn_sink, lambda_q1, lambda_k1, lambda_q2, lambda_k2, subln_g, w_o_a, w_o_b, w_out, ln1_g, ln1_b, router_w, router_bias, w_gate, w_up, w_down, ws_gate, ws_up, ws_down, ln2_g, ln2_b):
    y_prompt, y_sample = x_prompt, x_sample
    for l in range(DEPTH):
        wp = {
            "w_in": _prep_w_in(w_in[l]),
            "sink": attn_sink[l].astype(F32),
            "lam_p": jnp.stack([lambda_q1[l], lambda_k1[l], lambda_q2[l], lambda_k2[l]]).astype(F32),
            "subln_g": subln_g[l].astype(F32)[None, :],
            "w_o_a": w_o_a[l].astype(BF16), "w_o_b": w_o_b[l].astype(BF16), "w_out": w_out[l].astype(BF16),
            "ln1_g": ln1_g[l].astype(F32)[None, :], "ln1_b": ln1_b[l].astype(F32)[None, :],
            "router_wt": router_w[l].T.astype(BF16), "router_bias": router_bias[l].astype(F32)[:, None],
            "w_gate": w_gate[l], "w_up": w_up[l], "w_down": w_down[l],
            "ws_gate": ws_gate[l].astype(BF16), "ws_up": ws_up[l].astype(BF16), "ws_down": ws_down[l].astype(BF16),
            "ln2_g": ln2_g[l].astype(F32)[None, :], "ln2_b": ln2_b[l].astype(F32)[None, :],
        }
        y_prompt, y_sample = _encoder_layer((y_prompt, y_sample), l, wp)
    return (y_prompt, y_sample)
```

```python
import functools
import math

import jax
import jax.numpy as jnp
from jax import lax
from jax.experimental import pallas as pl
from jax.experimental.pallas import tpu as pltpu
from jax.experimental.pallas import tpu_sc as plsc

D_MODEL = 1024
HEAD_DIM = 64
ROPE_THETA = 10000.0
BLOCK = 128
A_Q_HEADS = 8
A_KV_HEADS = 2
WINDOW = 128
B_HEADS = 4
N_EXPERTS = 256
TOP_K = 8
N_GROUPS = 8
TOPK_GROUPS = 4
GROUP_SIZE = N_EXPERTS // N_GROUPS
D_EXPERT = 256
ROUTED_SCALE = 2.5
EXPERT_ROWS = 256
EXPERT_SLOTS = 4
DEPTH = 1
DN_ALPHA = (2 * DEPTH) ** 0.25
LN_EPS = 1e-5
RMS_EPS = 1e-5
NEG = -1e30
LOG2E = math.log2(math.e)

LANES = 128
SC_CORES = 2
SC_SUBCORES = 16
SC_WINDOW = 128
SC_SPLIT = 2
PIECE = D_MODEL // 2 // SC_SPLIT
VMEM_LIMIT_BYTES = 56 * 1024 * 1024

F32 = jnp.float32
BF16 = jnp.bfloat16
_NT = (((1,), (1,)), ((), ()))


def _params(sem, vmem=VMEM_LIMIT_BYTES):
    return pltpu.CompilerParams(dimension_semantics=sem, vmem_limit_bytes=vmem)


_C_QA, _C_KA, _C_VA, _C_QD, _C_KD, _C_VD, _C_GA, _C_GB, _C_END = 0, 512, 768, 1024, 1536, 2048, 2560, 3584, 4608


def _prep_w_in(w_in):
    cuts = [0, 512, 640, 768, 1280, 1792, 2304, 3328, 4352]
    qa, ka, va, qd, kd, vd, ga, gb = [w_in[:, cuts[i]:cuts[i + 1]] for i in range(8)]
    dup = lambda w: jnp.concatenate([w[:, :64], w[:, :64], w[:, 64:], w[:, 64:]], axis=1)
    return jnp.concatenate([qa, dup(ka), dup(va), qd, kd, vd, ga, gb], axis=1).astype(BF16)


def _rope_tables(s):
    half = HEAD_DIM // 2
    inv = 1.0 / (ROPE_THETA ** (jnp.arange(half, dtype=F32) / half))
    ang = jnp.arange(s, dtype=F32)[:, None] * inv[None, :]
    cos, sin = jnp.cos(ang), jnp.sin(ang)
    return (jnp.concatenate([cos, cos, cos, cos], axis=1),
            jnp.concatenate([-sin, sin, -sin, sin], axis=1))


def _in_proj_kernel(x_ref, w_ref, cos_ref, sin_ref,
                    qa_ref, ka_ref, va_ref, qd_ref, kd_ref, vd_ref, ga_ref, gb_ref):
    xb = x_ref[...].astype(BF16)
    cos = cos_ref[...]
    sin = sin_ref[...]
    lane = lax.broadcasted_iota(jnp.int32, cos.shape, 1)
    first_half = (lane & (HEAD_DIM // 2)) == 0

    def proj(c0, c1):
        return jnp.dot(xb, w_ref[:, c0:c1], preferred_element_type=F32)

    def rope_store(u, out_ref, scale):
        for j in range(u.shape[1] // LANES):
            uj = u[:, LANES * j:LANES * (j + 1)]
            rot = jnp.where(first_half, pltpu.roll(uj, LANES - 32, 1), pltpu.roll(uj, 32, 1))
            r = uj * cos + rot * sin
            if scale != 1.0:
                r = r * scale
            out_ref[:, LANES * j:LANES * (j + 1)] = r.astype(out_ref.dtype)

    scale = HEAD_DIM ** -0.5
    rope_store(proj(_C_QA, _C_KA), qa_ref, scale)
    rope_store(proj(_C_KA, _C_VA), ka_ref, 1.0)
    va_ref[...] = proj(_C_VA, _C_QD).astype(va_ref.dtype)
    rope_store(proj(_C_QD, _C_KD), qd_ref, scale * LOG2E)
    rope_store(proj(_C_KD, _C_VD), kd_ref, 1.0)
    vd_ref[...] = proj(_C_VD, _C_GA).astype(vd_ref.dtype)
    ga_ref[...] = jax.nn.sigmoid(proj(_C_GA, _C_GB)).astype(ga_ref.dtype)
    gb_ref[...] = jax.nn.sigmoid(proj(_C_GB, _C_END)).astype(gb_ref.dtype)


def _in_proj(x2, w_perm, cos_t, sin_t, seq, tm):
    t = x2.shape[0]
    nseq = seq // tm
    row = lambda i: (i, 0)
    widths = (512, 256, 256, 512, 512, 512, 1024, 1024)
    return pl.pallas_call(
        _in_proj_kernel,
        grid=(t // tm,),
        in_specs=[pl.BlockSpec((tm, D_MODEL), row),
                  pl.BlockSpec((D_MODEL, _C_END), lambda i: (0, 0)),
                  pl.BlockSpec((tm, LANES), lambda i: (i % nseq, 0)),
                  pl.BlockSpec((tm, LANES), lambda i: (i % nseq, 0))],
        out_specs=[pl.BlockSpec((tm, w), row) for w in widths],
        out_shape=[jax.ShapeDtypeStruct((t, w), BF16) for w in widths],
        compiler_params=_params(("arbitrary",)),
        name="in_proj",
    )(x2, w_perm, cos_t, sin_t)


def _attn_a_kernel(sink_ref, q_ref, kp_ref, kc_ref, kn_ref, vp_ref, vc_ref, vn_ref, o_ref, *, seq, tq):
    i = pl.program_id(1)
    kk = jnp.concatenate([kp_ref[...], kc_ref[...], kn_ref[...]], axis=0)
    vv = jnp.concatenate([vp_ref[...], vc_ref[...], vn_ref[...]], axis=0)
    lane = lax.broadcasted_iota(jnp.int32, (kk.shape[0], LANES), 1)
    lo = lane < HEAD_DIM
    zero = jnp.zeros((kk.shape[0], LANES), BF16)
    k_lo = [jnp.where(lo, kk[:, LANES * h:LANES * (h + 1)], zero) for h in range(A_KV_HEADS)]
    k_hi = [jnp.where(lo, zero, kk[:, LANES * h:LANES * (h + 1)]) for h in range(A_KV_HEADS)]
    v_lo = [jnp.where(lo, vv[:, LANES * h:LANES * (h + 1)], zero) for h in range(A_KV_HEADS)]
    v_hi = [jnp.where(lo, zero, vv[:, LANES * h:LANES * (h + 1)]) for h in range(A_KV_HEADS)]

    qi = lax.broadcasted_iota(jnp.int32, (BLOCK, 3 * BLOCK), 0)
    kj = lax.broadcasted_iota(jnp.int32, (BLOCK, 3 * BLOCK), 1)
    band = jnp.abs(kj - BLOCK - qi) <= WINDOW
    head_of_row = lax.broadcasted_iota(jnp.int32, (A_Q_HEADS * BLOCK, 1), 0) // BLOCK
    snk = jnp.zeros((A_Q_HEADS * BLOCK, 1), F32)
    for head in range(A_Q_HEADS):
        snk = jnp.where(head_of_row == head, sink_ref[head], snk)
    for j in range(tq // BLOCK):
        kpos = i * tq + (j - 1) * BLOCK + kj
        mask = band & (kpos >= 0) & (kpos < seq)
        r0, r1 = j * BLOCK, (j + 3) * BLOCK
        pieces = []
        for c in range(A_Q_HEADS // 2):
            q2 = q_ref[j * BLOCK:(j + 1) * BLOCK, LANES * c:LANES * (c + 1)]
            for kx in (k_lo[c // 2], k_hi[c // 2]):
                s = lax.dot_general(q2, kx[r0:r1], _NT, preferred_element_type=F32)
                pieces.append(jnp.where(mask, s, NEG))
        s = jnp.concatenate(pieces, axis=0)
        m = jnp.maximum(jnp.max(s, axis=-1, keepdims=True), snk)
        p = jnp.exp(s - m)
        inv = 1.0 / (jnp.sum(p, axis=-1, keepdims=True) + jnp.exp(snk - m))
        pb = p.astype(BF16)
        for c in range(A_Q_HEADS // 2):
            out = None
            for half, vx in enumerate((v_lo[c // 2], v_hi[c // 2])):
                rows = slice((2 * c + half) * BLOCK, (2 * c + half + 1) * BLOCK)
                pv = jnp.dot(pb[rows], vx[r0:r1], preferred_element_type=F32) * inv[rows]
                out = pv if out is None else out + pv
            o_ref[j * BLOCK:(j + 1) * BLOCK, LANES * c:LANES * (c + 1)] = out.astype(o_ref.dtype)


def _attn_a(qa, ka, va, sink, bsz, seq, tq):
    t = qa.shape[0]
    nq = seq // tq
    nb = seq // BLOCK
    r = tq // BLOCK
    cur = lambda b, i: (b * nq + i, 0)
    prev = lambda b, i: (b * nb + jnp.maximum(i * r - 1, 0), 0)
    nxt = lambda b, i: (b * nb + jnp.minimum((i + 1) * r, nb - 1), 0)
    kv_specs = [pl.BlockSpec((BLOCK, 256), prev), pl.BlockSpec((tq, 256), cur), pl.BlockSpec((BLOCK, 256), nxt)]
    return pl.pallas_call(
        functools.partial(_attn_a_kernel, seq=seq, tq=tq),
        grid=(bsz, nq),
        in_specs=[pl.BlockSpec(memory_space=pltpu.SMEM), pl.BlockSpec((tq, 512), cur)] + kv_specs + kv_specs,
        out_specs=pl.BlockSpec((tq, 512), cur),
        out_shape=jax.ShapeDtypeStruct((t, 512), BF16),
        compiler_params=_params(("arbitrary", "arbitrary")),
        name="attn_a",
    )(sink, qa, ka, ka, ka, va, va, va)


def _attn_b_kernel(lam_ref, g_ref, q_ref, k_ref, v_ref, o_ref, acc_ref, s_ref, *, tq, tk, lam_init):
    seq = q_ref.shape[0]
    n_q, n_chunks = seq // tq, seq // tk
    lane = lax.broadcasted_iota(jnp.int32, (tk, LANES), 1)
    lo = lane < HEAD_DIM
    zero = jnp.zeros((tk, LANES), BF16)
    ones_col = jnp.where(lane == 0, 1.0, 0.0).astype(BF16)
    minf = jnp.full((tq, 1), -jnp.inf, F32)

    lp = lam_ref[...]
    lam = (jnp.exp(jnp.sum(lp[0:1] * lp[1:2], axis=-1, keepdims=True))
           - jnp.exp(jnp.sum(lp[2:3] * lp[3:4], axis=-1, keepdims=True)) + lam_init)

    def q_tile(qi):
        return q_ref[pl.ds(pl.multiple_of(qi * tq, tq), tq), :]

    def scores(q, kc):
        k = k_ref[kc * tk:(kc + 1) * tk, :]
        return [lax.dot_general(q, kx, _NT, preferred_element_type=F32)
                for kx in (jnp.where(lo, k, zero), jnp.where(lo, zero, k))]

    def consume(kc, maxes):
        v_aug = jnp.concatenate([v_ref[kc * tk:(kc + 1) * tk, :], ones_col], axis=1)
        new = []
        for idx in range(2):
            s = s_ref[idx]
            m_new = jnp.maximum(maxes[idx], jnp.max(s, axis=-1, keepdims=True))
            alpha = jnp.exp2(maxes[idx] - m_new)
            p = jnp.exp2(s - m_new)
            acc_ref[idx] = alpha * acc_ref[idx] + jnp.dot(p.astype(BF16), v_aug, preferred_element_type=F32)
            new.append(m_new)
        return tuple(new)

    def tile(qi, carry):
        q = q_tile(qi)
        q_next = q_tile(jnp.minimum(qi + 1, n_q - 1))
        acc_ref[...] = jnp.zeros_like(acc_ref)
        maxes = (minf, minf)
        for kc in range(n_chunks):
            nxt = scores(q, kc + 1) if kc + 1 < n_chunks else scores(q_next, 0)
            maxes = consume(kc, maxes)
            s_ref[0], s_ref[1] = nxt
        acc1, acc2 = acc_ref[0], acc_ref[1]
        l1, l2 = acc1[:, LANES:LANES + 1], acc2[:, LANES:LANES + 1]
        o = acc1[:, :LANES] * (1.0 / l1) - lam * (acc2[:, :LANES] * (1.0 / l2))
        o = o * lax.rsqrt(jnp.mean(o * o, axis=-1, keepdims=True) + RMS_EPS)
        o = o * g_ref[...] * (1.0 - lam_init)
        o_ref[pl.ds(pl.multiple_of(qi * tq, tq), tq), :] = o.astype(o_ref.dtype)
        return carry

    s_ref[0], s_ref[1] = scores(q_tile(0), 0)
    lax.fori_loop(0, n_q, tile, 0)


def _attn_b(qd, kd, vd, lam_p, subln_g, bsz, seq, tq, tk, lam_init):
    t = qd.shape[0]
    seq_block = pl.BlockSpec((seq, LANES), lambda b, h: (b, h))
    return pl.pallas_call(
        functools.partial(_attn_b_kernel, tq=tq, tk=tk, lam_init=lam_init),
        grid=(bsz, B_HEADS),
        in_specs=[pl.BlockSpec((4, HEAD_DIM), lambda b, h: (0, 0)),
                  pl.BlockSpec((1, LANES), lambda b, h: (0, 0)),
                  seq_block, seq_block, seq_block],
        out_specs=seq_block,
        out_shape=jax.ShapeDtypeStruct((t, 512), BF16),
        scratch_shapes=[pltpu.VMEM((2, tq, 2 * LANES), F32), pltpu.VMEM((2, tq, tk), F32)],
        compiler_params=_params(("arbitrary", "arbitrary")),
        name="attn_b",
    )(lam_p, subln_g, qd, kd, vd)


HALF = D_MODEL // 2
_HI_MASK = -65536


def _pack_pair(lo, hi):
    lo_bits = lax.bitcast_convert_type(lo.astype(BF16).astype(F32), jnp.int32)
    hi_bits = lax.bitcast_convert_type(hi.astype(BF16).astype(F32), jnp.int32)
    return lax.shift_right_logical(lo_bits, 16) | hi_bits


def _unpack_pair(w):
    lo = lax.bitcast_convert_type(lax.shift_left(w, 16), F32)
    hi = lax.bitcast_convert_type(w & _HI_MASK, F32)
    return lo, hi


def _layer_norm(y, g, b):
    mu = jnp.mean(y, axis=-1, keepdims=True)
    d = y - mu
    var = jnp.mean(d * d, axis=-1, keepdims=True)
    return d * lax.rsqrt(var + LN_EPS) * g + b


def _first_argmax(vals, rowf, big):
    m = jnp.max(vals, axis=0, keepdims=True)
    idx = jnp.min(jnp.where(vals == m, rowf, big), axis=0, keepdims=True)
    return m, idx


def _post_attn_kernel(x_ref, oa_ref, ob_ref, ga_ref, gb_ref, woa_ref, wob_ref, wout_ref, g1_ref, b1_ref,
                      rwt_ref, rb_ref, wsg_ref, wsu_ref, wsd_ref, tri_ref, cnt_in_ref,
                      x1_ref, base_ref, eidx_ref, wts_ref, rank_ref, cnt_ref, cnt_scr):
    i = pl.program_id(0)
    tm = x_ref.shape[0]

    @pl.when(i == 0)
    def _():
        cnt_scr[...] = cnt_in_ref[...]

    a = jnp.dot(oa_ref[...], woa_ref[...], preferred_element_type=F32)
    b = jnp.dot(ob_ref[...], wob_ref[...], preferred_element_type=F32)
    merged = ga_ref[...].astype(F32) * a + gb_ref[...].astype(F32) * b
    mix = jnp.dot(merged.astype(BF16), wout_ref[...], preferred_element_type=F32)
    x1 = _layer_norm(DN_ALPHA * x_ref[...] + mix, g1_ref[...], b1_ref[...])
    x1p = _pack_pair(x1[:, :HALF], x1[:, HALF:])
    for piece in range(SC_SPLIT):
        x1_ref[piece] = x1p[:, piece * PIECE:(piece + 1) * PIECE]
    x1b = x1.astype(BF16)

    hg = jnp.dot(x1b, wsg_ref[...], preferred_element_type=F32)
    hu = jnp.dot(x1b, wsu_ref[...], preferred_element_type=F32)
    hs = (hg * jax.nn.sigmoid(hg)) * hu
    shared = jnp.dot(hs.astype(BF16), wsd_ref[...], preferred_element_type=F32)
    base_ref[...] = DN_ALPHA * x1 + shared

    logits = lax.dot_general(rwt_ref[...], x1b, _NT, preferred_element_type=F32)
    scores = jax.nn.sigmoid(logits)
    choice = scores + rb_ref[...]
    ninf = -jnp.inf
    grow = lax.broadcasted_iota(jnp.int32, (GROUP_SIZE, tm), 0).astype(F32)
    gscore = []
    for g in range(N_GROUPS):
        blk = choice[GROUP_SIZE * g:GROUP_SIZE * (g + 1)]
        m1, i1 = _first_argmax(blk, grow, float(GROUP_SIZE))
        m2 = jnp.max(jnp.where(grow == i1, ninf, blk), axis=0, keepdims=True)
        gscore.append(m1 + m2)
    selected = [jnp.zeros((1, tm), F32) for _ in range(N_GROUPS)]
    work = list(gscore)
    for _ in range(TOPK_GROUPS):
        best = work[0]
        for g in range(1, N_GROUPS):
            best = jnp.maximum(best, work[g])
        taken = jnp.zeros((1, tm), F32)
        for g in range(N_GROUPS):
            hit = jnp.where((work[g] == best) & (taken == 0.0), 1.0, 0.0)
            taken = jnp.maximum(taken, hit)
            selected[g] = jnp.maximum(selected[g], hit)
            work[g] = jnp.where(hit > 0.0, ninf, work[g])
    masked = jnp.concatenate(
        [jnp.where(selected[g] > 0.0, choice[GROUP_SIZE * g:GROUP_SIZE * (g + 1)], ninf) for g in range(N_GROUPS)],
        axis=0)

    rowf = lax.broadcasted_iota(jnp.int32, (N_EXPERTS, tm), 0).astype(F32)
    hits, idxs, ws = [], [], []
    for _ in range(TOP_K):
        _, idx = _first_argmax(masked, rowf, float(N_EXPERTS))
        hit = rowf == idx
        hits.append(hit)
        idxs.append(idx)
        ws.append(jnp.sum(jnp.where(hit, scores, 0.0), axis=0, keepdims=True))
        masked = jnp.where(hit, ninf, masked)
    wsum = ws[0]
    for k in range(1, TOP_K):
        wsum = wsum + ws[k]

    member = hits[0]
    for k in range(1, TOP_K):
        member = member | hits[k]
    member_f = jnp.where(member, 1.0, 0.0)
    before = jnp.dot(member_f.astype(BF16), tri_ref[...], preferred_element_type=F32) + cnt_scr[...]
    for k in range(TOP_K):
        eidx_ref[k:k + 1, :] = idxs[k].astype(jnp.int32)
        wts_ref[k:k + 1, :] = ws[k] / wsum * ROUTED_SCALE
        rank_ref[k:k + 1, :] = jnp.sum(jnp.where(hits[k], before, 0.0), axis=0, keepdims=True).astype(jnp.int32)
    cnt_scr[...] = cnt_scr[...] + jnp.sum(member_f, axis=1, keepdims=True)
    cnt_ref[...] = jnp.broadcast_to(cnt_scr[...], cnt_ref.shape).astype(jnp.int32)


def _post_attn(x2, oa, ob, ga, gb, woa, wob, wout, g1, b1, rwt, rb, wsg, wsu, wsd, cnt_in, tm):
    t = x2.shape[0]
    tri = jnp.triu(jnp.ones((tm, tm), F32), k=1).astype(BF16)
    row = lambda i: (i, 0)
    col = lambda i: (0, i)
    full = lambda i: (0, 0)
    wspec = lambda arr: pl.BlockSpec(arr.shape, full)
    weights = (woa, wob, wout, g1, b1, rwt, rb, wsg, wsu, wsd, tri, cnt_in)
    return pl.pallas_call(
        _post_attn_kernel,
        grid=(t // tm,),
        in_specs=[pl.BlockSpec((tm, D_MODEL), row), pl.BlockSpec((tm, 512), row), pl.BlockSpec((tm, 512), row),
                  pl.BlockSpec((tm, D_MODEL), row), pl.BlockSpec((tm, D_MODEL), row)] + [wspec(w) for w in weights],
        out_specs=[pl.BlockSpec((SC_SPLIT, tm, PIECE), lambda i: (0, i, 0)), pl.BlockSpec((tm, D_MODEL), row),
                   pl.BlockSpec((TOP_K, tm), col), pl.BlockSpec((TOP_K, tm), col), pl.BlockSpec((TOP_K, tm), col),
                   pl.BlockSpec((N_EXPERTS, LANES), full)],
        out_shape=[jax.ShapeDtypeStruct((SC_SPLIT, t, PIECE), jnp.int32), jax.ShapeDtypeStruct((t, D_MODEL), F32),
                   jax.ShapeDtypeStruct((TOP_K, t), jnp.int32), jax.ShapeDtypeStruct((TOP_K, t), F32),
                   jax.ShapeDtypeStruct((TOP_K, t), jnp.int32), jax.ShapeDtypeStruct((N_EXPERTS, LANES), jnp.int32)],
        scratch_shapes=[pltpu.VMEM((N_EXPERTS, 1), F32)],
        compiler_params=_params(("arbitrary",)),
        name="post_attn",
    )(x2, oa, ob, ga, gb, *weights)


def _dest_kernel(eidx_ref, rank_ref, pstart_ref, dest_ref):
    tm = eidx_ref.shape[1]
    rows = lax.broadcasted_iota(jnp.int32, (N_EXPERTS, tm), 0)
    pstart = pstart_ref[...]
    for k in range(TOP_K):
        hit = rows == eidx_ref[k:k + 1, :]
        start = jnp.sum(jnp.where(hit, pstart, 0).astype(F32), axis=0, keepdims=True).astype(jnp.int32)
        dest_ref[k:k + 1, :] = start + rank_ref[k:k + 1, :]


def _dest(eidx, rank, pstart, tm):
    t = eidx.shape[1]
    return pl.pallas_call(
        _dest_kernel,
        grid=(t // tm,),
        in_specs=[pl.BlockSpec((TOP_K, tm), lambda i: (0, i)), pl.BlockSpec((TOP_K, tm), lambda i: (0, i)),
                  pl.BlockSpec((N_EXPERTS, 1), lambda i: (0, 0))],
        out_specs=pl.BlockSpec((TOP_K, tm), lambda i: (0, i)),
        out_shape=jax.ShapeDtypeStruct((TOP_K, t), jnp.int32),
        compiler_params=_params(("arbitrary",)),
        name="dest",
    )(eidx, rank, pstart)


def _sc_mesh():
    return plsc.VectorSubcoreMesh(core_axis_name="core", subcore_axis_name="subcore",
                                  num_cores=SC_CORES, num_subcores=SC_SUBCORES)


def _sc_scatter_rows(xs_groups, dests, n_rows):
    n_groups = len(xs_groups)

    @functools.partial(pl.kernel, out_type=jax.ShapeDtypeStruct((SC_SPLIT, n_rows, PIECE), jnp.int32),
                       mesh=_sc_mesh(), scratch_types=[])
    def scatter(*refs):
        out_hbm = refs[2 * n_groups]
        for g in range(n_groups):
            x_hbm, idx_hbm = refs[2 * g], refs[2 * g + 1]
            windows = x_hbm.shape[1] // SC_WINDOW
            for piece in range(SC_SPLIT):
                def window(x_vmem, idx_vmem, piece=piece):
                    pltpu.sync_copy(x_vmem, out_hbm.at[piece].at[idx_vmem.at[0]])

                pltpu.emit_pipeline(
                    window,
                    grid=(TOP_K * windows,),
                    in_specs=[pl.BlockSpec((SC_WINDOW, PIECE), lambda i, windows=windows: (i % windows, 0)),
                              pl.BlockSpec((1, SC_WINDOW), lambda i: (0, i))],
                    out_specs=[],
                    core_axis_name=("core", "subcore"),
                    dimension_semantics=(pltpu.PARALLEL,),
                )(x_hbm.at[piece], idx_hbm)

    args = []
    for x, d in zip(xs_groups, dests):
        args += [x, d.reshape(1, -1)]
    return scatter(*args)


def _experts_kernel(cb_ref, cnt_ref, xs_ref, wg_ref, wu_ref, wd_ref, ys_ref,
                    wg_scr, wu_scr, wd_scr, xbuf, ybuf, sem_in, sem_out):
    e = pl.program_id(0)
    g0, g1, total = cb_ref[e], cb_ref[e + 1], cb_ref[N_EXPERTS]

    def rows(g):
        return pl.ds(pl.multiple_of(g * EXPERT_ROWS, EXPERT_ROWS), EXPERT_ROWS)

    def x_copy(g):
        s = g % EXPERT_SLOTS
        return pltpu.make_async_copy(xs_ref.at[:, rows(g), :], xbuf.at[s], sem_in.at[s])

    def y_copy(g):
        s = g % EXPERT_SLOTS
        return pltpu.make_async_copy(ybuf.at[s], ys_ref.at[:, rows(g), :], sem_out.at[s])

    @pl.when(e == 0)
    def _():
        for g in range(EXPERT_SLOTS - 1):
            pl.when(g < total)(lambda g=g: x_copy(g).start())

    @pl.when(g1 > g0)
    def _():
        wg_scr[...] = wg_ref[0].astype(BF16)
        wu_scr[...] = wu_ref[0].astype(BF16)
        wd_scr[...] = wd_ref[0].astype(BF16)

    def block(g, carry):
        slot = g % EXPERT_SLOTS

        @pl.when(g + EXPERT_SLOTS - 1 < total)
        def _():
            x_copy(g + EXPERT_SLOTS - 1).start()

        x_copy(g).wait()

        @pl.when(g >= EXPERT_SLOTS)
        def _():
            y_copy(g - EXPERT_SLOTS).wait()

        valid = lax.broadcasted_iota(jnp.int32, (EXPERT_ROWS, PIECE), 0) < cnt_ref[e] - (g - g0) * EXPERT_ROWS
        hg = hu = None
        for piece in range(SC_SPLIT):
            for part, col0 in zip(_unpack_pair(xbuf[slot, piece]), (piece * PIECE, HALF + piece * PIECE)):
                xb = jnp.where(valid, part, 0.0).astype(BF16)
                pg = jnp.dot(xb, wg_scr[col0:col0 + PIECE], preferred_element_type=F32)
                pu = jnp.dot(xb, wu_scr[col0:col0 + PIECE], preferred_element_type=F32)
                hg = pg if hg is None else hg + pg
                hu = pu if hu is None else hu + pu
        h = ((hg * jax.nn.sigmoid(hg)) * hu).astype(BF16)
        y = _pack_pair(jnp.dot(h, wd_scr[:, :HALF], preferred_element_type=F32),
                       jnp.dot(h, wd_scr[:, HALF:], preferred_element_type=F32))
        for piece in range(SC_SPLIT):
            ybuf[slot, piece] = y[:, piece * PIECE:(piece + 1) * PIECE]
        y_copy(g).start()
        return carry

    lax.fori_loop(g0, g1, block, 0)

    @pl.when(e == N_EXPERTS - 1)
    def _():
        for back in range(EXPERT_SLOTS, 0, -1):
            pl.when(total >= back)(lambda back=back: y_copy(total - back).wait())


def _experts(cum_blocks, counts, xs, w_gate, w_up, w_down):
    n_rows = xs.shape[1]
    wmap = lambda e, cb, cnt: (e, 0, 0)
    grid_spec = pltpu.PrefetchScalarGridSpec(
        num_scalar_prefetch=2,
        grid=(N_EXPERTS,),
        in_specs=[pl.BlockSpec(memory_space=pl.ANY),
                  pl.BlockSpec((1, D_MODEL, D_EXPERT), wmap),
                  pl.BlockSpec((1, D_MODEL, D_EXPERT), wmap),
                  pl.BlockSpec((1, D_EXPERT, D_MODEL), wmap)],
        out_specs=pl.BlockSpec(memory_space=pl.ANY),
        scratch_shapes=[pltpu.VMEM((D_MODEL, D_EXPERT), BF16), pltpu.VMEM((D_MODEL, D_EXPERT), BF16),
                        pltpu.VMEM((D_EXPERT, D_MODEL), BF16),
                        pltpu.VMEM((EXPERT_SLOTS, SC_SPLIT, EXPERT_ROWS, PIECE), jnp.int32),
                        pltpu.VMEM((EXPERT_SLOTS, SC_SPLIT, EXPERT_ROWS, PIECE), jnp.int32),
                        pltpu.SemaphoreType.DMA((EXPERT_SLOTS,)), pltpu.SemaphoreType.DMA((EXPERT_SLOTS,))])
    return pl.pallas_call(
        _experts_kernel,
        grid_spec=grid_spec,
        out_shape=jax.ShapeDtypeStruct((SC_SPLIT, n_rows, PIECE), jnp.int32),
        compiler_params=_params(("arbitrary",)),
        name="experts",
    )(cum_blocks, counts, xs, w_gate, w_up, w_down)


def _sc_gather_rows(table, idx):
    n, d = idx.shape[0], table.shape[2]

    @functools.partial(pl.kernel, out_type=jax.ShapeDtypeStruct((SC_SPLIT, n, d), table.dtype), mesh=_sc_mesh(),
                       scratch_types=[])
    def gather(table_hbm, idx_hbm, out_hbm):
        for piece in range(SC_SPLIT):
            def window(idx_vmem, out_vmem, piece=piece):
                pltpu.sync_copy(table_hbm.at[piece].at[idx_vmem.at[0]], out_vmem)

            pltpu.emit_pipeline(
                window,
                grid=(n // SC_WINDOW,),
                in_specs=[pl.BlockSpec((1, SC_WINDOW), lambda i: (0, i))],
                out_specs=[pl.BlockSpec((SC_WINDOW, d), lambda i: (i, 0))],
                core_axis_name=("core", "subcore"),
                dimension_semantics=(pltpu.PARALLEL,),
            )(idx_hbm, out_hbm.at[piece])

    return gather(table, idx.reshape(1, n))


def _combine_kernel(w_ref, base_ref, g2_ref, b2_ref, rows_ref, o_ref):
    w = w_ref[...]
    acc = [[base_ref[:, half * HALF + p * PIECE:half * HALF + (p + 1) * PIECE] for p in range(SC_SPLIT)]
           for half in range(2)]
    for k in range(TOP_K):
        wk = w[:, k:k + 1]
        for p in range(SC_SPLIT):
            lo, hi = _unpack_pair(rows_ref[p, k])
            acc[0][p] = acc[0][p] + lo * wk
            acc[1][p] = acc[1][p] + hi * wk
    o_ref[...] = _layer_norm(jnp.concatenate(acc[0] + acc[1], axis=1), g2_ref[...], b2_ref[...])


def _combine(wts_tk, base, g2, b2, rows, tm):
    t = base.shape[0]
    return pl.pallas_call(
        _combine_kernel,
        grid=(t // tm,),
        in_specs=[pl.BlockSpec((tm, TOP_K), lambda i: (i, 0)),
                  pl.BlockSpec((tm, D_MODEL), lambda i: (i, 0)),
                  pl.BlockSpec((1, D_MODEL), lambda i: (0, 0)),
                  pl.BlockSpec((1, D_MODEL), lambda i: (0, 0)),
                  pl.BlockSpec((SC_SPLIT, TOP_K, tm, PIECE), lambda i: (0, 0, i, 0))],
        out_specs=pl.BlockSpec((tm, D_MODEL), lambda i: (i, 0)),
        out_shape=jax.ShapeDtypeStruct((t, D_MODEL), F32),
        compiler_params=_params(("arbitrary",)),
        name="combine",
    )(wts_tk, base, g2, b2, rows)


def _tile(n, pref):
    t = pref
    while n % t:
        t //= 2
    return t


def _token_mixers(x, layer_idx, wp, cnt_in):
    bsz, seq, _ = x.shape
    t = bsz * seq
    x2 = x.reshape(t, D_MODEL)
    lam_init = 0.8 - 0.6 * math.exp(-0.3 * layer_idx)

    cos_t, sin_t = _rope_tables(seq)
    qa, ka, va, qd, kd, vd, ga, gb = _in_proj(x2, wp["w_in"], cos_t, sin_t, seq, _tile(seq, 256))
    oa = _attn_a(qa, ka, va, wp["sink"], bsz, seq, _tile(seq, 512))
    ob = _attn_b(qd, kd, vd, wp["lam_p"], wp["subln_g"], bsz, seq, _tile(seq, 512), _tile(seq, 1024), lam_init)
    return _post_attn(
        x2, oa, ob, ga, gb, wp["w_o_a"], wp["w_o_b"], wp["w_out"], wp["ln1_g"], wp["ln1_b"],
        wp["router_wt"], wp["router_bias"], wp["ws_gate"], wp["ws_up"], wp["ws_down"], cnt_in, _tile(t, 512))


def _encoder_layer(xs_in, layer_idx, wp):
    cnt_in = jnp.zeros((N_EXPERTS, 1), F32)
    routed = []
    for x in xs_in:
        x1, base, eidx, wts, rank, cnt = _token_mixers(x, layer_idx, wp, cnt_in)
        cnt_in = cnt[:, :1].astype(F32)
        routed.append((x1, base, eidx, wts, rank))

    n_assign = sum(x.shape[0] * x.shape[1] for x in xs_in) * TOP_K
    n_blk = -(-(n_assign + N_EXPERTS * (EXPERT_ROWS - 1)) // EXPERT_ROWS)
    counts = cnt[:, 0]
    padded = ((counts + EXPERT_ROWS - 1) // EXPERT_ROWS) * EXPERT_ROWS
    pend = jnp.cumsum(padded)
    pstart = pend - padded
    cum_blocks = jnp.concatenate([jnp.zeros((1,), jnp.int32), (pend // EXPERT_ROWS).astype(jnp.int32)])
    pstart_col = pstart.astype(jnp.int32)[:, None]

    dests = [_dest(eidx, rank, pstart_col, _tile(eidx.shape[1], 512)) for _, _, eidx, _, rank in routed]
    xs = _sc_scatter_rows([r[0] for r in routed], dests, n_blk * EXPERT_ROWS)
    ys = _experts(cum_blocks, counts.astype(jnp.int32), xs, wp["w_gate"], wp["w_up"], wp["w_down"])
    outs = []
    for x, dest, (x1, base, eidx, wts, rank) in zip(xs_in, dests, routed):
        t = base.shape[0]
        rows = _sc_gather_rows(ys, dest.reshape(-1)).reshape(SC_SPLIT, TOP_K, t, PIECE)
        y = _combine(wts.T, base, wp["ln2_g"], wp["ln2_b"], rows, _tile(t, 256))
        outs.append(y.reshape(x.shape))
    return outs


def kernel(x_prompt, x_sample, w_in, attn_sink, lambda_q1, lambda_k1, lambda_q2, lambda_k2, subln_g, w_o_a, w_o_b, w_out, ln1_g, ln1_b, router_w, router_bias, w_gate, w_up, w_down, ws_gate, ws_up, ws_down, ln2_g, ln2_b):
    y_prompt, y_sample = x_prompt, x_sample
    for l in range(DEPTH):
        wp = {
            "w_in": _prep_w_in(w_in[l]),
            "sink": attn_sink[l].astype(F32),
            "lam_p": jnp.stack([lambda_q1[l], lambda_k1[l], lambda_q2[l], lambda_k2[l]]).astype(F32),
            "subln_g": subln_g[l].astype(F32)[None, :],
            "w_o_a": w_o_a[l].astype(BF16), "w_o_b": w_o_b[l].astype(BF16), "w_out": w_out[l].astype(BF16),
            "ln1_g": ln1_g[l].astype(F32)[None, :], "ln1_b": ln1_b[l].astype(F32)[None, :],
            "router_wt": router_w[l].T.astype(BF16), "router_bias": router_bias[l].astype(F32)[:, None],
            "w_gate": w_gate[l], "w_up": w_up[l], "w_down": w_down[l],
            "ws_gate": ws_gate[l].astype(BF16), "ws_up": ws_up[l].astype(BF16), "ws_down": ws_down[l].astype(BF16),
            "ln2_g": ln2_g[l].astype(F32)[None, :], "ln2_b": ln2_b[l].astype(F32)[None, :],
        }
        y_prompt, y_sample = _encoder_layer((y_prompt, y_sample), l, wp)
    return (y_prompt, y_sample)
```

```python
import functools
import math

import jax
import jax.numpy as jnp
from jax import lax
from jax.experimental import pallas as pl
from jax.experimental.pallas import tpu as pltpu
from jax.experimental.pallas import tpu_sc as plsc

D_MODEL = 1024
HEAD_DIM = 64
ROPE_THETA = 10000.0
BLOCK = 128
A_Q_HEADS = 8
A_KV_HEADS = 2
WINDOW = 128
B_HEADS = 4
N_EXPERTS = 256
TOP_K = 8
N_GROUPS = 8
TOPK_GROUPS = 4
GROUP_SIZE = N_EXPERTS // N_GROUPS
D_EXPERT = 256
ROUTED_SCALE = 2.5
EXPERT_ROWS = 256
EXPERT_SLOTS = 4
DEPTH = 1
DN_ALPHA = (2 * DEPTH) ** 0.25
LN_EPS = 1e-5
RMS_EPS = 1e-5
NEG = -1e30
LOG2E = math.log2(math.e)

LANES = 128
SC_CORES = 2
SC_SUBCORES = 16
SC_WINDOW = 128
SC_SPLIT = 2
PIECE = D_MODEL // 2 // SC_SPLIT
VMEM_LIMIT_BYTES = 56 * 1024 * 1024

F32 = jnp.float32
BF16 = jnp.bfloat16
_NT = (((1,), (1,)), ((), ()))


def _params(sem, vmem=VMEM_LIMIT_BYTES):
    return pltpu.CompilerParams(dimension_semantics=sem, vmem_limit_bytes=vmem)


_C_QA, _C_KA, _C_VA, _C_QD, _C_KD, _C_VD, _C_GA, _C_GB, _C_END = 0, 512, 768, 1024, 1536, 2048, 2560, 3584, 4608


def _prep_w_in(w_in):
    cuts = [0, 512, 640, 768, 1280, 1792, 2304, 3328, 4352]
    qa, ka, va, qd, kd, vd, ga, gb = [w_in[:, cuts[i]:cuts[i + 1]] for i in range(8)]
    dup = lambda w: jnp.concatenate([w[:, :64], w[:, :64], w[:, 64:], w[:, 64:]], axis=1)
    return jnp.concatenate([qa, dup(ka), dup(va), qd, kd, vd, ga, gb], axis=1).astype(BF16)


def _rope_tables(s):
    half = HEAD_DIM // 2
    inv = 1.0 / (ROPE_THETA ** (jnp.arange(half, dtype=F32) / half))
    ang = jnp.arange(s, dtype=F32)[:, None] * inv[None, :]
    cos, sin = jnp.cos(ang), jnp.sin(ang)
    return (jnp.concatenate([cos, cos, cos, cos], axis=1),
            jnp.concatenate([-sin, sin, -sin, sin], axis=1))


def _in_proj_kernel(x_ref, w_ref, cos_ref, sin_ref,
                    qa_ref, ka_ref, va_ref, qd_ref, kd_ref, vd_ref, ga_ref, gb_ref):
    xb = x_ref[...].astype(BF16)
    cos = cos_ref[...]
    sin = sin_ref[...]
    lane = lax.broadcasted_iota(jnp.int32, cos.shape, 1)
    first_half = (lane & (HEAD_DIM // 2)) == 0

    def proj(c0, c1):
        return jnp.dot(xb, w_ref[:, c0:c1], preferred_element_type=F32)

    def rope_store(u, out_ref, scale):
        for j in range(u.shape[1] // LANES):
            uj = u[:, LANES * j:LANES * (j + 1)]
            rot = jnp.where(first_half, pltpu.roll(uj, LANES - 32, 1), pltpu.roll(uj, 32, 1))
            r = uj * cos + rot * sin
            if scale != 1.0:
                r = r * scale
            out_ref[:, LANES * j:LANES * (j + 1)] = r.astype(out_ref.dtype)

    scale = HEAD_DIM ** -0.5
    rope_store(proj(_C_QA, _C_KA), qa_ref, scale)
    rope_store(proj(_C_KA, _C_VA), ka_ref, 1.0)
    va_ref[...] = proj(_C_VA, _C_QD).astype(va_ref.dtype)
    rope_store(proj(_C_QD, _C_KD), qd_ref, scale * LOG2E)
    rope_store(proj(_C_KD, _C_VD), kd_ref, 1.0)
    vd_ref[...] = proj(_C_VD, _C_GA).astype(vd_ref.dtype)
    ga_ref[...] = jax.nn.sigmoid(proj(_C_GA, _C_GB)).astype(ga_ref.dtype)
    gb_ref[...] = jax.nn.sigmoid(proj(_C_GB, _C_END)).astype(gb_ref.dtype)


def _in_proj(x2, w_perm, cos_t, sin_t, seq, tm):
    t = x2.shape[0]
    nseq = seq // tm
    row = lambda i: (i, 0)
    widths = (512, 256, 256, 512, 512, 512, 1024, 1024)
    return pl.pallas_call(
        _in_proj_kernel,
        grid=(t // tm,),
        in_specs=[pl.BlockSpec((tm, D_MODEL), row),
                  pl.BlockSpec((D_MODEL, _C_END), lambda i: (0, 0)),
                  pl.BlockSpec((tm, LANES), lambda i: (i % nseq, 0)),
                  pl.BlockSpec((tm, LANES), lambda i: (i % nseq, 0))],
        out_specs=[pl.BlockSpec((tm, w), row) for w in widths],
        out_shape=[jax.ShapeDtypeStruct((t, w), BF16) for w in widths],
        compiler_params=_params(("arbitrary",)),
        name="in_proj",
    )(x2, w_perm, cos_t, sin_t)


def _attn_a_kernel(sink_ref, q_ref, kp_ref, kc_ref, kn_ref, vp_ref, vc_ref, vn_ref, o_ref, *, seq, tq):
    i = pl.program_id(1)
    kk = jnp.concatenate([kp_ref[...], kc_ref[...], kn_ref[...]], axis=0)
    vv = jnp.concatenate([vp_ref[...], vc_ref[...], vn_ref[...]], axis=0)
    lane = lax.broadcasted_iota(jnp.int32, (kk.shape[0], LANES), 1)
    lo = lane < HEAD_DIM
    zero = jnp.zeros((kk.shape[0], LANES), BF16)
    k_lo = [jnp.where(lo, kk[:, LANES * h:LANES * (h + 1)], zero) for h in range(A_KV_HEADS)]
    k_hi = [jnp.where(lo, zero, kk[:, LANES * h:LANES * (h + 1)]) for h in range(A_KV_HEADS)]
    v_lo = [jnp.where(lo, vv[:, LANES * h:LANES * (h + 1)], zero) for h in range(A_KV_HEADS)]
    v_hi = [jnp.where(lo, zero, vv[:, LANES * h:LANES * (h + 1)]) for h in range(A_KV_HEADS)]

    qi = lax.broadcasted_iota(jnp.int32, (BLOCK, 3 * BLOCK), 0)
    kj = lax.broadcasted_iota(jnp.int32, (BLOCK, 3 * BLOCK), 1)
    band = jnp.abs(kj - BLOCK - qi) <= WINDOW
    head_of_row = lax.broadcasted_iota(jnp.int32, (A_Q_HEADS * BLOCK, 1), 0) // BLOCK
    snk = jnp.zeros((A_Q_HEADS * BLOCK, 1), F32)
    for head in range(A_Q_HEADS):
        snk = jnp.where(head_of_row == head, sink_ref[head], snk)
    for j in range(tq // BLOCK):
        kpos = i * tq + (j - 1) * BLOCK + kj
        mask = band & (kpos >= 0) & (kpos < seq)
        r0, r1 = j * BLOCK, (j + 3) * BLOCK
        pieces = []
        for c in range(A_Q_HEADS // 2):
            q2 = q_ref[j * BLOCK:(j + 1) * BLOCK, LANES * c:LANES * (c + 1)]
            for kx in (k_lo[c // 2], k_hi[c // 2]):
                s = lax.dot_general(q2, kx[r0:r1], _NT, preferred_element_type=F32)
                pieces.append(jnp.where(mask, s, NEG))
        s = jnp.concatenate(pieces, axis=0)
        m = jnp.maximum(jnp.max(s, axis=-1, keepdims=True), snk)
        p = jnp.exp(s - m)
        inv = 1.0 / (jnp.sum(p, axis=-1, keepdims=True) + jnp.exp(snk - m))
        pb = p.astype(BF16)
        for c in range(A_Q_HEADS // 2):
            out = None
            for half, vx in enumerate((v_lo[c // 2], v_hi[c // 2])):
                rows = slice((2 * c + half) * BLOCK, (2 * c + half + 1) * BLOCK)
                pv = jnp.dot(pb[rows], vx[r0:r1], preferred_element_type=F32) * inv[rows]
                out = pv if out is None else out + pv
            o_ref[j * BLOCK:(j + 1) * BLOCK, LANES * c:LANES * (c + 1)] = out.astype(o_ref.dtype)


def _attn_a(qa, ka, va, sink, bsz, seq, tq):
    t = qa.shape[0]
    nq = seq // tq
    nb = seq // BLOCK
    r = tq // BLOCK
    cur = lambda b, i: (b * nq + i, 0)
    prev = lambda b, i: (b * nb + jnp.maximum(i * r - 1, 0), 0)
    nxt = lambda b, i: (b * nb + jnp.minimum((i + 1) * r, nb - 1), 0)
    kv_specs = [pl.BlockSpec((BLOCK, 256), prev), pl.BlockSpec((tq, 256), cur), pl.BlockSpec((BLOCK, 256), nxt)]
    return pl.pallas_call(
        functools.partial(_attn_a_kernel, seq=seq, tq=tq),
        grid=(bsz, nq),
        in_specs=[pl.BlockSpec(memory_space=pltpu.SMEM), pl.BlockSpec((tq, 512), cur)] + kv_specs + kv_specs,
        out_specs=pl.BlockSpec((tq, 512), cur),
        out_shape=jax.ShapeDtypeStruct((t, 512), BF16),
        compiler_params=_params(("arbitrary", "arbitrary")),
        name="attn_a",
    )(sink, qa, ka, ka, ka, va, va, va)


def _attn_b_kernel(lam_ref, g_ref, q_ref, k_ref, v_ref, o_ref, acc_ref, s_ref, *, tq, tk, lam_init):
    seq = q_ref.shape[0]
    n_q, n_chunks = seq // tq, seq // tk
    lane = lax.broadcasted_iota(jnp.int32, (tk, LANES), 1)
    lo = lane < HEAD_DIM
    zero = jnp.zeros((tk, LANES), BF16)
    ones_col = jnp.where(lane == 0, 1.0, 0.0).astype(BF16)
    minf = jnp.full((tq, 1), -jnp.inf, F32)

    lp = lam_ref[...]
    lam = (jnp.exp(jnp.sum(lp[0:1] * lp[1:2], axis=-1, keepdims=True))
           - jnp.exp(jnp.sum(lp[2:3] * lp[3:4], axis=-1, keepdims=True)) + lam_init)

    def q_tile(qi):
        return q_ref[pl.ds(pl.multiple_of(qi * tq, tq), tq), :]

    def scores(q, kc):
        k = k_ref[kc * tk:(kc + 1) * tk, :]
        return [lax.dot_general(q, kx, _NT, preferred_element_type=F32)
                for kx in (jnp.where(lo, k, zero), jnp.where(lo, zero, k))]

    def consume(kc, maxes):
        v_aug = jnp.concatenate([v_ref[kc * tk:(kc + 1) * tk, :], ones_col], axis=1)
        new = []
        for idx in range(2):
            s = s_ref[idx]
            m_new = jnp.maximum(maxes[idx], jnp.max(s, axis=-1, keepdims=True))
            alpha = jnp.exp2(maxes[idx] - m_new)
            p = jnp.exp2(s - m_new)
            acc_ref[idx] = alpha * acc_ref[idx] + jnp.dot(p.astype(BF16), v_aug, preferred_element_type=F32)
            new.append(m_new)
        return tuple(new)

    def tile(qi, carry):
        q = q_tile(qi)
        q_next = q_tile(jnp.minimum(qi + 1, n_q - 1))
        acc_ref[...] = jnp.zeros_like(acc_ref)
        maxes = (minf, minf)
        for kc in range(n_chunks):
            nxt = scores(q, kc + 1) if kc + 1 < n_chunks else scores(q_next, 0)
            maxes = consume(kc, maxes)
            s_ref[0], s_ref[1] = nxt
        acc1, acc2 = acc_ref[0], acc_ref[1]
        l1, l2 = acc1[:, LANES:LANES + 1], acc2[:, LANES:LANES + 1]
        o = acc1[:, :LANES] * (1.0 / l1) - lam * (acc2[:, :LANES] * (1.0 / l2))
        o = o * lax.rsqrt(jnp.mean(o * o, axis=-1, keepdims=True) + RMS_EPS)
        o = o * g_ref[...] * (1.0 - lam_init)
        o_ref[pl.ds(pl.multiple_of(qi * tq, tq), tq), :] = o.astype(o_ref.dtype)
        return carry

    s_ref[0], s_ref[1] = scores(q_tile(0), 0)
    lax.fori_loop(0, n_q, tile, 0)


def _attn_b(qd, kd, vd, lam_p, subln_g, bsz, seq, tq, tk, lam_init):
    t = qd.shape[0]
    seq_block = pl.BlockSpec((seq, LANES), lambda b, h: (b, h))
    return pl.pallas_call(
        functools.partial(_attn_b_kernel, tq=tq, tk=tk, lam_init=lam_init),
        grid=(bsz, B_HEADS),
        in_specs=[pl.BlockSpec((4, HEAD_DIM), lambda b, h: (0, 0)),
                  pl.BlockSpec((1, LANES), lambda b, h: (0, 0)),
                  seq_block, seq_block, seq_block],
        out_specs=seq_block,
        out_shape=jax.ShapeDtypeStruct((t, 512), BF16),
        scratch_shapes=[pltpu.VMEM((2, tq, 2 * LANES), F32), pltpu.VMEM((2, tq, tk), F32)],
        compiler_params=_params(("arbitrary", "arbitrary")),
        name="attn_b",
    )(lam_p, subln_g, qd, kd, vd)


HALF = D_MODEL // 2
_HI_MASK = -65536


def _pack_pair(lo, hi):
    lo_bits = lax.bitcast_convert_type(lo.astype(BF16).astype(F32), jnp.int32)
    hi_bits = lax.bitcast_convert_type(hi.astype(BF16).astype(F32), jnp.int32)
    return lax.shift_right_logical(lo_bits, 16) | hi_bits


def _unpack_pair(w):
    lo = lax.bitcast_convert_type(lax.shift_left(w, 16), F32)
    hi = lax.bitcast_convert_type(w & _HI_MASK, F32)
    return lo, hi


def _layer_norm(y, g, b):
    mu = jnp.mean(y, axis=-1, keepdims=True)
    d = y - mu
    var = jnp.mean(d * d, axis=-1, keepdims=True)
    return d * lax.rsqrt(var + LN_EPS) * g + b


def _first_argmax(vals, rowf, big):
    m = jnp.max(vals, axis=0, keepdims=True)
    idx = jnp.min(jnp.where(vals == m, rowf, big), axis=0, keepdims=True)
    return m, idx


def _post_attn_kernel(x_ref, oa_ref, ob_ref, ga_ref, gb_ref, woa_ref, wob_ref, wout_ref, g1_ref, b1_ref,
                      rwt_ref, rb_ref, wsg_ref, wsu_ref, wsd_ref, tri_ref, cnt_in_ref,
                      x1_ref, base_ref, eidx_ref, wts_ref, rank_ref, cnt_ref, cnt_scr):
    i = pl.program_id(0)
    tm = x_ref.shape[0]

    @pl.when(i == 0)
    def _():
        cnt_scr[...] = cnt_in_ref[...]

    a = jnp.dot(oa_ref[...], woa_ref[...], preferred_element_type=F32)
    b = jnp.dot(ob_ref[...], wob_ref[...], preferred_element_type=F32)
    merged = ga_ref[...].astype(F32) * a + gb_ref[...].astype(F32) * b
    mix = jnp.dot(merged.astype(BF16), wout_ref[...], preferred_element_type=F32)
    x1 = _layer_norm(DN_ALPHA * x_ref[...] + mix, g1_ref[...], b1_ref[...])
    x1p = _pack_pair(x1[:, :HALF], x1[:, HALF:])
    for piece in range(SC_SPLIT):
        x1_ref[piece] = x1p[:, piece * PIECE:(piece + 1) * PIECE]
    x1b = x1.astype(BF16)

    hg = jnp.dot(x1b, wsg_ref[...], preferred_element_type=F32)
    hu = jnp.dot(x1b, wsu_ref[...], preferred_element_type=F32)
    hs = (hg * jax.nn.sigmoid(hg)) * hu
    shared = jnp.dot(hs.astype(BF16), wsd_ref[...], preferred_element_type=F32)
    base_ref[...] = DN_ALPHA * x1 + shared

    logits = lax.dot_general(rwt_ref[...], x1b, _NT, preferred_element_type=F32)
    scores = jax.nn.sigmoid(logits)
    choice = scores + rb_ref[...]
    ninf = -jnp.inf
    grow = lax.broadcasted_iota(jnp.int32, (GROUP_SIZE, tm), 0).astype(F32)
    gscore = []
    for g in range(N_GROUPS):
        blk = choice[GROUP_SIZE * g:GROUP_SIZE * (g + 1)]
        m1, i1 = _first_argmax(blk, grow, float(GROUP_SIZE))
        m2 = jnp.max(jnp.where(grow == i1, ninf, blk), axis=0, keepdims=True)
        gscore.append(m1 + m2)
    selected = [jnp.zeros((1, tm), F32) for _ in range(N_GROUPS)]
    work = list(gscore)
    for _ in range(TOPK_GROUPS):
        best = work[0]
        for g in range(1, N_GROUPS):
            best = jnp.maximum(best, work[g])
        taken = jnp.zeros((1, tm), F32)
        for g in range(N_GROUPS):
            hit = jnp.where((work[g] == best) & (taken == 0.0), 1.0, 0.0)
            taken = jnp.maximum(taken, hit)
            selected[g] = jnp.maximum(selected[g], hit)
            work[g] = jnp.where(hit > 0.0, ninf, work[g])
    masked = jnp.concatenate(
        [jnp.where(selected[g] > 0.0, choice[GROUP_SIZE * g:GROUP_SIZE * (g + 1)], ninf) for g in range(N_GROUPS)],
        axis=0)

    rowf = lax.broadcasted_iota(jnp.int32, (N_EXPERTS, tm), 0).astype(F32)
    hits, idxs, ws = [], [], []
    for _ in range(TOP_K):
        _, idx = _first_argmax(masked, rowf, float(N_EXPERTS))
        hit = rowf == idx
        hits.append(hit)
        idxs.append(idx)
        ws.append(jnp.sum(jnp.where(hit, scores, 0.0), axis=0, keepdims=True))
        masked = jnp.where(hit, ninf, masked)
    wsum = ws[0]
    for k in range(1, TOP_K):
        wsum = wsum + ws[k]

    member = hits[0]
    for k in range(1, TOP_K):
        member = member | hits[k]
    member_f = jnp.where(member, 1.0, 0.0)
    before = jnp.dot(member_f.astype(BF16), tri_ref[...], preferred_element_type=F32) + cnt_scr[...]
    for k in range(TOP_K):
        eidx_ref[k:k + 1, :] = idxs[k].astype(jnp.int32)
        wts_ref[k:k + 1, :] = ws[k] / wsum * ROUTED_SCALE
        rank_ref[k:k + 1, :] = jnp.sum(jnp.where(hits[k], before, 0.0), axis=0, keepdims=True).astype(jnp.int32)
    cnt_scr[...] = cnt_scr[...] + jnp.sum(member_f, axis=1, keepdims=True)
    cnt_ref[...] = jnp.broadcast_to(cnt_scr[...], cnt_ref.shape).astype(jnp.int32)


def _post_attn(x2, oa, ob, ga, gb, woa, wob, wout, g1, b1, rwt, rb, wsg, wsu, wsd, cnt_in, tm):
    t = x2.shape[0]
    tri = jnp.triu(jnp.ones((tm, tm), F32), k=1).astype(BF16)
    row = lambda i: (i, 0)
    col = lambda i: (0, i)
    full = lambda i: (0, 0)
    wspec = lambda arr: pl.BlockSpec(arr.shape, full)
    weights = (woa, wob, wout, g1, b1, rwt, rb, wsg, wsu, wsd, tri, cnt_in)
    return pl.pallas_call(
        _post_attn_kernel,
        grid=(t // tm,),
        in_specs=[pl.BlockSpec((tm, D_MODEL), row), pl.BlockSpec((tm, 512), row), pl.BlockSpec((tm, 512), row),
                  pl.BlockSpec((tm, D_MODEL), row), pl.BlockSpec((tm, D_MODEL), row)] + [wspec(w) for w in weights],
        out_specs=[pl.BlockSpec((SC_SPLIT, tm, PIECE), lambda i: (0, i, 0)), pl.BlockSpec((tm, D_MODEL), row),
                   pl.BlockSpec((TOP_K, tm), col), pl.BlockSpec((TOP_K, tm), col), pl.BlockSpec((TOP_K, tm), col),
                   pl.BlockSpec((N_EXPERTS, LANES), full)],
        out_shape=[jax.ShapeDtypeStruct((SC_SPLIT, t, PIECE), jnp.int32), jax.ShapeDtypeStruct((t, D_MODEL), F32),
                   jax.ShapeDtypeStruct((TOP_K, t), jnp.int32), jax.ShapeDtypeStruct((TOP_K, t), F32),
                   jax.ShapeDtypeStruct((TOP_K, t), jnp.int32), jax.ShapeDtypeStruct((N_EXPERTS, LANES), jnp.int32)],
        scratch_shapes=[pltpu.VMEM((N_EXPERTS, 1), F32)],
        compiler_params=_params(("arbitrary",)),
        name="post_attn",
    )(x2, oa, ob, ga, gb, *weights)


def _dest_kernel(eidx_ref, rank_ref, pstart_ref, dest_ref):
    tm = eidx_ref.shape[1]
    rows = lax.broadcasted_iota(jnp.int32, (N_EXPERTS, tm), 0)
    pstart = pstart_ref[...]
    for k in range(TOP_K):
        hit = rows == eidx_ref[k:k + 1, :]
        start = jnp.sum(jnp.where(hit, pstart, 0).astype(F32), axis=0, keepdims=True).astype(jnp.int32)
        dest_ref[k:k + 1, :] = start + rank_ref[k:k + 1, :]


def _dest(eidx, rank, pstart, tm):
    t = eidx.shape[1]
    return pl.pallas_call(
        _dest_kernel,
        grid=(t // tm,),
        in_specs=[pl.BlockSpec((TOP_K, tm), lambda i: (0, i)), pl.BlockSpec((TOP_K, tm), lambda i: (0, i)),
                  pl.BlockSpec((N_EXPERTS, 1), lambda i: (0, 0))],
        out_specs=pl.BlockSpec((TOP_K, tm), lambda i: (0, i)),
        out_shape=jax.ShapeDtypeStruct((TOP_K, t), jnp.int32),
        compiler_params=_params(("arbitrary",)),
        name="dest",
    )(eidx, rank, pstart)


def _sc_mesh():
    return plsc.VectorSubcoreMesh(core_axis_name="core", subcore_axis_name="subcore",
                                  num_cores=SC_CORES, num_subcores=SC_SUBCORES)


def _sc_scatter_rows(xs_groups, dests, n_rows):
    n_groups = len(xs_groups)

    @functools.partial(pl.kernel, out_type=jax.ShapeDtypeStruct((SC_SPLIT, n_rows, PIECE), jnp.int32),
                       mesh=_sc_mesh(), scratch_types=[pltpu.SemaphoreType.DMA])
    def scatter(*refs):
        out_hbm, sem = refs[2 * n_groups], refs[2 * n_groups + 1]
        for g in range(n_groups):
            x_hbm, idx_hbm = refs[2 * g], refs[2 * g + 1]
            for piece in range(SC_SPLIT):
                def window(x_vmem, idx_vmem, piece=piece):
                    copies = [pltpu.async_copy(x_vmem, out_hbm.at[piece].at[idx_vmem.at[k]], sem)
                              for k in range(TOP_K)]
                    for copy in copies:
                        copy.wait()

                pltpu.emit_pipeline(
                    window,
                    grid=(x_hbm.shape[1] // SC_WINDOW,),
                    in_specs=[pl.BlockSpec((SC_WINDOW, PIECE), lambda i: (i, 0)),
                              pl.BlockSpec((TOP_K, SC_WINDOW), lambda i: (0, i))],
                    out_specs=[],
                    core_axis_name=("core", "subcore"),
                    dimension_semantics=(pltpu.PARALLEL,),
                )(x_hbm.at[piece], idx_hbm)

    args = []
    for x, d in zip(xs_groups, dests):
        args += [x, d]
    return scatter(*args)


def _experts_kernel(cb_ref, cnt_ref, xs_ref, wg_ref, wu_ref, wd_ref, ys_ref,
                    wg_scr, wu_scr, wd_scr, xbuf, ybuf, sem_in, sem_out):
    e = pl.program_id(0)
    g0, g1, total = cb_ref[e], cb_ref[e + 1], cb_ref[N_EXPERTS]

    def rows(g):
        return pl.ds(pl.multiple_of(g * EXPERT_ROWS, EXPERT_ROWS), EXPERT_ROWS)

    def x_copy(g):
        s = g % EXPERT_SLOTS
        return pltpu.make_async_copy(xs_ref.at[:, rows(g), :], xbuf.at[s], sem_in.at[s])

    def y_copy(g):
        s = g % EXPERT_SLOTS
        return pltpu.make_async_copy(ybuf.at[s], ys_ref.at[:, rows(g), :], sem_out.at[s])

    @pl.when(e == 0)
    def _():
        for g in range(EXPERT_SLOTS - 1):
            pl.when(g < total)(lambda g=g: x_copy(g).start())

    @pl.when(g1 > g0)
    def _():
        wg_scr[...] = wg_ref[0].astype(BF16)
        wu_scr[...] = wu_ref[0].astype(BF16)
        wd_scr[...] = wd_ref[0].astype(BF16)

    def block(g, carry):
        slot = g % EXPERT_SLOTS

        @pl.when(g + EXPERT_SLOTS - 1 < total)
        def _():
            x_copy(g + EXPERT_SLOTS - 1).start()

        x_copy(g).wait()

        @pl.when(g >= EXPERT_SLOTS)
        def _():
            y_copy(g - EXPERT_SLOTS).wait()

        valid = lax.broadcasted_iota(jnp.int32, (EXPERT_ROWS, PIECE), 0) < cnt_ref[e] - (g - g0) * EXPERT_ROWS
        hg = hu = None
        for piece in range(SC_SPLIT):
            for part, col0 in zip(_unpack_pair(xbuf[slot, piece]), (piece * PIECE, HALF + piece * PIECE)):
                xb = jnp.where(valid, part, 0.0).astype(BF16)
                pg = jnp.dot(xb, wg_scr[col0:col0 + PIECE], preferred_element_type=F32)
                pu = jnp.dot(xb, wu_scr[col0:col0 + PIECE], preferred_element_type=F32)
                hg = pg if hg is None else hg + pg
                hu = pu if hu is None else hu + pu
        h = ((hg * jax.nn.sigmoid(hg)) * hu).astype(BF16)
        y = _pack_pair(jnp.dot(h, wd_scr[:, :HALF], preferred_element_type=F32),
                       jnp.dot(h, wd_scr[:, HALF:], preferred_element_type=F32))
        for piece in range(SC_SPLIT):
            ybuf[slot, piece] = y[:, piece * PIECE:(piece + 1) * PIECE]
        y_copy(g).start()
        return carry

    lax.fori_loop(g0, g1, block, 0)

    @pl.when(e == N_EXPERTS - 1)
    def _():
        for back in range(EXPERT_SLOTS, 0, -1):
            pl.when(total >= back)(lambda back=back: y_copy(total - back).wait())


def _experts(cum_blocks, counts, xs, w_gate, w_up, w_down):
    n_rows = xs.shape[1]
    wmap = lambda e, cb, cnt: (e, 0, 0)
    grid_spec = pltpu.PrefetchScalarGridSpec(
        num_scalar_prefetch=2,
        grid=(N_EXPERTS,),
        in_specs=[pl.BlockSpec(memory_space=pl.ANY),
                  pl.BlockSpec((1, D_MODEL, D_EXPERT), wmap),
                  pl.BlockSpec((1, D_MODEL, D_EXPERT), wmap),
                  pl.BlockSpec((1, D_EXPERT, D_MODEL), wmap)],
        out_specs=pl.BlockSpec(memory_space=pl.ANY),
        scratch_shapes=[pltpu.VMEM((D_MODEL, D_EXPERT), BF16), pltpu.VMEM((D_MODEL, D_EXPERT), BF16),
                        pltpu.VMEM((D_EXPERT, D_MODEL), BF16),
                        pltpu.VMEM((EXPERT_SLOTS, SC_SPLIT, EXPERT_ROWS, PIECE), jnp.int32),
                        pltpu.VMEM((EXPERT_SLOTS, SC_SPLIT, EXPERT_ROWS, PIECE), jnp.int32),
                        pltpu.SemaphoreType.DMA((EXPERT_SLOTS,)), pltpu.SemaphoreType.DMA((EXPERT_SLOTS,))])
    return pl.pallas_call(
        _experts_kernel,
        grid_spec=grid_spec,
        out_shape=jax.ShapeDtypeStruct((SC_SPLIT, n_rows, PIECE), jnp.int32),
        compiler_params=_params(("arbitrary",)),
        name="experts",
    )(cum_blocks, counts, xs, w_gate, w_up, w_down)


def _sc_gather_rows(table, idx):
    n, d = idx.shape[0], table.shape[2]

    @functools.partial(pl.kernel, out_type=jax.ShapeDtypeStruct((SC_SPLIT, n, d), table.dtype), mesh=_sc_mesh(),
                       scratch_types=[])
    def gather(table_hbm, idx_hbm, out_hbm):
        for piece in range(SC_SPLIT):
            def window(idx_vmem, out_vmem, piece=piece):
                pltpu.sync_copy(table_hbm.at[piece].at[idx_vmem.at[0]], out_vmem)

            pltpu.emit_pipeline(
                window,
                grid=(n // SC_WINDOW,),
                in_specs=[pl.BlockSpec((1, SC_WINDOW), lambda i: (0, i))],
                out_specs=[pl.BlockSpec((SC_WINDOW, d), lambda i: (i, 0))],
                core_axis_name=("core", "subcore"),
                dimension_semantics=(pltpu.PARALLEL,),
            )(idx_hbm, out_hbm.at[piece])

    return gather(table, idx.reshape(1, n))


def _combine_kernel(w_ref, base_ref, g2_ref, b2_ref, rows_ref, o_ref):
    w = w_ref[...]
    acc = [[base_ref[:, half * HALF + p * PIECE:half * HALF + (p + 1) * PIECE] for p in range(SC_SPLIT)]
           for half in range(2)]
    for k in range(TOP_K):
        wk = w[:, k:k + 1]
        for p in range(SC_SPLIT):
            lo, hi = _unpack_pair(rows_ref[p, k])
            acc[0][p] = acc[0][p] + lo * wk
            acc[1][p] = acc[1][p] + hi * wk
    o_ref[...] = _layer_norm(jnp.concatenate(acc[0] + acc[1], axis=1), g2_ref[...], b2_ref[...])


def _combine(wts_tk, base, g2, b2, rows, tm):
    t = base.shape[0]
    return pl.pallas_call(
        _combine_kernel,
        grid=(t // tm,),
        in_specs=[pl.BlockSpec((tm, TOP_K), lambda i: (i, 0)),
                  pl.BlockSpec((tm, D_MODEL), lambda i: (i, 0)),
                  pl.BlockSpec((1, D_MODEL), lambda i: (0, 0)),
                  pl.BlockSpec((1, D_MODEL), lambda i: (0, 0)),
                  pl.BlockSpec((SC_SPLIT, TOP_K, tm, PIECE), lambda i: (0, 0, i, 0))],
        out_specs=pl.BlockSpec((tm, D_MODEL), lambda i: (i, 0)),
        out_shape=jax.ShapeDtypeStruct((t, D_MODEL), F32),
        compiler_params=_params(("arbitrary",)),
        name="combine",
    )(wts_tk, base, g2, b2, rows)


def _tile(n, pref):
    t = pref
    while n % t:
        t //= 2
    return t


def _token_mixers(x, layer_idx, wp, cnt_in):
    bsz, seq, _ = x.shape
    t = bsz * seq
    x2 = x.reshape(t, D_MODEL)
    lam_init = 0.8 - 0.6 * math.exp(-0.3 * layer_idx)

    cos_t, sin_t = _rope_tables(seq)
    qa, ka, va, qd, kd, vd, ga, gb = _in_proj(x2, wp["w_in"], cos_t, sin_t, seq, _tile(seq, 256))
    oa = _attn_a(qa, ka, va, wp["sink"], bsz, seq, _tile(seq, 512))
    ob = _attn_b(qd, kd, vd, wp["lam_p"], wp["subln_g"], bsz, seq, _tile(seq, 512), _tile(seq, 1024), lam_init)
    return _post_attn(
        x2, oa, ob, ga, gb, wp["w_o_a"], wp["w_o_b"], wp["w_out"], wp["ln1_g"], wp["ln1_b"],
        wp["router_wt"], wp["router_bias"], wp["ws_gate"], wp["ws_up"], wp["ws_down"], cnt_in, _tile(t, 512))


def _encoder_layer(xs_in, layer_idx, wp):
    cnt_in = jnp.zeros((N_EXPERTS, 1), F32)
    routed = []
    for x in xs_in:
        x1, base, eidx, wts, rank, cnt = _token_mixers(x, layer_idx, wp, cnt_in)
        cnt_in = cnt[:, :1].astype(F32)
        routed.append((x1, base, eidx, wts, rank))

    n_assign = sum(x.shape[0] * x.shape[1] for x in xs_in) * TOP_K
    n_blk = -(-(n_assign + N_EXPERTS * (EXPERT_ROWS - 1)) // EXPERT_ROWS)
    counts = cnt[:, 0]
    padded = ((counts + EXPERT_ROWS - 1) // EXPERT_ROWS) * EXPERT_ROWS
    pend = jnp.cumsum(padded)
    pstart = pend - padded
    cum_blocks = jnp.concatenate([jnp.zeros((1,), jnp.int32), (pend // EXPERT_ROWS).astype(jnp.int32)])
    pstart_col = pstart.astype(jnp.int32)[:, None]

    dests = [_dest(eidx, rank, pstart_col, _tile(eidx.shape[1], 512)) for _, _, eidx, _, rank in routed]
    xs = _sc_scatter_rows([r[0] for r in routed], dests, n_blk * EXPERT_ROWS)
    ys = _experts(cum_blocks, counts.astype(jnp.int32), xs, wp["w_gate"], wp["w_up"], wp["w_down"])
    outs = []
    for x, dest, (x1, base, eidx, wts, rank) in zip(xs_in, dests, routed):
        t = base.shape[0]
        rows = _sc_gather_rows(ys, dest.reshape(-1)).reshape(SC_SPLIT, TOP_K, t, PIECE)
        y = _combine(wts.T, base, wp["ln2_g"], wp["ln2_b"], rows, _tile(t, 256))
        outs.append(y.reshape(x.shape))
    return outs


def kernel(x_prompt, x_sample, w_in, attn_sink, lambda_q1, lambda_k1, lambda_q2, lambda_k2, subln_g, w_o_a, w_o_b, w_out, ln1_g, ln1_b, router_w, router_bias, w_gate, w_up, w_down, ws_gate, ws_up, ws_down, ln2_g, ln2_b):
    y_prompt, y_sample = x_prompt, x_sample
    for l in range(DEPTH):
        wp = {
            "w_in": _prep_w_in(w_in[l]),
            "sink": attn_sink[l].astype(F32),
            "lam_p": jnp.stack([lambda_q1[l], lambda_k1[l], lambda_q2[l], lambda_k2[l]]).astype(F32),
            "subln_g": subln_g[l].astype(F32)[None, :],
            "w_o_a": w_o_a[l].astype(BF16), "w_o_b": w_o_b[l].astype(BF16), "w_out": w_out[l].astype(BF16),
            "ln1_g": ln1_g[l].astype(F32)[None, :], "ln1_b": ln1_b[l].astype(F32)[None, :],
            "router_wt": router_w[l].T.astype(BF16), "router_bias": router_bias[l].astype(F32)[:, None],
            "w_gate": w_gate[l], "w_up": w_up[l], "w_down": w_down[l],
            "ws_gate": ws_gate[l].astype(BF16), "ws_up": ws_up[l].astype(BF16), "ws_down": ws_down[l].astype(BF16),
            "ln2_g": ln2_g[l].astype(F32)[None, :], "ln2_b": ln2_b[l].astype(F32)[None, :],
        }
        y_prompt, y_sample = _encoder_layer((y_prompt, y_sample), l, wp)
    return (y_prompt, y_sample)
```

```python
import functools
import math

import jax
import jax.numpy as jnp
from jax import lax
from jax.experimental import pallas as pl
from jax.experimental.pallas import tpu as pltpu
from jax.experimental.pallas import tpu_sc as plsc

D_MODEL = 1024
HEAD_DIM = 64
ROPE_THETA = 10000.0
BLOCK = 128
A_Q_HEADS = 8
A_KV_HEADS = 2
WINDOW = 128
B_HEADS = 4
N_EXPERTS = 256
TOP_K = 8
N_GROUPS = 8
TOPK_GROUPS = 4
GROUP_SIZE = N_EXPERTS // N_GROUPS
D_EXPERT = 256
ROUTED_SCALE = 2.5
EXPERT_ROWS = 256
EXPERT_SLOTS = 4
DEPTH = 1
DN_ALPHA = (2 * DEPTH) ** 0.25
LN_EPS = 1e-5
RMS_EPS = 1e-5
NEG = -1e30
LOG2E = math.log2(math.e)
SOFTMAX_ROWS = 16

LANES = 128
SC_CORES = 2
SC_SUBCORES = 16
SC_WINDOW = 128
SC_SPLIT = 2
PIECE = D_MODEL // 2 // SC_SPLIT
VMEM_LIMIT_BYTES = 56 * 1024 * 1024

F32 = jnp.float32
BF16 = jnp.bfloat16
_NT = (((1,), (1,)), ((), ()))


def _params(sem, vmem=VMEM_LIMIT_BYTES):
    return pltpu.CompilerParams(dimension_semantics=sem, vmem_limit_bytes=vmem)


_C_QA, _C_KA, _C_VA, _C_QD, _C_KD, _C_VD, _C_GA, _C_GB, _C_END = 0, 512, 768, 1024, 1536, 2048, 2560, 3584, 4608


def _prep_w_in(w_in):
    cuts = [0, 512, 640, 768, 1280, 1792, 2304, 3328, 4352]
    qa, ka, va, qd, kd, vd, ga, gb = [w_in[:, cuts[i]:cuts[i + 1]] for i in range(8)]
    dup = lambda w: jnp.concatenate([w[:, :64], w[:, :64], w[:, 64:], w[:, 64:]], axis=1)
    return jnp.concatenate([qa, dup(ka), dup(va), qd, kd, vd, ga, gb], axis=1).astype(BF16)


def _rope_tables(s):
    half = HEAD_DIM // 2
    inv = 1.0 / (ROPE_THETA ** (jnp.arange(half, dtype=F32) / half))
    ang = jnp.arange(s, dtype=F32)[:, None] * inv[None, :]
    cos, sin = jnp.cos(ang), jnp.sin(ang)
    return (jnp.concatenate([cos, cos, cos, cos], axis=1),
            jnp.concatenate([-sin, sin, -sin, sin], axis=1))


def _in_proj_kernel(x_ref, w_ref, cos_ref, sin_ref,
                    qa_ref, ka_ref, va_ref, qd_ref, kd_ref, vd_ref, ga_ref, gb_ref):
    xb = x_ref[...].astype(BF16)
    cos = cos_ref[...]
    sin = sin_ref[...]
    lane = lax.broadcasted_iota(jnp.int32, cos.shape, 1)
    first_half = (lane & (HEAD_DIM // 2)) == 0

    def proj(c0, c1):
        return jnp.dot(xb, w_ref[:, c0:c1], preferred_element_type=F32)

    def rope_store(u, out_ref, scale):
        for j in range(u.shape[1] // LANES):
            uj = u[:, LANES * j:LANES * (j + 1)]
            rot = jnp.where(first_half, pltpu.roll(uj, LANES - 32, 1), pltpu.roll(uj, 32, 1))
            r = uj * cos + rot * sin
            if scale != 1.0:
                r = r * scale
            out_ref[:, LANES * j:LANES * (j + 1)] = r.astype(out_ref.dtype)

    scale = HEAD_DIM ** -0.5
    rope_store(proj(_C_QA, _C_KA), qa_ref, scale)
    rope_store(proj(_C_KA, _C_VA), ka_ref, 1.0)
    va_ref[...] = proj(_C_VA, _C_QD).astype(va_ref.dtype)
    rope_store(proj(_C_QD, _C_KD), qd_ref, scale * LOG2E)
    rope_store(proj(_C_KD, _C_VD), kd_ref, 1.0)
    vd_ref[...] = proj(_C_VD, _C_GA).astype(vd_ref.dtype)
    ga_ref[...] = jax.nn.sigmoid(proj(_C_GA, _C_GB)).astype(ga_ref.dtype)
    gb_ref[...] = jax.nn.sigmoid(proj(_C_GB, _C_END)).astype(gb_ref.dtype)


def _in_proj(x2, w_perm, cos_t, sin_t, seq, tm):
    t = x2.shape[0]
    nseq = seq // tm
    row = lambda i: (i, 0)
    widths = (512, 256, 256, 512, 512, 512, 1024, 1024)
    return pl.pallas_call(
        _in_proj_kernel,
        grid=(t // tm,),
        in_specs=[pl.BlockSpec((tm, D_MODEL), row),
                  pl.BlockSpec((D_MODEL, _C_END), lambda i: (0, 0)),
                  pl.BlockSpec((tm, LANES), lambda i: (i % nseq, 0)),
                  pl.BlockSpec((tm, LANES), lambda i: (i % nseq, 0))],
        out_specs=[pl.BlockSpec((tm, w), row) for w in widths],
        out_shape=[jax.ShapeDtypeStruct((t, w), BF16) for w in widths],
        compiler_params=_params(("arbitrary",)),
        name="in_proj",
    )(x2, w_perm, cos_t, sin_t)


def _attn_a_kernel(sink_ref, q_ref, kp_ref, kc_ref, kn_ref, vp_ref, vc_ref, vn_ref, o_ref, *, seq, tq):
    i = pl.program_id(1)
    kk = jnp.concatenate([kp_ref[...], kc_ref[...], kn_ref[...]], axis=0)
    vv = jnp.concatenate([vp_ref[...], vc_ref[...], vn_ref[...]], axis=0)
    lane = lax.broadcasted_iota(jnp.int32, (kk.shape[0], LANES), 1)
    lo = lane < HEAD_DIM
    zero = jnp.zeros((kk.shape[0], LANES), BF16)
    k_lo = [jnp.where(lo, kk[:, LANES * h:LANES * (h + 1)], zero) for h in range(A_KV_HEADS)]
    k_hi = [jnp.where(lo, zero, kk[:, LANES * h:LANES * (h + 1)]) for h in range(A_KV_HEADS)]
    v_lo = [jnp.where(lo, vv[:, LANES * h:LANES * (h + 1)], zero) for h in range(A_KV_HEADS)]
    v_hi = [jnp.where(lo, zero, vv[:, LANES * h:LANES * (h + 1)]) for h in range(A_KV_HEADS)]

    qi = lax.broadcasted_iota(jnp.int32, (BLOCK, 3 * BLOCK), 0)
    kj = lax.broadcasted_iota(jnp.int32, (BLOCK, 3 * BLOCK), 1)
    band = jnp.abs(kj - BLOCK - qi) <= WINDOW
    head_of_row = lax.broadcasted_iota(jnp.int32, (A_Q_HEADS * BLOCK, 1), 0) // BLOCK
    snk = jnp.zeros((A_Q_HEADS * BLOCK, 1), F32)
    for head in range(A_Q_HEADS):
        snk = jnp.where(head_of_row == head, sink_ref[head], snk)
    for j in range(tq // BLOCK):
        kpos = i * tq + (j - 1) * BLOCK + kj
        mask = band & (kpos >= 0) & (kpos < seq)
        r0, r1 = j * BLOCK, (j + 3) * BLOCK
        pieces = []
        for c in range(A_Q_HEADS // 2):
            q2 = q_ref[j * BLOCK:(j + 1) * BLOCK, LANES * c:LANES * (c + 1)]
            for kx in (k_lo[c // 2], k_hi[c // 2]):
                s = lax.dot_general(q2, kx[r0:r1], _NT, preferred_element_type=F32)
                pieces.append(jnp.where(mask, s, NEG))
        s = jnp.concatenate(pieces, axis=0)
        m = jnp.maximum(jnp.max(s, axis=-1, keepdims=True), snk)
        p = jnp.exp(s - m)
        inv = 1.0 / (jnp.sum(p, axis=-1, keepdims=True) + jnp.exp(snk - m))
        pb = p.astype(BF16)
        for c in range(A_Q_HEADS // 2):
            out = None
            for half, vx in enumerate((v_lo[c // 2], v_hi[c // 2])):
                rows = slice((2 * c + half) * BLOCK, (2 * c + half + 1) * BLOCK)
                pv = jnp.dot(pb[rows], vx[r0:r1], preferred_element_type=F32) * inv[rows]
                out = pv if out is None else out + pv
            o_ref[j * BLOCK:(j + 1) * BLOCK, LANES * c:LANES * (c + 1)] = out.astype(o_ref.dtype)


def _attn_a(qa, ka, va, sink, bsz, seq, tq):
    t = qa.shape[0]
    nq = seq // tq
    nb = seq // BLOCK
    r = tq // BLOCK
    cur = lambda b, i: (b * nq + i, 0)
    prev = lambda b, i: (b * nb + jnp.maximum(i * r - 1, 0), 0)
    nxt = lambda b, i: (b * nb + jnp.minimum((i + 1) * r, nb - 1), 0)
    kv_specs = [pl.BlockSpec((BLOCK, 256), prev), pl.BlockSpec((tq, 256), cur), pl.BlockSpec((BLOCK, 256), nxt)]
    return pl.pallas_call(
        functools.partial(_attn_a_kernel, seq=seq, tq=tq),
        grid=(bsz, nq),
        in_specs=[pl.BlockSpec(memory_space=pltpu.SMEM), pl.BlockSpec((tq, 512), cur)] + kv_specs + kv_specs,
        out_specs=pl.BlockSpec((tq, 512), cur),
        out_shape=jax.ShapeDtypeStruct((t, 512), BF16),
        compiler_params=_params(("arbitrary", "arbitrary")),
        name="attn_a",
    )(sink, qa, ka, ka, ka, va, va, va)


def _attn_b_kernel(lam_ref, g_ref, q_ref, k_ref, v_ref, o_ref, acc_ref, s_ref, p_ref, m_ref, alpha_ref,
                   *, tq, tk, lam_init):
    seq = q_ref.shape[0]
    n_q, n_chunks = seq // tq, seq // tk
    lane = lax.broadcasted_iota(jnp.int32, (tk, LANES), 1)
    lo = lane < HEAD_DIM
    zero = jnp.zeros((tk, LANES), BF16)
    ones_col = jnp.where(lane == 0, 1.0, 0.0).astype(BF16)

    lp = lam_ref[...]
    lam = (jnp.exp(jnp.sum(lp[0:1] * lp[1:2], axis=-1, keepdims=True))
           - jnp.exp(jnp.sum(lp[2:3] * lp[3:4], axis=-1, keepdims=True)) + lam_init)

    def q_tile(qi):
        return q_ref[pl.ds(pl.multiple_of(qi * tq, tq), tq), :]

    def scores(q, kc, slot):
        k = k_ref[kc * tk:(kc + 1) * tk, :]
        for idx, kx in enumerate((jnp.where(lo, k, zero), jnp.where(lo, zero, k))):
            s_ref[slot, idx] = lax.dot_general(q, kx, _NT, preferred_element_type=F32)

    def consume(kc, slot):
        v_aug = jnp.concatenate([v_ref[kc * tk:(kc + 1) * tk, :], ones_col], axis=1)
        for idx in range(2):
            for r in range(tq // SOFTMAX_ROWS):
                rows = slice(r * SOFTMAX_ROWS, (r + 1) * SOFTMAX_ROWS)
                s = s_ref[slot, idx, rows, :]
                m_old = m_ref[idx, rows, :]
                m_new = jnp.maximum(m_old, jnp.max(s, axis=-1, keepdims=True))
                m_ref[idx, rows, :] = m_new
                alpha_ref[idx, rows, :] = jnp.exp2(m_old - m_new)
                p_ref[idx, rows, :] = jnp.exp2(s - m_new).astype(BF16)
            acc_ref[idx] = (alpha_ref[idx] * acc_ref[idx]
                            + jnp.dot(p_ref[idx], v_aug, preferred_element_type=F32))

    across_tiles = n_chunks % 2 == 0

    def tile(qi, carry):
        q = q_tile(qi)
        acc_ref[...] = jnp.zeros_like(acc_ref)
        m_ref[...] = jnp.full(m_ref.shape, -jnp.inf, F32)
        if not across_tiles:
            scores(q, 0, 0)
        for kc in range(n_chunks):
            slot = kc % 2
            if kc + 1 < n_chunks:
                scores(q, kc + 1, 1 - slot)
            elif across_tiles:
                scores(q_tile(jnp.minimum(qi + 1, n_q - 1)), 0, 1 - slot)
            consume(kc, slot)
        acc1, acc2 = acc_ref[0], acc_ref[1]
        l1, l2 = acc1[:, LANES:LANES + 1], acc2[:, LANES:LANES + 1]
        o = acc1[:, :LANES] * (1.0 / l1) - lam * (acc2[:, :LANES] * (1.0 / l2))
        o = o * lax.rsqrt(jnp.mean(o * o, axis=-1, keepdims=True) + RMS_EPS)
        o = o * g_ref[...] * (1.0 - lam_init)
        o_ref[pl.ds(pl.multiple_of(qi * tq, tq), tq), :] = o.astype(o_ref.dtype)
        return carry

    if across_tiles:
        scores(q_tile(0), 0, 0)
    lax.fori_loop(0, n_q, tile, 0)


def _attn_b(qd, kd, vd, lam_p, subln_g, bsz, seq, tq, tk, lam_init):
    t = qd.shape[0]
    seq_block = pl.BlockSpec((seq, LANES), lambda b, h: (b, h))
    return pl.pallas_call(
        functools.partial(_attn_b_kernel, tq=tq, tk=tk, lam_init=lam_init),
        grid=(bsz, B_HEADS),
        in_specs=[pl.BlockSpec((4, HEAD_DIM), lambda b, h: (0, 0)),
                  pl.BlockSpec((1, LANES), lambda b, h: (0, 0)),
                  seq_block, seq_block, seq_block],
        out_specs=seq_block,
        out_shape=jax.ShapeDtypeStruct((t, 512), BF16),
        scratch_shapes=[pltpu.VMEM((2, tq, 2 * LANES), F32),
                        pltpu.VMEM((2, 2, tq, tk), F32),
                        pltpu.VMEM((2, tq, tk), BF16),
                        pltpu.VMEM((2, tq, 1), F32),
                        pltpu.VMEM((2, tq, 1), F32)],
        compiler_params=_params(("arbitrary", "arbitrary")),
        name="attn_b",
    )(lam_p, subln_g, qd, kd, vd)


HALF = D_MODEL // 2
_HI_MASK = -65536


def _pack_pair(lo, hi):
    lo_bits = lax.bitcast_convert_type(lo.astype(BF16).astype(F32), jnp.int32)
    hi_bits = lax.bitcast_convert_type(hi.astype(BF16).astype(F32), jnp.int32)
    return lax.shift_right_logical(lo_bits, 16) | hi_bits


def _unpack_pair(w):
    lo = lax.bitcast_convert_type(lax.shift_left(w, 16), F32)
    hi = lax.bitcast_convert_type(w & _HI_MASK, F32)
    return lo, hi


def _layer_norm(y, g, b):
    mu = jnp.mean(y, axis=-1, keepdims=True)
    d = y - mu
    var = jnp.mean(d * d, axis=-1, keepdims=True)
    return d * lax.rsqrt(var + LN_EPS) * g + b


def _first_argmax(vals, rowf, big):
    m = jnp.max(vals, axis=0, keepdims=True)
    idx = jnp.min(jnp.where(vals == m, rowf, big), axis=0, keepdims=True)
    return m, idx


def _post_attn_kernel(x_ref, oa_ref, ob_ref, ga_ref, gb_ref, woa_ref, wob_ref, wout_ref, g1_ref, b1_ref,
                      rwt_ref, rb_ref, wsg_ref, wsu_ref, wsd_ref, tri_ref, cnt_in_ref,
                      x1_ref, base_ref, eidx_ref, wts_ref, rank_ref, cnt_ref, cnt_scr):
    i = pl.program_id(0)
    tm = x_ref.shape[0]

    @pl.when(i == 0)
    def _():
        cnt_scr[...] = cnt_in_ref[...]

    a = jnp.dot(oa_ref[...], woa_ref[...], preferred_element_type=F32)
    b = jnp.dot(ob_ref[...], wob_ref[...], preferred_element_type=F32)
    merged = ga_ref[...].astype(F32) * a + gb_ref[...].astype(F32) * b
    mix = jnp.dot(merged.astype(BF16), wout_ref[...], preferred_element_type=F32)
    x1 = _layer_norm(DN_ALPHA * x_ref[...] + mix, g1_ref[...], b1_ref[...])
    x1p = _pack_pair(x1[:, :HALF], x1[:, HALF:])
    for piece in range(SC_SPLIT):
        x1_ref[piece] = x1p[:, piece * PIECE:(piece + 1) * PIECE]
    x1b = x1.astype(BF16)

    hg = jnp.dot(x1b, wsg_ref[...], preferred_element_type=F32)
    hu = jnp.dot(x1b, wsu_ref[...], preferred_element_type=F32)
    hs = (hg * jax.nn.sigmoid(hg)) * hu
    shared = jnp.dot(hs.astype(BF16), wsd_ref[...], preferred_element_type=F32)
    base_ref[...] = DN_ALPHA * x1 + shared

    logits = lax.dot_general(rwt_ref[...], x1b, _NT, preferred_element_type=F32)
    scores = jax.nn.sigmoid(logits)
    choice = scores + rb_ref[...]
    ninf = -jnp.inf
    grow = lax.broadcasted_iota(jnp.int32, (GROUP_SIZE, tm), 0).astype(F32)
    gscore = []
    for g in range(N_GROUPS):
        blk = choice[GROUP_SIZE * g:GROUP_SIZE * (g + 1)]
        m1, i1 = _first_argmax(blk, grow, float(GROUP_SIZE))
        m2 = jnp.max(jnp.where(grow == i1, ninf, blk), axis=0, keepdims=True)
        gscore.append(m1 + m2)
    selected = [jnp.zeros((1, tm), F32) for _ in range(N_GROUPS)]
    work = list(gscore)
    for _ in range(TOPK_GROUPS):
        best = work[0]
        for g in range(1, N_GROUPS):
            best = jnp.maximum(best, work[g])
        taken = jnp.zeros((1, tm), F32)
        for g in range(N_GROUPS):
            hit = jnp.where((work[g] == best) & (taken == 0.0), 1.0, 0.0)
            taken = jnp.maximum(taken, hit)
            selected[g] = jnp.maximum(selected[g], hit)
            work[g] = jnp.where(hit > 0.0, ninf, work[g])
    masked = jnp.concatenate(
        [jnp.where(selected[g] > 0.0, choice[GROUP_SIZE * g:GROUP_SIZE * (g + 1)], ninf) for g in range(N_GROUPS)],
        axis=0)

    rowf = lax.broadcasted_iota(jnp.int32, (N_EXPERTS, tm), 0).astype(F32)
    hits, idxs, ws = [], [], []
    for _ in range(TOP_K):
        _, idx = _first_argmax(masked, rowf, float(N_EXPERTS))
        hit = rowf == idx
        hits.append(hit)
        idxs.append(idx)
        ws.append(jnp.sum(jnp.where(hit, scores, 0.0), axis=0, keepdims=True))
        masked = jnp.where(hit, ninf, masked)
    wsum = ws[0]
    for k in range(1, TOP_K):
        wsum = wsum + ws[k]

    member = hits[0]
    for k in range(1, TOP_K):
        member = member | hits[k]
    member_f = jnp.where(member, 1.0, 0.0)
    before = jnp.dot(member_f.astype(BF16), tri_ref[...], preferred_element_type=F32) + cnt_scr[...]
    for k in range(TOP_K):
        eidx_ref[k:k + 1, :] = idxs[k].astype(jnp.int32)
        wts_ref[k:k + 1, :] = ws[k] / wsum * ROUTED_SCALE
        rank_ref[k:k + 1, :] = jnp.sum(jnp.where(hits[k], before, 0.0), axis=0, keepdims=True).astype(jnp.int32)
    cnt_scr[...] = cnt_scr[...] + jnp.sum(member_f, axis=1, keepdims=True)
    cnt_ref[...] = jnp.broadcast_to(cnt_scr[...], cnt_ref.shape).astype(jnp.int32)


def _post_attn(x2, oa, ob, ga, gb, woa, wob, wout, g1, b1, rwt, rb, wsg, wsu, wsd, cnt_in, tm):
    t = x2.shape[0]
    tri = jnp.triu(jnp.ones((tm, tm), F32), k=1).astype(BF16)
    row = lambda i: (i, 0)
    col = lambda i: (0, i)
    full = lambda i: (0, 0)
    wspec = lambda arr: pl.BlockSpec(arr.shape, full)
    weights = (woa, wob, wout, g1, b1, rwt, rb, wsg, wsu, wsd, tri, cnt_in)
    return pl.pallas_call(
        _post_attn_kernel,
        grid=(t // tm,),
        in_specs=[pl.BlockSpec((tm, D_MODEL), row), pl.BlockSpec((tm, 512), row), pl.BlockSpec((tm, 512), row),
                  pl.BlockSpec((tm, D_MODEL), row), pl.BlockSpec((tm, D_MODEL), row)] + [wspec(w) for w in weights],
        out_specs=[pl.BlockSpec((SC_SPLIT, tm, PIECE), lambda i: (0, i, 0)), pl.BlockSpec((tm, D_MODEL), row),
                   pl.BlockSpec((TOP_K, tm), col), pl.BlockSpec((TOP_K, tm), col), pl.BlockSpec((TOP_K, tm), col),
                   pl.BlockSpec((N_EXPERTS, LANES), full)],
        out_shape=[jax.ShapeDtypeStruct((SC_SPLIT, t, PIECE), jnp.int32), jax.ShapeDtypeStruct((t, D_MODEL), F32),
                   jax.ShapeDtypeStruct((TOP_K, t), jnp.int32), jax.ShapeDtypeStruct((TOP_K, t), F32),
                   jax.ShapeDtypeStruct((TOP_K, t), jnp.int32), jax.ShapeDtypeStruct((N_EXPERTS, LANES), jnp.int32)],
        scratch_shapes=[pltpu.VMEM((N_EXPERTS, 1), F32)],
        compiler_params=_params(("arbitrary",)),
        name="post_attn",
    )(x2, oa, ob, ga, gb, *weights)


def _dest_kernel(eidx_ref, rank_ref, pstart_ref, dest_ref):
    tm = eidx_ref.shape[1]
    rows = lax.broadcasted_iota(jnp.int32, (N_EXPERTS, tm), 0)
    pstart = pstart_ref[...]
    for k in range(TOP_K):
        hit = rows == eidx_ref[k:k + 1, :]
        start = jnp.sum(jnp.where(hit, pstart, 0).astype(F32), axis=0, keepdims=True).astype(jnp.int32)
        dest_ref[k:k + 1, :] = start + rank_ref[k:k + 1, :]


def _dest(eidx, rank, pstart, tm):
    t = eidx.shape[1]
    return pl.pallas_call(
        _dest_kernel,
        grid=(t // tm,),
        in_specs=[pl.BlockSpec((TOP_K, tm), lambda i: (0, i)), pl.BlockSpec((TOP_K, tm), lambda i: (0, i)),
                  pl.BlockSpec((N_EXPERTS, 1), lambda i: (0, 0))],
        out_specs=pl.BlockSpec((TOP_K, tm), lambda i: (0, i)),
        out_shape=jax.ShapeDtypeStruct((TOP_K, t), jnp.int32),
        compiler_params=_params(("arbitrary",)),
        name="dest",
    )(eidx, rank, pstart)


def _sc_mesh():
    return plsc.VectorSubcoreMesh(core_axis_name="core", subcore_axis_name="subcore",
                                  num_cores=SC_CORES, num_subcores=SC_SUBCORES)


def _sc_scatter_rows(xs_groups, dests, n_rows):
    n_groups = len(xs_groups)

    @functools.partial(pl.kernel, out_type=jax.ShapeDtypeStruct((SC_SPLIT, n_rows, PIECE), jnp.int32),
                       mesh=_sc_mesh(), scratch_types=[pltpu.SemaphoreType.DMA])
    def scatter(*refs):
        out_hbm, sem = refs[2 * n_groups], refs[2 * n_groups + 1]
        for g in range(n_groups):
            x_hbm, idx_hbm = refs[2 * g], refs[2 * g + 1]
            for piece in range(SC_SPLIT):
                def window(x_vmem, idx_vmem, piece=piece):
                    copies = [pltpu.async_copy(x_vmem, out_hbm.at[piece].at[idx_vmem.at[k]], sem)
                              for k in range(TOP_K)]
                    for copy in copies:
                        copy.wait()

                pltpu.emit_pipeline(
                    window,
                    grid=(x_hbm.shape[1] // SC_WINDOW,),
                    in_specs=[pl.BlockSpec((SC_WINDOW, PIECE), lambda i: (i, 0)),
                              pl.BlockSpec((TOP_K, SC_WINDOW), lambda i: (0, i))],
                    out_specs=[],
                    core_axis_name=("core", "subcore"),
                    dimension_semantics=(pltpu.PARALLEL,),
                )(x_hbm.at[piece], idx_hbm)

    args = []
    for x, d in zip(xs_groups, dests):
        args += [x, d]
    return scatter(*args)


def _experts_kernel(cb_ref, cnt_ref, xs_ref, wg_ref, wu_ref, wd_ref, ys_ref,
                    wg_scr, wu_scr, wd_scr, xbuf, ybuf, sem_in, sem_out):
    e = pl.program_id(0)
    g0, g1, total = cb_ref[e], cb_ref[e + 1], cb_ref[N_EXPERTS]

    def rows(g):
        return pl.ds(pl.multiple_of(g * EXPERT_ROWS, EXPERT_ROWS), EXPERT_ROWS)

    def x_copy(g):
        s = g % EXPERT_SLOTS
        return pltpu.make_async_copy(xs_ref.at[:, rows(g), :], xbuf.at[s], sem_in.at[s])

    def y_copy(g):
        s = g % EXPERT_SLOTS
        return pltpu.make_async_copy(ybuf.at[s], ys_ref.at[:, rows(g), :], sem_out.at[s])

    @pl.when(e == 0)
    def _():
        for g in range(EXPERT_SLOTS - 1):
            pl.when(g < total)(lambda g=g: x_copy(g).start())

    @pl.when(g1 > g0)
    def _():
        wg_scr[...] = wg_ref[0].astype(BF16)
        wu_scr[...] = wu_ref[0].astype(BF16)
        wd_scr[...] = wd_ref[0].astype(BF16)

    def block(g, carry):
        slot = g % EXPERT_SLOTS

        @pl.when(g + EXPERT_SLOTS - 1 < total)
        def _():
            x_copy(g + EXPERT_SLOTS - 1).start()

        x_copy(g).wait()

        @pl.when(g >= EXPERT_SLOTS)
        def _():
            y_copy(g - EXPERT_SLOTS).wait()

        valid = lax.broadcasted_iota(jnp.int32, (EXPERT_ROWS, PIECE), 0) < cnt_ref[e] - (g - g0) * EXPERT_ROWS
        hg = hu = None
        for piece in range(SC_SPLIT):
            for part, col0 in zip(_unpack_pair(xbuf[slot, piece]), (piece * PIECE, HALF + piece * PIECE)):
                xb = jnp.where(valid, part, 0.0).astype(BF16)
                pg = jnp.dot(xb, wg_scr[col0:col0 + PIECE], preferred_element_type=F32)
                pu = jnp.dot(xb, wu_scr[col0:col0 + PIECE], preferred_element_type=F32)
                hg = pg if hg is None else hg + pg
                hu = pu if hu is None else hu + pu
        h = ((hg * jax.nn.sigmoid(hg)) * hu).astype(BF16)
        y = _pack_pair(jnp.dot(h, wd_scr[:, :HALF], preferred_element_type=F32),
                       jnp.dot(h, wd_scr[:, HALF:], preferred_element_type=F32))
        for piece in range(SC_SPLIT):
            ybuf[slot, piece] = y[:, piece * PIECE:(piece + 1) * PIECE]
        y_copy(g).start()
        return carry

    lax.fori_loop(g0, g1, block, 0)

    @pl.when(e == N_EXPERTS - 1)
    def _():
        for back in range(EXPERT_SLOTS, 0, -1):
            pl.when(total >= back)(lambda back=back: y_copy(total - back).wait())


def _experts(cum_blocks, counts, xs, w_gate, w_up, w_down):
    n_rows = xs.shape[1]
    wmap = lambda e, cb, cnt: (e, 0, 0)
    grid_spec = pltpu.PrefetchScalarGridSpec(
        num_scalar_prefetch=2,
        grid=(N_EXPERTS,),
        in_specs=[pl.BlockSpec(memory_space=pl.ANY),
                  pl.BlockSpec((1, D_MODEL, D_EXPERT), wmap),
                  pl.BlockSpec((1, D_MODEL, D_EXPERT), wmap),
                  pl.BlockSpec((1, D_EXPERT, D_MODEL), wmap)],
        out_specs=pl.BlockSpec(memory_space=pl.ANY),
        scratch_shapes=[pltpu.VMEM((D_MODEL, D_EXPERT), BF16), pltpu.VMEM((D_MODEL, D_EXPERT), BF16),
                        pltpu.VMEM((D_EXPERT, D_MODEL), BF16),
                        pltpu.VMEM((EXPERT_SLOTS, SC_SPLIT, EXPERT_ROWS, PIECE), jnp.int32),
                        pltpu.VMEM((EXPERT_SLOTS, SC_SPLIT, EXPERT_ROWS, PIECE), jnp.int32),
                        pltpu.SemaphoreType.DMA((EXPERT_SLOTS,)), pltpu.SemaphoreType.DMA((EXPERT_SLOTS,))])
    return pl.pallas_call(
        _experts_kernel,
        grid_spec=grid_spec,
        out_shape=jax.ShapeDtypeStruct((SC_SPLIT, n_rows, PIECE), jnp.int32),
        compiler_params=_params(("arbitrary",)),
        name="experts",
    )(cum_blocks, counts, xs, w_gate, w_up, w_down)


def _sc_gather_rows(table, idx):
    n, d = idx.shape[0], table.shape[2]

    @functools.partial(pl.kernel, out_type=jax.ShapeDtypeStruct((SC_SPLIT, n, d), table.dtype), mesh=_sc_mesh(),
                       scratch_types=[])
    def gather(table_hbm, idx_hbm, out_hbm):
        for piece in range(SC_SPLIT):
            def window(idx_vmem, out_vmem, piece=piece):
                pltpu.sync_copy(table_hbm.at[piece].at[idx_vmem.at[0]], out_vmem)

            pltpu.emit_pipeline(
                window,
                grid=(n // SC_WINDOW,),
                in_specs=[pl.BlockSpec((1, SC_WINDOW), lambda i: (0, i))],
                out_specs=[pl.BlockSpec((SC_WINDOW, d), lambda i: (i, 0))],
                core_axis_name=("core", "subcore"),
                dimension_semantics=(pltpu.PARALLEL,),
            )(idx_hbm, out_hbm.at[piece])

    return gather(table, idx.reshape(1, n))


def _combine_kernel(w_ref, base_ref, g2_ref, b2_ref, rows_ref, o_ref):
    w = w_ref[...]
    acc = [[base_ref[:, half * HALF + p * PIECE:half * HALF + (p + 1) * PIECE] for p in range(SC_SPLIT)]
           for half in range(2)]
    for k in range(TOP_K):
        wk = w[:, k:k + 1]
        for p in range(SC_SPLIT):
            lo, hi = _unpack_pair(rows_ref[p, k])
            acc[0][p] = acc[0][p] + lo * wk
            acc[1][p] = acc[1][p] + hi * wk
    o_ref[...] = _layer_norm(jnp.concatenate(acc[0] + acc[1], axis=1), g2_ref[...], b2_ref[...])


def _combine(wts_tk, base, g2, b2, rows, tm):
    t = base.shape[0]
    return pl.pallas_call(
        _combine_kernel,
        grid=(t // tm,),
        in_specs=[pl.BlockSpec((tm, TOP_K), lambda i: (i, 0)),
                  pl.BlockSpec((tm, D_MODEL), lambda i: (i, 0)),
                  pl.BlockSpec((1, D_MODEL), lambda i: (0, 0)),
                  pl.BlockSpec((1, D_MODEL), lambda i: (0, 0)),
                  pl.BlockSpec((SC_SPLIT, TOP_K, tm, PIECE), lambda i: (0, 0, i, 0))],
        out_specs=pl.BlockSpec((tm, D_MODEL), lambda i: (i, 0)),
        out_shape=jax.ShapeDtypeStruct((t, D_MODEL), F32),
        compiler_params=_params(("arbitrary",)),
        name="combine",
    )(wts_tk, base, g2, b2, rows)


def _tile(n, pref):
    t = pref
    while n % t:
        t //= 2
    return t


def _token_mixers(x, layer_idx, wp, cnt_in):
    bsz, seq, _ = x.shape
    t = bsz * seq
    x2 = x.reshape(t, D_MODEL)
    lam_init = 0.8 - 0.6 * math.exp(-0.3 * layer_idx)

    cos_t, sin_t = _rope_tables(seq)
    qa, ka, va, qd, kd, vd, ga, gb = _in_proj(x2, wp["w_in"], cos_t, sin_t, seq, _tile(seq, 256))
    oa = _attn_a(qa, ka, va, wp["sink"], bsz, seq, _tile(seq, 512))
    ob = _attn_b(qd, kd, vd, wp["lam_p"], wp["subln_g"], bsz, seq, _tile(seq, 512), _tile(seq, 1024), lam_init)
    return _post_attn(
        x2, oa, ob, ga, gb, wp["w_o_a"], wp["w_o_b"], wp["w_out"], wp["ln1_g"], wp["ln1_b"],
        wp["router_wt"], wp["router_bias"], wp["ws_gate"], wp["ws_up"], wp["ws_down"], cnt_in, _tile(t, 512))


def _encoder_layer(xs_in, layer_idx, wp):
    cnt_in = jnp.zeros((N_EXPERTS, 1), F32)
    routed = []
    for x in xs_in:
        x1, base, eidx, wts, rank, cnt = _token_mixers(x, layer_idx, wp, cnt_in)
        cnt_in = cnt[:, :1].astype(F32)
        routed.append((x1, base, eidx, wts, rank))

    n_assign = sum(x.shape[0] * x.shape[1] for x in xs_in) * TOP_K
    n_blk = -(-(n_assign + N_EXPERTS * (EXPERT_ROWS - 1)) // EXPERT_ROWS)
    counts = cnt[:, 0]
    padded = ((counts + EXPERT_ROWS - 1) // EXPERT_ROWS) * EXPERT_ROWS
    pend = jnp.cumsum(padded)
    pstart = pend - padded
    cum_blocks = jnp.concatenate([jnp.zeros((1,), jnp.int32), (pend // EXPERT_ROWS).astype(jnp.int32)])
    pstart_col = pstart.astype(jnp.int32)[:, None]

    dests = [_dest(eidx, rank, pstart_col, _tile(eidx.shape[1], 512)) for _, _, eidx, _, rank in routed]
    xs = _sc_scatter_rows([r[0] for r in routed], dests, n_blk * EXPERT_ROWS)
    ys = _experts(cum_blocks, counts.astype(jnp.int32), xs, wp["w_gate"], wp["w_up"], wp["w_down"])
    outs = []
    for x, dest, (x1, base, eidx, wts, rank) in zip(xs_in, dests, routed):
        t = base.shape[0]
        rows = _sc_gather_rows(ys, dest.reshape(-1)).reshape(SC_SPLIT, TOP_K, t, PIECE)
        y = _combine(wts.T, base, wp["ln2_g"], wp["ln2_b"], rows, _tile(t, 256))
        outs.append(y.reshape(x.shape))
    return outs


def kernel(x_prompt, x_sample, w_in, attn_sink, lambda_q1, lambda_k1, lambda_q2, lambda_k2, subln_g, w_o_a, w_o_b, w_out, ln1_g, ln1_b, router_w, router_bias, w_gate, w_up, w_down, ws_gate, ws_up, ws_down, ln2_g, ln2_b):
    y_prompt, y_sample = x_prompt, x_sample
    for l in range(DEPTH):
        wp = {
            "w_in": _prep_w_in(w_in[l]),
            "sink": attn_sink[l].astype(F32),
            "lam_p": jnp.stack([lambda_q1[l], lambda_k1[l], lambda_q2[l], lambda_k2[l]]).astype(F32),
            "subln_g": subln_g[l].astype(F32)[None, :],
            "w_o_a": w_o_a[l].astype(BF16), "w_o_b": w_o_b[l].astype(BF16), "w_out": w_out[l].astype(BF16),
            "ln1_g": ln1_g[l].astype(F32)[None, :], "ln1_b": ln1_b[l].astype(F32)[None, :],
            "router_wt": router_w[l].T.astype(BF16), "router_bias": router_bias[l].astype(F32)[:, None],
            "w_gate": w_gate[l], "w_up": w_up[l], "w_down": w_down[l],
            "ws_gate": ws_gate[l].astype(BF16), "ws_up": ws_up[l].astype(BF16), "ws_down": ws_down[l].astype(BF16),
            "ln2_g": ln2_g[l].astype(F32)[None, :], "ln2_b": ln2_b[l].astype(F32)[None, :],
        }
        y_prompt, y_sample = _encoder_layer((y_prompt, y_sample), l, wp)
    return (y_prompt, y_sample)
```

```python
import functools
import math

import jax
import jax.numpy as jnp
from jax import lax
from jax.experimental import pallas as pl
from jax.experimental.pallas import tpu as pltpu
from jax.experimental.pallas import tpu_sc as plsc

D_MODEL = 1024
HEAD_DIM = 64
ROPE_THETA = 10000.0
BLOCK = 128
A_Q_HEADS = 8
A_KV_HEADS = 2
WINDOW = 128
B_HEADS = 4
N_EXPERTS = 256
TOP_K = 8
N_GROUPS = 8
TOPK_GROUPS = 4
GROUP_SIZE = N_EXPERTS // N_GROUPS
D_EXPERT = 256
ROUTED_SCALE = 2.5
EXPERT_ROWS = 256
EXPERT_SLOTS = 4
DEPTH = 1
DN_ALPHA = (2 * DEPTH) ** 0.25
LN_EPS = 1e-5
RMS_EPS = 1e-5
NEG = -1e30
LOG2E = math.log2(math.e)
SOFTMAX_ROWS = 16

LANES = 128
SC_CORES = 2
SC_SUBCORES = 16
SC_WINDOW = 128
SC_SPLIT = 2
PIECE = D_MODEL // 2 // SC_SPLIT
VMEM_LIMIT_BYTES = 56 * 1024 * 1024

F32 = jnp.float32
BF16 = jnp.bfloat16
_NT = (((1,), (1,)), ((), ()))


def _params(sem, vmem=VMEM_LIMIT_BYTES):
    return pltpu.CompilerParams(dimension_semantics=sem, vmem_limit_bytes=vmem)


_C_QA, _C_KA, _C_VA, _C_QD, _C_KD, _C_VD, _C_GA, _C_GB, _C_END = 0, 512, 768, 1024, 1536, 2048, 2560, 3584, 4608


def _prep_w_in(w_in):
    cuts = [0, 512, 640, 768, 1280, 1792, 2304, 3328, 4352]
    qa, ka, va, qd, kd, vd, ga, gb = [w_in[:, cuts[i]:cuts[i + 1]] for i in range(8)]
    dup = lambda w: jnp.concatenate([w[:, :64], w[:, :64], w[:, 64:], w[:, 64:]], axis=1)
    return jnp.concatenate([qa, dup(ka), dup(va), qd, kd, vd, ga, gb], axis=1).astype(BF16)


def _rope_tables(s):
    half = HEAD_DIM // 2
    inv = 1.0 / (ROPE_THETA ** (jnp.arange(half, dtype=F32) / half))
    ang = jnp.arange(s, dtype=F32)[:, None] * inv[None, :]
    cos, sin = jnp.cos(ang), jnp.sin(ang)
    return (jnp.concatenate([cos, cos, cos, cos], axis=1),
            jnp.concatenate([-sin, sin, -sin, sin], axis=1))


def _in_proj_kernel(x_ref, w_ref, cos_ref, sin_ref,
                    qa_ref, ka_ref, va_ref, qd_ref, kd_ref, vd_ref, ga_ref, gb_ref):
    xb = x_ref[...].astype(BF16)
    cos = cos_ref[...]
    sin = sin_ref[...]
    lane = lax.broadcasted_iota(jnp.int32, cos.shape, 1)
    first_half = (lane & (HEAD_DIM // 2)) == 0

    def proj(c0, c1):
        return jnp.dot(xb, w_ref[:, c0:c1], preferred_element_type=F32)

    def rope_store(u, out_ref, scale):
        for j in range(u.shape[1] // LANES):
            uj = u[:, LANES * j:LANES * (j + 1)]
            rot = jnp.where(first_half, pltpu.roll(uj, LANES - 32, 1), pltpu.roll(uj, 32, 1))
            r = uj * cos + rot * sin
            if scale != 1.0:
                r = r * scale
            out_ref[:, LANES * j:LANES * (j + 1)] = r.astype(out_ref.dtype)

    scale = HEAD_DIM ** -0.5
    rope_store(proj(_C_QA, _C_KA), qa_ref, scale)
    rope_store(proj(_C_KA, _C_VA), ka_ref, 1.0)
    va_ref[...] = proj(_C_VA, _C_QD).astype(va_ref.dtype)
    rope_store(proj(_C_QD, _C_KD), qd_ref, scale * LOG2E)
    rope_store(proj(_C_KD, _C_VD), kd_ref, 1.0)
    vd_ref[...] = proj(_C_VD, _C_GA).astype(vd_ref.dtype)
    ga_ref[...] = jax.nn.sigmoid(proj(_C_GA, _C_GB)).astype(ga_ref.dtype)
    gb_ref[...] = jax.nn.sigmoid(proj(_C_GB, _C_END)).astype(gb_ref.dtype)


def _in_proj(x2, w_perm, cos_t, sin_t, seq, tm):
    t = x2.shape[0]
    nseq = seq // tm
    row = lambda i: (i, 0)
    widths = (512, 256, 256, 512, 512, 512, 1024, 1024)
    return pl.pallas_call(
        _in_proj_kernel,
        grid=(t // tm,),
        in_specs=[pl.BlockSpec((tm, D_MODEL), row),
                  pl.BlockSpec((D_MODEL, _C_END), lambda i: (0, 0)),
                  pl.BlockSpec((tm, LANES), lambda i: (i % nseq, 0)),
                  pl.BlockSpec((tm, LANES), lambda i: (i % nseq, 0))],
        out_specs=[pl.BlockSpec((tm, w), row) for w in widths],
        out_shape=[jax.ShapeDtypeStruct((t, w), BF16) for w in widths],
        compiler_params=_params(("arbitrary",)),
        name="in_proj",
    )(x2, w_perm, cos_t, sin_t)


def _attn_a_kernel(sink_ref, q_ref, kp_ref, kc_ref, kn_ref, vp_ref, vc_ref, vn_ref, o_ref, *, seq, tq):
    i = pl.program_id(1)
    kk = jnp.concatenate([kp_ref[...], kc_ref[...], kn_ref[...]], axis=0)
    vv = jnp.concatenate([vp_ref[...], vc_ref[...], vn_ref[...]], axis=0)
    lane = lax.broadcasted_iota(jnp.int32, (kk.shape[0], LANES), 1)
    lo = lane < HEAD_DIM
    zero = jnp.zeros((kk.shape[0], LANES), BF16)
    k_lo = [jnp.where(lo, kk[:, LANES * h:LANES * (h + 1)], zero) for h in range(A_KV_HEADS)]
    k_hi = [jnp.where(lo, zero, kk[:, LANES * h:LANES * (h + 1)]) for h in range(A_KV_HEADS)]
    v_lo = [jnp.where(lo, vv[:, LANES * h:LANES * (h + 1)], zero) for h in range(A_KV_HEADS)]
    v_hi = [jnp.where(lo, zero, vv[:, LANES * h:LANES * (h + 1)]) for h in range(A_KV_HEADS)]

    qi = lax.broadcasted_iota(jnp.int32, (BLOCK, 3 * BLOCK), 0)
    kj = lax.broadcasted_iota(jnp.int32, (BLOCK, 3 * BLOCK), 1)
    band = jnp.abs(kj - BLOCK - qi) <= WINDOW
    head_of_row = lax.broadcasted_iota(jnp.int32, (A_Q_HEADS * BLOCK, 1), 0) // BLOCK
    snk = jnp.zeros((A_Q_HEADS * BLOCK, 1), F32)
    for head in range(A_Q_HEADS):
        snk = jnp.where(head_of_row == head, sink_ref[head], snk)
    for j in range(tq // BLOCK):
        kpos = i * tq + (j - 1) * BLOCK + kj
        mask = band & (kpos >= 0) & (kpos < seq)
        r0, r1 = j * BLOCK, (j + 3) * BLOCK
        pieces = []
        for c in range(A_Q_HEADS // 2):
            q2 = q_ref[j * BLOCK:(j + 1) * BLOCK, LANES * c:LANES * (c + 1)]
            for kx in (k_lo[c // 2], k_hi[c // 2]):
                s = lax.dot_general(q2, kx[r0:r1], _NT, preferred_element_type=F32)
                pieces.append(jnp.where(mask, s, NEG))
        s = jnp.concatenate(pieces, axis=0)
        m = jnp.maximum(jnp.max(s, axis=-1, keepdims=True), snk)
        p = jnp.exp(s - m)
        inv = 1.0 / (jnp.sum(p, axis=-1, keepdims=True) + jnp.exp(snk - m))
        pb = p.astype(BF16)
        for c in range(A_Q_HEADS // 2):
            out = None
            for half, vx in enumerate((v_lo[c // 2], v_hi[c // 2])):
                rows = slice((2 * c + half) * BLOCK, (2 * c + half + 1) * BLOCK)
                pv = jnp.dot(pb[rows], vx[r0:r1], preferred_element_type=F32) * inv[rows]
                out = pv if out is None else out + pv
            o_ref[j * BLOCK:(j + 1) * BLOCK, LANES * c:LANES * (c + 1)] = out.astype(o_ref.dtype)


def _attn_a(qa, ka, va, sink, bsz, seq, tq):
    t = qa.shape[0]
    nq = seq // tq
    nb = seq // BLOCK
    r = tq // BLOCK
    cur = lambda b, i: (b * nq + i, 0)
    prev = lambda b, i: (b * nb + jnp.maximum(i * r - 1, 0), 0)
    nxt = lambda b, i: (b * nb + jnp.minimum((i + 1) * r, nb - 1), 0)
    kv_specs = [pl.BlockSpec((BLOCK, 256), prev), pl.BlockSpec((tq, 256), cur), pl.BlockSpec((BLOCK, 256), nxt)]
    return pl.pallas_call(
        functools.partial(_attn_a_kernel, seq=seq, tq=tq),
        grid=(bsz, nq),
        in_specs=[pl.BlockSpec(memory_space=pltpu.SMEM), pl.BlockSpec((tq, 512), cur)] + kv_specs + kv_specs,
        out_specs=pl.BlockSpec((tq, 512), cur),
        out_shape=jax.ShapeDtypeStruct((t, 512), BF16),
        compiler_params=_params(("arbitrary", "arbitrary")),
        name="attn_a",
    )(sink, qa, ka, ka, ka, va, va, va)


def _attn_b_kernel(lam_ref, g_ref, q_ref, k_ref, v_ref, o_ref, acc_ref, s_ref, p_ref, m_ref, alpha_ref,
                   *, tq, tk, lam_init):
    seq = q_ref.shape[0]
    n_q, n_chunks = seq // tq, seq // tk
    lane = lax.broadcasted_iota(jnp.int32, (tk, LANES), 1)
    lo = lane < HEAD_DIM
    zero = jnp.zeros((tk, LANES), BF16)
    ones_col = jnp.where(lane == 0, 1.0, 0.0).astype(BF16)

    lp = lam_ref[...]
    lam = (jnp.exp(jnp.sum(lp[0:1] * lp[1:2], axis=-1, keepdims=True))
           - jnp.exp(jnp.sum(lp[2:3] * lp[3:4], axis=-1, keepdims=True)) + lam_init)

    def q_tile(qi):
        return q_ref[pl.ds(pl.multiple_of(qi * tq, tq), tq), :]

    def scores(q, kc, slot):
        k = k_ref[kc * tk:(kc + 1) * tk, :]
        for idx, kx in enumerate((jnp.where(lo, k, zero), jnp.where(lo, zero, k))):
            s_ref[slot, idx] = lax.dot_general(q, kx, _NT, preferred_element_type=F32)

    def consume(kc, slot):
        v_aug = jnp.concatenate([v_ref[kc * tk:(kc + 1) * tk, :], ones_col], axis=1)
        for idx in range(2):
            for r in range(tq // SOFTMAX_ROWS):
                rows = slice(r * SOFTMAX_ROWS, (r + 1) * SOFTMAX_ROWS)
                s = s_ref[slot, idx, rows, :]
                m_old = m_ref[idx, rows, :]
                m_new = jnp.maximum(m_old, jnp.max(s, axis=-1, keepdims=True))
                m_ref[idx, rows, :] = m_new
                alpha_ref[idx, rows, :] = jnp.exp2(m_old - m_new)
                p_ref[idx, rows, :] = jnp.exp2(s - m_new).astype(BF16)
            acc_ref[idx] = (alpha_ref[idx] * acc_ref[idx]
                            + jnp.dot(p_ref[idx], v_aug, preferred_element_type=F32))

    across_tiles = n_chunks % 2 == 0

    def tile(qi, carry):
        q = q_tile(qi)
        acc_ref[...] = jnp.zeros_like(acc_ref)
        m_ref[...] = jnp.full(m_ref.shape, -jnp.inf, F32)
        if not across_tiles:
            scores(q, 0, 0)
        for kc in range(n_chunks):
            slot = kc % 2
            if kc + 1 < n_chunks:
                scores(q, kc + 1, 1 - slot)
            elif across_tiles:
                scores(q_tile(jnp.minimum(qi + 1, n_q - 1)), 0, 1 - slot)
            consume(kc, slot)
        acc1, acc2 = acc_ref[0], acc_ref[1]
        l1, l2 = acc1[:, LANES:LANES + 1], acc2[:, LANES:LANES + 1]
        o = acc1[:, :LANES] * (1.0 / l1) - lam * (acc2[:, :LANES] * (1.0 / l2))
        o = o * lax.rsqrt(jnp.mean(o * o, axis=-1, keepdims=True) + RMS_EPS)
        o = o * g_ref[...] * (1.0 - lam_init)
        o_ref[pl.ds(pl.multiple_of(qi * tq, tq), tq), :] = o.astype(o_ref.dtype)
        return carry

    if across_tiles:
        scores(q_tile(0), 0, 0)
    lax.fori_loop(0, n_q, tile, 0)


def _attn_b(qd, kd, vd, lam_p, subln_g, bsz, seq, tq, tk, lam_init):
    t = qd.shape[0]
    seq_block = pl.BlockSpec((seq, LANES), lambda b, h: (b, h))
    return pl.pallas_call(
        functools.partial(_attn_b_kernel, tq=tq, tk=tk, lam_init=lam_init),
        grid=(bsz, B_HEADS),
        in_specs=[pl.BlockSpec((4, HEAD_DIM), lambda b, h: (0, 0)),
                  pl.BlockSpec((1, LANES), lambda b, h: (0, 0)),
                  seq_block, seq_block, seq_block],
        out_specs=seq_block,
        out_shape=jax.ShapeDtypeStruct((t, 512), BF16),
        scratch_shapes=[pltpu.VMEM((2, tq, 2 * LANES), F32),
                        pltpu.VMEM((2, 2, tq, tk), F32),
                        pltpu.VMEM((2, tq, tk), BF16),
                        pltpu.VMEM((2, tq, 1), F32),
                        pltpu.VMEM((2, tq, 1), F32)],
        compiler_params=_params(("arbitrary", "arbitrary")),
        name="attn_b",
    )(lam_p, subln_g, qd, kd, vd)


HALF = D_MODEL // 2
_HI_MASK = -65536


def _pack_pair(lo, hi):
    lo_bits = lax.bitcast_convert_type(lo.astype(BF16).astype(F32), jnp.int32)
    hi_bits = lax.bitcast_convert_type(hi.astype(BF16).astype(F32), jnp.int32)
    return lax.shift_right_logical(lo_bits, 16) | hi_bits


def _unpack_pair(w):
    lo = lax.bitcast_convert_type(lax.shift_left(w, 16), F32)
    hi = lax.bitcast_convert_type(w & _HI_MASK, F32)
    return lo, hi


def _layer_norm(y, g, b):
    mu = jnp.mean(y, axis=-1, keepdims=True)
    d = y - mu
    var = jnp.mean(d * d, axis=-1, keepdims=True)
    return d * lax.rsqrt(var + LN_EPS) * g + b


def _first_argmax(vals, rowf, big):
    m = jnp.max(vals, axis=0, keepdims=True)
    idx = jnp.min(jnp.where(vals == m, rowf, big), axis=0, keepdims=True)
    return m, idx


def _post_attn_kernel(x_ref, oa_ref, ob_ref, ga_ref, gb_ref, woa_ref, wob_ref, wout_ref, g1_ref, b1_ref,
                      rwt_ref, rb_ref, wsg_ref, wsu_ref, wsd_ref, tri_ref, cnt_in_ref,
                      x1_ref, base_ref, eidx_ref, wts_ref, rank_ref, cnt_ref, cnt_scr):
    i = pl.program_id(0)
    tm = x_ref.shape[0]

    @pl.when(i == 0)
    def _():
        cnt_scr[...] = cnt_in_ref[...]

    a = jnp.dot(oa_ref[...], woa_ref[...], preferred_element_type=F32)
    b = jnp.dot(ob_ref[...], wob_ref[...], preferred_element_type=F32)
    merged = ga_ref[...].astype(F32) * a + gb_ref[...].astype(F32) * b
    mix = jnp.dot(merged.astype(BF16), wout_ref[...], preferred_element_type=F32)
    x1 = _layer_norm(DN_ALPHA * x_ref[...] + mix, g1_ref[...], b1_ref[...])
    x1p = _pack_pair(x1[:, :HALF], x1[:, HALF:])
    for piece in range(SC_SPLIT):
        x1_ref[piece] = x1p[:, piece * PIECE:(piece + 1) * PIECE]
    x1b = x1.astype(BF16)

    hg = jnp.dot(x1b, wsg_ref[...], preferred_element_type=F32)
    hu = jnp.dot(x1b, wsu_ref[...], preferred_element_type=F32)
    hs = (hg * jax.nn.sigmoid(hg)) * hu
    shared = jnp.dot(hs.astype(BF16), wsd_ref[...], preferred_element_type=F32)
    base_ref[...] = DN_ALPHA * x1 + shared

    logits = lax.dot_general(rwt_ref[...], x1b, _NT, preferred_element_type=F32)
    scores = jax.nn.sigmoid(logits)
    choice = scores + rb_ref[...]
    ninf = -jnp.inf
    grow = lax.broadcasted_iota(jnp.int32, (GROUP_SIZE, tm), 0).astype(F32)
    gscore = []
    for g in range(N_GROUPS):
        blk = choice[GROUP_SIZE * g:GROUP_SIZE * (g + 1)]
        m1, i1 = _first_argmax(blk, grow, float(GROUP_SIZE))
        m2 = jnp.max(jnp.where(grow == i1, ninf, blk), axis=0, keepdims=True)
        gscore.append(m1 + m2)
    selected = [jnp.zeros((1, tm), F32) for _ in range(N_GROUPS)]
    work = list(gscore)
    for _ in range(TOPK_GROUPS):
        best = work[0]
        for g in range(1, N_GROUPS):
            best = jnp.maximum(best, work[g])
        taken = jnp.zeros((1, tm), F32)
        for g in range(N_GROUPS):
            hit = jnp.where((work[g] == best) & (taken == 0.0), 1.0, 0.0)
            taken = jnp.maximum(taken, hit)
            selected[g] = jnp.maximum(selected[g], hit)
            work[g] = jnp.where(hit > 0.0, ninf, work[g])
    masked = jnp.concatenate(
        [jnp.where(selected[g] > 0.0, choice[GROUP_SIZE * g:GROUP_SIZE * (g + 1)], ninf) for g in range(N_GROUPS)],
        axis=0)

    rowf = lax.broadcasted_iota(jnp.int32, (N_EXPERTS, tm), 0).astype(F32)
    hits, idxs, ws = [], [], []
    for _ in range(TOP_K):
        _, idx = _first_argmax(masked, rowf, float(N_EXPERTS))
        hit = rowf == idx
        hits.append(hit)
        idxs.append(idx)
        ws.append(jnp.sum(jnp.where(hit, scores, 0.0), axis=0, keepdims=True))
        masked = jnp.where(hit, ninf, masked)
    wsum = ws[0]
    for k in range(1, TOP_K):
        wsum = wsum + ws[k]

    member = hits[0]
    for k in range(1, TOP_K):
        member = member | hits[k]
    member_f = jnp.where(member, 1.0, 0.0)
    before = jnp.dot(member_f.astype(BF16), tri_ref[...], preferred_element_type=F32) + cnt_scr[...]
    for k in range(TOP_K):
        eidx_ref[k:k + 1, :] = idxs[k].astype(jnp.int32)
        wts_ref[k:k + 1, :] = ws[k] / wsum * ROUTED_SCALE
        rank_ref[k:k + 1, :] = jnp.sum(jnp.where(hits[k], before, 0.0), axis=0, keepdims=True).astype(jnp.int32)
    cnt_scr[...] = cnt_scr[...] + jnp.sum(member_f, axis=1, keepdims=True)
    cnt_ref[...] = jnp.broadcast_to(cnt_scr[...], cnt_ref.shape).astype(jnp.int32)


def _post_attn(x2, oa, ob, ga, gb, woa, wob, wout, g1, b1, rwt, rb, wsg, wsu, wsd, cnt_in, tm):
    t = x2.shape[0]
    tri = jnp.triu(jnp.ones((tm, tm), F32), k=1).astype(BF16)
    row = lambda i: (i, 0)
    col = lambda i: (0, i)
    full = lambda i: (0, 0)
    wspec = lambda arr: pl.BlockSpec(arr.shape, full)
    weights = (woa, wob, wout, g1, b1, rwt, rb, wsg, wsu, wsd, tri, cnt_in)
    return pl.pallas_call(
        _post_attn_kernel,
        grid=(t // tm,),
        in_specs=[pl.BlockSpec((tm, D_MODEL), row), pl.BlockSpec((tm, 512), row), pl.BlockSpec((tm, 512), row),
                  pl.BlockSpec((tm, D_MODEL), row), pl.BlockSpec((tm, D_MODEL), row)] + [wspec(w) for w in weights],
        out_specs=[pl.BlockSpec((SC_SPLIT, tm, PIECE), lambda i: (0, i, 0)), pl.BlockSpec((tm, D_MODEL), row),
                   pl.BlockSpec((TOP_K, tm), col), pl.BlockSpec((TOP_K, tm), col), pl.BlockSpec((TOP_K, tm), col),
                   pl.BlockSpec((N_EXPERTS, LANES), full)],
        out_shape=[jax.ShapeDtypeStruct((SC_SPLIT, t, PIECE), jnp.int32), jax.ShapeDtypeStruct((t, D_MODEL), F32),
                   jax.ShapeDtypeStruct((TOP_K, t), jnp.int32), jax.ShapeDtypeStruct((TOP_K, t), F32),
                   jax.ShapeDtypeStruct((TOP_K, t), jnp.int32), jax.ShapeDtypeStruct((N_EXPERTS, LANES), jnp.int32)],
        scratch_shapes=[pltpu.VMEM((N_EXPERTS, 1), F32)],
        compiler_params=_params(("arbitrary",)),
        name="post_attn",
    )(x2, oa, ob, ga, gb, *weights)


def _dest_kernel(eidx_ref, rank_ref, pstart_ref, dest_ref):
    tm = eidx_ref.shape[1]
    rows = lax.broadcasted_iota(jnp.int32, (N_EXPERTS, tm), 0)
    pstart = pstart_ref[...]
    for k in range(TOP_K):
        hit = rows == eidx_ref[k:k + 1, :]
        start = jnp.sum(jnp.where(hit, pstart, 0).astype(F32), axis=0, keepdims=True).astype(jnp.int32)
        dest_ref[k:k + 1, :] = start + rank_ref[k:k + 1, :]


def _dest(eidx, rank, pstart, tm):
    t = eidx.shape[1]
    return pl.pallas_call(
        _dest_kernel,
        grid=(t // tm,),
        in_specs=[pl.BlockSpec((TOP_K, tm), lambda i: (0, i)), pl.BlockSpec((TOP_K, tm), lambda i: (0, i)),
                  pl.BlockSpec((N_EXPERTS, 1), lambda i: (0, 0))],
        out_specs=pl.BlockSpec((TOP_K, tm), lambda i: (0, i)),
        out_shape=jax.ShapeDtypeStruct((TOP_K, t), jnp.int32),
        compiler_params=_params(("arbitrary",)),
        name="dest",
    )(eidx, rank, pstart)


def _sc_mesh():
    return plsc.VectorSubcoreMesh(core_axis_name="core", subcore_axis_name="subcore",
                                  num_cores=SC_CORES, num_subcores=SC_SUBCORES)


def _sc_scatter_rows(xs_groups, dests, n_rows):
    n_groups = len(xs_groups)

    @functools.partial(pl.kernel, out_type=jax.ShapeDtypeStruct((SC_SPLIT, n_rows, PIECE), jnp.int32),
                       mesh=_sc_mesh(), scratch_types=[pltpu.SemaphoreType.DMA])
    def scatter(*refs):
        out_hbm, sem = refs[2 * n_groups], refs[2 * n_groups + 1]
        for g in range(n_groups):
            x_hbm, idx_hbm = refs[2 * g], refs[2 * g + 1]
            for piece in range(SC_SPLIT):
                def window(x_vmem, idx_vmem, piece=piece):
                    copies = [pltpu.async_copy(x_vmem, out_hbm.at[piece].at[idx_vmem.at[k]], sem)
                              for k in range(TOP_K)]
                    for copy in copies:
                        copy.wait()

                pltpu.emit_pipeline(
                    window,
                    grid=(x_hbm.shape[1] // SC_WINDOW,),
                    in_specs=[pl.BlockSpec((SC_WINDOW, PIECE), lambda i: (i, 0)),
                              pl.BlockSpec((TOP_K, SC_WINDOW), lambda i: (0, i))],
                    out_specs=[],
                    core_axis_name=("core", "subcore"),
                    dimension_semantics=(pltpu.PARALLEL,),
                )(x_hbm.at[piece], idx_hbm)

    args = []
    for x, d in zip(xs_groups, dests):
        args += [x, d]
    return scatter(*args)


def _experts_kernel(cb_ref, cnt_ref, xs_ref, wg_ref, wu_ref, wd_ref, ys_ref,
                    wg_scr, wu_scr, wd_scr, xbuf, ybuf, sem_in, sem_out):
    e = pl.program_id(0)
    g0, g1, total = cb_ref[e], cb_ref[e + 1], cb_ref[N_EXPERTS]

    def rows(g):
        return pl.ds(pl.multiple_of(g * EXPERT_ROWS, EXPERT_ROWS), EXPERT_ROWS)

    def x_copy(g):
        s = g % EXPERT_SLOTS
        return pltpu.make_async_copy(xs_ref.at[:, rows(g), :], xbuf.at[s], sem_in.at[s])

    def y_copy(g):
        s = g % EXPERT_SLOTS
        return pltpu.make_async_copy(ybuf.at[s], ys_ref.at[:, rows(g), :], sem_out.at[s])

    @pl.when(e == 0)
    def _():
        for g in range(EXPERT_SLOTS - 1):
            pl.when(g < total)(lambda g=g: x_copy(g).start())

    @pl.when(g1 > g0)
    def _():
        wg_scr[...] = wg_ref[0].astype(BF16)
        wu_scr[...] = wu_ref[0].astype(BF16)
        wd_scr[...] = wd_ref[0].astype(BF16)

    def prefetch(g):
        pl.when(g < total)(lambda: x_copy(g).start())

    def release(g):
        pl.when(g >= EXPERT_SLOTS)(lambda: y_copy(g - EXPERT_SLOTS).wait())

    def compute(g):
        slot = g % EXPERT_SLOTS
        valid = lax.broadcasted_iota(jnp.int32, (EXPERT_ROWS, PIECE), 0) < cnt_ref[e] - (g - g0) * EXPERT_ROWS
        hg = hu = None
        for piece in range(SC_SPLIT):
            for part, col0 in zip(_unpack_pair(xbuf[slot, piece]), (piece * PIECE, HALF + piece * PIECE)):
                xb = jnp.where(valid, part, 0.0).astype(BF16)
                pg = jnp.dot(xb, wg_scr[col0:col0 + PIECE], preferred_element_type=F32)
                pu = jnp.dot(xb, wu_scr[col0:col0 + PIECE], preferred_element_type=F32)
                hg = pg if hg is None else hg + pg
                hu = pu if hu is None else hu + pu
        h = ((hg * jax.nn.sigmoid(hg)) * hu).astype(BF16)
        y = _pack_pair(jnp.dot(h, wd_scr[:, :HALF], preferred_element_type=F32),
                       jnp.dot(h, wd_scr[:, HALF:], preferred_element_type=F32))
        for piece in range(SC_SPLIT):
            ybuf[slot, piece] = y[:, piece * PIECE:(piece + 1) * PIECE]

    def pair(j, carry):
        g = g0 + 2 * j
        prefetch(g + EXPERT_SLOTS - 1)
        x_copy(g).wait()
        x_copy(g + 1).wait()
        release(g)
        release(g + 1)
        compute(g)
        compute(g + 1)
        y_copy(g).start()
        y_copy(g + 1).start()
        prefetch(g + EXPERT_SLOTS)
        return carry

    lax.fori_loop(0, lax.shift_right_logical(g1 - g0, 1), pair, 0)

    @pl.when(((g1 - g0) & 1) == 1)
    def _():
        g = g1 - 1
        prefetch(g + EXPERT_SLOTS - 1)
        x_copy(g).wait()
        release(g)
        compute(g)
        y_copy(g).start()

    @pl.when(e == N_EXPERTS - 1)
    def _():
        for back in range(EXPERT_SLOTS, 0, -1):
            pl.when(total >= back)(lambda back=back: y_copy(total - back).wait())


def _experts(cum_blocks, counts, xs, w_gate, w_up, w_down):
    n_rows = xs.shape[1]
    wmap = lambda e, cb, cnt: (e, 0, 0)
    grid_spec = pltpu.PrefetchScalarGridSpec(
        num_scalar_prefetch=2,
        grid=(N_EXPERTS,),
        in_specs=[pl.BlockSpec(memory_space=pl.ANY),
                  pl.BlockSpec((1, D_MODEL, D_EXPERT), wmap),
                  pl.BlockSpec((1, D_MODEL, D_EXPERT), wmap),
                  pl.BlockSpec((1, D_EXPERT, D_MODEL), wmap)],
        out_specs=pl.BlockSpec(memory_space=pl.ANY),
        scratch_shapes=[pltpu.VMEM((D_MODEL, D_EXPERT), BF16), pltpu.VMEM((D_MODEL, D_EXPERT), BF16),
                        pltpu.VMEM((D_EXPERT, D_MODEL), BF16),
                        pltpu.VMEM((EXPERT_SLOTS, SC_SPLIT, EXPERT_ROWS, PIECE), jnp.int32),
                        pltpu.VMEM((EXPERT_SLOTS, SC_SPLIT, EXPERT_ROWS, PIECE), jnp.int32),
                        pltpu.SemaphoreType.DMA((EXPERT_SLOTS,)), pltpu.SemaphoreType.DMA((EXPERT_SLOTS,))])
    return pl.pallas_call(
        _experts_kernel,
        grid_spec=grid_spec,
        out_shape=jax.ShapeDtypeStruct((SC_SPLIT, n_rows, PIECE), jnp.int32),
        compiler_params=_params(("arbitrary",)),
        name="experts",
    )(cum_blocks, counts, xs, w_gate, w_up, w_down)


def _sc_gather_rows(table, idx):
    n, d = idx.shape[0], table.shape[2]

    @functools.partial(pl.kernel, out_type=jax.ShapeDtypeStruct((SC_SPLIT, n, d), table.dtype), mesh=_sc_mesh(),
                       scratch_types=[])
    def gather(table_hbm, idx_hbm, out_hbm):
        for piece in range(SC_SPLIT):
            def window(idx_vmem, out_vmem, piece=piece):
                pltpu.sync_copy(table_hbm.at[piece].at[idx_vmem.at[0]], out_vmem)

            pltpu.emit_pipeline(
                window,
                grid=(n // SC_WINDOW,),
                in_specs=[pl.BlockSpec((1, SC_WINDOW), lambda i: (0, i))],
                out_specs=[pl.BlockSpec((SC_WINDOW, d), lambda i: (i, 0))],
                core_axis_name=("core", "subcore"),
                dimension_semantics=(pltpu.PARALLEL,),
            )(idx_hbm, out_hbm.at[piece])

    return gather(table, idx.reshape(1, n))


def _combine_kernel(w_ref, base_ref, g2_ref, b2_ref, rows_ref, o_ref):
    w = w_ref[...]
    acc = [[base_ref[:, half * HALF + p * PIECE:half * HALF + (p + 1) * PIECE] for p in range(SC_SPLIT)]
           for half in range(2)]
    for k in range(TOP_K):
        wk = w[:, k:k + 1]
        for p in range(SC_SPLIT):
            lo, hi = _unpack_pair(rows_ref[p, k])
            acc[0][p] = acc[0][p] + lo * wk
            acc[1][p] = acc[1][p] + hi * wk
    o_ref[...] = _layer_norm(jnp.concatenate(acc[0] + acc[1], axis=1), g2_ref[...], b2_ref[...])


def _combine(wts_tk, base, g2, b2, rows, tm):
    t = base.shape[0]
    return pl.pallas_call(
        _combine_kernel,
        grid=(t // tm,),
        in_specs=[pl.BlockSpec((tm, TOP_K), lambda i: (i, 0)),
                  pl.BlockSpec((tm, D_MODEL), lambda i: (i, 0)),
                  pl.BlockSpec((1, D_MODEL), lambda i: (0, 0)),
                  pl.BlockSpec((1, D_MODEL), lambda i: (0, 0)),
                  pl.BlockSpec((SC_SPLIT, TOP_K, tm, PIECE), lambda i: (0, 0, i, 0))],
        out_specs=pl.BlockSpec((tm, D_MODEL), lambda i: (i, 0)),
        out_shape=jax.ShapeDtypeStruct((t, D_MODEL), F32),
        compiler_params=_params(("arbitrary",)),
        name="combine",
    )(wts_tk, base, g2, b2, rows)


def _tile(n, pref):
    t = pref
    while n % t:
        t //= 2
    return t


def _token_mixers(x, layer_idx, wp, cnt_in):
    bsz, seq, _ = x.shape
    t = bsz * seq
    x2 = x.reshape(t, D_MODEL)
    lam_init = 0.8 - 0.6 * math.exp(-0.3 * layer_idx)

    cos_t, sin_t = _rope_tables(seq)
    qa, ka, va, qd, kd, vd, ga, gb = _in_proj(x2, wp["w_in"], cos_t, sin_t, seq, _tile(seq, 256))
    oa = _attn_a(qa, ka, va, wp["sink"], bsz, seq, _tile(seq, 512))
    ob = _attn_b(qd, kd, vd, wp["lam_p"], wp["subln_g"], bsz, seq, _tile(seq, 512), _tile(seq, 1024), lam_init)
    return _post_attn(
        x2, oa, ob, ga, gb, wp["w_o_a"], wp["w_o_b"], wp["w_out"], wp["ln1_g"], wp["ln1_b"],
        wp["router_wt"], wp["router_bias"], wp["ws_gate"], wp["ws_up"], wp["ws_down"], cnt_in, _tile(t, 512))


def _encoder_layer(xs_in, layer_idx, wp):
    cnt_in = jnp.zeros((N_EXPERTS, 1), F32)
    routed = []
    for x in xs_in:
        x1, base, eidx, wts, rank, cnt = _token_mixers(x, layer_idx, wp, cnt_in)
        cnt_in = cnt[:, :1].astype(F32)
        routed.append((x1, base, eidx, wts, rank))

    n_assign = sum(x.shape[0] * x.shape[1] for x in xs_in) * TOP_K
    n_blk = -(-(n_assign + N_EXPERTS * (EXPERT_ROWS - 1)) // EXPERT_ROWS)
    counts = cnt[:, 0]
    padded = ((counts + EXPERT_ROWS - 1) // EXPERT_ROWS) * EXPERT_ROWS
    pend = jnp.cumsum(padded)
    pstart = pend - padded
    cum_blocks = jnp.concatenate([jnp.zeros((1,), jnp.int32), (pend // EXPERT_ROWS).astype(jnp.int32)])
    pstart_col = pstart.astype(jnp.int32)[:, None]

    dests = [_dest(eidx, rank, pstart_col, _tile(eidx.shape[1], 512)) for _, _, eidx, _, rank in routed]
    xs = _sc_scatter_rows([r[0] for r in routed], dests, n_blk * EXPERT_ROWS)
    ys = _experts(cum_blocks, counts.astype(jnp.int32), xs, wp["w_gate"], wp["w_up"], wp["w_down"])
    outs = []
    for x, dest, (x1, base, eidx, wts, rank) in zip(xs_in, dests, routed):
        t = base.shape[0]
        rows = _sc_gather_rows(ys, dest.reshape(-1)).reshape(SC_SPLIT, TOP_K, t, PIECE)
        y = _combine(wts.T, base, wp["ln2_g"], wp["ln2_b"], rows, _tile(t, 256))
        outs.append(y.reshape(x.shape))
    return outs


def kernel(x_prompt, x_sample, w_in, attn_sink, lambda_q1, lambda_k1, lambda_q2, lambda_k2, subln_g, w_o_a, w_o_b, w_out, ln1_g, ln1_b, router_w, router_bias, w_gate, w_up, w_down, ws_gate, ws_up, ws_down, ln2_g, ln2_b):
    y_prompt, y_sample = x_prompt, x_sample
    for l in range(DEPTH):
        wp = {
            "w_in": _prep_w_in(w_in[l]),
            "sink": attn_sink[l].astype(F32),
            "lam_p": jnp.stack([lambda_q1[l], lambda_k1[l], lambda_q2[l], lambda_k2[l]]).astype(F32),
            "subln_g": subln_g[l].astype(F32)[None, :],
            "w_o_a": w_o_a[l].astype(BF16), "w_o_b": w_o_b[l].astype(BF16), "w_out": w_out[l].astype(BF16),
            "ln1_g": ln1_g[l].astype(F32)[None, :], "ln1_b": ln1_b[l].astype(F32)[None, :],
            "router_wt": router_w[l].T.astype(BF16), "router_bias": router_bias[l].astype(F32)[:, None],
            "w_gate": w_gate[l], "w_up": w_up[l], "w_down": w_down[l],
            "ws_gate": ws_gate[l].astype(BF16), "ws_up": ws_up[l].astype(BF16), "ws_down": ws_down[l].astype(BF16),
            "ln2_g": ln2_g[l].astype(F32)[None, :], "ln2_b": ln2_b[l].astype(F32)[None, :],
        }
        y_prompt, y_sample = _encoder_layer((y_prompt, y_sample), l, wp)
    return (y_prompt, y_sample)
```

```python
import functools
import math

import jax
import jax.numpy as jnp
from jax import lax
from jax.experimental import pallas as pl
from jax.experimental.pallas import tpu as pltpu
from jax.experimental.pallas import tpu_sc as plsc

D_MODEL = 1024
HEAD_DIM = 64
ROPE_THETA = 10000.0
BLOCK = 128
A_Q_HEADS = 8
A_KV_HEADS = 2
WINDOW = 128
B_HEADS = 4
N_EXPERTS = 256
TOP_K = 8
N_GROUPS = 8
TOPK_GROUPS = 4
GROUP_SIZE = N_EXPERTS // N_GROUPS
D_EXPERT = 256
ROUTED_SCALE = 2.5
EXPERT_ROWS = 512
EXPERT_SLOTS = 4
DEPTH = 1
DN_ALPHA = (2 * DEPTH) ** 0.25
LN_EPS = 1e-5
RMS_EPS = 1e-5
NEG = -1e30
LOG2E = math.log2(math.e)
SOFTMAX_ROWS = 16

LANES = 128
SC_CORES = 2
SC_SUBCORES = 16
SC_WINDOW = 128
SC_SPLIT = 2
PIECE = D_MODEL // 2 // SC_SPLIT
VMEM_LIMIT_BYTES = 56 * 1024 * 1024

F32 = jnp.float32
BF16 = jnp.bfloat16
_NT = (((1,), (1,)), ((), ()))


def _params(sem, vmem=VMEM_LIMIT_BYTES):
    return pltpu.CompilerParams(dimension_semantics=sem, vmem_limit_bytes=vmem)


_C_QA, _C_KA, _C_VA, _C_QD, _C_KD, _C_VD, _C_GA, _C_GB, _C_END = 0, 512, 768, 1024, 1536, 2048, 2560, 3584, 4608


def _prep_w_in(w_in):
    cuts = [0, 512, 640, 768, 1280, 1792, 2304, 3328, 4352]
    qa, ka, va, qd, kd, vd, ga, gb = [w_in[:, cuts[i]:cuts[i + 1]] for i in range(8)]
    dup = lambda w: jnp.concatenate([w[:, :64], w[:, :64], w[:, 64:], w[:, 64:]], axis=1)
    return jnp.concatenate([qa, dup(ka), dup(va), qd, kd, vd, ga, gb], axis=1).astype(BF16)


def _rope_tables(s):
    half = HEAD_DIM // 2
    inv = 1.0 / (ROPE_THETA ** (jnp.arange(half, dtype=F32) / half))
    ang = jnp.arange(s, dtype=F32)[:, None] * inv[None, :]
    cos, sin = jnp.cos(ang), jnp.sin(ang)
    return (jnp.concatenate([cos, cos, cos, cos], axis=1),
            jnp.concatenate([-sin, sin, -sin, sin], axis=1))


def _in_proj_kernel(x_ref, w_ref, cos_ref, sin_ref,
                    qa_ref, ka_ref, va_ref, qd_ref, kd_ref, vd_ref, ga_ref, gb_ref):
    xb = x_ref[...].astype(BF16)
    cos = cos_ref[...]
    sin = sin_ref[...]
    lane = lax.broadcasted_iota(jnp.int32, cos.shape, 1)
    first_half = (lane & (HEAD_DIM // 2)) == 0

    def proj(c0, c1):
        return jnp.dot(xb, w_ref[:, c0:c1], preferred_element_type=F32)

    def rope_store(u, out_ref, scale):
        for j in range(u.shape[1] // LANES):
            uj = u[:, LANES * j:LANES * (j + 1)]
            rot = jnp.where(first_half, pltpu.roll(uj, LANES - 32, 1), pltpu.roll(uj, 32, 1))
            r = uj * cos + rot * sin
            if scale != 1.0:
                r = r * scale
            out_ref[:, LANES * j:LANES * (j + 1)] = r.astype(out_ref.dtype)

    scale = HEAD_DIM ** -0.5
    rope_store(proj(_C_QA, _C_KA), qa_ref, scale)
    rope_store(proj(_C_KA, _C_VA), ka_ref, 1.0)
    va_ref[...] = proj(_C_VA, _C_QD).astype(va_ref.dtype)
    rope_store(proj(_C_QD, _C_KD), qd_ref, scale * LOG2E)
    rope_store(proj(_C_KD, _C_VD), kd_ref, 1.0)
    vd_ref[...] = proj(_C_VD, _C_GA).astype(vd_ref.dtype)
    ga_ref[...] = jax.nn.sigmoid(proj(_C_GA, _C_GB)).astype(ga_ref.dtype)
    gb_ref[...] = jax.nn.sigmoid(proj(_C_GB, _C_END)).astype(gb_ref.dtype)


def _in_proj(x2, w_perm, cos_t, sin_t, seq, tm):
    t = x2.shape[0]
    nseq = seq // tm
    row = lambda i: (i, 0)
    widths = (512, 256, 256, 512, 512, 512, 1024, 1024)
    return pl.pallas_call(
        _in_proj_kernel,
        grid=(t // tm,),
        in_specs=[pl.BlockSpec((tm, D_MODEL), row),
                  pl.BlockSpec((D_MODEL, _C_END), lambda i: (0, 0)),
                  pl.BlockSpec((tm, LANES), lambda i: (i % nseq, 0)),
                  pl.BlockSpec((tm, LANES), lambda i: (i % nseq, 0))],
        out_specs=[pl.BlockSpec((tm, w), row) for w in widths],
        out_shape=[jax.ShapeDtypeStruct((t, w), BF16) for w in widths],
        compiler_params=_params(("arbitrary",)),
        name="in_proj",
    )(x2, w_perm, cos_t, sin_t)


def _attn_a_kernel(sink_ref, q_ref, kp_ref, kc_ref, kn_ref, vp_ref, vc_ref, vn_ref, o_ref, *, seq, tq):
    i = pl.program_id(1)
    kk = jnp.concatenate([kp_ref[...], kc_ref[...], kn_ref[...]], axis=0)
    vv = jnp.concatenate([vp_ref[...], vc_ref[...], vn_ref[...]], axis=0)
    lane = lax.broadcasted_iota(jnp.int32, (kk.shape[0], LANES), 1)
    lo = lane < HEAD_DIM
    zero = jnp.zeros((kk.shape[0], LANES), BF16)
    k_lo = [jnp.where(lo, kk[:, LANES * h:LANES * (h + 1)], zero) for h in range(A_KV_HEADS)]
    k_hi = [jnp.where(lo, zero, kk[:, LANES * h:LANES * (h + 1)]) for h in range(A_KV_HEADS)]
    v_lo = [jnp.where(lo, vv[:, LANES * h:LANES * (h + 1)], zero) for h in range(A_KV_HEADS)]
    v_hi = [jnp.where(lo, zero, vv[:, LANES * h:LANES * (h + 1)]) for h in range(A_KV_HEADS)]

    qi = lax.broadcasted_iota(jnp.int32, (BLOCK, 3 * BLOCK), 0)
    kj = lax.broadcasted_iota(jnp.int32, (BLOCK, 3 * BLOCK), 1)
    band = jnp.abs(kj - BLOCK - qi) <= WINDOW
    head_of_row = lax.broadcasted_iota(jnp.int32, (A_Q_HEADS * BLOCK, 1), 0) // BLOCK
    snk = jnp.zeros((A_Q_HEADS * BLOCK, 1), F32)
    for head in range(A_Q_HEADS):
        snk = jnp.where(head_of_row == head, sink_ref[head], snk)
    for j in range(tq // BLOCK):
        kpos = i * tq + (j - 1) * BLOCK + kj
        mask = band & (kpos >= 0) & (kpos < seq)
        r0, r1 = j * BLOCK, (j + 3) * BLOCK
        pieces = []
        for c in range(A_Q_HEADS // 2):
            q2 = q_ref[j * BLOCK:(j + 1) * BLOCK, LANES * c:LANES * (c + 1)]
            for kx in (k_lo[c // 2], k_hi[c // 2]):
                s = lax.dot_general(q2, kx[r0:r1], _NT, preferred_element_type=F32)
                pieces.append(jnp.where(mask, s, NEG))
        s = jnp.concatenate(pieces, axis=0)
        m = jnp.maximum(jnp.max(s, axis=-1, keepdims=True), snk)
        p = jnp.exp(s - m)
        inv = 1.0 / (jnp.sum(p, axis=-1, keepdims=True) + jnp.exp(snk - m))
        pb = p.astype(BF16)
        for c in range(A_Q_HEADS // 2):
            out = None
            for half, vx in enumerate((v_lo[c // 2], v_hi[c // 2])):
                rows = slice((2 * c + half) * BLOCK, (2 * c + half + 1) * BLOCK)
                pv = jnp.dot(pb[rows], vx[r0:r1], preferred_element_type=F32) * inv[rows]
                out = pv if out is None else out + pv
            o_ref[j * BLOCK:(j + 1) * BLOCK, LANES * c:LANES * (c + 1)] = out.astype(o_ref.dtype)


def _attn_a(qa, ka, va, sink, bsz, seq, tq):
    t = qa.shape[0]
    nq = seq // tq
    nb = seq // BLOCK
    r = tq // BLOCK
    cur = lambda b, i: (b * nq + i, 0)
    prev = lambda b, i: (b * nb + jnp.maximum(i * r - 1, 0), 0)
    nxt = lambda b, i: (b * nb + jnp.minimum((i + 1) * r, nb - 1), 0)
    kv_specs = [pl.BlockSpec((BLOCK, 256), prev), pl.BlockSpec((tq, 256), cur), pl.BlockSpec((BLOCK, 256), nxt)]
    return pl.pallas_call(
        functools.partial(_attn_a_kernel, seq=seq, tq=tq),
        grid=(bsz, nq),
        in_specs=[pl.BlockSpec(memory_space=pltpu.SMEM), pl.BlockSpec((tq, 512), cur)] + kv_specs + kv_specs,
        out_specs=pl.BlockSpec((tq, 512), cur),
        out_shape=jax.ShapeDtypeStruct((t, 512), BF16),
        compiler_params=_params(("arbitrary", "arbitrary")),
        name="attn_a",
    )(sink, qa, ka, ka, ka, va, va, va)


def _attn_b_kernel(lam_ref, g_ref, q_ref, k_ref, v_ref, o_ref, acc_ref, s_ref, p_ref, m_ref, alpha_ref,
                   *, tq, tk, lam_init):
    seq = q_ref.shape[0]
    n_q, n_chunks = seq // tq, seq // tk
    lane = lax.broadcasted_iota(jnp.int32, (tk, LANES), 1)
    lo = lane < HEAD_DIM
    zero = jnp.zeros((tk, LANES), BF16)
    ones_col = jnp.where(lane == 0, 1.0, 0.0).astype(BF16)

    lp = lam_ref[...]
    lam = (jnp.exp(jnp.sum(lp[0:1] * lp[1:2], axis=-1, keepdims=True))
           - jnp.exp(jnp.sum(lp[2:3] * lp[3:4], axis=-1, keepdims=True)) + lam_init)

    def q_tile(qi):
        return q_ref[pl.ds(pl.multiple_of(qi * tq, tq), tq), :]

    def scores(q, kc, slot):
        k = k_ref[kc * tk:(kc + 1) * tk, :]
        for idx, kx in enumerate((jnp.where(lo, k, zero), jnp.where(lo, zero, k))):
            s_ref[slot, idx] = lax.dot_general(q, kx, _NT, preferred_element_type=F32)

    def consume(kc, slot):
        v_aug = jnp.concatenate([v_ref[kc * tk:(kc + 1) * tk, :], ones_col], axis=1)
        for idx in range(2):
            for r in range(tq // SOFTMAX_ROWS):
                rows = slice(r * SOFTMAX_ROWS, (r + 1) * SOFTMAX_ROWS)
                s = s_ref[slot, idx, rows, :]
                m_old = m_ref[idx, rows, :]
                m_new = jnp.maximum(m_old, jnp.max(s, axis=-1, keepdims=True))
                m_ref[idx, rows, :] = m_new
                alpha_ref[idx, rows, :] = jnp.exp2(m_old - m_new)
                p_ref[idx, rows, :] = jnp.exp2(s - m_new).astype(BF16)
            acc_ref[idx] = (alpha_ref[idx] * acc_ref[idx]
                            + jnp.dot(p_ref[idx], v_aug, preferred_element_type=F32))

    across_tiles = n_chunks % 2 == 0

    def tile(qi, carry):
        q = q_tile(qi)
        acc_ref[...] = jnp.zeros_like(acc_ref)
        m_ref[...] = jnp.full(m_ref.shape, -jnp.inf, F32)
        if not across_tiles:
            scores(q, 0, 0)
        for kc in range(n_chunks):
            slot = kc % 2
            if kc + 1 < n_chunks:
                scores(q, kc + 1, 1 - slot)
            elif across_tiles:
                scores(q_tile(jnp.minimum(qi + 1, n_q - 1)), 0, 1 - slot)
            consume(kc, slot)
        acc1, acc2 = acc_ref[0], acc_ref[1]
        l1, l2 = acc1[:, LANES:LANES + 1], acc2[:, LANES:LANES + 1]
        o = acc1[:, :LANES] * (1.0 / l1) - lam * (acc2[:, :LANES] * (1.0 / l2))
        o = o * lax.rsqrt(jnp.mean(o * o, axis=-1, keepdims=True) + RMS_EPS)
        o = o * g_ref[...] * (1.0 - lam_init)
        o_ref[pl.ds(pl.multiple_of(qi * tq, tq), tq), :] = o.astype(o_ref.dtype)
        return carry

    if across_tiles:
        scores(q_tile(0), 0, 0)
    lax.fori_loop(0, n_q, tile, 0)


def _attn_b(qd, kd, vd, lam_p, subln_g, bsz, seq, tq, tk, lam_init):
    t = qd.shape[0]
    seq_block = pl.BlockSpec((seq, LANES), lambda b, h: (b, h))
    return pl.pallas_call(
        functools.partial(_attn_b_kernel, tq=tq, tk=tk, lam_init=lam_init),
        grid=(bsz, B_HEADS),
        in_specs=[pl.BlockSpec((4, HEAD_DIM), lambda b, h: (0, 0)),
                  pl.BlockSpec((1, LANES), lambda b, h: (0, 0)),
                  seq_block, seq_block, seq_block],
        out_specs=seq_block,
        out_shape=jax.ShapeDtypeStruct((t, 512), BF16),
        scratch_shapes=[pltpu.VMEM((2, tq, 2 * LANES), F32),
                        pltpu.VMEM((2, 2, tq, tk), F32),
                        pltpu.VMEM((2, tq, tk), BF16),
                        pltpu.VMEM((2, tq, 1), F32),
                        pltpu.VMEM((2, tq, 1), F32)],
        compiler_params=_params(("arbitrary", "arbitrary")),
        name="attn_b",
    )(lam_p, subln_g, qd, kd, vd)


HALF = D_MODEL // 2
_HI_MASK = -65536


def _pack_pair(lo, hi):
    lo_bits = lax.bitcast_convert_type(lo.astype(BF16).astype(F32), jnp.int32)
    hi_bits = lax.bitcast_convert_type(hi.astype(BF16).astype(F32), jnp.int32)
    return lax.shift_right_logical(lo_bits, 16) | hi_bits


def _unpack_pair(w):
    lo = lax.bitcast_convert_type(lax.shift_left(w, 16), F32)
    hi = lax.bitcast_convert_type(w & _HI_MASK, F32)
    return lo, hi


def _layer_norm(y, g, b):
    mu = jnp.mean(y, axis=-1, keepdims=True)
    d = y - mu
    var = jnp.mean(d * d, axis=-1, keepdims=True)
    return d * lax.rsqrt(var + LN_EPS) * g + b


def _first_argmax(vals, rowf, big):
    m = jnp.max(vals, axis=0, keepdims=True)
    idx = jnp.min(jnp.where(vals == m, rowf, big), axis=0, keepdims=True)
    return m, idx


def _post_attn_kernel(x_ref, oa_ref, ob_ref, ga_ref, gb_ref, woa_ref, wob_ref, wout_ref, g1_ref, b1_ref,
                      rwt_ref, rb_ref, tri_ref, cnt_in_ref,
                      x1_ref, x1f_ref, eidx_ref, wts_ref, rank_ref, cnt_ref, cnt_scr):
    i = pl.program_id(0)
    tm = x_ref.shape[0]

    @pl.when(i == 0)
    def _():
        cnt_scr[...] = cnt_in_ref[...]

    a = jnp.dot(oa_ref[...], woa_ref[...], preferred_element_type=F32)
    b = jnp.dot(ob_ref[...], wob_ref[...], preferred_element_type=F32)
    merged = ga_ref[...].astype(F32) * a + gb_ref[...].astype(F32) * b
    mix = jnp.dot(merged.astype(BF16), wout_ref[...], preferred_element_type=F32)
    x1 = _layer_norm(DN_ALPHA * x_ref[...] + mix, g1_ref[...], b1_ref[...])
    x1p = _pack_pair(x1[:, :HALF], x1[:, HALF:])
    for piece in range(SC_SPLIT):
        x1_ref[piece] = x1p[:, piece * PIECE:(piece + 1) * PIECE]
    x1b = x1.astype(BF16)
    x1f_ref[...] = x1

    logits = lax.dot_general(rwt_ref[...], x1b, _NT, preferred_element_type=F32)
    scores = jax.nn.sigmoid(logits)
    choice = scores + rb_ref[...]
    ninf = -jnp.inf
    grow = lax.broadcasted_iota(jnp.int32, (GROUP_SIZE, tm), 0).astype(F32)
    gscore = []
    for g in range(N_GROUPS):
        blk = choice[GROUP_SIZE * g:GROUP_SIZE * (g + 1)]
        m1, i1 = _first_argmax(blk, grow, float(GROUP_SIZE))
        m2 = jnp.max(jnp.where(grow == i1, ninf, blk), axis=0, keepdims=True)
        gscore.append(m1 + m2)
    selected = [jnp.zeros((1, tm), F32) for _ in range(N_GROUPS)]
    work = list(gscore)
    for _ in range(TOPK_GROUPS):
        best = work[0]
        for g in range(1, N_GROUPS):
            best = jnp.maximum(best, work[g])
        taken = jnp.zeros((1, tm), F32)
        for g in range(N_GROUPS):
            hit = jnp.where((work[g] == best) & (taken == 0.0), 1.0, 0.0)
            taken = jnp.maximum(taken, hit)
            selected[g] = jnp.maximum(selected[g], hit)
            work[g] = jnp.where(hit > 0.0, ninf, work[g])
    masked = jnp.concatenate(
        [jnp.where(selected[g] > 0.0, choice[GROUP_SIZE * g:GROUP_SIZE * (g + 1)], ninf) for g in range(N_GROUPS)],
        axis=0)

    rowf = lax.broadcasted_iota(jnp.int32, (N_EXPERTS, tm), 0).astype(F32)
    hits, idxs, ws = [], [], []
    for _ in range(TOP_K):
        _, idx = _first_argmax(masked, rowf, float(N_EXPERTS))
        hit = rowf == idx
        hits.append(hit)
        idxs.append(idx)
        ws.append(jnp.sum(jnp.where(hit, scores, 0.0), axis=0, keepdims=True))
        masked = jnp.where(hit, ninf, masked)
    wsum = ws[0]
    for k in range(1, TOP_K):
        wsum = wsum + ws[k]

    member = hits[0]
    for k in range(1, TOP_K):
        member = member | hits[k]
    member_f = jnp.where(member, 1.0, 0.0)
    before = jnp.dot(member_f.astype(BF16), tri_ref[...], preferred_element_type=F32) + cnt_scr[...]
    for k in range(TOP_K):
        eidx_ref[k:k + 1, :] = idxs[k].astype(jnp.int32)
        wts_ref[k:k + 1, :] = ws[k] / wsum * ROUTED_SCALE
        rank_ref[k:k + 1, :] = jnp.sum(jnp.where(hits[k], before, 0.0), axis=0, keepdims=True).astype(jnp.int32)
    cnt_scr[...] = cnt_scr[...] + jnp.sum(member_f, axis=1, keepdims=True)
    cnt_ref[...] = jnp.broadcast_to(cnt_scr[...], cnt_ref.shape).astype(jnp.int32)


def _post_attn(x2, oa, ob, ga, gb, woa, wob, wout, g1, b1, rwt, rb, cnt_in, tm):
    t = x2.shape[0]
    tri = jnp.triu(jnp.ones((tm, tm), F32), k=1).astype(BF16)
    row = lambda i: (i, 0)
    col = lambda i: (0, i)
    full = lambda i: (0, 0)
    wspec = lambda arr: pl.BlockSpec(arr.shape, full)
    weights = (woa, wob, wout, g1, b1, rwt, rb, tri, cnt_in)
    return pl.pallas_call(
        _post_attn_kernel,
        grid=(t // tm,),
        in_specs=[pl.BlockSpec((tm, D_MODEL), row), pl.BlockSpec((tm, 512), row), pl.BlockSpec((tm, 512), row),
                  pl.BlockSpec((tm, D_MODEL), row), pl.BlockSpec((tm, D_MODEL), row)] + [wspec(w) for w in weights],
        out_specs=[pl.BlockSpec((SC_SPLIT, tm, PIECE), lambda i: (0, i, 0)), pl.BlockSpec((tm, D_MODEL), row),
                   pl.BlockSpec((TOP_K, tm), col), pl.BlockSpec((TOP_K, tm), col), pl.BlockSpec((TOP_K, tm), col),
                   pl.BlockSpec((N_EXPERTS, LANES), full)],
        out_shape=[jax.ShapeDtypeStruct((SC_SPLIT, t, PIECE), jnp.int32), jax.ShapeDtypeStruct((t, D_MODEL), F32),
                   jax.ShapeDtypeStruct((TOP_K, t), jnp.int32), jax.ShapeDtypeStruct((TOP_K, t), F32),
                   jax.ShapeDtypeStruct((TOP_K, t), jnp.int32), jax.ShapeDtypeStruct((N_EXPERTS, LANES), jnp.int32)],
        scratch_shapes=[pltpu.VMEM((N_EXPERTS, 1), F32)],
        compiler_params=_params(("arbitrary",)),
        name="post_attn",
    )(x2, oa, ob, ga, gb, *weights)


def _dest_kernel(eidx_ref, rank_ref, pstart_ref, dest_ref):
    tm = eidx_ref.shape[1]
    rows = lax.broadcasted_iota(jnp.int32, (N_EXPERTS, tm), 0)
    pstart = pstart_ref[...]
    for k in range(TOP_K):
        hit = rows == eidx_ref[k:k + 1, :]
        start = jnp.sum(jnp.where(hit, pstart, 0).astype(F32), axis=0, keepdims=True).astype(jnp.int32)
        dest_ref[k:k + 1, :] = start + rank_ref[k:k + 1, :]


def _dest(eidx, rank, pstart, tm):
    t = eidx.shape[1]
    return pl.pallas_call(
        _dest_kernel,
        grid=(t // tm,),
        in_specs=[pl.BlockSpec((TOP_K, tm), lambda i: (0, i)), pl.BlockSpec((TOP_K, tm), lambda i: (0, i)),
                  pl.BlockSpec((N_EXPERTS, 1), lambda i: (0, 0))],
        out_specs=pl.BlockSpec((TOP_K, tm), lambda i: (0, i)),
        out_shape=jax.ShapeDtypeStruct((TOP_K, t), jnp.int32),
        compiler_params=_params(("arbitrary",)),
        name="dest",
    )(eidx, rank, pstart)


def _sc_mesh():
    return plsc.VectorSubcoreMesh(core_axis_name="core", subcore_axis_name="subcore",
                                  num_cores=SC_CORES, num_subcores=SC_SUBCORES)


def _sc_scatter_rows(xs_groups, dests, n_rows):
    n_groups = len(xs_groups)

    @functools.partial(pl.kernel, out_type=jax.ShapeDtypeStruct((SC_SPLIT, n_rows, PIECE), jnp.int32),
                       mesh=_sc_mesh(), scratch_types=[pltpu.SemaphoreType.DMA])
    def scatter(*refs):
        out_hbm, sem = refs[2 * n_groups], refs[2 * n_groups + 1]
        for g in range(n_groups):
            x_hbm, idx_hbm = refs[2 * g], refs[2 * g + 1]
            for piece in range(SC_SPLIT):
                def window(x_vmem, idx_vmem, piece=piece):
                    copies = [pltpu.async_copy(x_vmem, out_hbm.at[piece].at[idx_vmem.at[k]], sem)
                              for k in range(TOP_K)]
                    for copy in copies:
                        copy.wait()

                pltpu.emit_pipeline(
                    window,
                    grid=(x_hbm.shape[1] // SC_WINDOW,),
                    in_specs=[pl.BlockSpec((SC_WINDOW, PIECE), lambda i: (i, 0)),
                              pl.BlockSpec((TOP_K, SC_WINDOW), lambda i: (0, i))],
                    out_specs=[],
                    core_axis_name=("core", "subcore"),
                    dimension_semantics=(pltpu.PARALLEL,),
                )(x_hbm.at[piece], idx_hbm)

    args = []
    for x, d in zip(xs_groups, dests):
        args += [x, d]
    return scatter(*args)


def _experts_kernel(cb_ref, cnt_ref, xs_ref, wg_ref, wu_ref, wd_ref, ys_ref,
                    wg_scr, wu_scr, wd_scr, xbuf, ybuf, sem_in, sem_out):
    e = pl.program_id(0)
    g0, g1, total = cb_ref[e], cb_ref[e + 1], cb_ref[N_EXPERTS]

    def rows(g):
        return pl.ds(pl.multiple_of(g * EXPERT_ROWS, EXPERT_ROWS), EXPERT_ROWS)

    def x_copy(g):
        s = g % EXPERT_SLOTS
        return pltpu.make_async_copy(xs_ref.at[:, rows(g), :], xbuf.at[s], sem_in.at[s])

    def y_copy(g):
        s = g % EXPERT_SLOTS
        return pltpu.make_async_copy(ybuf.at[s], ys_ref.at[:, rows(g), :], sem_out.at[s])

    @pl.when(e == 0)
    def _():
        for g in range(EXPERT_SLOTS - 1):
            pl.when(g < total)(lambda g=g: x_copy(g).start())

    @pl.when(g1 > g0)
    def _():
        wg_scr[...] = wg_ref[0].astype(BF16)
        wu_scr[...] = wu_ref[0].astype(BF16)
        wd_scr[...] = wd_ref[0].astype(BF16)

    def block(g, carry):
        slot = g % EXPERT_SLOTS

        @pl.when(g + EXPERT_SLOTS - 1 < total)
        def _():
            x_copy(g + EXPERT_SLOTS - 1).start()

        x_copy(g).wait()

        @pl.when(g >= EXPERT_SLOTS)
        def _():
            y_copy(g - EXPERT_SLOTS).wait()

        valid = lax.broadcasted_iota(jnp.int32, (EXPERT_ROWS, PIECE), 0) < cnt_ref[e] - (g - g0) * EXPERT_ROWS
        hg = hu = None
        for piece in range(SC_SPLIT):
            for part, col0 in zip(_unpack_pair(xbuf[slot, piece]), (piece * PIECE, HALF + piece * PIECE)):
                xb = jnp.where(valid, part, 0.0).astype(BF16)
                pg = jnp.dot(xb, wg_scr[col0:col0 + PIECE], preferred_element_type=F32)
                pu = jnp.dot(xb, wu_scr[col0:col0 + PIECE], preferred_element_type=F32)
                hg = pg if hg is None else hg + pg
                hu = pu if hu is None else hu + pu
        h = ((hg * jax.nn.sigmoid(hg)) * hu).astype(BF16)
        y = _pack_pair(jnp.dot(h, wd_scr[:, :HALF], preferred_element_type=F32),
                       jnp.dot(h, wd_scr[:, HALF:], preferred_element_type=F32))
        for piece in range(SC_SPLIT):
            ybuf[slot, piece] = y[:, piece * PIECE:(piece + 1) * PIECE]
        y_copy(g).start()
        return carry

    lax.fori_loop(g0, g1, block, 0)

    @pl.when(e == N_EXPERTS - 1)
    def _():
        for back in range(EXPERT_SLOTS, 0, -1):
            pl.when(total >= back)(lambda back=back: y_copy(total - back).wait())


def _experts(cum_blocks, counts, xs, w_gate, w_up, w_down):
    n_rows = xs.shape[1]
    wmap = lambda e, cb, cnt: (e, 0, 0)
    grid_spec = pltpu.PrefetchScalarGridSpec(
        num_scalar_prefetch=2,
        grid=(N_EXPERTS,),
        in_specs=[pl.BlockSpec(memory_space=pl.ANY),
                  pl.BlockSpec((1, D_MODEL, D_EXPERT), wmap),
                  pl.BlockSpec((1, D_MODEL, D_EXPERT), wmap),
                  pl.BlockSpec((1, D_EXPERT, D_MODEL), wmap)],
        out_specs=pl.BlockSpec(memory_space=pl.ANY),
        scratch_shapes=[pltpu.VMEM((D_MODEL, D_EXPERT), BF16), pltpu.VMEM((D_MODEL, D_EXPERT), BF16),
                        pltpu.VMEM((D_EXPERT, D_MODEL), BF16),
                        pltpu.VMEM((EXPERT_SLOTS, SC_SPLIT, EXPERT_ROWS, PIECE), jnp.int32),
                        pltpu.VMEM((EXPERT_SLOTS, SC_SPLIT, EXPERT_ROWS, PIECE), jnp.int32),
                        pltpu.SemaphoreType.DMA((EXPERT_SLOTS,)), pltpu.SemaphoreType.DMA((EXPERT_SLOTS,))])
    return pl.pallas_call(
        _experts_kernel,
        grid_spec=grid_spec,
        out_shape=jax.ShapeDtypeStruct((SC_SPLIT, n_rows, PIECE), jnp.int32),
        compiler_params=_params(("arbitrary",)),
        name="experts",
    )(cum_blocks, counts, xs, w_gate, w_up, w_down)


def _sc_gather_rows(table, idx):
    n, d = idx.shape[0], table.shape[2]

    @functools.partial(pl.kernel, out_type=jax.ShapeDtypeStruct((SC_SPLIT, n, d), table.dtype), mesh=_sc_mesh(),
                       scratch_types=[])
    def gather(table_hbm, idx_hbm, out_hbm):
        for piece in range(SC_SPLIT):
            def window(idx_vmem, out_vmem, piece=piece):
                pltpu.sync_copy(table_hbm.at[piece].at[idx_vmem.at[0]], out_vmem)

            pltpu.emit_pipeline(
                window,
                grid=(n // SC_WINDOW,),
                in_specs=[pl.BlockSpec((1, SC_WINDOW), lambda i: (0, i))],
                out_specs=[pl.BlockSpec((SC_WINDOW, d), lambda i: (i, 0))],
                core_axis_name=("core", "subcore"),
                dimension_semantics=(pltpu.PARALLEL,),
            )(idx_hbm, out_hbm.at[piece])

    return gather(table, idx.reshape(1, n))


def _combine_kernel(w_ref, x1_ref, g2_ref, b2_ref, wsg_ref, wsu_ref, wsd_ref, rows_ref, o_ref):
    w = w_ref[...]
    x1 = x1_ref[...]
    x1b = x1.astype(BF16)
    hg = jnp.dot(x1b, wsg_ref[...], preferred_element_type=F32)
    hu = jnp.dot(x1b, wsu_ref[...], preferred_element_type=F32)
    hs = (hg * jax.nn.sigmoid(hg)) * hu
    base = DN_ALPHA * x1 + jnp.dot(hs.astype(BF16), wsd_ref[...], preferred_element_type=F32)
    acc = [[base[:, half * HALF + p * PIECE:half * HALF + (p + 1) * PIECE] for p in range(SC_SPLIT)]
           for half in range(2)]
    for k in range(TOP_K):
        wk = w[:, k:k + 1]
        for p in range(SC_SPLIT):
            lo, hi = _unpack_pair(rows_ref[p, k])
            acc[0][p] = acc[0][p] + lo * wk
            acc[1][p] = acc[1][p] + hi * wk
    o_ref[...] = _layer_norm(jnp.concatenate(acc[0] + acc[1], axis=1), g2_ref[...], b2_ref[...])


def _combine(wts_tk, x1, g2, b2, wsg, wsu, wsd, rows, tm):
    t = x1.shape[0]
    full = lambda i: (0, 0)
    return pl.pallas_call(
        _combine_kernel,
        grid=(t // tm,),
        in_specs=[pl.BlockSpec((tm, TOP_K), lambda i: (i, 0)),
                  pl.BlockSpec((tm, D_MODEL), lambda i: (i, 0)),
                  pl.BlockSpec((1, D_MODEL), full),
                  pl.BlockSpec((1, D_MODEL), full),
                  pl.BlockSpec(wsg.shape, full), pl.BlockSpec(wsu.shape, full), pl.BlockSpec(wsd.shape, full),
                  pl.BlockSpec((SC_SPLIT, TOP_K, tm, PIECE), lambda i: (0, 0, i, 0))],
        out_specs=pl.BlockSpec((tm, D_MODEL), lambda i: (i, 0)),
        out_shape=jax.ShapeDtypeStruct((t, D_MODEL), F32),
        compiler_params=_params(("arbitrary",)),
        name="combine",
    )(wts_tk, x1, g2, b2, wsg, wsu, wsd, rows)


def _tile(n, pref):
    t = pref
    while n % t:
        t //= 2
    return t


def _token_mixers(x, layer_idx, wp, cnt_in):
    bsz, seq, _ = x.shape
    t = bsz * seq
    x2 = x.reshape(t, D_MODEL)
    lam_init = 0.8 - 0.6 * math.exp(-0.3 * layer_idx)

    cos_t, sin_t = _rope_tables(seq)
    qa, ka, va, qd, kd, vd, ga, gb = _in_proj(x2, wp["w_in"], cos_t, sin_t, seq, _tile(seq, 256))
    oa = _attn_a(qa, ka, va, wp["sink"], bsz, seq, _tile(seq, 512))
    ob = _attn_b(qd, kd, vd, wp["lam_p"], wp["subln_g"], bsz, seq, _tile(seq, 512), _tile(seq, 1024), lam_init)
    return _post_attn(
        x2, oa, ob, ga, gb, wp["w_o_a"], wp["w_o_b"], wp["w_out"], wp["ln1_g"], wp["ln1_b"],
        wp["router_wt"], wp["router_bias"], cnt_in, _tile(t, 512))


def _encoder_layer(xs_in, layer_idx, wp):
    cnt_in = jnp.zeros((N_EXPERTS, 1), F32)
    routed = []
    for x in xs_in:
        x1p, x1f, eidx, wts, rank, cnt = _token_mixers(x, layer_idx, wp, cnt_in)
        cnt_in = cnt[:, :1].astype(F32)
        routed.append((x1p, x1f, eidx, wts, rank))

    n_assign = sum(x.shape[0] * x.shape[1] for x in xs_in) * TOP_K
    n_blk = -(-(n_assign + N_EXPERTS * (EXPERT_ROWS - 1)) // EXPERT_ROWS)
    counts = cnt[:, 0]
    padded = ((counts + EXPERT_ROWS - 1) // EXPERT_ROWS) * EXPERT_ROWS
    pend = jnp.cumsum(padded)
    pstart = pend - padded
    cum_blocks = jnp.concatenate([jnp.zeros((1,), jnp.int32), (pend // EXPERT_ROWS).astype(jnp.int32)])
    pstart_col = pstart.astype(jnp.int32)[:, None]

    dests = [_dest(eidx, rank, pstart_col, _tile(eidx.shape[1], 512)) for _, _, eidx, _, rank in routed]
    xs = _sc_scatter_rows([r[0] for r in routed], dests, n_blk * EXPERT_ROWS)
    ys = _experts(cum_blocks, counts.astype(jnp.int32), xs, wp["w_gate"], wp["w_up"], wp["w_down"])
    outs = []
    for x, dest, (x1p, x1f, eidx, wts, rank) in zip(xs_in, dests, routed):
        t = x1f.shape[0]
        rows = _sc_gather_rows(ys, dest.reshape(-1)).reshape(SC_SPLIT, TOP_K, t, PIECE)
        y = _combine(wts.T, x1f, wp["ln2_g"], wp["ln2_b"], wp["ws_gate"], wp["ws_up"], wp["ws_down"], rows,
                     _tile(t, 256))
        outs.append(y.reshape(x.shape))
    return outs


def kernel(x_prompt, x_sample, w_in, attn_sink, lambda_q1, lambda_k1, lambda_q2, lambda_k2, subln_g, w_o_a, w_o_b, w_out, ln1_g, ln1_b, router_w, router_bias, w_gate, w_up, w_down, ws_gate, ws_up, ws_down, ln2_g, ln2_b):
    y_prompt, y_sample = x_prompt, x_sample
    for l in range(DEPTH):
        wp = {
            "w_in": _prep_w_in(w_in[l]),
            "sink": attn_sink[l].astype(F32),
            "lam_p": jnp.stack([lambda_q1[l], lambda_k1[l], lambda_q2[l], lambda_k2[l]]).astype(F32),
            "subln_g": subln_g[l].astype(F32)[None, :],
            "w_o_a": w_o_a[l].astype(BF16), "w_o_b": w_o_b[l].astype(BF16), "w_out": w_out[l].astype(BF16),
            "ln1_g": ln1_g[l].astype(F32)[None, :], "ln1_b": ln1_b[l].astype(F32)[None, :],
            "router_wt": router_w[l].T.astype(BF16), "router_bias": router_bias[l].astype(F32)[:, None],
            "w_gate": w_gate[l], "w_up": w_up[l], "w_down": w_down[l],
            "ws_gate": ws_gate[l].astype(BF16), "ws_up": ws_up[l].astype(BF16), "ws_down": ws_down[l].astype(BF16),
            "ln2_g": ln2_g[l].astype(F32)[None, :], "ln2_b": ln2_b[l].astype(F32)[None, :],
        }
        y_prompt, y_sample = _encoder_layer((y_prompt, y_sample), l, wp)
    return (y_prompt, y_sample)
```

```python
import functools
import math

import jax
import jax.numpy as jnp
from jax import lax
from jax.experimental import pallas as pl
from jax.experimental.pallas import tpu as pltpu
from jax.experimental.pallas import tpu_sc as plsc

D_MODEL = 1024
HEAD_DIM = 64
ROPE_THETA = 10000.0
BLOCK = 128
A_Q_HEADS = 8
A_KV_HEADS = 2
WINDOW = 128
B_HEADS = 4
N_EXPERTS = 256
TOP_K = 8
N_GROUPS = 8
TOPK_GROUPS = 4
GROUP_SIZE = N_EXPERTS // N_GROUPS
D_EXPERT = 256
ROUTED_SCALE = 2.5
EXPERT_ROWS = 512
EXPERT_SLOTS = 4
DEPTH = 1
DN_ALPHA = (2 * DEPTH) ** 0.25
LN_EPS = 1e-5
RMS_EPS = 1e-5
NEG = -1e30
LOG2E = math.log2(math.e)
SOFTMAX_ROWS = 16

LANES = 128
SC_CORES = 2
SC_SUBCORES = 16
SC_WINDOW = 128
SC_SPLIT = 2
PIECE = D_MODEL // 2 // SC_SPLIT
VMEM_LIMIT_BYTES = 56 * 1024 * 1024

F32 = jnp.float32
BF16 = jnp.bfloat16
_NT = (((1,), (1,)), ((), ()))


def _params(sem, vmem=VMEM_LIMIT_BYTES):
    return pltpu.CompilerParams(dimension_semantics=sem, vmem_limit_bytes=vmem)


_C_QA, _C_KA, _C_VA, _C_QD, _C_KD, _C_VD, _C_GA, _C_GB, _C_END = 0, 512, 768, 1024, 1536, 2048, 2560, 3584, 4608


def _prep_w_in(w_in):
    cuts = [0, 512, 640, 768, 1280, 1792, 2304, 3328, 4352]
    qa, ka, va, qd, kd, vd, ga, gb = [w_in[:, cuts[i]:cuts[i + 1]] for i in range(8)]
    dup = lambda w: jnp.concatenate([w[:, :64], w[:, :64], w[:, 64:], w[:, 64:]], axis=1)
    return jnp.concatenate([qa, dup(ka), dup(va), qd, kd, vd, ga, gb], axis=1).astype(BF16)


def _rope_tables(s):
    half = HEAD_DIM // 2
    inv = 1.0 / (ROPE_THETA ** (jnp.arange(half, dtype=F32) / half))
    ang = jnp.arange(s, dtype=F32)[:, None] * inv[None, :]
    cos, sin = jnp.cos(ang), jnp.sin(ang)
    return (jnp.concatenate([cos, cos, cos, cos], axis=1),
            jnp.concatenate([-sin, sin, -sin, sin], axis=1))


def _in_proj_kernel(x_ref, w_ref, cos_ref, sin_ref,
                    qa_ref, ka_ref, va_ref, qd_ref, kd_ref, vd_ref, ga_ref, gb_ref):
    xb = x_ref[...].astype(BF16)
    cos = cos_ref[...]
    sin = sin_ref[...]
    lane = lax.broadcasted_iota(jnp.int32, cos.shape, 1)
    first_half = (lane & (HEAD_DIM // 2)) == 0

    def proj(c0, c1):
        return jnp.dot(xb, w_ref[:, c0:c1], preferred_element_type=F32)

    def rope_store(u, out_ref, scale):
        for j in range(u.shape[1] // LANES):
            uj = u[:, LANES * j:LANES * (j + 1)]
            rot = jnp.where(first_half, pltpu.roll(uj, LANES - 32, 1), pltpu.roll(uj, 32, 1))
            r = uj * cos + rot * sin
            if scale != 1.0:
                r = r * scale
            out_ref[:, LANES * j:LANES * (j + 1)] = r.astype(out_ref.dtype)

    scale = HEAD_DIM ** -0.5
    rope_store(proj(_C_QA, _C_KA), qa_ref, scale)
    rope_store(proj(_C_KA, _C_VA), ka_ref, 1.0)
    va_ref[...] = proj(_C_VA, _C_QD).astype(va_ref.dtype)
    rope_store(proj(_C_QD, _C_KD), qd_ref, scale * LOG2E)
    rope_store(proj(_C_KD, _C_VD), kd_ref, 1.0)
    vd_ref[...] = proj(_C_VD, _C_GA).astype(vd_ref.dtype)
    ga_ref[...] = jax.nn.sigmoid(proj(_C_GA, _C_GB)).astype(ga_ref.dtype)
    gb_ref[...] = jax.nn.sigmoid(proj(_C_GB, _C_END)).astype(gb_ref.dtype)


def _in_proj(x2, w_perm, cos_t, sin_t, seq, tm):
    t = x2.shape[0]
    nseq = seq // tm
    row = lambda i: (i, 0)
    widths = (512, 256, 256, 512, 512, 512, 1024, 1024)
    return pl.pallas_call(
        _in_proj_kernel,
        grid=(t // tm,),
        in_specs=[pl.BlockSpec((tm, D_MODEL), row),
                  pl.BlockSpec((D_MODEL, _C_END), lambda i: (0, 0)),
                  pl.BlockSpec((tm, LANES), lambda i: (i % nseq, 0)),
                  pl.BlockSpec((tm, LANES), lambda i: (i % nseq, 0))],
        out_specs=[pl.BlockSpec((tm, w), row) for w in widths],
        out_shape=[jax.ShapeDtypeStruct((t, w), BF16) for w in widths],
        compiler_params=_params(("arbitrary",)),
        name="in_proj",
    )(x2, w_perm, cos_t, sin_t)


def _attn_a_kernel(sink_ref, q_ref, kp_ref, kc_ref, kn_ref, vp_ref, vc_ref, vn_ref, o_ref, *, seq, tq):
    i = pl.program_id(1)
    kk = jnp.concatenate([kp_ref[...], kc_ref[...], kn_ref[...]], axis=0)
    vv = jnp.concatenate([vp_ref[...], vc_ref[...], vn_ref[...]], axis=0)
    lane = lax.broadcasted_iota(jnp.int32, (kk.shape[0], LANES), 1)
    lo = lane < HEAD_DIM
    zero = jnp.zeros((kk.shape[0], LANES), BF16)
    k_lo = [jnp.where(lo, kk[:, LANES * h:LANES * (h + 1)], zero) for h in range(A_KV_HEADS)]
    k_hi = [jnp.where(lo, zero, kk[:, LANES * h:LANES * (h + 1)]) for h in range(A_KV_HEADS)]
    v_lo = [jnp.where(lo, vv[:, LANES * h:LANES * (h + 1)], zero) for h in range(A_KV_HEADS)]
    v_hi = [jnp.where(lo, zero, vv[:, LANES * h:LANES * (h + 1)]) for h in range(A_KV_HEADS)]

    qi = lax.broadcasted_iota(jnp.int32, (BLOCK, 3 * BLOCK), 0)
    kj = lax.broadcasted_iota(jnp.int32, (BLOCK, 3 * BLOCK), 1)
    band = jnp.abs(kj - BLOCK - qi) <= WINDOW
    head_of_row = lax.broadcasted_iota(jnp.int32, (A_Q_HEADS * BLOCK, 1), 0) // BLOCK
    snk = jnp.zeros((A_Q_HEADS * BLOCK, 1), F32)
    for head in range(A_Q_HEADS):
        snk = jnp.where(head_of_row == head, sink_ref[head], snk)
    for j in range(tq // BLOCK):
        kpos = i * tq + (j - 1) * BLOCK + kj
        mask = band & (kpos >= 0) & (kpos < seq)
        r0, r1 = j * BLOCK, (j + 3) * BLOCK
        pieces = []
        for c in range(A_Q_HEADS // 2):
            q2 = q_ref[j * BLOCK:(j + 1) * BLOCK, LANES * c:LANES * (c + 1)]
            for kx in (k_lo[c // 2], k_hi[c // 2]):
                s = lax.dot_general(q2, kx[r0:r1], _NT, preferred_element_type=F32)
                pieces.append(jnp.where(mask, s, NEG))
        s = jnp.concatenate(pieces, axis=0)
        m = jnp.maximum(jnp.max(s, axis=-1, keepdims=True), snk)
        p = jnp.exp(s - m)
        inv = 1.0 / (jnp.sum(p, axis=-1, keepdims=True) + jnp.exp(snk - m))
        pb = p.astype(BF16)
        for c in range(A_Q_HEADS // 2):
            out = None
            for half, vx in enumerate((v_lo[c // 2], v_hi[c // 2])):
                rows = slice((2 * c + half) * BLOCK, (2 * c + half + 1) * BLOCK)
                pv = jnp.dot(pb[rows], vx[r0:r1], preferred_element_type=F32) * inv[rows]
                out = pv if out is None else out + pv
            o_ref[j * BLOCK:(j + 1) * BLOCK, LANES * c:LANES * (c + 1)] = out.astype(o_ref.dtype)


def _attn_a(qa, ka, va, sink, bsz, seq, tq):
    t = qa.shape[0]
    nq = seq // tq
    nb = seq // BLOCK
    r = tq // BLOCK
    cur = lambda b, i: (b * nq + i, 0)
    prev = lambda b, i: (b * nb + jnp.maximum(i * r - 1, 0), 0)
    nxt = lambda b, i: (b * nb + jnp.minimum((i + 1) * r, nb - 1), 0)
    kv_specs = [pl.BlockSpec((BLOCK, 256), prev), pl.BlockSpec((tq, 256), cur), pl.BlockSpec((BLOCK, 256), nxt)]
    return pl.pallas_call(
        functools.partial(_attn_a_kernel, seq=seq, tq=tq),
        grid=(bsz, nq),
        in_specs=[pl.BlockSpec(memory_space=pltpu.SMEM), pl.BlockSpec((tq, 512), cur)] + kv_specs + kv_specs,
        out_specs=pl.BlockSpec((tq, 512), cur),
        out_shape=jax.ShapeDtypeStruct((t, 512), BF16),
        compiler_params=_params(("arbitrary", "arbitrary")),
        name="attn_a",
    )(sink, qa, ka, ka, ka, va, va, va)


def _attn_b_kernel(lam_ref, g_ref, q_ref, k_ref, v_ref, o_ref, acc_ref, s_ref, p_ref, m_ref, alpha_ref,
                   *, tq, tk, lam_init):
    seq = q_ref.shape[0]
    n_q, n_chunks = seq // tq, seq // tk
    lo = lax.broadcasted_iota(jnp.int32, (tq, LANES), 1) < HEAD_DIM
    zero = jnp.zeros((tq, LANES), BF16)
    ones_col = jnp.where(lax.broadcasted_iota(jnp.int32, (tk, LANES), 1) == 0, 1.0, 0.0).astype(BF16)

    lp = lam_ref[...]
    lam = (jnp.exp(jnp.sum(lp[0:1] * lp[1:2], axis=-1, keepdims=True))
           - jnp.exp(jnp.sum(lp[2:3] * lp[3:4], axis=-1, keepdims=True)) + lam_init)

    def q_tile(qi):
        q = q_ref[pl.ds(pl.multiple_of(qi * tq, tq), tq), :]
        return jnp.concatenate([jnp.where(lo, q, zero), jnp.where(lo, zero, q)], axis=0)

    def scores(q2, kc, slot):
        s_ref[slot] = lax.dot_general(q2, k_ref[kc * tk:(kc + 1) * tk, :], _NT, preferred_element_type=F32)

    def consume(kc, slot):
        v_aug = jnp.concatenate([v_ref[kc * tk:(kc + 1) * tk, :], ones_col], axis=1)
        for r in range(2 * tq // SOFTMAX_ROWS):
            rows = slice(r * SOFTMAX_ROWS, (r + 1) * SOFTMAX_ROWS)
            s = s_ref[slot, rows, :]
            m_old = m_ref[rows, :]
            m_new = jnp.maximum(m_old, jnp.max(s, axis=-1, keepdims=True))
            m_ref[rows, :] = m_new
            alpha_ref[rows, :] = jnp.exp2(m_old - m_new)
            p_ref[rows, :] = jnp.exp2(s - m_new).astype(BF16)
        acc_ref[...] = alpha_ref[...] * acc_ref[...] + jnp.dot(p_ref[...], v_aug, preferred_element_type=F32)

    across_tiles = n_chunks % 2 == 0

    def tile(qi, carry):
        q = q_tile(qi)
        acc_ref[...] = jnp.zeros_like(acc_ref)
        m_ref[...] = jnp.full(m_ref.shape, -jnp.inf, F32)
        if not across_tiles:
            scores(q, 0, 0)
        for kc in range(n_chunks):
            slot = kc % 2
            if kc + 1 < n_chunks:
                scores(q, kc + 1, 1 - slot)
            elif across_tiles:
                scores(q_tile(jnp.minimum(qi + 1, n_q - 1)), 0, 1 - slot)
            consume(kc, slot)
        acc1, acc2 = acc_ref[:tq], acc_ref[tq:]
        l1, l2 = acc1[:, LANES:LANES + 1], acc2[:, LANES:LANES + 1]
        o = acc1[:, :LANES] * (1.0 / l1) - lam * (acc2[:, :LANES] * (1.0 / l2))
        o = o * lax.rsqrt(jnp.mean(o * o, axis=-1, keepdims=True) + RMS_EPS)
        o = o * g_ref[...] * (1.0 - lam_init)
        o_ref[pl.ds(pl.multiple_of(qi * tq, tq), tq), :] = o.astype(o_ref.dtype)
        return carry

    if across_tiles:
        scores(q_tile(0), 0, 0)
    lax.fori_loop(0, n_q, tile, 0)


def _attn_b(qd, kd, vd, lam_p, subln_g, bsz, seq, tq, tk, lam_init):
    t = qd.shape[0]
    seq_block = pl.BlockSpec((seq, LANES), lambda b, h: (b, h))
    return pl.pallas_call(
        functools.partial(_attn_b_kernel, tq=tq, tk=tk, lam_init=lam_init),
        grid=(bsz, B_HEADS),
        in_specs=[pl.BlockSpec((4, HEAD_DIM), lambda b, h: (0, 0)),
                  pl.BlockSpec((1, LANES), lambda b, h: (0, 0)),
                  seq_block, seq_block, seq_block],
        out_specs=seq_block,
        out_shape=jax.ShapeDtypeStruct((t, 512), BF16),
        scratch_shapes=[pltpu.VMEM((2 * tq, 2 * LANES), F32),
                        pltpu.VMEM((2, 2 * tq, tk), F32),
                        pltpu.VMEM((2 * tq, tk), BF16),
                        pltpu.VMEM((2 * tq, 1), F32),
                        pltpu.VMEM((2 * tq, 1), F32)],
        compiler_params=_params(("arbitrary", "arbitrary")),
        name="attn_b",
    )(lam_p, subln_g, qd, kd, vd)


HALF = D_MODEL // 2
_HI_MASK = -65536


def _pack_pair(lo, hi):
    lo_bits = lax.bitcast_convert_type(lo.astype(BF16).astype(F32), jnp.int32)
    hi_bits = lax.bitcast_convert_type(hi.astype(BF16).astype(F32), jnp.int32)
    return lax.shift_right_logical(lo_bits, 16) | hi_bits


def _unpack_pair(w):
    lo = lax.bitcast_convert_type(lax.shift_left(w, 16), F32)
    hi = lax.bitcast_convert_type(w & _HI_MASK, F32)
    return lo, hi


def _layer_norm(y, g, b):
    mu = jnp.mean(y, axis=-1, keepdims=True)
    d = y - mu
    var = jnp.mean(d * d, axis=-1, keepdims=True)
    return d * lax.rsqrt(var + LN_EPS) * g + b


def _first_argmax(vals, rowf, big):
    m = jnp.max(vals, axis=0, keepdims=True)
    idx = jnp.min(jnp.where(vals == m, rowf, big), axis=0, keepdims=True)
    return m, idx


def _post_attn_kernel(x_ref, oa_ref, ob_ref, ga_ref, gb_ref, woa_ref, wob_ref, wout_ref, g1_ref, b1_ref,
                      rwt_ref, rb_ref, tri_ref, cnt_in_ref,
                      x1_ref, x1f_ref, eidx_ref, wts_ref, rank_ref, cnt_ref, cnt_scr):
    i = pl.program_id(0)
    tm = x_ref.shape[0]

    @pl.when(i == 0)
    def _():
        cnt_scr[...] = cnt_in_ref[...]

    a = jnp.dot(oa_ref[...], woa_ref[...], preferred_element_type=F32)
    b = jnp.dot(ob_ref[...], wob_ref[...], preferred_element_type=F32)
    merged = ga_ref[...].astype(F32) * a + gb_ref[...].astype(F32) * b
    mix = jnp.dot(merged.astype(BF16), wout_ref[...], preferred_element_type=F32)
    x1 = _layer_norm(DN_ALPHA * x_ref[...] + mix, g1_ref[...], b1_ref[...])
    x1p = _pack_pair(x1[:, :HALF], x1[:, HALF:])
    for piece in range(SC_SPLIT):
        x1_ref[piece] = x1p[:, piece * PIECE:(piece + 1) * PIECE]
    x1b = x1.astype(BF16)
    x1f_ref[...] = x1

    logits = lax.dot_general(rwt_ref[...], x1b, _NT, preferred_element_type=F32)
    scores = jax.nn.sigmoid(logits)
    choice = scores + rb_ref[...]
    ninf = -jnp.inf
    grow = lax.broadcasted_iota(jnp.int32, (GROUP_SIZE, tm), 0).astype(F32)
    gscore = []
    for g in range(N_GROUPS):
        blk = choice[GROUP_SIZE * g:GROUP_SIZE * (g + 1)]
        m1, i1 = _first_argmax(blk, grow, float(GROUP_SIZE))
        m2 = jnp.max(jnp.where(grow == i1, ninf, blk), axis=0, keepdims=True)
        gscore.append(m1 + m2)
    selected = [jnp.zeros((1, tm), F32) for _ in range(N_GROUPS)]
    work = list(gscore)
    for _ in range(TOPK_GROUPS):
        best = work[0]
        for g in range(1, N_GROUPS):
            best = jnp.maximum(best, work[g])
        taken = jnp.zeros((1, tm), F32)
        for g in range(N_GROUPS):
            hit = jnp.where((work[g] == best) & (taken == 0.0), 1.0, 0.0)
            taken = jnp.maximum(taken, hit)
            selected[g] = jnp.maximum(selected[g], hit)
            work[g] = jnp.where(hit > 0.0, ninf, work[g])
    masked = jnp.concatenate(
        [jnp.where(selected[g] > 0.0, choice[GROUP_SIZE * g:GROUP_SIZE * (g + 1)], ninf) for g in range(N_GROUPS)],
        axis=0)

    rowf = lax.broadcasted_iota(jnp.int32, (N_EXPERTS, tm), 0).astype(F32)
    hits, idxs, ws = [], [], []
    for _ in range(TOP_K):
        _, idx = _first_argmax(masked, rowf, float(N_EXPERTS))
        hit = rowf == idx
        hits.append(hit)
        idxs.append(idx)
        ws.append(jnp.sum(jnp.where(hit, scores, 0.0), axis=0, keepdims=True))
        masked = jnp.where(hit, ninf, masked)
    wsum = ws[0]
    for k in range(1, TOP_K):
        wsum = wsum + ws[k]

    member = hits[0]
    for k in range(1, TOP_K):
        member = member | hits[k]
    member_f = jnp.where(member, 1.0, 0.0)
    before = jnp.dot(member_f.astype(BF16), tri_ref[...], preferred_element_type=F32) + cnt_scr[...]
    for k in range(TOP_K):
        eidx_ref[k:k + 1, :] = idxs[k].astype(jnp.int32)
        wts_ref[k:k + 1, :] = ws[k] / wsum * ROUTED_SCALE
        rank_ref[k:k + 1, :] = jnp.sum(jnp.where(hits[k], before, 0.0), axis=0, keepdims=True).astype(jnp.int32)
    cnt_scr[...] = cnt_scr[...] + jnp.sum(member_f, axis=1, keepdims=True)
    cnt_ref[...] = jnp.broadcast_to(cnt_scr[...], cnt_ref.shape).astype(jnp.int32)


def _post_attn(x2, oa, ob, ga, gb, woa, wob, wout, g1, b1, rwt, rb, cnt_in, tm):
    t = x2.shape[0]
    tri = jnp.triu(jnp.ones((tm, tm), F32), k=1).astype(BF16)
    row = lambda i: (i, 0)
    col = lambda i: (0, i)
    full = lambda i: (0, 0)
    wspec = lambda arr: pl.BlockSpec(arr.shape, full)
    weights = (woa, wob, wout, g1, b1, rwt, rb, tri, cnt_in)
    return pl.pallas_call(
        _post_attn_kernel,
        grid=(t // tm,),
        in_specs=[pl.BlockSpec((tm, D_MODEL), row), pl.BlockSpec((tm, 512), row), pl.BlockSpec((tm, 512), row),
                  pl.BlockSpec((tm, D_MODEL), row), pl.BlockSpec((tm, D_MODEL), row)] + [wspec(w) for w in weights],
        out_specs=[pl.BlockSpec((SC_SPLIT, tm, PIECE), lambda i: (0, i, 0)), pl.BlockSpec((tm, D_MODEL), row),
                   pl.BlockSpec((TOP_K, tm), col), pl.BlockSpec((TOP_K, tm), col), pl.BlockSpec((TOP_K, tm), col),
                   pl.BlockSpec((N_EXPERTS, LANES), full)],
        out_shape=[jax.ShapeDtypeStruct((SC_SPLIT, t, PIECE), jnp.int32), jax.ShapeDtypeStruct((t, D_MODEL), F32),
                   jax.ShapeDtypeStruct((TOP_K, t), jnp.int32), jax.ShapeDtypeStruct((TOP_K, t), F32),
                   jax.ShapeDtypeStruct((TOP_K, t), jnp.int32), jax.ShapeDtypeStruct((N_EXPERTS, LANES), jnp.int32)],
        scratch_shapes=[pltpu.VMEM((N_EXPERTS, 1), F32)],
        compiler_params=_params(("arbitrary",)),
        name="post_attn",
    )(x2, oa, ob, ga, gb, *weights)


def _dest_kernel(eidx_ref, rank_ref, pstart_ref, dest_ref):
    tm = eidx_ref.shape[1]
    rows = lax.broadcasted_iota(jnp.int32, (N_EXPERTS, tm), 0)
    pstart = pstart_ref[...]
    for k in range(TOP_K):
        hit = rows == eidx_ref[k:k + 1, :]
        start = jnp.sum(jnp.where(hit, pstart, 0).astype(F32), axis=0, keepdims=True).astype(jnp.int32)
        dest_ref[k:k + 1, :] = start + rank_ref[k:k + 1, :]


def _dest(eidx, rank, pstart, tm):
    t = eidx.shape[1]
    return pl.pallas_call(
        _dest_kernel,
        grid=(t // tm,),
        in_specs=[pl.BlockSpec((TOP_K, tm), lambda i: (0, i)), pl.BlockSpec((TOP_K, tm), lambda i: (0, i)),
                  pl.BlockSpec((N_EXPERTS, 1), lambda i: (0, 0))],
        out_specs=pl.BlockSpec((TOP_K, tm), lambda i: (0, i)),
        out_shape=jax.ShapeDtypeStruct((TOP_K, t), jnp.int32),
        compiler_params=_params(("arbitrary",)),
        name="dest",
    )(eidx, rank, pstart)


def _sc_mesh():
    return plsc.VectorSubcoreMesh(core_axis_name="core", subcore_axis_name="subcore",
                                  num_cores=SC_CORES, num_subcores=SC_SUBCORES)


def _sc_scatter_rows(xs_groups, dests, n_rows):
    n_groups = len(xs_groups)

    @functools.partial(pl.kernel, out_type=jax.ShapeDtypeStruct((SC_SPLIT, n_rows, PIECE), jnp.int32),
                       mesh=_sc_mesh(), scratch_types=[pltpu.SemaphoreType.DMA])
    def scatter(*refs):
        out_hbm, sem = refs[2 * n_groups], refs[2 * n_groups + 1]
        for g in range(n_groups):
            x_hbm, idx_hbm = refs[2 * g], refs[2 * g + 1]
            for piece in range(SC_SPLIT):
                def window(x_vmem, idx_vmem, piece=piece):
                    copies = [pltpu.async_copy(x_vmem, out_hbm.at[piece].at[idx_vmem.at[k]], sem)
                              for k in range(TOP_K)]
                    for copy in copies:
                        copy.wait()

                pltpu.emit_pipeline(
                    window,
                    grid=(x_hbm.shape[1] // SC_WINDOW,),
                    in_specs=[pl.BlockSpec((SC_WINDOW, PIECE), lambda i: (i, 0)),
                              pl.BlockSpec((TOP_K, SC_WINDOW), lambda i: (0, i))],
                    out_specs=[],
                    core_axis_name=("core", "subcore"),
                    dimension_semantics=(pltpu.PARALLEL,),
                )(x_hbm.at[piece], idx_hbm)

    args = []
    for x, d in zip(xs_groups, dests):
        args += [x, d]
    return scatter(*args)


def _experts_kernel(cb_ref, cnt_ref, xs_ref, wg_ref, wu_ref, wd_ref, ys_ref,
                    wg_scr, wu_scr, wd_scr, xbuf, ybuf, sem_in, sem_out):
    e = pl.program_id(0)
    g0, g1, total = cb_ref[e], cb_ref[e + 1], cb_ref[N_EXPERTS]

    def rows(g):
        return pl.ds(pl.multiple_of(g * EXPERT_ROWS, EXPERT_ROWS), EXPERT_ROWS)

    def x_copy(g):
        s = g % EXPERT_SLOTS
        return pltpu.make_async_copy(xs_ref.at[:, rows(g), :], xbuf.at[s], sem_in.at[s])

    def y_copy(g):
        s = g % EXPERT_SLOTS
        return pltpu.make_async_copy(ybuf.at[s], ys_ref.at[:, rows(g), :], sem_out.at[s])

    @pl.when(e == 0)
    def _():
        for g in range(EXPERT_SLOTS - 1):
            pl.when(g < total)(lambda g=g: x_copy(g).start())

    @pl.when(g1 > g0)
    def _():
        wg_scr[...] = wg_ref[0].astype(BF16)
        wu_scr[...] = wu_ref[0].astype(BF16)
        wd_scr[...] = wd_ref[0].astype(BF16)

    def block(g, carry):
        slot = g % EXPERT_SLOTS

        @pl.when(g + EXPERT_SLOTS - 1 < total)
        def _():
            x_copy(g + EXPERT_SLOTS - 1).start()

        x_copy(g).wait()

        @pl.when(g >= EXPERT_SLOTS)
        def _():
            y_copy(g - EXPERT_SLOTS).wait()

        valid = lax.broadcasted_iota(jnp.int32, (EXPERT_ROWS, PIECE), 0) < cnt_ref[e] - (g - g0) * EXPERT_ROWS
        hg = hu = None
        for piece in range(SC_SPLIT):
            for part, col0 in zip(_unpack_pair(xbuf[slot, piece]), (piece * PIECE, HALF + piece * PIECE)):
                xb = jnp.where(valid, part, 0.0).astype(BF16)
                pg = jnp.dot(xb, wg_scr[col0:col0 + PIECE], preferred_element_type=F32)
                pu = jnp.dot(xb, wu_scr[col0:col0 + PIECE], preferred_element_type=F32)
                hg = pg if hg is None else hg + pg
                hu = pu if hu is None else hu + pu
        h = ((hg * jax.nn.sigmoid(hg)) * hu).astype(BF16)
        y = _pack_pair(jnp.dot(h, wd_scr[:, :HALF], preferred_element_type=F32),
                       jnp.dot(h, wd_scr[:, HALF:], preferred_element_type=F32))
        for piece in range(SC_SPLIT):
            ybuf[slot, piece] = y[:, piece * PIECE:(piece + 1) * PIECE]
        y_copy(g).start()
        return carry

    lax.fori_loop(g0, g1, block, 0)

    @pl.when(e == N_EXPERTS - 1)
    def _():
        for back in range(EXPERT_SLOTS, 0, -1):
            pl.when(total >= back)(lambda back=back: y_copy(total - back).wait())


def _experts(cum_blocks, counts, xs, w_gate, w_up, w_down):
    n_rows = xs.shape[1]
    wmap = lambda e, cb, cnt: (e, 0, 0)
    grid_spec = pltpu.PrefetchScalarGridSpec(
        num_scalar_prefetch=2,
        grid=(N_EXPERTS,),
        in_specs=[pl.BlockSpec(memory_space=pl.ANY),
                  pl.BlockSpec((1, D_MODEL, D_EXPERT), wmap),
                  pl.BlockSpec((1, D_MODEL, D_EXPERT), wmap),
                  pl.BlockSpec((1, D_EXPERT, D_MODEL), wmap)],
        out_specs=pl.BlockSpec(memory_space=pl.ANY),
        scratch_shapes=[pltpu.VMEM((D_MODEL, D_EXPERT), BF16), pltpu.VMEM((D_MODEL, D_EXPERT), BF16),
                        pltpu.VMEM((D_EXPERT, D_MODEL), BF16),
                        pltpu.VMEM((EXPERT_SLOTS, SC_SPLIT, EXPERT_ROWS, PIECE), jnp.int32),
                        pltpu.VMEM((EXPERT_SLOTS, SC_SPLIT, EXPERT_ROWS, PIECE), jnp.int32),
                        pltpu.SemaphoreType.DMA((EXPERT_SLOTS,)), pltpu.SemaphoreType.DMA((EXPERT_SLOTS,))])
    return pl.pallas_call(
        _experts_kernel,
        grid_spec=grid_spec,
        out_shape=jax.ShapeDtypeStruct((SC_SPLIT, n_rows, PIECE), jnp.int32),
        compiler_params=_params(("arbitrary",)),
        name="experts",
    )(cum_blocks, counts, xs, w_gate, w_up, w_down)


def _sc_gather_rows(table, idx):
    n, d = idx.shape[0], table.shape[2]

    @functools.partial(pl.kernel, out_type=jax.ShapeDtypeStruct((SC_SPLIT, n, d), table.dtype), mesh=_sc_mesh(),
                       scratch_types=[])
    def gather(table_hbm, idx_hbm, out_hbm):
        for piece in range(SC_SPLIT):
            def window(idx_vmem, out_vmem, piece=piece):
                pltpu.sync_copy(table_hbm.at[piece].at[idx_vmem.at[0]], out_vmem)

            pltpu.emit_pipeline(
                window,
                grid=(n // SC_WINDOW,),
                in_specs=[pl.BlockSpec((1, SC_WINDOW), lambda i: (0, i))],
                out_specs=[pl.BlockSpec((SC_WINDOW, d), lambda i: (i, 0))],
                core_axis_name=("core", "subcore"),
                dimension_semantics=(pltpu.PARALLEL,),
            )(idx_hbm, out_hbm.at[piece])

    return gather(table, idx.reshape(1, n))


def _combine_kernel(w_ref, x1_ref, g2_ref, b2_ref, wsg_ref, wsu_ref, wsd_ref, rows_ref, o_ref):
    w = w_ref[...]
    x1 = x1_ref[...]
    x1b = x1.astype(BF16)
    hg = jnp.dot(x1b, wsg_ref[...], preferred_element_type=F32)
    hu = jnp.dot(x1b, wsu_ref[...], preferred_element_type=F32)
    hs = (hg * jax.nn.sigmoid(hg)) * hu
    base = DN_ALPHA * x1 + jnp.dot(hs.astype(BF16), wsd_ref[...], preferred_element_type=F32)
    acc = [[base[:, half * HALF + p * PIECE:half * HALF + (p + 1) * PIECE] for p in range(SC_SPLIT)]
           for half in range(2)]
    for k in range(TOP_K):
        wk = w[:, k:k + 1]
        for p in range(SC_SPLIT):
            lo, hi = _unpack_pair(rows_ref[p, k])
            acc[0][p] = acc[0][p] + lo * wk
            acc[1][p] = acc[1][p] + hi * wk
    o_ref[...] = _layer_norm(jnp.concatenate(acc[0] + acc[1], axis=1), g2_ref[...], b2_ref[...])


def _combine(wts_tk, x1, g2, b2, wsg, wsu, wsd, rows, tm):
    t = x1.shape[0]
    full = lambda i: (0, 0)
    return pl.pallas_call(
        _combine_kernel,
        grid=(t // tm,),
        in_specs=[pl.BlockSpec((tm, TOP_K), lambda i: (i, 0)),
                  pl.BlockSpec((tm, D_MODEL), lambda i: (i, 0)),
                  pl.BlockSpec((1, D_MODEL), full),
                  pl.BlockSpec((1, D_MODEL), full),
                  pl.BlockSpec(wsg.shape, full), pl.BlockSpec(wsu.shape, full), pl.BlockSpec(wsd.shape, full),
                  pl.BlockSpec((SC_SPLIT, TOP_K, tm, PIECE), lambda i: (0, 0, i, 0))],
        out_specs=pl.BlockSpec((tm, D_MODEL), lambda i: (i, 0)),
        out_shape=jax.ShapeDtypeStruct((t, D_MODEL), F32),
        compiler_params=_params(("arbitrary",)),
        name="combine",
    )(wts_tk, x1, g2, b2, wsg, wsu, wsd, rows)


def _tile(n, pref):
    t = pref
    while n % t:
        t //= 2
    return t


def _token_mixers(x, layer_idx, wp, cnt_in):
    bsz, seq, _ = x.shape
    t = bsz * seq
    x2 = x.reshape(t, D_MODEL)
    lam_init = 0.8 - 0.6 * math.exp(-0.3 * layer_idx)

    cos_t, sin_t = _rope_tables(seq)
    qa, ka, va, qd, kd, vd, ga, gb = _in_proj(x2, wp["w_in"], cos_t, sin_t, seq, _tile(seq, 256))
    oa = _attn_a(qa, ka, va, wp["sink"], bsz, seq, _tile(seq, 512))
    ob = _attn_b(qd, kd, vd, wp["lam_p"], wp["subln_g"], bsz, seq, _tile(seq, 512), _tile(seq, 1024), lam_init)
    return _post_attn(
        x2, oa, ob, ga, gb, wp["w_o_a"], wp["w_o_b"], wp["w_out"], wp["ln1_g"], wp["ln1_b"],
        wp["router_wt"], wp["router_bias"], cnt_in, _tile(t, 512))


def _encoder_layer(xs_in, layer_idx, wp):
    cnt_in = jnp.zeros((N_EXPERTS, 1), F32)
    routed = []
    for x in xs_in:
        x1p, x1f, eidx, wts, rank, cnt = _token_mixers(x, layer_idx, wp, cnt_in)
        cnt_in = cnt[:, :1].astype(F32)
        routed.append((x1p, x1f, eidx, wts, rank))

    n_assign = sum(x.shape[0] * x.shape[1] for x in xs_in) * TOP_K
    n_blk = -(-(n_assign + N_EXPERTS * (EXPERT_ROWS - 1)) // EXPERT_ROWS)
    counts = cnt[:, 0]
    padded = ((counts + EXPERT_ROWS - 1) // EXPERT_ROWS) * EXPERT_ROWS
    pend = jnp.cumsum(padded)
    pstart = pend - padded
    cum_blocks = jnp.concatenate([jnp.zeros((1,), jnp.int32), (pend // EXPERT_ROWS).astype(jnp.int32)])
    pstart_col = pstart.astype(jnp.int32)[:, None]

    dests = [_dest(eidx, rank, pstart_col, _tile(eidx.shape[1], 512)) for _, _, eidx, _, rank in routed]
    xs = _sc_scatter_rows([r[0] for r in routed], dests, n_blk * EXPERT_ROWS)
    ys = _experts(cum_blocks, counts.astype(jnp.int32), xs, wp["w_gate"], wp["w_up"], wp["w_down"])
    outs = []
    for x, dest, (x1p, x1f, eidx, wts, rank) in zip(xs_in, dests, routed):
        t = x1f.shape[0]
        rows = _sc_gather_rows(ys, dest.reshape(-1)).reshape(SC_SPLIT, TOP_K, t, PIECE)
        y = _combine(wts.T, x1f, wp["ln2_g"], wp["ln2_b"], wp["ws_gate"], wp["ws_up"], wp["ws_down"], rows,
                     _tile(t, 256))
        outs.append(y.reshape(x.shape))
    return outs


def kernel(x_prompt, x_sample, w_in, attn_sink, lambda_q1, lambda_k1, lambda_q2, lambda_k2, subln_g, w_o_a, w_o_b, w_out, ln1_g, ln1_b, router_w, router_bias, w_gate, w_up, w_down, ws_gate, ws_up, ws_down, ln2_g, ln2_b):
    y_prompt, y_sample = x_prompt, x_sample
    for l in range(DEPTH):
        wp = {
            "w_in": _prep_w_in(w_in[l]),
            "sink": attn_sink[l].astype(F32),
            "lam_p": jnp.stack([lambda_q1[l], lambda_k1[l], lambda_q2[l], lambda_k2[l]]).astype(F32),
            "subln_g": subln_g[l].astype(F32)[None, :],
            "w_o_a": w_o_a[l].astype(BF16), "w_o_b": w_o_b[l].astype(BF16), "w_out": w_out[l].astype(BF16),
            "ln1_g": ln1_g[l].astype(F32)[None, :], "ln1_b": ln1_b[l].astype(F32)[None, :],
            "router_wt": router_w[l].T.astype(BF16), "router_bias": router_bias[l].astype(F32)[:, None],
            "w_gate": w_gate[l], "w_up": w_up[l], "w_down": w_down[l],
            "ws_gate": ws_gate[l].astype(BF16), "ws_up": ws_up[l].astype(BF16), "ws_down": ws_down[l].astype(BF16),
            "ln2_g": ln2_g[l].astype(F32)[None, :], "ln2_b": ln2_b[l].astype(F32)[None, :],
        }
        y_prompt, y_sample = _encoder_layer((y_prompt, y_sample), l, wp)
    return (y_prompt, y_sample)
```

```python
import functools
import math

import jax
import jax.numpy as jnp
from jax import lax
from jax.experimental import pallas as pl
from jax.experimental.pallas import tpu as pltpu
from jax.experimental.pallas import tpu_sc as plsc

D_MODEL = 1024
HEAD_DIM = 64
ROPE_THETA = 10000.0
BLOCK = 128
A_Q_HEADS = 8
A_KV_HEADS = 2
WINDOW = 128
B_HEADS = 4
N_EXPERTS = 256
TOP_K = 8
N_GROUPS = 8
TOPK_GROUPS = 4
GROUP_SIZE = N_EXPERTS // N_GROUPS
D_EXPERT = 256
ROUTED_SCALE = 2.5
EXPERT_ROWS = 512
EXPERT_SLOTS = 4
DEPTH = 1
DN_ALPHA = (2 * DEPTH) ** 0.25
LN_EPS = 1e-5
RMS_EPS = 1e-5
NEG = -1e30
LOG2E = math.log2(math.e)
SOFTMAX_ROWS = 16

LANES = 128
SC_CORES = 2
SC_SUBCORES = 16
SC_WINDOW = 128
SC_SPLIT = 2
PIECE = D_MODEL // 2 // SC_SPLIT
VMEM_LIMIT_BYTES = 56 * 1024 * 1024

F32 = jnp.float32
BF16 = jnp.bfloat16
_NT = (((1,), (1,)), ((), ()))


def _params(sem, vmem=VMEM_LIMIT_BYTES):
    return pltpu.CompilerParams(dimension_semantics=sem, vmem_limit_bytes=vmem)


_C_QA, _C_KA, _C_VA, _C_QD, _C_KD, _C_VD, _C_END = 0, 512, 768, 1024, 1536, 2048, 2560


def _prep_w_in(w_in):
    cuts = [0, 512, 640, 768, 1280, 1792, 2304, 4352]
    qa, ka, va, qd, kd, vd, gates = [w_in[:, cuts[i]:cuts[i + 1]] for i in range(7)]
    dup = lambda w: jnp.concatenate([w[:, :64], w[:, :64], w[:, 64:], w[:, 64:]], axis=1)
    return jnp.concatenate([qa, dup(ka), dup(va), qd, kd, vd], axis=1).astype(BF16), gates.astype(BF16)


def _rope_tables(s):
    half = HEAD_DIM // 2
    inv = 1.0 / (ROPE_THETA ** (jnp.arange(half, dtype=F32) / half))
    ang = jnp.arange(s, dtype=F32)[:, None] * inv[None, :]
    cos, sin = jnp.cos(ang), jnp.sin(ang)
    return (jnp.concatenate([cos, cos, cos, cos], axis=1),
            jnp.concatenate([-sin, sin, -sin, sin], axis=1))


def _in_proj_kernel(x_ref, w_ref, cos_ref, sin_ref, qa_ref, ka_ref, va_ref, qd_ref, kd_ref, vd_ref):
    xb = x_ref[...].astype(BF16)
    cos = cos_ref[...]
    sin = sin_ref[...]
    lane = lax.broadcasted_iota(jnp.int32, cos.shape, 1)
    first_half = (lane & (HEAD_DIM // 2)) == 0

    def proj(c0, c1):
        return jnp.dot(xb, w_ref[:, c0:c1], preferred_element_type=F32)

    def rope_store(u, out_ref, scale):
        for j in range(u.shape[1] // LANES):
            uj = u[:, LANES * j:LANES * (j + 1)]
            rot = jnp.where(first_half, pltpu.roll(uj, LANES - 32, 1), pltpu.roll(uj, 32, 1))
            r = uj * cos + rot * sin
            if scale != 1.0:
                r = r * scale
            out_ref[:, LANES * j:LANES * (j + 1)] = r.astype(out_ref.dtype)

    scale = HEAD_DIM ** -0.5
    rope_store(proj(_C_QA, _C_KA), qa_ref, scale)
    rope_store(proj(_C_KA, _C_VA), ka_ref, 1.0)
    va_ref[...] = proj(_C_VA, _C_QD).astype(va_ref.dtype)
    rope_store(proj(_C_QD, _C_KD), qd_ref, scale * LOG2E)
    rope_store(proj(_C_KD, _C_VD), kd_ref, 1.0)
    vd_ref[...] = proj(_C_VD, _C_END).astype(vd_ref.dtype)


def _in_proj(x2, w_perm, cos_t, sin_t, seq, tm):
    t = x2.shape[0]
    nseq = seq // tm
    row = lambda i: (i, 0)
    widths = (512, 256, 256, 512, 512, 512)
    return pl.pallas_call(
        _in_proj_kernel,
        grid=(t // tm,),
        in_specs=[pl.BlockSpec((tm, D_MODEL), row),
                  pl.BlockSpec((D_MODEL, _C_END), lambda i: (0, 0)),
                  pl.BlockSpec((tm, LANES), lambda i: (i % nseq, 0)),
                  pl.BlockSpec((tm, LANES), lambda i: (i % nseq, 0))],
        out_specs=[pl.BlockSpec((tm, w), row) for w in widths],
        out_shape=[jax.ShapeDtypeStruct((t, w), BF16) for w in widths],
        compiler_params=_params(("arbitrary",)),
        name="in_proj",
    )(x2, w_perm, cos_t, sin_t)


def _attn_a_kernel(sink_ref, q_ref, kp_ref, kc_ref, kn_ref, vp_ref, vc_ref, vn_ref, o_ref, *, seq, tq):
    i = pl.program_id(1)
    kk = jnp.concatenate([kp_ref[...], kc_ref[...], kn_ref[...]], axis=0)
    vv = jnp.concatenate([vp_ref[...], vc_ref[...], vn_ref[...]], axis=0)
    lane = lax.broadcasted_iota(jnp.int32, (kk.shape[0], LANES), 1)
    lo = lane < HEAD_DIM
    zero = jnp.zeros((kk.shape[0], LANES), BF16)
    k_lo = [jnp.where(lo, kk[:, LANES * h:LANES * (h + 1)], zero) for h in range(A_KV_HEADS)]
    k_hi = [jnp.where(lo, zero, kk[:, LANES * h:LANES * (h + 1)]) for h in range(A_KV_HEADS)]
    v_lo = [jnp.where(lo, vv[:, LANES * h:LANES * (h + 1)], zero) for h in range(A_KV_HEADS)]
    v_hi = [jnp.where(lo, zero, vv[:, LANES * h:LANES * (h + 1)]) for h in range(A_KV_HEADS)]

    qi = lax.broadcasted_iota(jnp.int32, (BLOCK, 3 * BLOCK), 0)
    kj = lax.broadcasted_iota(jnp.int32, (BLOCK, 3 * BLOCK), 1)
    band = jnp.abs(kj - BLOCK - qi) <= WINDOW
    head_of_row = lax.broadcasted_iota(jnp.int32, (A_Q_HEADS * BLOCK, 1), 0) // BLOCK
    snk = jnp.zeros((A_Q_HEADS * BLOCK, 1), F32)
    for head in range(A_Q_HEADS):
        snk = jnp.where(head_of_row == head, sink_ref[head], snk)
    for j in range(tq // BLOCK):
        kpos = i * tq + (j - 1) * BLOCK + kj
        mask = band & (kpos >= 0) & (kpos < seq)
        r0, r1 = j * BLOCK, (j + 3) * BLOCK
        pieces = []
        for c in range(A_Q_HEADS // 2):
            q2 = q_ref[j * BLOCK:(j + 1) * BLOCK, LANES * c:LANES * (c + 1)]
            for kx in (k_lo[c // 2], k_hi[c // 2]):
                s = lax.dot_general(q2, kx[r0:r1], _NT, preferred_element_type=F32)
                pieces.append(jnp.where(mask, s, NEG))
        s = jnp.concatenate(pieces, axis=0)
        m = jnp.maximum(jnp.max(s, axis=-1, keepdims=True), snk)
        p = jnp.exp(s - m)
        inv = 1.0 / (jnp.sum(p, axis=-1, keepdims=True) + jnp.exp(snk - m))
        pb = p.astype(BF16)
        for c in range(A_Q_HEADS // 2):
            out = None
            for half, vx in enumerate((v_lo[c // 2], v_hi[c // 2])):
                rows = slice((2 * c + half) * BLOCK, (2 * c + half + 1) * BLOCK)
                pv = jnp.dot(pb[rows], vx[r0:r1], preferred_element_type=F32) * inv[rows]
                out = pv if out is None else out + pv
            o_ref[j * BLOCK:(j + 1) * BLOCK, LANES * c:LANES * (c + 1)] = out.astype(o_ref.dtype)


def _attn_a(qa, ka, va, sink, bsz, seq, tq):
    t = qa.shape[0]
    nq = seq // tq
    nb = seq // BLOCK
    r = tq // BLOCK
    cur = lambda b, i: (b * nq + i, 0)
    prev = lambda b, i: (b * nb + jnp.maximum(i * r - 1, 0), 0)
    nxt = lambda b, i: (b * nb + jnp.minimum((i + 1) * r, nb - 1), 0)
    kv_specs = [pl.BlockSpec((BLOCK, 256), prev), pl.BlockSpec((tq, 256), cur), pl.BlockSpec((BLOCK, 256), nxt)]
    return pl.pallas_call(
        functools.partial(_attn_a_kernel, seq=seq, tq=tq),
        grid=(bsz, nq),
        in_specs=[pl.BlockSpec(memory_space=pltpu.SMEM), pl.BlockSpec((tq, 512), cur)] + kv_specs + kv_specs,
        out_specs=pl.BlockSpec((tq, 512), cur),
        out_shape=jax.ShapeDtypeStruct((t, 512), BF16),
        compiler_params=_params(("arbitrary", "arbitrary")),
        name="attn_a",
    )(sink, qa, ka, ka, ka, va, va, va)


def _attn_b_kernel(lam_ref, g_ref, q_ref, k_ref, v_ref, o_ref, acc_ref, s_ref, p_ref, m_ref, alpha_ref,
                   *, tq, tk, lam_init):
    seq = q_ref.shape[0]
    n_q, n_chunks = seq // tq, seq // tk
    lo = lax.broadcasted_iota(jnp.int32, (tq, LANES), 1) < HEAD_DIM
    zero = jnp.zeros((tq, LANES), BF16)
    ones_col = jnp.where(lax.broadcasted_iota(jnp.int32, (tk, LANES), 1) == 0, 1.0, 0.0).astype(BF16)

    lp = lam_ref[...]
    lam = (jnp.exp(jnp.sum(lp[0:1] * lp[1:2], axis=-1, keepdims=True))
           - jnp.exp(jnp.sum(lp[2:3] * lp[3:4], axis=-1, keepdims=True)) + lam_init)

    def q_tile(qi):
        q = q_ref[pl.ds(pl.multiple_of(qi * tq, tq), tq), :]
        return jnp.concatenate([jnp.where(lo, q, zero), jnp.where(lo, zero, q)], axis=0)

    def scores(q2, kc, slot):
        s_ref[slot] = lax.dot_general(q2, k_ref[kc * tk:(kc + 1) * tk, :], _NT, preferred_element_type=F32)

    def consume(kc, slot):
        v_aug = jnp.concatenate([v_ref[kc * tk:(kc + 1) * tk, :], ones_col], axis=1)
        for r in range(2 * tq // SOFTMAX_ROWS):
            rows = slice(r * SOFTMAX_ROWS, (r + 1) * SOFTMAX_ROWS)
            s = s_ref[slot, rows, :]
            m_old = m_ref[rows, :]
            m_new = jnp.maximum(m_old, jnp.max(s, axis=-1, keepdims=True))
            m_ref[rows, :] = m_new
            alpha_ref[rows, :] = jnp.exp2(m_old - m_new)
            p_ref[rows, :] = jnp.exp2(s - m_new).astype(BF16)
        acc_ref[...] = alpha_ref[...] * acc_ref[...] + jnp.dot(p_ref[...], v_aug, preferred_element_type=F32)

    across_tiles = n_chunks % 2 == 0

    def tile(qi, carry):
        q = q_tile(qi)
        acc_ref[...] = jnp.zeros_like(acc_ref)
        m_ref[...] = jnp.full(m_ref.shape, -jnp.inf, F32)
        if not across_tiles:
            scores(q, 0, 0)
        for kc in range(n_chunks):
            slot = kc % 2
            if kc + 1 < n_chunks:
                scores(q, kc + 1, 1 - slot)
            elif across_tiles:
                scores(q_tile(jnp.minimum(qi + 1, n_q - 1)), 0, 1 - slot)
            consume(kc, slot)
        acc1, acc2 = acc_ref[:tq], acc_ref[tq:]
        l1, l2 = acc1[:, LANES:LANES + 1], acc2[:, LANES:LANES + 1]
        o = acc1[:, :LANES] * (1.0 / l1) - lam * (acc2[:, :LANES] * (1.0 / l2))
        o = o * lax.rsqrt(jnp.mean(o * o, axis=-1, keepdims=True) + RMS_EPS)
        o = o * g_ref[...] * (1.0 - lam_init)
        o_ref[pl.ds(pl.multiple_of(qi * tq, tq), tq), :] = o.astype(o_ref.dtype)
        return carry

    if across_tiles:
        scores(q_tile(0), 0, 0)
    lax.fori_loop(0, n_q, tile, 0)


def _attn_b(qd, kd, vd, lam_p, subln_g, bsz, seq, tq, tk, lam_init):
    t = qd.shape[0]
    seq_block = pl.BlockSpec((seq, LANES), lambda b, h: (b, h))
    return pl.pallas_call(
        functools.partial(_attn_b_kernel, tq=tq, tk=tk, lam_init=lam_init),
        grid=(bsz, B_HEADS),
        in_specs=[pl.BlockSpec((4, HEAD_DIM), lambda b, h: (0, 0)),
                  pl.BlockSpec((1, LANES), lambda b, h: (0, 0)),
                  seq_block, seq_block, seq_block],
        out_specs=seq_block,
        out_shape=jax.ShapeDtypeStruct((t, 512), BF16),
        scratch_shapes=[pltpu.VMEM((2 * tq, 2 * LANES), F32),
                        pltpu.VMEM((2, 2 * tq, tk), F32),
                        pltpu.VMEM((2 * tq, tk), BF16),
                        pltpu.VMEM((2 * tq, 1), F32),
                        pltpu.VMEM((2 * tq, 1), F32)],
        compiler_params=_params(("arbitrary", "arbitrary")),
        name="attn_b",
    )(lam_p, subln_g, qd, kd, vd)


HALF = D_MODEL // 2
_HI_MASK = -65536


def _pack_pair(lo, hi):
    lo_bits = lax.bitcast_convert_type(lo.astype(BF16).astype(F32), jnp.int32)
    hi_bits = lax.bitcast_convert_type(hi.astype(BF16).astype(F32), jnp.int32)
    return lax.shift_right_logical(lo_bits, 16) | hi_bits


def _unpack_pair(w):
    lo = lax.bitcast_convert_type(lax.shift_left(w, 16), F32)
    hi = lax.bitcast_convert_type(w & _HI_MASK, F32)
    return lo, hi


def _layer_norm(y, g, b):
    mu = jnp.mean(y, axis=-1, keepdims=True)
    d = y - mu
    var = jnp.mean(d * d, axis=-1, keepdims=True)
    return d * lax.rsqrt(var + LN_EPS) * g + b


def _first_argmax(vals, rowf, big):
    m = jnp.max(vals, axis=0, keepdims=True)
    idx = jnp.min(jnp.where(vals == m, rowf, big), axis=0, keepdims=True)
    return m, idx


def _post_attn_kernel(x_ref, oa_ref, ob_ref, wgate_ref, woa_ref, wob_ref, wout_ref, g1_ref, b1_ref,
                      rwt_ref, rb_ref, tri_ref, cnt_in_ref,
                      x1_ref, x1f_ref, eidx_ref, wts_ref, rank_ref, cnt_ref, cnt_scr):
    i = pl.program_id(0)
    tm = x_ref.shape[0]

    @pl.when(i == 0)
    def _():
        cnt_scr[...] = cnt_in_ref[...]

    x = x_ref[...]
    gates = jax.nn.sigmoid(jnp.dot(x.astype(BF16), wgate_ref[...], preferred_element_type=F32))
    a = jnp.dot(oa_ref[...], woa_ref[...], preferred_element_type=F32)
    b = jnp.dot(ob_ref[...], wob_ref[...], preferred_element_type=F32)
    merged = gates[:, :D_MODEL] * a + gates[:, D_MODEL:] * b
    mix = jnp.dot(merged.astype(BF16), wout_ref[...], preferred_element_type=F32)
    x1 = _layer_norm(DN_ALPHA * x + mix, g1_ref[...], b1_ref[...])
    x1p = _pack_pair(x1[:, :HALF], x1[:, HALF:])
    for piece in range(SC_SPLIT):
        x1_ref[piece] = x1p[:, piece * PIECE:(piece + 1) * PIECE]
    x1b = x1.astype(BF16)
    x1f_ref[...] = x1

    logits = lax.dot_general(rwt_ref[...], x1b, _NT, preferred_element_type=F32)
    scores = jax.nn.sigmoid(logits)
    choice = scores + rb_ref[...]
    ninf = -jnp.inf
    grow = lax.broadcasted_iota(jnp.int32, (GROUP_SIZE, tm), 0).astype(F32)
    gscore = []
    for g in range(N_GROUPS):
        blk = choice[GROUP_SIZE * g:GROUP_SIZE * (g + 1)]
        m1, i1 = _first_argmax(blk, grow, float(GROUP_SIZE))
        m2 = jnp.max(jnp.where(grow == i1, ninf, blk), axis=0, keepdims=True)
        gscore.append(m1 + m2)
    selected = [jnp.zeros((1, tm), F32) for _ in range(N_GROUPS)]
    work = list(gscore)
    for _ in range(TOPK_GROUPS):
        best = work[0]
        for g in range(1, N_GROUPS):
            best = jnp.maximum(best, work[g])
        taken = jnp.zeros((1, tm), F32)
        for g in range(N_GROUPS):
            hit = jnp.where((work[g] == best) & (taken == 0.0), 1.0, 0.0)
            taken = jnp.maximum(taken, hit)
            selected[g] = jnp.maximum(selected[g], hit)
            work[g] = jnp.where(hit > 0.0, ninf, work[g])
    masked = jnp.concatenate(
        [jnp.where(selected[g] > 0.0, choice[GROUP_SIZE * g:GROUP_SIZE * (g + 1)], ninf) for g in range(N_GROUPS)],
        axis=0)

    rowf = lax.broadcasted_iota(jnp.int32, (N_EXPERTS, tm), 0).astype(F32)
    hits, idxs, ws = [], [], []
    for _ in range(TOP_K):
        _, idx = _first_argmax(masked, rowf, float(N_EXPERTS))
        hit = rowf == idx
        hits.append(hit)
        idxs.append(idx)
        ws.append(jnp.sum(jnp.where(hit, scores, 0.0), axis=0, keepdims=True))
        masked = jnp.where(hit, ninf, masked)
    wsum = ws[0]
    for k in range(1, TOP_K):
        wsum = wsum + ws[k]

    member = hits[0]
    for k in range(1, TOP_K):
        member = member | hits[k]
    member_f = jnp.where(member, 1.0, 0.0)
    before = jnp.dot(member_f.astype(BF16), tri_ref[...], preferred_element_type=F32) + cnt_scr[...]
    for k in range(TOP_K):
        eidx_ref[k:k + 1, :] = idxs[k].astype(jnp.int32)
        wts_ref[k:k + 1, :] = ws[k] / wsum * ROUTED_SCALE
        rank_ref[k:k + 1, :] = jnp.sum(jnp.where(hits[k], before, 0.0), axis=0, keepdims=True).astype(jnp.int32)
    cnt_scr[...] = cnt_scr[...] + jnp.sum(member_f, axis=1, keepdims=True)
    cnt_ref[...] = jnp.broadcast_to(cnt_scr[...], cnt_ref.shape).astype(jnp.int32)


def _post_attn(x2, oa, ob, wgate, woa, wob, wout, g1, b1, rwt, rb, cnt_in, tm):
    t = x2.shape[0]
    tri = jnp.triu(jnp.ones((tm, tm), F32), k=1).astype(BF16)
    row = lambda i: (i, 0)
    col = lambda i: (0, i)
    full = lambda i: (0, 0)
    wspec = lambda arr: pl.BlockSpec(arr.shape, full)
    weights = (wgate, woa, wob, wout, g1, b1, rwt, rb, tri, cnt_in)
    return pl.pallas_call(
        _post_attn_kernel,
        grid=(t // tm,),
        in_specs=[pl.BlockSpec((tm, D_MODEL), row), pl.BlockSpec((tm, 512), row), pl.BlockSpec((tm, 512), row)]
                 + [wspec(w) for w in weights],
        out_specs=[pl.BlockSpec((SC_SPLIT, tm, PIECE), lambda i: (0, i, 0)), pl.BlockSpec((tm, D_MODEL), row),
                   pl.BlockSpec((TOP_K, tm), col), pl.BlockSpec((TOP_K, tm), col), pl.BlockSpec((TOP_K, tm), col),
                   pl.BlockSpec((N_EXPERTS, LANES), full)],
        out_shape=[jax.ShapeDtypeStruct((SC_SPLIT, t, PIECE), jnp.int32), jax.ShapeDtypeStruct((t, D_MODEL), F32),
                   jax.ShapeDtypeStruct((TOP_K, t), jnp.int32), jax.ShapeDtypeStruct((TOP_K, t), F32),
                   jax.ShapeDtypeStruct((TOP_K, t), jnp.int32), jax.ShapeDtypeStruct((N_EXPERTS, LANES), jnp.int32)],
        scratch_shapes=[pltpu.VMEM((N_EXPERTS, 1), F32)],
        compiler_params=_params(("arbitrary",)),
        name="post_attn",
    )(x2, oa, ob, *weights)


def _dest_kernel(eidx_ref, rank_ref, pstart_ref, dest_ref):
    tm = eidx_ref.shape[1]
    rows = lax.broadcasted_iota(jnp.int32, (N_EXPERTS, tm), 0)
    pstart = pstart_ref[...]
    for k in range(TOP_K):
        hit = rows == eidx_ref[k:k + 1, :]
        start = jnp.sum(jnp.where(hit, pstart, 0).astype(F32), axis=0, keepdims=True).astype(jnp.int32)
        dest_ref[k:k + 1, :] = start + rank_ref[k:k + 1, :]


def _dest(eidx, rank, pstart, tm):
    t = eidx.shape[1]
    return pl.pallas_call(
        _dest_kernel,
        grid=(t // tm,),
        in_specs=[pl.BlockSpec((TOP_K, tm), lambda i: (0, i)), pl.BlockSpec((TOP_K, tm), lambda i: (0, i)),
                  pl.BlockSpec((N_EXPERTS, 1), lambda i: (0, 0))],
        out_specs=pl.BlockSpec((TOP_K, tm), lambda i: (0, i)),
        out_shape=jax.ShapeDtypeStruct((TOP_K, t), jnp.int32),
        compiler_params=_params(("arbitrary",)),
        name="dest",
    )(eidx, rank, pstart)


def _sc_mesh():
    return plsc.VectorSubcoreMesh(core_axis_name="core", subcore_axis_name="subcore",
                                  num_cores=SC_CORES, num_subcores=SC_SUBCORES)


def _sc_scatter_rows(xs_groups, dests, n_rows):
    n_groups = len(xs_groups)

    @functools.partial(pl.kernel, out_type=jax.ShapeDtypeStruct((SC_SPLIT, n_rows, PIECE), jnp.int32),
                       mesh=_sc_mesh(), scratch_types=[pltpu.SemaphoreType.DMA])
    def scatter(*refs):
        out_hbm, sem = refs[2 * n_groups], refs[2 * n_groups + 1]
        for g in range(n_groups):
            x_hbm, idx_hbm = refs[2 * g], refs[2 * g + 1]
            for piece in range(SC_SPLIT):
                def window(x_vmem, idx_vmem, piece=piece):
                    copies = [pltpu.async_copy(x_vmem, out_hbm.at[piece].at[idx_vmem.at[k]], sem)
                              for k in range(TOP_K)]
                    for copy in copies:
                        copy.wait()

                pltpu.emit_pipeline(
                    window,
                    grid=(x_hbm.shape[1] // SC_WINDOW,),
                    in_specs=[pl.BlockSpec((SC_WINDOW, PIECE), lambda i: (i, 0)),
                              pl.BlockSpec((TOP_K, SC_WINDOW), lambda i: (0, i))],
                    out_specs=[],
                    core_axis_name=("core", "subcore"),
                    dimension_semantics=(pltpu.PARALLEL,),
                )(x_hbm.at[piece], idx_hbm)

    args = []
    for x, d in zip(xs_groups, dests):
        args += [x, d]
    return scatter(*args)


def _experts_kernel(cb_ref, cnt_ref, xs_ref, wg_ref, wu_ref, wd_ref, ys_ref,
                    wg_scr, wu_scr, wd_scr, xbuf, ybuf, sem_in, sem_out):
    e = pl.program_id(0)
    g0, g1, total = cb_ref[e], cb_ref[e + 1], cb_ref[N_EXPERTS]

    def rows(g):
        return pl.ds(pl.multiple_of(g * EXPERT_ROWS, EXPERT_ROWS), EXPERT_ROWS)

    def x_copy(g):
        s = g % EXPERT_SLOTS
        return pltpu.make_async_copy(xs_ref.at[:, rows(g), :], xbuf.at[s], sem_in.at[s])

    def y_copy(g):
        s = g % EXPERT_SLOTS
        return pltpu.make_async_copy(ybuf.at[s], ys_ref.at[:, rows(g), :], sem_out.at[s])

    @pl.when(e == 0)
    def _():
        for g in range(EXPERT_SLOTS - 1):
            pl.when(g < total)(lambda g=g: x_copy(g).start())

    @pl.when(g1 > g0)
    def _():
        wg_scr[...] = wg_ref[0].astype(BF16)
        wu_scr[...] = wu_ref[0].astype(BF16)
        wd_scr[...] = wd_ref[0].astype(BF16)

    def block(g, carry):
        slot = g % EXPERT_SLOTS

        @pl.when(g + EXPERT_SLOTS - 1 < total)
        def _():
            x_copy(g + EXPERT_SLOTS - 1).start()

        x_copy(g).wait()

        @pl.when(g >= EXPERT_SLOTS)
        def _():
            y_copy(g - EXPERT_SLOTS).wait()

        valid = lax.broadcasted_iota(jnp.int32, (EXPERT_ROWS, PIECE), 0) < cnt_ref[e] - (g - g0) * EXPERT_ROWS
        hg = hu = None
        for piece in range(SC_SPLIT):
            for part, col0 in zip(_unpack_pair(xbuf[slot, piece]), (piece * PIECE, HALF + piece * PIECE)):
                xb = jnp.where(valid, part, 0.0).astype(BF16)
                pg = jnp.dot(xb, wg_scr[col0:col0 + PIECE], preferred_element_type=F32)
                pu = jnp.dot(xb, wu_scr[col0:col0 + PIECE], preferred_element_type=F32)
                hg = pg if hg is None else hg + pg
                hu = pu if hu is None else hu + pu
        h = ((hg * jax.nn.sigmoid(hg)) * hu).astype(BF16)
        y = _pack_pair(jnp.dot(h, wd_scr[:, :HALF], preferred_element_type=F32),
                       jnp.dot(h, wd_scr[:, HALF:], preferred_element_type=F32))
        for piece in range(SC_SPLIT):
            ybuf[slot, piece] = y[:, piece * PIECE:(piece + 1) * PIECE]
        y_copy(g).start()
        return carry

    lax.fori_loop(g0, g1, block, 0)

    @pl.when(e == N_EXPERTS - 1)
    def _():
        for back in range(EXPERT_SLOTS, 0, -1):
            pl.when(total >= back)(lambda back=back: y_copy(total - back).wait())


def _experts(cum_blocks, counts, xs, w_gate, w_up, w_down):
    n_rows = xs.shape[1]
    wmap = lambda e, cb, cnt: (e, 0, 0)
    grid_spec = pltpu.PrefetchScalarGridSpec(
        num_scalar_prefetch=2,
        grid=(N_EXPERTS,),
        in_specs=[pl.BlockSpec(memory_space=pl.ANY),
                  pl.BlockSpec((1, D_MODEL, D_EXPERT), wmap),
                  pl.BlockSpec((1, D_MODEL, D_EXPERT), wmap),
                  pl.BlockSpec((1, D_EXPERT, D_MODEL), wmap)],
        out_specs=pl.BlockSpec(memory_space=pl.ANY),
        scratch_shapes=[pltpu.VMEM((D_MODEL, D_EXPERT), BF16), pltpu.VMEM((D_MODEL, D_EXPERT), BF16),
                        pltpu.VMEM((D_EXPERT, D_MODEL), BF16),
                        pltpu.VMEM((EXPERT_SLOTS, SC_SPLIT, EXPERT_ROWS, PIECE), jnp.int32),
                        pltpu.VMEM((EXPERT_SLOTS, SC_SPLIT, EXPERT_ROWS, PIECE), jnp.int32),
                        pltpu.SemaphoreType.DMA((EXPERT_SLOTS,)), pltpu.SemaphoreType.DMA((EXPERT_SLOTS,))])
    return pl.pallas_call(
        _experts_kernel,
        grid_spec=grid_spec,
        out_shape=jax.ShapeDtypeStruct((SC_SPLIT, n_rows, PIECE), jnp.int32),
        compiler_params=_params(("arbitrary",)),
        name="experts",
    )(cum_blocks, counts, xs, w_gate, w_up, w_down)


def _sc_gather_rows(table, idx):
    n, d = idx.shape[0], table.shape[2]

    @functools.partial(pl.kernel, out_type=jax.ShapeDtypeStruct((SC_SPLIT, n, d), table.dtype), mesh=_sc_mesh(),
                       scratch_types=[])
    def gather(table_hbm, idx_hbm, out_hbm):
        for piece in range(SC_SPLIT):
            def window(idx_vmem, out_vmem, piece=piece):
                pltpu.sync_copy(table_hbm.at[piece].at[idx_vmem.at[0]], out_vmem)

            pltpu.emit_pipeline(
                window,
                grid=(n // SC_WINDOW,),
                in_specs=[pl.BlockSpec((1, SC_WINDOW), lambda i: (0, i))],
                out_specs=[pl.BlockSpec((SC_WINDOW, d), lambda i: (i, 0))],
                core_axis_name=("core", "subcore"),
                dimension_semantics=(pltpu.PARALLEL,),
            )(idx_hbm, out_hbm.at[piece])

    return gather(table, idx.reshape(1, n))


def _combine_kernel(w_ref, x1_ref, g2_ref, b2_ref, wsg_ref, wsu_ref, wsd_ref, rows_ref, o_ref):
    w = w_ref[...]
    x1 = x1_ref[...]
    x1b = x1.astype(BF16)
    hg = jnp.dot(x1b, wsg_ref[...], preferred_element_type=F32)
    hu = jnp.dot(x1b, wsu_ref[...], preferred_element_type=F32)
    hs = (hg * jax.nn.sigmoid(hg)) * hu
    base = DN_ALPHA * x1 + jnp.dot(hs.astype(BF16), wsd_ref[...], preferred_element_type=F32)
    acc = [[base[:, half * HALF + p * PIECE:half * HALF + (p + 1) * PIECE] for p in range(SC_SPLIT)]
           for half in range(2)]
    for k in range(TOP_K):
        wk = w[:, k:k + 1]
        for p in range(SC_SPLIT):
            lo, hi = _unpack_pair(rows_ref[p, k])
            acc[0][p] = acc[0][p] + lo * wk
            acc[1][p] = acc[1][p] + hi * wk
    o_ref[...] = _layer_norm(jnp.concatenate(acc[0] + acc[1], axis=1), g2_ref[...], b2_ref[...])


def _combine(wts_tk, x1, g2, b2, wsg, wsu, wsd, rows, tm):
    t = x1.shape[0]
    full = lambda i: (0, 0)
    return pl.pallas_call(
        _combine_kernel,
        grid=(t // tm,),
        in_specs=[pl.BlockSpec((tm, TOP_K), lambda i: (i, 0)),
                  pl.BlockSpec((tm, D_MODEL), lambda i: (i, 0)),
                  pl.BlockSpec((1, D_MODEL), full),
                  pl.BlockSpec((1, D_MODEL), full),
                  pl.BlockSpec(wsg.shape, full), pl.BlockSpec(wsu.shape, full), pl.BlockSpec(wsd.shape, full),
                  pl.BlockSpec((SC_SPLIT, TOP_K, tm, PIECE), lambda i: (0, 0, i, 0))],
        out_specs=pl.BlockSpec((tm, D_MODEL), lambda i: (i, 0)),
        out_shape=jax.ShapeDtypeStruct((t, D_MODEL), F32),
        compiler_params=_params(("arbitrary",)),
        name="combine",
    )(wts_tk, x1, g2, b2, wsg, wsu, wsd, rows)


def _tile(n, pref):
    t = pref
    while n % t:
        t //= 2
    return t


def _token_mixers(x, layer_idx, wp, cnt_in):
    bsz, seq, _ = x.shape
    t = bsz * seq
    x2 = x.reshape(t, D_MODEL)
    lam_init = 0.8 - 0.6 * math.exp(-0.3 * layer_idx)

    cos_t, sin_t = _rope_tables(seq)
    qa, ka, va, qd, kd, vd = _in_proj(x2, wp["w_qkv"], cos_t, sin_t, seq, _tile(seq, 256))
    oa = _attn_a(qa, ka, va, wp["sink"], bsz, seq, _tile(seq, 512))
    ob = _attn_b(qd, kd, vd, wp["lam_p"], wp["subln_g"], bsz, seq, _tile(seq, 512), _tile(seq, 1024), lam_init)
    return _post_attn(
        x2, oa, ob, wp["w_gates"], wp["w_o_a"], wp["w_o_b"], wp["w_out"], wp["ln1_g"], wp["ln1_b"],
        wp["router_wt"], wp["router_bias"], cnt_in, _tile(t, 512))


def _encoder_layer(xs_in, layer_idx, wp):
    cnt_in = jnp.zeros((N_EXPERTS, 1), F32)
    routed = []
    for x in xs_in:
        x1p, x1f, eidx, wts, rank, cnt = _token_mixers(x, layer_idx, wp, cnt_in)
        cnt_in = cnt[:, :1].astype(F32)
        routed.append((x1p, x1f, eidx, wts, rank))

    n_assign = sum(x.shape[0] * x.shape[1] for x in xs_in) * TOP_K
    n_blk = -(-(n_assign + N_EXPERTS * (EXPERT_ROWS - 1)) // EXPERT_ROWS)
    counts = cnt[:, 0]
    padded = ((counts + EXPERT_ROWS - 1) // EXPERT_ROWS) * EXPERT_ROWS
    pend = jnp.cumsum(padded)
    pstart = pend - padded
    cum_blocks = jnp.concatenate([jnp.zeros((1,), jnp.int32), (pend // EXPERT_ROWS).astype(jnp.int32)])
    pstart_col = pstart.astype(jnp.int32)[:, None]

    dests = [_dest(eidx, rank, pstart_col, _tile(eidx.shape[1], 512)) for _, _, eidx, _, rank in routed]
    xs = _sc_scatter_rows([r[0] for r in routed], dests, n_blk * EXPERT_ROWS)
    ys = _experts(cum_blocks, counts.astype(jnp.int32), xs, wp["w_gate"], wp["w_up"], wp["w_down"])
    outs = []
    for x, dest, (x1p, x1f, eidx, wts, rank) in zip(xs_in, dests, routed):
        t = x1f.shape[0]
        rows = _sc_gather_rows(ys, dest.reshape(-1)).reshape(SC_SPLIT, TOP_K, t, PIECE)
        y = _combine(wts.T, x1f, wp["ln2_g"], wp["ln2_b"], wp["ws_gate"], wp["ws_up"], wp["ws_down"], rows,
                     _tile(t, 256))
        outs.append(y.reshape(x.shape))
    return outs


def kernel(x_prompt, x_sample, w_in, attn_sink, lambda_q1, lambda_k1, lambda_q2, lambda_k2, subln_g, w_o_a, w_o_b, w_out, ln1_g, ln1_b, router_w, router_bias, w_gate, w_up, w_down, ws_gate, ws_up, ws_down, ln2_g, ln2_b):
    y_prompt, y_sample = x_prompt, x_sample
    for l in range(DEPTH):
        w_qkv, w_gates = _prep_w_in(w_in[l])
        wp = {
            "w_qkv": w_qkv, "w_gates": w_gates,
            "sink": attn_sink[l].astype(F32),
            "lam_p": jnp.stack([lambda_q1[l], lambda_k1[l], lambda_q2[l], lambda_k2[l]]).astype(F32),
            "subln_g": subln_g[l].astype(F32)[None, :],
            "w_o_a": w_o_a[l].astype(BF16), "w_o_b": w_o_b[l].astype(BF16), "w_out": w_out[l].astype(BF16),
            "ln1_g": ln1_g[l].astype(F32)[None, :], "ln1_b": ln1_b[l].astype(F32)[None, :],
            "router_wt": router_w[l].T.astype(BF16), "router_bias": router_bias[l].astype(F32)[:, None],
            "w_gate": w_gate[l], "w_up": w_up[l], "w_down": w_down[l],
            "ws_gate": ws_gate[l].astype(BF16), "ws_up": ws_up[l].astype(BF16), "ws_down": ws_down[l].astype(BF16),
            "ln2_g": ln2_g[l].astype(F32)[None, :], "ln2_b": ln2_b[l].astype(F32)[None, :],
        }
        y_prompt, y_sample = _encoder_layer((y_prompt, y_sample), l, wp)
    return (y_prompt, y_sample)
```

```python
import functools
import math

import jax
import jax.numpy as jnp
from jax import lax
from jax.experimental import pallas as pl
from jax.experimental.pallas import tpu as pltpu
from jax.experimental.pallas import tpu_sc as plsc

D_MODEL = 1024
HEAD_DIM = 64
ROPE_THETA = 10000.0
BLOCK = 128
A_Q_HEADS = 8
A_KV_HEADS = 2
WINDOW = 128
B_HEADS = 4
N_EXPERTS = 256
TOP_K = 8
N_GROUPS = 8
TOPK_GROUPS = 4
GROUP_SIZE = N_EXPERTS // N_GROUPS
D_EXPERT = 256
ROUTED_SCALE = 2.5
EXPERT_ROWS = 512
EXPERT_SLOTS = 4
DEPTH = 1
DN_ALPHA = (2 * DEPTH) ** 0.25
LN_EPS = 1e-5
RMS_EPS = 1e-5
NEG = -1e30
LOG2E = math.log2(math.e)
SOFTMAX_ROWS = 16
POST_SPLIT = 2

LANES = 128
SC_CORES = 2
SC_SUBCORES = 16
SC_WINDOW = 128
SC_SPLIT = 2
PIECE = D_MODEL // 2 // SC_SPLIT
VMEM_LIMIT_BYTES = 56 * 1024 * 1024

F32 = jnp.float32
BF16 = jnp.bfloat16
_NT = (((1,), (1,)), ((), ()))


def _params(sem, vmem=VMEM_LIMIT_BYTES):
    return pltpu.CompilerParams(dimension_semantics=sem, vmem_limit_bytes=vmem)


_C_QA, _C_KA, _C_VA, _C_QD, _C_KD, _C_VD, _C_END = 0, 512, 768, 1024, 1536, 2048, 2560


def _prep_w_in(w_in):
    cuts = [0, 512, 640, 768, 1280, 1792, 2304, 4352]
    qa, ka, va, qd, kd, vd, gates = [w_in[:, cuts[i]:cuts[i + 1]] for i in range(7)]
    dup = lambda w: jnp.concatenate([w[:, :64], w[:, :64], w[:, 64:], w[:, 64:]], axis=1)
    return jnp.concatenate([qa, dup(ka), dup(va), qd, kd, vd], axis=1).astype(BF16), gates.astype(BF16)


def _rope_tables(s):
    half = HEAD_DIM // 2
    inv = 1.0 / (ROPE_THETA ** (jnp.arange(half, dtype=F32) / half))
    ang = jnp.arange(s, dtype=F32)[:, None] * inv[None, :]
    cos, sin = jnp.cos(ang), jnp.sin(ang)
    return (jnp.concatenate([cos, cos, cos, cos], axis=1),
            jnp.concatenate([-sin, sin, -sin, sin], axis=1))


def _in_proj_kernel(x_ref, w_ref, cos_ref, sin_ref, qa_ref, ka_ref, va_ref, qd_ref, kd_ref, vd_ref):
    xb = x_ref[...].astype(BF16)
    cos = cos_ref[...]
    sin = sin_ref[...]
    lane = lax.broadcasted_iota(jnp.int32, cos.shape, 1)
    first_half = (lane & (HEAD_DIM // 2)) == 0

    def proj(c0, c1):
        return jnp.dot(xb, w_ref[:, c0:c1], preferred_element_type=F32)

    def rope_store(u, out_ref, scale):
        for j in range(u.shape[1] // LANES):
            uj = u[:, LANES * j:LANES * (j + 1)]
            rot = jnp.where(first_half, pltpu.roll(uj, LANES - 32, 1), pltpu.roll(uj, 32, 1))
            r = uj * cos + rot * sin
            if scale != 1.0:
                r = r * scale
            out_ref[:, LANES * j:LANES * (j + 1)] = r.astype(out_ref.dtype)

    scale = HEAD_DIM ** -0.5
    rope_store(proj(_C_QA, _C_KA), qa_ref, scale)
    rope_store(proj(_C_KA, _C_VA), ka_ref, 1.0)
    va_ref[...] = proj(_C_VA, _C_QD).astype(va_ref.dtype)
    rope_store(proj(_C_QD, _C_KD), qd_ref, scale * LOG2E)
    rope_store(proj(_C_KD, _C_VD), kd_ref, 1.0)
    vd_ref[...] = proj(_C_VD, _C_END).astype(vd_ref.dtype)


def _in_proj(x2, w_perm, cos_t, sin_t, seq, tm):
    t = x2.shape[0]
    nseq = seq // tm
    row = lambda i: (i, 0)
    widths = (512, 256, 256, 512, 512, 512)
    return pl.pallas_call(
        _in_proj_kernel,
        grid=(t // tm,),
        in_specs=[pl.BlockSpec((tm, D_MODEL), row),
                  pl.BlockSpec((D_MODEL, _C_END), lambda i: (0, 0)),
                  pl.BlockSpec((tm, LANES), lambda i: (i % nseq, 0)),
                  pl.BlockSpec((tm, LANES), lambda i: (i % nseq, 0))],
        out_specs=[pl.BlockSpec((tm, w), row) for w in widths],
        out_shape=[jax.ShapeDtypeStruct((t, w), BF16) for w in widths],
        compiler_params=_params(("arbitrary",)),
        name="in_proj",
    )(x2, w_perm, cos_t, sin_t)


def _attn_a_kernel(sink_ref, q_ref, kp_ref, kc_ref, kn_ref, vp_ref, vc_ref, vn_ref, o_ref, *, seq, tq):
    i = pl.program_id(1)
    kk = jnp.concatenate([kp_ref[...], kc_ref[...], kn_ref[...]], axis=0)
    vv = jnp.concatenate([vp_ref[...], vc_ref[...], vn_ref[...]], axis=0)
    lane = lax.broadcasted_iota(jnp.int32, (kk.shape[0], LANES), 1)
    lo = lane < HEAD_DIM
    zero = jnp.zeros((kk.shape[0], LANES), BF16)
    k_lo = [jnp.where(lo, kk[:, LANES * h:LANES * (h + 1)], zero) for h in range(A_KV_HEADS)]
    k_hi = [jnp.where(lo, zero, kk[:, LANES * h:LANES * (h + 1)]) for h in range(A_KV_HEADS)]
    v_lo = [jnp.where(lo, vv[:, LANES * h:LANES * (h + 1)], zero) for h in range(A_KV_HEADS)]
    v_hi = [jnp.where(lo, zero, vv[:, LANES * h:LANES * (h + 1)]) for h in range(A_KV_HEADS)]

    qi = lax.broadcasted_iota(jnp.int32, (BLOCK, 3 * BLOCK), 0)
    kj = lax.broadcasted_iota(jnp.int32, (BLOCK, 3 * BLOCK), 1)
    band = jnp.abs(kj - BLOCK - qi) <= WINDOW
    head_of_row = lax.broadcasted_iota(jnp.int32, (A_Q_HEADS * BLOCK, 1), 0) // BLOCK
    snk = jnp.zeros((A_Q_HEADS * BLOCK, 1), F32)
    for head in range(A_Q_HEADS):
        snk = jnp.where(head_of_row == head, sink_ref[head], snk)
    for j in range(tq // BLOCK):
        kpos = i * tq + (j - 1) * BLOCK + kj
        mask = band & (kpos >= 0) & (kpos < seq)
        r0, r1 = j * BLOCK, (j + 3) * BLOCK
        pieces = []
        for c in range(A_Q_HEADS // 2):
            q2 = q_ref[j * BLOCK:(j + 1) * BLOCK, LANES * c:LANES * (c + 1)]
            for kx in (k_lo[c // 2], k_hi[c // 2]):
                s = lax.dot_general(q2, kx[r0:r1], _NT, preferred_element_type=F32)
                pieces.append(jnp.where(mask, s, NEG))
        s = jnp.concatenate(pieces, axis=0)
        m = jnp.maximum(jnp.max(s, axis=-1, keepdims=True), snk)
        p = jnp.exp(s - m)
        inv = 1.0 / (jnp.sum(p, axis=-1, keepdims=True) + jnp.exp(snk - m))
        pb = p.astype(BF16)
        for c in range(A_Q_HEADS // 2):
            out = None
            for half, vx in enumerate((v_lo[c // 2], v_hi[c // 2])):
                rows = slice((2 * c + half) * BLOCK, (2 * c + half + 1) * BLOCK)
                pv = jnp.dot(pb[rows], vx[r0:r1], preferred_element_type=F32) * inv[rows]
                out = pv if out is None else out + pv
            o_ref[j * BLOCK:(j + 1) * BLOCK, LANES * c:LANES * (c + 1)] = out.astype(o_ref.dtype)


def _attn_a(qa, ka, va, sink, bsz, seq, tq):
    t = qa.shape[0]
    nq = seq // tq
    nb = seq // BLOCK
    r = tq // BLOCK
    cur = lambda b, i: (b * nq + i, 0)
    prev = lambda b, i: (b * nb + jnp.maximum(i * r - 1, 0), 0)
    nxt = lambda b, i: (b * nb + jnp.minimum((i + 1) * r, nb - 1), 0)
    kv_specs = [pl.BlockSpec((BLOCK, 256), prev), pl.BlockSpec((tq, 256), cur), pl.BlockSpec((BLOCK, 256), nxt)]
    return pl.pallas_call(
        functools.partial(_attn_a_kernel, seq=seq, tq=tq),
        grid=(bsz, nq),
        in_specs=[pl.BlockSpec(memory_space=pltpu.SMEM), pl.BlockSpec((tq, 512), cur)] + kv_specs + kv_specs,
        out_specs=pl.BlockSpec((tq, 512), cur),
        out_shape=jax.ShapeDtypeStruct((t, 512), BF16),
        compiler_params=_params(("arbitrary", "arbitrary")),
        name="attn_a",
    )(sink, qa, ka, ka, ka, va, va, va)


def _attn_b_kernel(lam_ref, g_ref, q_ref, k_ref, v_ref, o_ref, acc_ref, s_ref, p_ref, m_ref, alpha_ref,
                   *, tq, tk, lam_init):
    seq = q_ref.shape[0]
    n_q, n_chunks = seq // tq, seq // tk
    lo = lax.broadcasted_iota(jnp.int32, (tq, LANES), 1) < HEAD_DIM
    zero = jnp.zeros((tq, LANES), BF16)
    ones_col = jnp.where(lax.broadcasted_iota(jnp.int32, (tk, LANES), 1) == 0, 1.0, 0.0).astype(BF16)

    lp = lam_ref[...]
    lam = (jnp.exp(jnp.sum(lp[0:1] * lp[1:2], axis=-1, keepdims=True))
           - jnp.exp(jnp.sum(lp[2:3] * lp[3:4], axis=-1, keepdims=True)) + lam_init)

    def q_tile(qi):
        q = q_ref[pl.ds(pl.multiple_of(qi * tq, tq), tq), :]
        return jnp.concatenate([jnp.where(lo, q, zero), jnp.where(lo, zero, q)], axis=0)

    def scores(q2, kc, slot):
        s_ref[slot] = lax.dot_general(q2, k_ref[kc * tk:(kc + 1) * tk, :], _NT, preferred_element_type=F32)

    def consume(kc, slot):
        v_aug = jnp.concatenate([v_ref[kc * tk:(kc + 1) * tk, :], ones_col], axis=1)
        for r in range(2 * tq // SOFTMAX_ROWS):
            rows = slice(r * SOFTMAX_ROWS, (r + 1) * SOFTMAX_ROWS)
            s = s_ref[slot, rows, :]
            m_old = m_ref[rows, :]
            m_new = jnp.maximum(m_old, jnp.max(s, axis=-1, keepdims=True))
            m_ref[rows, :] = m_new
            alpha_ref[rows, :] = jnp.exp2(m_old - m_new)
            p_ref[rows, :] = jnp.exp2(s - m_new).astype(BF16)
        acc_ref[...] = alpha_ref[...] * acc_ref[...] + jnp.dot(p_ref[...], v_aug, preferred_element_type=F32)

    across_tiles = n_chunks % 2 == 0

    def tile(qi, carry):
        q = q_tile(qi)
        acc_ref[...] = jnp.zeros_like(acc_ref)
        m_ref[...] = jnp.full(m_ref.shape, -jnp.inf, F32)
        if not across_tiles:
            scores(q, 0, 0)
        for kc in range(n_chunks):
            slot = kc % 2
            if kc + 1 < n_chunks:
                scores(q, kc + 1, 1 - slot)
            elif across_tiles:
                scores(q_tile(jnp.minimum(qi + 1, n_q - 1)), 0, 1 - slot)
            consume(kc, slot)
        acc1, acc2 = acc_ref[:tq], acc_ref[tq:]
        l1, l2 = acc1[:, LANES:LANES + 1], acc2[:, LANES:LANES + 1]
        o = acc1[:, :LANES] * (1.0 / l1) - lam * (acc2[:, :LANES] * (1.0 / l2))
        o = o * lax.rsqrt(jnp.mean(o * o, axis=-1, keepdims=True) + RMS_EPS)
        o = o * g_ref[...] * (1.0 - lam_init)
        o_ref[pl.ds(pl.multiple_of(qi * tq, tq), tq), :] = o.astype(o_ref.dtype)
        return carry

    if across_tiles:
        scores(q_tile(0), 0, 0)
    lax.fori_loop(0, n_q, tile, 0)


def _attn_b(qd, kd, vd, lam_p, subln_g, bsz, seq, tq, tk, lam_init):
    t = qd.shape[0]
    seq_block = pl.BlockSpec((seq, LANES), lambda b, h: (b, h))
    return pl.pallas_call(
        functools.partial(_attn_b_kernel, tq=tq, tk=tk, lam_init=lam_init),
        grid=(bsz, B_HEADS),
        in_specs=[pl.BlockSpec((4, HEAD_DIM), lambda b, h: (0, 0)),
                  pl.BlockSpec((1, LANES), lambda b, h: (0, 0)),
                  seq_block, seq_block, seq_block],
        out_specs=seq_block,
        out_shape=jax.ShapeDtypeStruct((t, 512), BF16),
        scratch_shapes=[pltpu.VMEM((2 * tq, 2 * LANES), F32),
                        pltpu.VMEM((2, 2 * tq, tk), F32),
                        pltpu.VMEM((2 * tq, tk), BF16),
                        pltpu.VMEM((2 * tq, 1), F32),
                        pltpu.VMEM((2 * tq, 1), F32)],
        compiler_params=_params(("arbitrary", "arbitrary")),
        name="attn_b",
    )(lam_p, subln_g, qd, kd, vd)


HALF = D_MODEL // 2
_HI_MASK = -65536


def _pack_pair(lo, hi):
    lo_bits = lax.bitcast_convert_type(lo.astype(BF16).astype(F32), jnp.int32)
    hi_bits = lax.bitcast_convert_type(hi.astype(BF16).astype(F32), jnp.int32)
    return lax.shift_right_logical(lo_bits, 16) | hi_bits


def _unpack_pair(w):
    lo = lax.bitcast_convert_type(lax.shift_left(w, 16), F32)
    hi = lax.bitcast_convert_type(w & _HI_MASK, F32)
    return lo, hi


def _layer_norm(y, g, b):
    mu = jnp.mean(y, axis=-1, keepdims=True)
    d = y - mu
    var = jnp.mean(d * d, axis=-1, keepdims=True)
    return d * lax.rsqrt(var + LN_EPS) * g + b


def _first_argmax(vals, rowf, big):
    m = jnp.max(vals, axis=0, keepdims=True)
    idx = jnp.min(jnp.where(vals == m, rowf, big), axis=0, keepdims=True)
    return m, idx


def _post_attn_kernel(x_ref, oa_ref, ob_ref, wgate_ref, woa_ref, wob_ref, wout_ref, g1_ref, b1_ref,
                      rwt_ref, rb_ref, tri_ref, cnt_in_ref,
                      x1_ref, x1f_ref, eidx_ref, wts_ref, rank_ref, cnt_ref, cnt_scr):
    i = pl.program_id(0)
    tm = x_ref.shape[0]

    @pl.when(i == 0)
    def _():
        cnt_scr[...] = cnt_in_ref[...]

    subs = [slice(r * (tm // POST_SPLIT), (r + 1) * (tm // POST_SPLIT)) for r in range(POST_SPLIT)]
    xs = [x_ref[rows, :] for rows in subs]
    gates = [jax.nn.sigmoid(jnp.dot(x.astype(BF16), wgate_ref[...], preferred_element_type=F32))
             for x in xs]
    ab = [(jnp.dot(oa_ref[rows, :], woa_ref[...], preferred_element_type=F32),
           jnp.dot(ob_ref[rows, :], wob_ref[...], preferred_element_type=F32)) for rows in subs]
    merged = [(g[:, :D_MODEL] * a + g[:, D_MODEL:] * b).astype(BF16) for g, (a, b) in zip(gates, ab)]
    mix = [jnp.dot(m, wout_ref[...], preferred_element_type=F32) for m in merged]
    x1s = [_layer_norm(DN_ALPHA * x + mx, g1_ref[...], b1_ref[...]) for x, mx in zip(xs, mix)]
    for rows, x1 in zip(subs, x1s):
        x1p = _pack_pair(x1[:, :HALF], x1[:, HALF:])
        for piece in range(SC_SPLIT):
            x1_ref[piece, rows, :] = x1p[:, piece * PIECE:(piece + 1) * PIECE]
        x1f_ref[rows, :] = x1
    x1b = jnp.concatenate([x1.astype(BF16) for x1 in x1s], axis=0)

    logits = lax.dot_general(rwt_ref[...], x1b, _NT, preferred_element_type=F32)
    scores = jax.nn.sigmoid(logits)
    choice = scores + rb_ref[...]
    ninf = -jnp.inf
    grow = lax.broadcasted_iota(jnp.int32, (GROUP_SIZE, tm), 0).astype(F32)
    gscore = []
    for g in range(N_GROUPS):
        blk = choice[GROUP_SIZE * g:GROUP_SIZE * (g + 1)]
        m1, i1 = _first_argmax(blk, grow, float(GROUP_SIZE))
        m2 = jnp.max(jnp.where(grow == i1, ninf, blk), axis=0, keepdims=True)
        gscore.append(m1 + m2)
    selected = [jnp.zeros((1, tm), F32) for _ in range(N_GROUPS)]
    work = list(gscore)
    for _ in range(TOPK_GROUPS):
        best = work[0]
        for g in range(1, N_GROUPS):
            best = jnp.maximum(best, work[g])
        taken = jnp.zeros((1, tm), F32)
        for g in range(N_GROUPS):
            hit = jnp.where((work[g] == best) & (taken == 0.0), 1.0, 0.0)
            taken = jnp.maximum(taken, hit)
            selected[g] = jnp.maximum(selected[g], hit)
            work[g] = jnp.where(hit > 0.0, ninf, work[g])
    masked = jnp.concatenate(
        [jnp.where(selected[g] > 0.0, choice[GROUP_SIZE * g:GROUP_SIZE * (g + 1)], ninf) for g in range(N_GROUPS)],
        axis=0)

    rowf = lax.broadcasted_iota(jnp.int32, (N_EXPERTS, tm), 0).astype(F32)
    hits, idxs, ws = [], [], []
    for _ in range(TOP_K):
        _, idx = _first_argmax(masked, rowf, float(N_EXPERTS))
        hit = rowf == idx
        hits.append(hit)
        idxs.append(idx)
        ws.append(jnp.sum(jnp.where(hit, scores, 0.0), axis=0, keepdims=True))
        masked = jnp.where(hit, ninf, masked)
    wsum = ws[0]
    for k in range(1, TOP_K):
        wsum = wsum + ws[k]

    member = hits[0]
    for k in range(1, TOP_K):
        member = member | hits[k]
    member_f = jnp.where(member, 1.0, 0.0)
    before = jnp.dot(member_f.astype(BF16), tri_ref[...], preferred_element_type=F32) + cnt_scr[...]
    for k in range(TOP_K):
        eidx_ref[k:k + 1, :] = idxs[k].astype(jnp.int32)
        wts_ref[k:k + 1, :] = ws[k] / wsum * ROUTED_SCALE
        rank_ref[k:k + 1, :] = jnp.sum(jnp.where(hits[k], before, 0.0), axis=0, keepdims=True).astype(jnp.int32)
    cnt_scr[...] = cnt_scr[...] + jnp.sum(member_f, axis=1, keepdims=True)
    cnt_ref[...] = jnp.broadcast_to(cnt_scr[...], cnt_ref.shape).astype(jnp.int32)


def _post_attn(x2, oa, ob, wgate, woa, wob, wout, g1, b1, rwt, rb, cnt_in, tm):
    t = x2.shape[0]
    tri = jnp.triu(jnp.ones((tm, tm), F32), k=1).astype(BF16)
    row = lambda i: (i, 0)
    col = lambda i: (0, i)
    full = lambda i: (0, 0)
    wspec = lambda arr: pl.BlockSpec(arr.shape, full)
    weights = (wgate, woa, wob, wout, g1, b1, rwt, rb, tri, cnt_in)
    return pl.pallas_call(
        _post_attn_kernel,
        grid=(t // tm,),
        in_specs=[pl.BlockSpec((tm, D_MODEL), row), pl.BlockSpec((tm, 512), row), pl.BlockSpec((tm, 512), row)]
                 + [wspec(w) for w in weights],
        out_specs=[pl.BlockSpec((SC_SPLIT, tm, PIECE), lambda i: (0, i, 0)), pl.BlockSpec((tm, D_MODEL), row),
                   pl.BlockSpec((TOP_K, tm), col), pl.BlockSpec((TOP_K, tm), col), pl.BlockSpec((TOP_K, tm), col),
                   pl.BlockSpec((N_EXPERTS, LANES), full)],
        out_shape=[jax.ShapeDtypeStruct((SC_SPLIT, t, PIECE), jnp.int32), jax.ShapeDtypeStruct((t, D_MODEL), F32),
                   jax.ShapeDtypeStruct((TOP_K, t), jnp.int32), jax.ShapeDtypeStruct((TOP_K, t), F32),
                   jax.ShapeDtypeStruct((TOP_K, t), jnp.int32), jax.ShapeDtypeStruct((N_EXPERTS, LANES), jnp.int32)],
        scratch_shapes=[pltpu.VMEM((N_EXPERTS, 1), F32)],
        compiler_params=_params(("arbitrary",)),
        name="post_attn",
    )(x2, oa, ob, *weights)


def _dest_kernel(eidx_ref, rank_ref, pstart_ref, dest_ref):
    tm = eidx_ref.shape[1]
    rows = lax.broadcasted_iota(jnp.int32, (N_EXPERTS, tm), 0)
    pstart = pstart_ref[...]
    for k in range(TOP_K):
        hit = rows == eidx_ref[k:k + 1, :]
        start = jnp.sum(jnp.where(hit, pstart, 0).astype(F32), axis=0, keepdims=True).astype(jnp.int32)
        dest_ref[k:k + 1, :] = start + rank_ref[k:k + 1, :]


def _dest(eidx, rank, pstart, tm):
    t = eidx.shape[1]
    return pl.pallas_call(
        _dest_kernel,
        grid=(t // tm,),
        in_specs=[pl.BlockSpec((TOP_K, tm), lambda i: (0, i)), pl.BlockSpec((TOP_K, tm), lambda i: (0, i)),
                  pl.BlockSpec((N_EXPERTS, 1), lambda i: (0, 0))],
        out_specs=pl.BlockSpec((TOP_K, tm), lambda i: (0, i)),
        out_shape=jax.ShapeDtypeStruct((TOP_K, t), jnp.int32),
        compiler_params=_params(("arbitrary",)),
        name="dest",
    )(eidx, rank, pstart)


def _sc_mesh():
    return plsc.VectorSubcoreMesh(core_axis_name="core", subcore_axis_name="subcore",
                                  num_cores=SC_CORES, num_subcores=SC_SUBCORES)


def _sc_scatter_rows(xs_groups, dests, n_rows):
    n_groups = len(xs_groups)

    @functools.partial(pl.kernel, out_type=jax.ShapeDtypeStruct((SC_SPLIT, n_rows, PIECE), jnp.int32),
                       mesh=_sc_mesh(), scratch_types=[pltpu.SemaphoreType.DMA])
    def scatter(*refs):
        out_hbm, sem = refs[2 * n_groups], refs[2 * n_groups + 1]
        for g in range(n_groups):
            x_hbm, idx_hbm = refs[2 * g], refs[2 * g + 1]
            for piece in range(SC_SPLIT):
                def window(x_vmem, idx_vmem, piece=piece):
                    copies = [pltpu.async_copy(x_vmem, out_hbm.at[piece].at[idx_vmem.at[k]], sem)
                              for k in range(TOP_K)]
                    for copy in copies:
                        copy.wait()

                pltpu.emit_pipeline(
                    window,
                    grid=(x_hbm.shape[1] // SC_WINDOW,),
                    in_specs=[pl.BlockSpec((SC_WINDOW, PIECE), lambda i: (i, 0)),
                              pl.BlockSpec((TOP_K, SC_WINDOW), lambda i: (0, i))],
                    out_specs=[],
                    core_axis_name=("core", "subcore"),
                    dimension_semantics=(pltpu.PARALLEL,),
                )(x_hbm.at[piece], idx_hbm)

    args = []
    for x, d in zip(xs_groups, dests):
        args += [x, d]
    return scatter(*args)


def _experts_kernel(cb_ref, cnt_ref, xs_ref, wg_ref, wu_ref, wd_ref, ys_ref,
                    wg_scr, wu_scr, wd_scr, xbuf, ybuf, sem_in, sem_out):
    e = pl.program_id(0)
    g0, g1, total = cb_ref[e], cb_ref[e + 1], cb_ref[N_EXPERTS]

    def rows(g):
        return pl.ds(pl.multiple_of(g * EXPERT_ROWS, EXPERT_ROWS), EXPERT_ROWS)

    def x_copy(g):
        s = g % EXPERT_SLOTS
        return pltpu.make_async_copy(xs_ref.at[:, rows(g), :], xbuf.at[s], sem_in.at[s])

    def y_copy(g):
        s = g % EXPERT_SLOTS
        return pltpu.make_async_copy(ybuf.at[s], ys_ref.at[:, rows(g), :], sem_out.at[s])

    @pl.when(e == 0)
    def _():
        for g in range(EXPERT_SLOTS - 1):
            pl.when(g < total)(lambda g=g: x_copy(g).start())

    @pl.when(g1 > g0)
    def _():
        wg_scr[...] = wg_ref[0].astype(BF16)
        wu_scr[...] = wu_ref[0].astype(BF16)
        wd_scr[...] = wd_ref[0].astype(BF16)

    def block(g, carry):
        slot = g % EXPERT_SLOTS

        @pl.when(g + EXPERT_SLOTS - 1 < total)
        def _():
            x_copy(g + EXPERT_SLOTS - 1).start()

        x_copy(g).wait()

        @pl.when(g >= EXPERT_SLOTS)
        def _():
            y_copy(g - EXPERT_SLOTS).wait()

        valid = lax.broadcasted_iota(jnp.int32, (EXPERT_ROWS, PIECE), 0) < cnt_ref[e] - (g - g0) * EXPERT_ROWS
        hg = hu = None
        for piece in range(SC_SPLIT):
            for part, col0 in zip(_unpack_pair(xbuf[slot, piece]), (piece * PIECE, HALF + piece * PIECE)):
                xb = jnp.where(valid, part, 0.0).astype(BF16)
                pg = jnp.dot(xb, wg_scr[col0:col0 + PIECE], preferred_element_type=F32)
                pu = jnp.dot(xb, wu_scr[col0:col0 + PIECE], preferred_element_type=F32)
                hg = pg if hg is None else hg + pg
                hu = pu if hu is None else hu + pu
        h = ((hg * jax.nn.sigmoid(hg)) * hu).astype(BF16)
        y = _pack_pair(jnp.dot(h, wd_scr[:, :HALF], preferred_element_type=F32),
                       jnp.dot(h, wd_scr[:, HALF:], preferred_element_type=F32))
        for piece in range(SC_SPLIT):
            ybuf[slot, piece] = y[:, piece * PIECE:(piece + 1) * PIECE]
        y_copy(g).start()
        return carry

    lax.fori_loop(g0, g1, block, 0)

    @pl.when(e == N_EXPERTS - 1)
    def _():
        for back in range(EXPERT_SLOTS, 0, -1):
            pl.when(total >= back)(lambda back=back: y_copy(total - back).wait())


def _experts(cum_blocks, counts, xs, w_gate, w_up, w_down):
    n_rows = xs.shape[1]
    wmap = lambda e, cb, cnt: (e, 0, 0)
    grid_spec = pltpu.PrefetchScalarGridSpec(
        num_scalar_prefetch=2,
        grid=(N_EXPERTS,),
        in_specs=[pl.BlockSpec(memory_space=pl.ANY),
                  pl.BlockSpec((1, D_MODEL, D_EXPERT), wmap),
                  pl.BlockSpec((1, D_MODEL, D_EXPERT), wmap),
                  pl.BlockSpec((1, D_EXPERT, D_MODEL), wmap)],
        out_specs=pl.BlockSpec(memory_space=pl.ANY),
        scratch_shapes=[pltpu.VMEM((D_MODEL, D_EXPERT), BF16), pltpu.VMEM((D_MODEL, D_EXPERT), BF16),
                        pltpu.VMEM((D_EXPERT, D_MODEL), BF16),
                        pltpu.VMEM((EXPERT_SLOTS, SC_SPLIT, EXPERT_ROWS, PIECE), jnp.int32),
                        pltpu.VMEM((EXPERT_SLOTS, SC_SPLIT, EXPERT_ROWS, PIECE), jnp.int32),
                        pltpu.SemaphoreType.DMA((EXPERT_SLOTS,)), pltpu.SemaphoreType.DMA((EXPERT_SLOTS,))])
    return pl.pallas_call(
        _experts_kernel,
        grid_spec=grid_spec,
        out_shape=jax.ShapeDtypeStruct((SC_SPLIT, n_rows, PIECE), jnp.int32),
        compiler_params=_params(("arbitrary",)),
        name="experts",
    )(cum_blocks, counts, xs, w_gate, w_up, w_down)


def _sc_gather_rows(table, idx):
    n, d = idx.shape[0], table.shape[2]

    @functools.partial(pl.kernel, out_type=jax.ShapeDtypeStruct((SC_SPLIT, n, d), table.dtype), mesh=_sc_mesh(),
                       scratch_types=[])
    def gather(table_hbm, idx_hbm, out_hbm):
        for piece in range(SC_SPLIT):
            def window(idx_vmem, out_vmem, piece=piece):
                pltpu.sync_copy(table_hbm.at[piece].at[idx_vmem.at[0]], out_vmem)

            pltpu.emit_pipeline(
                window,
                grid=(n // SC_WINDOW,),
                in_specs=[pl.BlockSpec((1, SC_WINDOW), lambda i: (0, i))],
                out_specs=[pl.BlockSpec((SC_WINDOW, d), lambda i: (i, 0))],
                core_axis_name=("core", "subcore"),
                dimension_semantics=(pltpu.PARALLEL,),
            )(idx_hbm, out_hbm.at[piece])

    return gather(table, idx.reshape(1, n))


def _combine_kernel(w_ref, x1_ref, g2_ref, b2_ref, wsg_ref, wsu_ref, wsd_ref, rows_ref, o_ref):
    w = w_ref[...]
    x1 = x1_ref[...]
    x1b = x1.astype(BF16)
    hg = jnp.dot(x1b, wsg_ref[...], preferred_element_type=F32)
    hu = jnp.dot(x1b, wsu_ref[...], preferred_element_type=F32)
    hs = (hg * jax.nn.sigmoid(hg)) * hu
    base = DN_ALPHA * x1 + jnp.dot(hs.astype(BF16), wsd_ref[...], preferred_element_type=F32)
    acc = [[base[:, half * HALF + p * PIECE:half * HALF + (p + 1) * PIECE] for p in range(SC_SPLIT)]
           for half in range(2)]
    for k in range(TOP_K):
        wk = w[:, k:k + 1]
        for p in range(SC_SPLIT):
            lo, hi = _unpack_pair(rows_ref[p, k])
            acc[0][p] = acc[0][p] + lo * wk
            acc[1][p] = acc[1][p] + hi * wk
    o_ref[...] = _layer_norm(jnp.concatenate(acc[0] + acc[1], axis=1), g2_ref[...], b2_ref[...])


def _combine(wts_tk, x1, g2, b2, wsg, wsu, wsd, rows, tm):
    t = x1.shape[0]
    full = lambda i: (0, 0)
    return pl.pallas_call(
        _combine_kernel,
        grid=(t // tm,),
        in_specs=[pl.BlockSpec((tm, TOP_K), lambda i: (i, 0)),
                  pl.BlockSpec((tm, D_MODEL), lambda i: (i, 0)),
                  pl.BlockSpec((1, D_MODEL), full),
                  pl.BlockSpec((1, D_MODEL), full),
                  pl.BlockSpec(wsg.shape, full), pl.BlockSpec(wsu.shape, full), pl.BlockSpec(wsd.shape, full),
                  pl.BlockSpec((SC_SPLIT, TOP_K, tm, PIECE), lambda i: (0, 0, i, 0))],
        out_specs=pl.BlockSpec((tm, D_MODEL), lambda i: (i, 0)),
        out_shape=jax.ShapeDtypeStruct((t, D_MODEL), F32),
        compiler_params=_params(("arbitrary",)),
        name="combine",
    )(wts_tk, x1, g2, b2, wsg, wsu, wsd, rows)


def _tile(n, pref):
    t = pref
    while n % t:
        t //= 2
    return t


def _token_mixers(x, layer_idx, wp, cnt_in):
    bsz, seq, _ = x.shape
    t = bsz * seq
    x2 = x.reshape(t, D_MODEL)
    lam_init = 0.8 - 0.6 * math.exp(-0.3 * layer_idx)

    cos_t, sin_t = _rope_tables(seq)
    qa, ka, va, qd, kd, vd = _in_proj(x2, wp["w_qkv"], cos_t, sin_t, seq, _tile(seq, 256))
    oa = _attn_a(qa, ka, va, wp["sink"], bsz, seq, _tile(seq, 512))
    ob = _attn_b(qd, kd, vd, wp["lam_p"], wp["subln_g"], bsz, seq, _tile(seq, 512), _tile(seq, 1024), lam_init)
    return _post_attn(
        x2, oa, ob, wp["w_gates"], wp["w_o_a"], wp["w_o_b"], wp["w_out"], wp["ln1_g"], wp["ln1_b"],
        wp["router_wt"], wp["router_bias"], cnt_in, _tile(t, 512))


def _encoder_layer(xs_in, layer_idx, wp):
    cnt_in = jnp.zeros((N_EXPERTS, 1), F32)
    routed = []
    for x in xs_in:
        x1p, x1f, eidx, wts, rank, cnt = _token_mixers(x, layer_idx, wp, cnt_in)
        cnt_in = cnt[:, :1].astype(F32)
        routed.append((x1p, x1f, eidx, wts, rank))

    n_assign = sum(x.shape[0] * x.shape[1] for x in xs_in) * TOP_K
    n_blk = -(-(n_assign + N_EXPERTS * (EXPERT_ROWS - 1)) // EXPERT_ROWS)
    counts = cnt[:, 0]
    padded = ((counts + EXPERT_ROWS - 1) // EXPERT_ROWS) * EXPERT_ROWS
    pend = jnp.cumsum(padded)
    pstart = pend - padded
    cum_blocks = jnp.concatenate([jnp.zeros((1,), jnp.int32), (pend // EXPERT_ROWS).astype(jnp.int32)])
    pstart_col = pstart.astype(jnp.int32)[:, None]

    dests = [_dest(eidx, rank, pstart_col, _tile(eidx.shape[1], 512)) for _, _, eidx, _, rank in routed]
    xs = _sc_scatter_rows([r[0] for r in routed], dests, n_blk * EXPERT_ROWS)
    ys = _experts(cum_blocks, counts.astype(jnp.int32), xs, wp["w_gate"], wp["w_up"], wp["w_down"])
    outs = []
    for x, dest, (x1p, x1f, eidx, wts, rank) in zip(xs_in, dests, routed):
        t = x1f.shape[0]
        rows = _sc_gather_rows(ys, dest.reshape(-1)).reshape(SC_SPLIT, TOP_K, t, PIECE)
        y = _combine(wts.T, x1f, wp["ln2_g"], wp["ln2_b"], wp["ws_gate"], wp["ws_up"], wp["ws_down"], rows,
                     _tile(t, 256))
        outs.append(y.reshape(x.shape))
    return outs


def kernel(x_prompt, x_sample, w_in, attn_sink, lambda_q1, lambda_k1, lambda_q2, lambda_k2, subln_g, w_o_a, w_o_b, w_out, ln1_g, ln1_b, router_w, router_bias, w_gate, w_up, w_down, ws_gate, ws_up, ws_down, ln2_g, ln2_b):
    y_prompt, y_sample = x_prompt, x_sample
    for l in range(DEPTH):
        w_qkv, w_gates = _prep_w_in(w_in[l])
        wp = {
            "w_qkv": w_qkv, "w_gates": w_gates,
            "sink": attn_sink[l].astype(F32),
            "lam_p": jnp.stack([lambda_q1[l], lambda_k1[l], lambda_q2[l], lambda_k2[l]]).astype(F32),
            "subln_g": subln_g[l].astype(F32)[None, :],
            "w_o_a": w_o_a[l].astype(BF16), "w_o_b": w_o_b[l].astype(BF16), "w_out": w_out[l].astype(BF16),
            "ln1_g": ln1_g[l].astype(F32)[None, :], "ln1_b": ln1_b[l].astype(F32)[None, :],
            "router_wt": router_w[l].T.astype(BF16), "router_bias": router_bias[l].astype(F32)[:, None],
            "w_gate": w_gate[l], "w_up": w_up[l], "w_down": w_down[l],
            "ws_gate": ws_gate[l].astype(BF16), "ws_up": ws_up[l].astype(BF16), "ws_down": ws_down[l].astype(BF16),
            "ln2_g": ln2_g[l].astype(F32)[None, :], "ln2_b": ln2_b[l].astype(F32)[None, :],
        }
        y_prompt, y_sample = _encoder_layer((y_prompt, y_sample), l, wp)
    return (y_prompt, y_sample)
```

```python
import functools
import math

import jax
import jax.numpy as jnp
from jax import lax
from jax.experimental import pallas as pl
from jax.experimental.pallas import tpu as pltpu
from jax.experimental.pallas import tpu_sc as plsc

D_MODEL = 1024
HEAD_DIM = 64
ROPE_THETA = 10000.0
BLOCK = 128
A_Q_HEADS = 8
A_KV_HEADS = 2
WINDOW = 128
B_HEADS = 4
N_EXPERTS = 256
TOP_K = 8
N_GROUPS = 8
TOPK_GROUPS = 4
GROUP_SIZE = N_EXPERTS // N_GROUPS
D_EXPERT = 256
ROUTED_SCALE = 2.5
EXPERT_ROWS = 512
EXPERT_SLOTS = 4
DEPTH = 1
DN_ALPHA = (2 * DEPTH) ** 0.25
LN_EPS = 1e-5
RMS_EPS = 1e-5
NEG = -1e30
LOG2E = math.log2(math.e)
SOFTMAX_ROWS = 16

LANES = 128
SC_CORES = 2
SC_SUBCORES = 16
SC_WINDOW = 128
SC_SPLIT = 2
PIECE = D_MODEL // 2 // SC_SPLIT
VMEM_LIMIT_BYTES = 56 * 1024 * 1024

F32 = jnp.float32
BF16 = jnp.bfloat16
_NT = (((1,), (1,)), ((), ()))


def _params(sem, vmem=VMEM_LIMIT_BYTES):
    return pltpu.CompilerParams(dimension_semantics=sem, vmem_limit_bytes=vmem)


_C_QA, _C_KA, _C_VA, _C_QD, _C_KD, _C_VD, _C_END = 0, 512, 768, 1024, 1536, 2048, 2560


def _prep_w_in(w_in):
    cuts = [0, 512, 640, 768, 1280, 1792, 2304, 4352]
    qa, ka, va, qd, kd, vd, gates = [w_in[:, cuts[i]:cuts[i + 1]] for i in range(7)]
    dup = lambda w: jnp.concatenate([w[:, :64], w[:, :64], w[:, 64:], w[:, 64:]], axis=1)
    return jnp.concatenate([qa, dup(ka), dup(va), qd, kd, vd], axis=1).astype(BF16), gates.astype(BF16)


def _rope_tables(s):
    half = HEAD_DIM // 2
    inv = 1.0 / (ROPE_THETA ** (jnp.arange(half, dtype=F32) / half))
    ang = jnp.arange(s, dtype=F32)[:, None] * inv[None, :]
    cos, sin = jnp.cos(ang), jnp.sin(ang)
    return (jnp.concatenate([cos, cos, cos, cos], axis=1),
            jnp.concatenate([-sin, sin, -sin, sin], axis=1))


def _in_proj_kernel(x_ref, w_ref, cos_ref, sin_ref, qa_ref, ka_ref, va_ref, qd_ref, kd_ref, vd_ref):
    xb = x_ref[...].astype(BF16)
    cos = cos_ref[...]
    sin = sin_ref[...]
    lane = lax.broadcasted_iota(jnp.int32, cos.shape, 1)
    first_half = (lane & (HEAD_DIM // 2)) == 0

    def proj(c0, c1):
        return jnp.dot(xb, w_ref[:, c0:c1], preferred_element_type=F32)

    def rope_store(u, out_ref, scale):
        for j in range(u.shape[1] // LANES):
            uj = u[:, LANES * j:LANES * (j + 1)]
            rot = jnp.where(first_half, pltpu.roll(uj, LANES - 32, 1), pltpu.roll(uj, 32, 1))
            r = uj * cos + rot * sin
            if scale != 1.0:
                r = r * scale
            out_ref[:, LANES * j:LANES * (j + 1)] = r.astype(out_ref.dtype)

    scale = HEAD_DIM ** -0.5
    rope_store(proj(_C_QA, _C_KA), qa_ref, scale)
    rope_store(proj(_C_KA, _C_VA), ka_ref, 1.0)
    va_ref[...] = proj(_C_VA, _C_QD).astype(va_ref.dtype)
    rope_store(proj(_C_QD, _C_KD), qd_ref, scale * LOG2E)
    rope_store(proj(_C_KD, _C_VD), kd_ref, 1.0)
    vd_ref[...] = proj(_C_VD, _C_END).astype(vd_ref.dtype)


def _in_proj(x2, w_perm, cos_t, sin_t, seq, tm):
    t = x2.shape[0]
    nseq = seq // tm
    row = lambda i: (i, 0)
    widths = (512, 256, 256, 512, 512, 512)
    return pl.pallas_call(
        _in_proj_kernel,
        grid=(t // tm,),
        in_specs=[pl.BlockSpec((tm, D_MODEL), row),
                  pl.BlockSpec((D_MODEL, _C_END), lambda i: (0, 0)),
                  pl.BlockSpec((tm, LANES), lambda i: (i % nseq, 0)),
                  pl.BlockSpec((tm, LANES), lambda i: (i % nseq, 0))],
        out_specs=[pl.BlockSpec((tm, w), row) for w in widths],
        out_shape=[jax.ShapeDtypeStruct((t, w), BF16) for w in widths],
        compiler_params=_params(("arbitrary",)),
        name="in_proj",
    )(x2, w_perm, cos_t, sin_t)


def _attn_a_kernel(sink_ref, q_ref, kp_ref, kc_ref, kn_ref, vp_ref, vc_ref, vn_ref, o_ref, *, seq, tq):
    i = pl.program_id(1)
    kk = jnp.concatenate([kp_ref[...], kc_ref[...], kn_ref[...]], axis=0)
    vv = jnp.concatenate([vp_ref[...], vc_ref[...], vn_ref[...]], axis=0)
    lane = lax.broadcasted_iota(jnp.int32, (kk.shape[0], LANES), 1)
    lo = lane < HEAD_DIM
    zero = jnp.zeros((kk.shape[0], LANES), BF16)
    k_lo = [jnp.where(lo, kk[:, LANES * h:LANES * (h + 1)], zero) for h in range(A_KV_HEADS)]
    k_hi = [jnp.where(lo, zero, kk[:, LANES * h:LANES * (h + 1)]) for h in range(A_KV_HEADS)]
    v_lo = [jnp.where(lo, vv[:, LANES * h:LANES * (h + 1)], zero) for h in range(A_KV_HEADS)]
    v_hi = [jnp.where(lo, zero, vv[:, LANES * h:LANES * (h + 1)]) for h in range(A_KV_HEADS)]

    qi = lax.broadcasted_iota(jnp.int32, (BLOCK, 3 * BLOCK), 0)
    kj = lax.broadcasted_iota(jnp.int32, (BLOCK, 3 * BLOCK), 1)
    band = jnp.abs(kj - BLOCK - qi) <= WINDOW
    head_of_row = lax.broadcasted_iota(jnp.int32, (A_Q_HEADS * BLOCK, 1), 0) // BLOCK
    snk = jnp.zeros((A_Q_HEADS * BLOCK, 1), F32)
    for head in range(A_Q_HEADS):
        snk = jnp.where(head_of_row == head, sink_ref[head], snk)
    for j in range(tq // BLOCK):
        kpos = i * tq + (j - 1) * BLOCK + kj
        mask = band & (kpos >= 0) & (kpos < seq)
        r0, r1 = j * BLOCK, (j + 3) * BLOCK
        pieces = []
        for c in range(A_Q_HEADS // 2):
            q2 = q_ref[j * BLOCK:(j + 1) * BLOCK, LANES * c:LANES * (c + 1)]
            for kx in (k_lo[c // 2], k_hi[c // 2]):
                s = lax.dot_general(q2, kx[r0:r1], _NT, preferred_element_type=F32)
                pieces.append(jnp.where(mask, s, NEG))
        s = jnp.concatenate(pieces, axis=0)
        m = jnp.maximum(jnp.max(s, axis=-1, keepdims=True), snk)
        p = jnp.exp(s - m)
        inv = 1.0 / (jnp.sum(p, axis=-1, keepdims=True) + jnp.exp(snk - m))
        pb = p.astype(BF16)
        for c in range(A_Q_HEADS // 2):
            out = None
            for half, vx in enumerate((v_lo[c // 2], v_hi[c // 2])):
                rows = slice((2 * c + half) * BLOCK, (2 * c + half + 1) * BLOCK)
                pv = jnp.dot(pb[rows], vx[r0:r1], preferred_element_type=F32) * inv[rows]
                out = pv if out is None else out + pv
            o_ref[j * BLOCK:(j + 1) * BLOCK, LANES * c:LANES * (c + 1)] = out.astype(o_ref.dtype)


def _attn_a(qa, ka, va, sink, bsz, seq, tq):
    t = qa.shape[0]
    nq = seq // tq
    nb = seq // BLOCK
    r = tq // BLOCK
    cur = lambda b, i: (b * nq + i, 0)
    prev = lambda b, i: (b * nb + jnp.maximum(i * r - 1, 0), 0)
    nxt = lambda b, i: (b * nb + jnp.minimum((i + 1) * r, nb - 1), 0)
    kv_specs = [pl.BlockSpec((BLOCK, 256), prev), pl.BlockSpec((tq, 256), cur), pl.BlockSpec((BLOCK, 256), nxt)]
    return pl.pallas_call(
        functools.partial(_attn_a_kernel, seq=seq, tq=tq),
        grid=(bsz, nq),
        in_specs=[pl.BlockSpec(memory_space=pltpu.SMEM), pl.BlockSpec((tq, 512), cur)] + kv_specs + kv_specs,
        out_specs=pl.BlockSpec((tq, 512), cur),
        out_shape=jax.ShapeDtypeStruct((t, 512), BF16),
        compiler_params=_params(("arbitrary", "arbitrary")),
        name="attn_a",
    )(sink, qa, ka, ka, ka, va, va, va)


def _attn_b_kernel(lam_ref, g_ref, q_ref, k_ref, v_ref, o_ref, acc_ref, s_ref, p_ref, m_ref, alpha_ref,
                   *, tq, tk, lam_init):
    seq = q_ref.shape[0]
    n_q, n_chunks = seq // tq, seq // tk
    lo = lax.broadcasted_iota(jnp.int32, (tq, LANES), 1) < HEAD_DIM
    zero = jnp.zeros((tq, LANES), BF16)
    ones_col = jnp.where(lax.broadcasted_iota(jnp.int32, (tk, LANES), 1) == 0, 1.0, 0.0).astype(BF16)

    lp = lam_ref[...]
    lam = (jnp.exp(jnp.sum(lp[0:1] * lp[1:2], axis=-1, keepdims=True))
           - jnp.exp(jnp.sum(lp[2:3] * lp[3:4], axis=-1, keepdims=True)) + lam_init)

    def q_tile(qi):
        q = q_ref[pl.ds(pl.multiple_of(qi * tq, tq), tq), :]
        return jnp.concatenate([jnp.where(lo, q, zero), jnp.where(lo, zero, q)], axis=0)

    def scores(q2, kc, slot):
        s_ref[slot] = lax.dot_general(q2, k_ref[kc * tk:(kc + 1) * tk, :], _NT, preferred_element_type=F32)

    def consume(kc, slot):
        v_aug = jnp.concatenate([v_ref[kc * tk:(kc + 1) * tk, :], ones_col], axis=1)
        for r in range(2 * tq // SOFTMAX_ROWS):
            rows = slice(r * SOFTMAX_ROWS, (r + 1) * SOFTMAX_ROWS)
            s = s_ref[slot, rows, :]
            m_old = m_ref[rows, :]
            m_new = jnp.maximum(m_old, jnp.max(s, axis=-1, keepdims=True))
            m_ref[rows, :] = m_new
            alpha_ref[rows, :] = jnp.exp2(m_old - m_new)
            p_ref[rows, :] = jnp.exp2(s - m_new).astype(BF16)
        acc_ref[...] = alpha_ref[...] * acc_ref[...] + jnp.dot(p_ref[...], v_aug, preferred_element_type=F32)

    across_tiles = n_chunks % 2 == 0

    def tile(qi, carry):
        q = q_tile(qi)
        acc_ref[...] = jnp.zeros_like(acc_ref)
        m_ref[...] = jnp.full(m_ref.shape, -jnp.inf, F32)
        if not across_tiles:
            scores(q, 0, 0)
        for kc in range(n_chunks):
            slot = kc % 2
            if kc + 1 < n_chunks:
                scores(q, kc + 1, 1 - slot)
            elif across_tiles:
                scores(q_tile(jnp.minimum(qi + 1, n_q - 1)), 0, 1 - slot)
            consume(kc, slot)
        acc1, acc2 = acc_ref[:tq], acc_ref[tq:]
        l1, l2 = acc1[:, LANES:LANES + 1], acc2[:, LANES:LANES + 1]
        o = acc1[:, :LANES] * (1.0 / l1) - lam * (acc2[:, :LANES] * (1.0 / l2))
        o = o * lax.rsqrt(jnp.mean(o * o, axis=-1, keepdims=True) + RMS_EPS)
        o = o * g_ref[...] * (1.0 - lam_init)
        o_ref[pl.ds(pl.multiple_of(qi * tq, tq), tq), :] = o.astype(o_ref.dtype)
        return carry

    if across_tiles:
        scores(q_tile(0), 0, 0)
    lax.fori_loop(0, n_q, tile, 0)


def _attn_b(qd, kd, vd, lam_p, subln_g, bsz, seq, tq, tk, lam_init):
    t = qd.shape[0]
    seq_block = pl.BlockSpec((seq, LANES), lambda b, h: (b, h))
    return pl.pallas_call(
        functools.partial(_attn_b_kernel, tq=tq, tk=tk, lam_init=lam_init),
        grid=(bsz, B_HEADS),
        in_specs=[pl.BlockSpec((4, HEAD_DIM), lambda b, h: (0, 0)),
                  pl.BlockSpec((1, LANES), lambda b, h: (0, 0)),
                  seq_block, seq_block, seq_block],
        out_specs=seq_block,
        out_shape=jax.ShapeDtypeStruct((t, 512), BF16),
        scratch_shapes=[pltpu.VMEM((2 * tq, 2 * LANES), F32),
                        pltpu.VMEM((2, 2 * tq, tk), F32),
                        pltpu.VMEM((2 * tq, tk), BF16),
                        pltpu.VMEM((2 * tq, 1), F32),
                        pltpu.VMEM((2 * tq, 1), F32)],
        compiler_params=_params(("arbitrary", "arbitrary")),
        name="attn_b",
    )(lam_p, subln_g, qd, kd, vd)


HALF = D_MODEL // 2
_HI_MASK = -65536


def _pack_pair(lo, hi):
    lo_bits = lax.bitcast_convert_type(lo.astype(BF16).astype(F32), jnp.int32)
    hi_bits = lax.bitcast_convert_type(hi.astype(BF16).astype(F32), jnp.int32)
    return lax.shift_right_logical(lo_bits, 16) | hi_bits


def _unpack_pair(w):
    lo = lax.bitcast_convert_type(lax.shift_left(w, 16), F32)
    hi = lax.bitcast_convert_type(w & _HI_MASK, F32)
    return lo, hi


def _layer_norm(y, g, b):
    mu = jnp.mean(y, axis=-1, keepdims=True)
    d = y - mu
    var = jnp.mean(d * d, axis=-1, keepdims=True)
    return d * lax.rsqrt(var + LN_EPS) * g + b


def _first_argmax(vals, rowf, big):
    m = jnp.max(vals, axis=0, keepdims=True)
    idx = jnp.min(jnp.where(vals == m, rowf, big), axis=0, keepdims=True)
    return m, idx


def _post_attn_kernel(x_ref, oa_ref, ob_ref, wgate_ref, woa_ref, wob_ref, wout_ref, g1_ref, b1_ref,
                      rwt_ref, rb_ref, tri_ref, cnt_in_ref,
                      x1_ref, x1f_ref, eidx_ref, wts_ref, rank_ref, cnt_ref, cnt_scr):
    i = pl.program_id(0)
    tm = x_ref.shape[0]

    @pl.when(i == 0)
    def _():
        cnt_scr[...] = cnt_in_ref[...]

    x = x_ref[...]
    gates = jax.nn.sigmoid(jnp.dot(x.astype(BF16), wgate_ref[...], preferred_element_type=F32))
    a = jnp.dot(oa_ref[...], woa_ref[...], preferred_element_type=F32)
    b = jnp.dot(ob_ref[...], wob_ref[...], preferred_element_type=F32)
    merged = gates[:, :D_MODEL] * a + gates[:, D_MODEL:] * b
    mix = jnp.dot(merged.astype(BF16), wout_ref[...], preferred_element_type=F32)
    x1 = _layer_norm(DN_ALPHA * x + mix, g1_ref[...], b1_ref[...])
    x1p = _pack_pair(x1[:, :HALF], x1[:, HALF:])
    for piece in range(SC_SPLIT):
        x1_ref[piece] = x1p[:, piece * PIECE:(piece + 1) * PIECE]
    x1b = x1.astype(BF16)
    x1f_ref[...] = x1

    logits = lax.dot_general(rwt_ref[...], x1b, _NT, preferred_element_type=F32)
    scores = jax.nn.sigmoid(logits)
    choice = scores + rb_ref[...]
    ninf = -jnp.inf
    grow = lax.broadcasted_iota(jnp.int32, (GROUP_SIZE, tm), 0).astype(F32)
    gscore = []
    for g in range(N_GROUPS):
        blk = choice[GROUP_SIZE * g:GROUP_SIZE * (g + 1)]
        m1, i1 = _first_argmax(blk, grow, float(GROUP_SIZE))
        m2 = jnp.max(jnp.where(grow == i1, ninf, blk), axis=0, keepdims=True)
        gscore.append(m1 + m2)
    selected = [jnp.zeros((1, tm), F32) for _ in range(N_GROUPS)]
    work = list(gscore)
    for _ in range(TOPK_GROUPS):
        best = work[0]
        for g in range(1, N_GROUPS):
            best = jnp.maximum(best, work[g])
        taken = jnp.zeros((1, tm), F32)
        for g in range(N_GROUPS):
            hit = jnp.where((work[g] == best) & (taken == 0.0), 1.0, 0.0)
            taken = jnp.maximum(taken, hit)
            selected[g] = jnp.maximum(selected[g], hit)
            work[g] = jnp.where(hit > 0.0, ninf, work[g])
    masked = jnp.concatenate(
        [jnp.where(selected[g] > 0.0, choice[GROUP_SIZE * g:GROUP_SIZE * (g + 1)], ninf) for g in range(N_GROUPS)],
        axis=0)

    rowf = lax.broadcasted_iota(jnp.int32, (N_EXPERTS, tm), 0).astype(F32)
    hits, idxs, ws = [], [], []
    for _ in range(TOP_K):
        _, idx = _first_argmax(masked, rowf, float(N_EXPERTS))
        hit = rowf == idx
        hits.append(hit)
        idxs.append(idx)
        ws.append(jnp.sum(jnp.where(hit, scores, 0.0), axis=0, keepdims=True))
        masked = jnp.where(hit, ninf, masked)
    wsum = ws[0]
    for k in range(1, TOP_K):
        wsum = wsum + ws[k]

    member = hits[0]
    for k in range(1, TOP_K):
        member = member | hits[k]
    member_f = jnp.where(member, 1.0, 0.0)
    before = jnp.dot(member_f.astype(BF16), tri_ref[...], preferred_element_type=F32) + cnt_scr[...]
    for k in range(TOP_K):
        eidx_ref[k:k + 1, :] = idxs[k].astype(jnp.int32)
        wts_ref[k:k + 1, :] = ws[k] / wsum * ROUTED_SCALE
        rank_ref[k:k + 1, :] = jnp.sum(jnp.where(hits[k], before, 0.0), axis=0, keepdims=True).astype(jnp.int32)
    cnt_scr[...] = cnt_scr[...] + jnp.sum(member_f, axis=1, keepdims=True)
    cnt_ref[...] = jnp.broadcast_to(cnt_scr[...], cnt_ref.shape).astype(jnp.int32)


def _post_attn(x2, oa, ob, wgate, woa, wob, wout, g1, b1, rwt, rb, cnt_in, tm):
    t = x2.shape[0]
    tri = jnp.triu(jnp.ones((tm, tm), F32), k=1).astype(BF16)
    row = lambda i: (i, 0)
    col = lambda i: (0, i)
    full = lambda i: (0, 0)
    wspec = lambda arr: pl.BlockSpec(arr.shape, full)
    weights = (wgate, woa, wob, wout, g1, b1, rwt, rb, tri, cnt_in)
    return pl.pallas_call(
        _post_attn_kernel,
        grid=(t // tm,),
        in_specs=[pl.BlockSpec((tm, D_MODEL), row), pl.BlockSpec((tm, 512), row), pl.BlockSpec((tm, 512), row)]
                 + [wspec(w) for w in weights],
        out_specs=[pl.BlockSpec((SC_SPLIT, tm, PIECE), lambda i: (0, i, 0)), pl.BlockSpec((tm, D_MODEL), row),
                   pl.BlockSpec((TOP_K, tm), col), pl.BlockSpec((TOP_K, tm), col), pl.BlockSpec((TOP_K, tm), col),
                   pl.BlockSpec((N_EXPERTS, LANES), full)],
        out_shape=[jax.ShapeDtypeStruct((SC_SPLIT, t, PIECE), jnp.int32), jax.ShapeDtypeStruct((t, D_MODEL), F32),
                   jax.ShapeDtypeStruct((TOP_K, t), jnp.int32), jax.ShapeDtypeStruct((TOP_K, t), F32),
                   jax.ShapeDtypeStruct((TOP_K, t), jnp.int32), jax.ShapeDtypeStruct((N_EXPERTS, LANES), jnp.int32)],
        scratch_shapes=[pltpu.VMEM((N_EXPERTS, 1), F32)],
        compiler_params=_params(("arbitrary",)),
        name="post_attn",
    )(x2, oa, ob, *weights)


def _dest_kernel(eidx_ref, rank_ref, pstart_ref, dest_ref):
    tm = eidx_ref.shape[1]
    rows = lax.broadcasted_iota(jnp.int32, (N_EXPERTS, tm), 0)
    pstart = pstart_ref[...]
    for k in range(TOP_K):
        hit = rows == eidx_ref[k:k + 1, :]
        start = jnp.sum(jnp.where(hit, pstart, 0).astype(F32), axis=0, keepdims=True).astype(jnp.int32)
        dest_ref[k:k + 1, :] = start + rank_ref[k:k + 1, :]


def _dest(eidx, rank, pstart, tm):
    t = eidx.shape[1]
    return pl.pallas_call(
        _dest_kernel,
        grid=(t // tm,),
        in_specs=[pl.BlockSpec((TOP_K, tm), lambda i: (0, i)), pl.BlockSpec((TOP_K, tm), lambda i: (0, i)),
                  pl.BlockSpec((N_EXPERTS, 1), lambda i: (0, 0))],
        out_specs=pl.BlockSpec((TOP_K, tm), lambda i: (0, i)),
        out_shape=jax.ShapeDtypeStruct((TOP_K, t), jnp.int32),
        compiler_params=_params(("arbitrary",)),
        name="dest",
    )(eidx, rank, pstart)


def _sc_mesh():
    return plsc.VectorSubcoreMesh(core_axis_name="core", subcore_axis_name="subcore",
                                  num_cores=SC_CORES, num_subcores=SC_SUBCORES)


def _sc_scatter_rows(xs_groups, dests, n_rows):
    n_groups = len(xs_groups)

    @functools.partial(pl.kernel, out_type=jax.ShapeDtypeStruct((SC_SPLIT, n_rows, PIECE), jnp.int32),
                       mesh=_sc_mesh(), scratch_types=[pltpu.SemaphoreType.DMA])
    def scatter(*refs):
        out_hbm, sem = refs[2 * n_groups], refs[2 * n_groups + 1]
        for g in range(n_groups):
            x_hbm, idx_hbm = refs[2 * g], refs[2 * g + 1]
            for piece in range(SC_SPLIT):
                def window(x_vmem, idx_vmem, piece=piece):
                    copies = [pltpu.async_copy(x_vmem, out_hbm.at[piece].at[idx_vmem.at[k]], sem)
                              for k in range(TOP_K)]
                    for copy in copies:
                        copy.wait()

                pltpu.emit_pipeline(
                    window,
                    grid=(x_hbm.shape[1] // SC_WINDOW,),
                    in_specs=[pl.BlockSpec((SC_WINDOW, PIECE), lambda i: (i, 0)),
                              pl.BlockSpec((TOP_K, SC_WINDOW), lambda i: (0, i))],
                    out_specs=[],
                    core_axis_name=("core", "subcore"),
                    dimension_semantics=(pltpu.PARALLEL,),
                )(x_hbm.at[piece], idx_hbm)

    args = []
    for x, d in zip(xs_groups, dests):
        args += [x, d]
    return scatter(*args)


def _experts_kernel(cb_ref, cnt_ref, xs_ref, wg_ref, wu_ref, wd_ref, ys_ref,
                    wg_scr, wu_scr, wd_scr, xbuf, ybuf, sem_in, sem_out):
    e = pl.program_id(0)
    g0, g1, total = cb_ref[e], cb_ref[e + 1], cb_ref[N_EXPERTS]

    def rows(g):
        return pl.ds(pl.multiple_of(g * EXPERT_ROWS, EXPERT_ROWS), EXPERT_ROWS)

    def x_copy(g):
        s = g % EXPERT_SLOTS
        return pltpu.make_async_copy(xs_ref.at[:, rows(g), :], xbuf.at[s], sem_in.at[s])

    def y_copy(g):
        s = g % EXPERT_SLOTS
        return pltpu.make_async_copy(ybuf.at[s], ys_ref.at[:, rows(g), :], sem_out.at[s])

    @pl.when(e == 0)
    def _():
        for g in range(EXPERT_SLOTS - 1):
            pl.when(g < total)(lambda g=g: x_copy(g).start())

    @pl.when(g1 > g0)
    def _():
        wg_scr[...] = wg_ref[0].astype(BF16)
        wu_scr[...] = wu_ref[0].astype(BF16)
        wd_scr[...] = wd_ref[0].astype(BF16)

    def block(g, carry):
        slot = g % EXPERT_SLOTS

        @pl.when(g + EXPERT_SLOTS - 1 < total)
        def _():
            x_copy(g + EXPERT_SLOTS - 1).start()

        x_copy(g).wait()

        @pl.when(g >= EXPERT_SLOTS)
        def _():
            y_copy(g - EXPERT_SLOTS).wait()

        valid = lax.broadcasted_iota(jnp.int32, (EXPERT_ROWS, PIECE), 0) < cnt_ref[e] - (g - g0) * EXPERT_ROWS
        hg = hu = None
        for piece in range(SC_SPLIT):
            for part, col0 in zip(_unpack_pair(xbuf[slot, piece]), (piece * PIECE, HALF + piece * PIECE)):
                xb = jnp.where(valid, part, 0.0).astype(BF16)
                pg = jnp.dot(xb, wg_scr[col0:col0 + PIECE], preferred_element_type=F32)
                pu = jnp.dot(xb, wu_scr[col0:col0 + PIECE], preferred_element_type=F32)
                hg = pg if hg is None else hg + pg
                hu = pu if hu is None else hu + pu
        h = ((hg * jax.nn.sigmoid(hg)) * hu).astype(BF16)
        y = _pack_pair(jnp.dot(h, wd_scr[:, :HALF], preferred_element_type=F32),
                       jnp.dot(h, wd_scr[:, HALF:], preferred_element_type=F32))
        for piece in range(SC_SPLIT):
            ybuf[slot, piece] = y[:, piece * PIECE:(piece + 1) * PIECE]
        y_copy(g).start()
        return carry

    lax.fori_loop(g0, g1, block, 0)

    @pl.when(e == N_EXPERTS - 1)
    def _():
        for back in range(EXPERT_SLOTS, 0, -1):
            pl.when(total >= back)(lambda back=back: y_copy(total - back).wait())


def _experts(cum_blocks, counts, xs, w_gate, w_up, w_down):
    n_rows = xs.shape[1]
    wmap = lambda e, cb, cnt: (e, 0, 0)
    grid_spec = pltpu.PrefetchScalarGridSpec(
        num_scalar_prefetch=2,
        grid=(N_EXPERTS,),
        in_specs=[pl.BlockSpec(memory_space=pl.ANY),
                  pl.BlockSpec((1, D_MODEL, D_EXPERT), wmap),
                  pl.BlockSpec((1, D_MODEL, D_EXPERT), wmap),
                  pl.BlockSpec((1, D_EXPERT, D_MODEL), wmap)],
        out_specs=pl.BlockSpec(memory_space=pl.ANY),
        scratch_shapes=[pltpu.VMEM((D_MODEL, D_EXPERT), BF16), pltpu.VMEM((D_MODEL, D_EXPERT), BF16),
                        pltpu.VMEM((D_EXPERT, D_MODEL), BF16),
                        pltpu.VMEM((EXPERT_SLOTS, SC_SPLIT, EXPERT_ROWS, PIECE), jnp.int32),
                        pltpu.VMEM((EXPERT_SLOTS, SC_SPLIT, EXPERT_ROWS, PIECE), jnp.int32),
                        pltpu.SemaphoreType.DMA((EXPERT_SLOTS,)), pltpu.SemaphoreType.DMA((EXPERT_SLOTS,))])
    return pl.pallas_call(
        _experts_kernel,
        grid_spec=grid_spec,
        out_shape=jax.ShapeDtypeStruct((SC_SPLIT, n_rows, PIECE), jnp.int32),
        compiler_params=_params(("arbitrary",)),
        name="experts",
    )(cum_blocks, counts, xs, w_gate, w_up, w_down)


def _sc_gather_rows(table, idx):
    n, d = idx.shape[0], table.shape[2]

    @functools.partial(pl.kernel, out_type=jax.ShapeDtypeStruct((SC_SPLIT, n, d), table.dtype), mesh=_sc_mesh(),
                       scratch_types=[])
    def gather(table_hbm, idx_hbm, out_hbm):
        for piece in range(SC_SPLIT):
            def window(idx_vmem, out_vmem, piece=piece):
                pltpu.sync_copy(table_hbm.at[piece].at[idx_vmem.at[0]], out_vmem)

            pltpu.emit_pipeline(
                window,
                grid=(n // SC_WINDOW,),
                in_specs=[pl.BlockSpec((1, SC_WINDOW), lambda i: (0, i))],
                out_specs=[pl.BlockSpec((SC_WINDOW, d), lambda i: (i, 0))],
                core_axis_name=("core", "subcore"),
                dimension_semantics=(pltpu.PARALLEL,),
            )(idx_hbm, out_hbm.at[piece])

    return gather(table, idx.reshape(1, n))


def _combine_kernel(w_ref, x1_ref, g2_ref, b2_ref, wsg_ref, wsu_ref, wsd_ref, rows_ref, o_ref):
    w = w_ref[...]
    x1 = x1_ref[...]
    x1b = x1.astype(BF16)
    hg = jnp.dot(x1b, wsg_ref[...], preferred_element_type=F32)
    hu = jnp.dot(x1b, wsu_ref[...], preferred_element_type=F32)
    hs = (hg * jax.nn.sigmoid(hg)) * hu
    base = DN_ALPHA * x1 + jnp.dot(hs.astype(BF16), wsd_ref[...], preferred_element_type=F32)
    acc = [[base[:, half * HALF + p * PIECE:half * HALF + (p + 1) * PIECE] for p in range(SC_SPLIT)]
           for half in range(2)]
    for k in range(TOP_K):
        wk = w[:, k:k + 1]
        for p in range(SC_SPLIT):
            lo, hi = _unpack_pair(rows_ref[p, k])
            acc[0][p] = acc[0][p] + lo * wk
            acc[1][p] = acc[1][p] + hi * wk
    o_ref[...] = _layer_norm(jnp.concatenate(acc[0] + acc[1], axis=1), g2_ref[...], b2_ref[...])


def _combine(wts_tk, x1, g2, b2, wsg, wsu, wsd, rows, tm):
    t = x1.shape[0]
    full = lambda i: (0, 0)
    return pl.pallas_call(
        _combine_kernel,
        grid=(t // tm,),
        in_specs=[pl.BlockSpec((tm, TOP_K), lambda i: (i, 0)),
                  pl.BlockSpec((tm, D_MODEL), lambda i: (i, 0)),
                  pl.BlockSpec((1, D_MODEL), full),
                  pl.BlockSpec((1, D_MODEL), full),
                  pl.BlockSpec(wsg.shape, full), pl.BlockSpec(wsu.shape, full), pl.BlockSpec(wsd.shape, full),
                  pl.BlockSpec((SC_SPLIT, TOP_K, tm, PIECE), lambda i: (0, 0, i, 0))],
        out_specs=pl.BlockSpec((tm, D_MODEL), lambda i: (i, 0)),
        out_shape=jax.ShapeDtypeStruct((t, D_MODEL), F32),
        compiler_params=_params(("arbitrary",)),
        name="combine",
    )(wts_tk, x1, g2, b2, wsg, wsu, wsd, rows)


def _tile(n, pref):
    t = pref
    while n % t:
        t //= 2
    return t


def _token_mixers(x, layer_idx, wp, cnt_in):
    bsz, seq, _ = x.shape
    t = bsz * seq
    x2 = x.reshape(t, D_MODEL)
    lam_init = 0.8 - 0.6 * math.exp(-0.3 * layer_idx)

    cos_t, sin_t = _rope_tables(seq)
    qa, ka, va, qd, kd, vd = _in_proj(x2, wp["w_qkv"], cos_t, sin_t, seq, _tile(seq, 512))
    oa = _attn_a(qa, ka, va, wp["sink"], bsz, seq, _tile(seq, 512))
    ob = _attn_b(qd, kd, vd, wp["lam_p"], wp["subln_g"], bsz, seq, _tile(seq, 512), _tile(seq, 1024), lam_init)
    return _post_attn(
        x2, oa, ob, wp["w_gates"], wp["w_o_a"], wp["w_o_b"], wp["w_out"], wp["ln1_g"], wp["ln1_b"],
        wp["router_wt"], wp["router_bias"], cnt_in, _tile(t, 512))


def _encoder_layer(xs_in, layer_idx, wp):
    cnt_in = jnp.zeros((N_EXPERTS, 1), F32)
    routed = []
    for x in xs_in:
        x1p, x1f, eidx, wts, rank, cnt = _token_mixers(x, layer_idx, wp, cnt_in)
        cnt_in = cnt[:, :1].astype(F32)
        routed.append((x1p, x1f, eidx, wts, rank))

    n_assign = sum(x.shape[0] * x.shape[1] for x in xs_in) * TOP_K
    n_blk = -(-(n_assign + N_EXPERTS * (EXPERT_ROWS - 1)) // EXPERT_ROWS)
    counts = cnt[:, 0]
    padded = ((counts + EXPERT_ROWS - 1) // EXPERT_ROWS) * EXPERT_ROWS
    pend = jnp.cumsum(padded)
    pstart = pend - padded
    cum_blocks = jnp.concatenate([jnp.zeros((1,), jnp.int32), (pend // EXPERT_ROWS).astype(jnp.int32)])
    pstart_col = pstart.astype(jnp.int32)[:, None]

    dests = [_dest(eidx, rank, pstart_col, _tile(eidx.shape[1], 512)) for _, _, eidx, _, rank in routed]
    xs = _sc_scatter_rows([r[0] for r in routed], dests, n_blk * EXPERT_ROWS)
    ys = _experts(cum_blocks, counts.astype(jnp.int32), xs, wp["w_gate"], wp["w_up"], wp["w_down"])
    outs = []
    for x, dest, (x1p, x1f, eidx, wts, rank) in zip(xs_in, dests, routed):
        t = x1f.shape[0]
        rows = _sc_gather_rows(ys, dest.reshape(-1)).reshape(SC_SPLIT, TOP_K, t, PIECE)
        y = _combine(wts.T, x1f, wp["ln2_g"], wp["ln2_b"], wp["ws_gate"], wp["ws_up"], wp["ws_down"], rows,
                     _tile(t, 256))
        outs.append(y.reshape(x.shape))
    return outs


def kernel(x_prompt, x_sample, w_in, attn_sink, lambda_q1, lambda_k1, lambda_q2, lambda_k2, subln_g, w_o_a, w_o_b, w_out, ln1_g, ln1_b, router_w, router_bias, w_gate, w_up, w_down, ws_gate, ws_up, ws_down, ln2_g, ln2_b):
    y_prompt, y_sample = x_prompt, x_sample
    for l in range(DEPTH):
        w_qkv, w_gates = _prep_w_in(w_in[l])
        wp = {
            "w_qkv": w_qkv, "w_gates": w_gates,
            "sink": attn_sink[l].astype(F32),
            "lam_p": jnp.stack([lambda_q1[l], lambda_k1[l], lambda_q2[l], lambda_k2[l]]).astype(F32),
            "subln_g": subln_g[l].astype(F32)[None, :],
            "w_o_a": w_o_a[l].astype(BF16), "w_o_b": w_o_b[l].astype(BF16), "w_out": w_out[l].astype(BF16),
            "ln1_g": ln1_g[l].astype(F32)[None, :], "ln1_b": ln1_b[l].astype(F32)[None, :],
            "router_wt": router_w[l].T.astype(BF16), "router_bias": router_bias[l].astype(F32)[:, None],
            "w_gate": w_gate[l], "w_up": w_up[l], "w_down": w_down[l],
            "ws_gate": ws_gate[l].astype(BF16), "ws_up": ws_up[l].astype(BF16), "ws_down": ws_down[l].astype(BF16),
            "ln2_g": ln2_g[l].astype(F32)[None, :], "ln2_b": ln2_b[l].astype(F32)[None, :],
        }
        y_prompt, y_sample = _encoder_layer((y_prompt, y_sample), l, wp)
    return (y_prompt, y_sample)
```

```python
import functools
import math

import jax
import jax.numpy as jnp
from jax import lax
from jax.experimental import pallas as pl
from jax.experimental.pallas import tpu as pltpu
from jax.experimental.pallas import tpu_sc as plsc

D_MODEL = 1024
HEAD_DIM = 64
ROPE_THETA = 10000.0
BLOCK = 128
A_Q_HEADS = 8
A_KV_HEADS = 2
WINDOW = 128
B_HEADS = 4
N_EXPERTS = 256
TOP_K = 8
N_GROUPS = 8
TOPK_GROUPS = 4
GROUP_SIZE = N_EXPERTS // N_GROUPS
D_EXPERT = 256
ROUTED_SCALE = 2.5
EXPERT_ROWS = 512
EXPERT_SLOTS = 4
DEPTH = 1
DN_ALPHA = (2 * DEPTH) ** 0.25
LN_EPS = 1e-5
RMS_EPS = 1e-5
NEG = -1e30
LOG2E = math.log2(math.e)
SOFTMAX_ROWS = 16

LANES = 128
SC_CORES = 2
SC_SUBCORES = 16
SC_WINDOW = 128
SC_SPLIT = 2
PIECE = D_MODEL // 2 // SC_SPLIT
VMEM_LIMIT_BYTES = 56 * 1024 * 1024

F32 = jnp.float32
BF16 = jnp.bfloat16
_NT = (((1,), (1,)), ((), ()))


def _params(sem, vmem=VMEM_LIMIT_BYTES):
    return pltpu.CompilerParams(dimension_semantics=sem, vmem_limit_bytes=vmem)


_C_QA, _C_KA, _C_VA, _C_QD, _C_KD, _C_VD, _C_END = 0, 512, 768, 1024, 1536, 2048, 2560


def _prep_w_in(w_in):
    cuts = [0, 512, 640, 768, 1280, 1792, 2304, 4352]
    qa, ka, va, qd, kd, vd, gates = [w_in[:, cuts[i]:cuts[i + 1]] for i in range(7)]
    dup = lambda w: jnp.concatenate([w[:, :64], w[:, :64], w[:, 64:], w[:, 64:]], axis=1)
    return jnp.concatenate([qa, dup(ka), dup(va), qd, kd, vd], axis=1).astype(BF16), gates.astype(BF16)


def _rope_tables(s):
    half = HEAD_DIM // 2
    inv = 1.0 / (ROPE_THETA ** (jnp.arange(half, dtype=F32) / half))
    ang = jnp.arange(s, dtype=F32)[:, None] * inv[None, :]
    cos, sin = jnp.cos(ang), jnp.sin(ang)
    return (jnp.concatenate([cos, cos, cos, cos], axis=1),
            jnp.concatenate([-sin, sin, -sin, sin], axis=1))


def _in_proj_kernel(x_ref, w_ref, cos_ref, sin_ref, qa_ref, ka_ref, va_ref, qd_ref, kd_ref, vd_ref):
    xb = x_ref[...].astype(BF16)
    cos = cos_ref[...]
    sin = sin_ref[...]
    lane = lax.broadcasted_iota(jnp.int32, cos.shape, 1)
    first_half = (lane & (HEAD_DIM // 2)) == 0

    def proj(c0, c1):
        return jnp.dot(xb, w_ref[:, c0:c1], preferred_element_type=F32)

    def rope_store(u, out_ref, scale):
        for j in range(u.shape[1] // LANES):
            uj = u[:, LANES * j:LANES * (j + 1)]
            rot = jnp.where(first_half, pltpu.roll(uj, LANES - 32, 1), pltpu.roll(uj, 32, 1))
            r = uj * cos + rot * sin
            if scale != 1.0:
                r = r * scale
            out_ref[:, LANES * j:LANES * (j + 1)] = r.astype(out_ref.dtype)

    scale = HEAD_DIM ** -0.5
    rope_store(proj(_C_QA, _C_KA), qa_ref, scale)
    rope_store(proj(_C_KA, _C_VA), ka_ref, 1.0)
    va_ref[...] = proj(_C_VA, _C_QD).astype(va_ref.dtype)
    rope_store(proj(_C_QD, _C_KD), qd_ref, scale * LOG2E)
    rope_store(proj(_C_KD, _C_VD), kd_ref, 1.0)
    vd_ref[...] = proj(_C_VD, _C_END).astype(vd_ref.dtype)


def _in_proj(x2, w_perm, cos_t, sin_t, seq, tm):
    t = x2.shape[0]
    nseq = seq // tm
    row = lambda i: (i, 0)
    widths = (512, 256, 256, 512, 512, 512)
    return pl.pallas_call(
        _in_proj_kernel,
        grid=(t // tm,),
        in_specs=[pl.BlockSpec((tm, D_MODEL), row),
                  pl.BlockSpec((D_MODEL, _C_END), lambda i: (0, 0)),
                  pl.BlockSpec((tm, LANES), lambda i: (i % nseq, 0)),
                  pl.BlockSpec((tm, LANES), lambda i: (i % nseq, 0))],
        out_specs=[pl.BlockSpec((tm, w), row) for w in widths],
        out_shape=[jax.ShapeDtypeStruct((t, w), BF16) for w in widths],
        compiler_params=_params(("arbitrary",)),
        name="in_proj",
    )(x2, w_perm, cos_t, sin_t)


def _attn_a_kernel(sink_ref, q_ref, kp_ref, kc_ref, kn_ref, vp_ref, vc_ref, vn_ref, o_ref, *, seq, tq):
    i = pl.program_id(1)
    kk = jnp.concatenate([kp_ref[...], kc_ref[...], kn_ref[...]], axis=0)
    vv = jnp.concatenate([vp_ref[...], vc_ref[...], vn_ref[...]], axis=0)
    lane = lax.broadcasted_iota(jnp.int32, (kk.shape[0], LANES), 1)
    lo = lane < HEAD_DIM
    zero = jnp.zeros((kk.shape[0], LANES), BF16)
    k_lo = [jnp.where(lo, kk[:, LANES * h:LANES * (h + 1)], zero) for h in range(A_KV_HEADS)]
    k_hi = [jnp.where(lo, zero, kk[:, LANES * h:LANES * (h + 1)]) for h in range(A_KV_HEADS)]
    v_lo = [jnp.where(lo, vv[:, LANES * h:LANES * (h + 1)], zero) for h in range(A_KV_HEADS)]
    v_hi = [jnp.where(lo, zero, vv[:, LANES * h:LANES * (h + 1)]) for h in range(A_KV_HEADS)]

    qi = lax.broadcasted_iota(jnp.int32, (BLOCK, 3 * BLOCK), 0)
    kj = lax.broadcasted_iota(jnp.int32, (BLOCK, 3 * BLOCK), 1)
    band = jnp.abs(kj - BLOCK - qi) <= WINDOW
    head_of_row = lax.broadcasted_iota(jnp.int32, (A_Q_HEADS * BLOCK, 1), 0) // BLOCK
    snk = jnp.zeros((A_Q_HEADS * BLOCK, 1), F32)
    for head in range(A_Q_HEADS):
        snk = jnp.where(head_of_row == head, sink_ref[head], snk)
    for j in range(tq // BLOCK):
        kpos = i * tq + (j - 1) * BLOCK + kj
        mask = band & (kpos >= 0) & (kpos < seq)
        r0, r1 = j * BLOCK, (j + 3) * BLOCK
        pieces = []
        for c in range(A_Q_HEADS // 2):
            q2 = q_ref[j * BLOCK:(j + 1) * BLOCK, LANES * c:LANES * (c + 1)]
            for kx in (k_lo[c // 2], k_hi[c // 2]):
                s = lax.dot_general(q2, kx[r0:r1], _NT, preferred_element_type=F32)
                pieces.append(jnp.where(mask, s, NEG))
        s = jnp.concatenate(pieces, axis=0)
        m = jnp.maximum(jnp.max(s, axis=-1, keepdims=True), snk)
        p = jnp.exp(s - m)
        inv = 1.0 / (jnp.sum(p, axis=-1, keepdims=True) + jnp.exp(snk - m))
        pb = p.astype(BF16)
        for c in range(A_Q_HEADS // 2):
            out = None
            for half, vx in enumerate((v_lo[c // 2], v_hi[c // 2])):
                rows = slice((2 * c + half) * BLOCK, (2 * c + half + 1) * BLOCK)
                pv = jnp.dot(pb[rows], vx[r0:r1], preferred_element_type=F32) * inv[rows]
                out = pv if out is None else out + pv
            o_ref[j * BLOCK:(j + 1) * BLOCK, LANES * c:LANES * (c + 1)] = out.astype(o_ref.dtype)


def _attn_a(qa, ka, va, sink, bsz, seq, tq):
    t = qa.shape[0]
    nq = seq // tq
    nb = seq // BLOCK
    r = tq // BLOCK
    cur = lambda b, i: (b * nq + i, 0)
    prev = lambda b, i: (b * nb + jnp.maximum(i * r - 1, 0), 0)
    nxt = lambda b, i: (b * nb + jnp.minimum((i + 1) * r, nb - 1), 0)
    kv_specs = [pl.BlockSpec((BLOCK, 256), prev), pl.BlockSpec((tq, 256), cur), pl.BlockSpec((BLOCK, 256), nxt)]
    return pl.pallas_call(
        functools.partial(_attn_a_kernel, seq=seq, tq=tq),
        grid=(bsz, nq),
        in_specs=[pl.BlockSpec(memory_space=pltpu.SMEM), pl.BlockSpec((tq, 512), cur)] + kv_specs + kv_specs,
        out_specs=pl.BlockSpec((tq, 512), cur),
        out_shape=jax.ShapeDtypeStruct((t, 512), BF16),
        compiler_params=_params(("arbitrary", "arbitrary")),
        name="attn_a",
    )(sink, qa, ka, ka, ka, va, va, va)


def _attn_b_kernel(lam_ref, g_ref, q_ref, k_ref, v_ref, o_ref, acc_ref, s_ref, p_ref, m_ref, alpha_ref,
                   *, tq, tk, lam_init):
    seq = q_ref.shape[0]
    n_q, n_chunks = seq // tq, seq // tk
    lo = lax.broadcasted_iota(jnp.int32, (tq, LANES), 1) < HEAD_DIM
    zero = jnp.zeros((tq, LANES), BF16)
    ones_col = jnp.where(lax.broadcasted_iota(jnp.int32, (tk, LANES), 1) == 0, 1.0, 0.0).astype(BF16)

    lp = lam_ref[...]
    lam = (jnp.exp(jnp.sum(lp[0:1] * lp[1:2], axis=-1, keepdims=True))
           - jnp.exp(jnp.sum(lp[2:3] * lp[3:4], axis=-1, keepdims=True)) + lam_init)

    def q_tile(qi):
        q = q_ref[pl.ds(pl.multiple_of(qi * tq, tq), tq), :]
        return jnp.concatenate([jnp.where(lo, q, zero), jnp.where(lo, zero, q)], axis=0)

    def scores(q2, kc, slot):
        s_ref[slot] = lax.dot_general(q2, k_ref[kc * tk:(kc + 1) * tk, :], _NT, preferred_element_type=F32)

    def consume(kc, slot):
        v_aug = jnp.concatenate([v_ref[kc * tk:(kc + 1) * tk, :], ones_col], axis=1)
        for r in range(2 * tq // SOFTMAX_ROWS):
            rows = slice(r * SOFTMAX_ROWS, (r + 1) * SOFTMAX_ROWS)
            s = s_ref[slot, rows, :]
            m_old = m_ref[rows, :]
            m_new = jnp.maximum(m_old, jnp.max(s, axis=-1, keepdims=True))
            m_ref[rows, :] = m_new
            alpha_ref[rows, :] = jnp.exp2(m_old - m_new)
            p_ref[rows, :] = jnp.exp2(s - m_new).astype(BF16)
        acc_ref[...] = alpha_ref[...] * acc_ref[...] + jnp.dot(p_ref[...], v_aug, preferred_element_type=F32)

    across_tiles = n_chunks % 2 == 0

    def tile(qi, carry):
        q = q_tile(qi)
        acc_ref[...] = jnp.zeros_like(acc_ref)
        m_ref[...] = jnp.full(m_ref.shape, -jnp.inf, F32)
        if not across_tiles:
            scores(q, 0, 0)
        for kc in range(n_chunks):
            slot = kc % 2
            if kc + 1 < n_chunks:
                scores(q, kc + 1, 1 - slot)
            elif across_tiles:
                scores(q_tile(jnp.minimum(qi + 1, n_q - 1)), 0, 1 - slot)
            consume(kc, slot)
        acc1, acc2 = acc_ref[:tq], acc_ref[tq:]
        l1, l2 = acc1[:, LANES:LANES + 1], acc2[:, LANES:LANES + 1]
        o = acc1[:, :LANES] * (1.0 / l1) - lam * (acc2[:, :LANES] * (1.0 / l2))
        o = o * lax.rsqrt(jnp.mean(o * o, axis=-1, keepdims=True) + RMS_EPS)
        o = o * g_ref[...] * (1.0 - lam_init)
        o_ref[pl.ds(pl.multiple_of(qi * tq, tq), tq), :] = o.astype(o_ref.dtype)
        return carry

    if across_tiles:
        scores(q_tile(0), 0, 0)
    lax.fori_loop(0, n_q, tile, 0)


def _attn_b(qd, kd, vd, lam_p, subln_g, bsz, seq, tq, tk, lam_init):
    t = qd.shape[0]
    seq_block = pl.BlockSpec((seq, LANES), lambda b, h: (b, h))
    return pl.pallas_call(
        functools.partial(_attn_b_kernel, tq=tq, tk=tk, lam_init=lam_init),
        grid=(bsz, B_HEADS),
        in_specs=[pl.BlockSpec((4, HEAD_DIM), lambda b, h: (0, 0)),
                  pl.BlockSpec((1, LANES), lambda b, h: (0, 0)),
                  seq_block, seq_block, seq_block],
        out_specs=seq_block,
        out_shape=jax.ShapeDtypeStruct((t, 512), BF16),
        scratch_shapes=[pltpu.VMEM((2 * tq, 2 * LANES), F32),
                        pltpu.VMEM((2, 2 * tq, tk), F32),
                        pltpu.VMEM((2 * tq, tk), BF16),
                        pltpu.VMEM((2 * tq, 1), F32),
                        pltpu.VMEM((2 * tq, 1), F32)],
        compiler_params=_params(("arbitrary", "arbitrary")),
        name="attn_b",
    )(lam_p, subln_g, qd, kd, vd)


HALF = D_MODEL // 2
_HI_MASK = -65536


def _pack_pair(lo, hi):
    lo_bits = lax.bitcast_convert_type(lo.astype(BF16).astype(F32), jnp.int32)
    hi_bits = lax.bitcast_convert_type(hi.astype(BF16).astype(F32), jnp.int32)
    return lax.shift_right_logical(lo_bits, 16) | hi_bits


def _unpack_pair(w):
    lo = lax.bitcast_convert_type(lax.shift_left(w, 16), F32)
    hi = lax.bitcast_convert_type(w & _HI_MASK, F32)
    return lo, hi


def _layer_norm(y, g, b):
    mu = jnp.mean(y, axis=-1, keepdims=True)
    d = y - mu
    var = jnp.mean(d * d, axis=-1, keepdims=True)
    return d * lax.rsqrt(var + LN_EPS) * g + b


def _first_argmax(vals, rowf, big):
    m = jnp.max(vals, axis=0, keepdims=True)
    idx = jnp.min(jnp.where(vals == m, rowf, big), axis=0, keepdims=True)
    return m, idx


def _post_attn_kernel(x_ref, oa_ref, ob_ref, wgate_ref, woa_ref, wob_ref, wout_ref, g1_ref, b1_ref,
                      rwt_ref, rb_ref, tri_ref, cnt_in_ref,
                      x1_ref, x1f_ref, eidx_ref, wts_ref, rank_ref, cnt_ref, cnt_scr):
    i = pl.program_id(0)
    tm = x_ref.shape[0]

    @pl.when(i == 0)
    def _():
        cnt_scr[...] = cnt_in_ref[...]

    x = x_ref[...]
    gates = jax.nn.sigmoid(jnp.dot(x.astype(BF16), wgate_ref[...], preferred_element_type=F32))
    a = jnp.dot(oa_ref[...], woa_ref[...], preferred_element_type=F32)
    b = jnp.dot(ob_ref[...], wob_ref[...], preferred_element_type=F32)
    merged = gates[:, :D_MODEL] * a + gates[:, D_MODEL:] * b
    mix = jnp.dot(merged.astype(BF16), wout_ref[...], preferred_element_type=F32)
    x1 = _layer_norm(DN_ALPHA * x + mix, g1_ref[...], b1_ref[...])
    x1p = _pack_pair(x1[:, :HALF], x1[:, HALF:])
    for piece in range(SC_SPLIT):
        x1_ref[piece] = x1p[:, piece * PIECE:(piece + 1) * PIECE]
    x1b = x1.astype(BF16)
    x1f_ref[...] = x1

    logits = lax.dot_general(rwt_ref[...], x1b, _NT, preferred_element_type=F32)
    scores = jax.nn.sigmoid(logits)
    choice = scores + rb_ref[...]
    ninf = -jnp.inf
    grow = lax.broadcasted_iota(jnp.int32, (GROUP_SIZE, tm), 0).astype(F32)
    gscore = []
    for g in range(N_GROUPS):
        blk = choice[GROUP_SIZE * g:GROUP_SIZE * (g + 1)]
        m1, i1 = _first_argmax(blk, grow, float(GROUP_SIZE))
        m2 = jnp.max(jnp.where(grow == i1, ninf, blk), axis=0, keepdims=True)
        gscore.append(m1 + m2)
    selected = [jnp.zeros((1, tm), F32) for _ in range(N_GROUPS)]
    work = list(gscore)
    for _ in range(TOPK_GROUPS):
        best = work[0]
        for g in range(1, N_GROUPS):
            best = jnp.maximum(best, work[g])
        taken = jnp.zeros((1, tm), F32)
        for g in range(N_GROUPS):
            hit = jnp.where((work[g] == best) & (taken == 0.0), 1.0, 0.0)
            taken = jnp.maximum(taken, hit)
            selected[g] = jnp.maximum(selected[g], hit)
            work[g] = jnp.where(hit > 0.0, ninf, work[g])
    masked = jnp.concatenate(
        [jnp.where(selected[g] > 0.0, choice[GROUP_SIZE * g:GROUP_SIZE * (g + 1)], ninf) for g in range(N_GROUPS)],
        axis=0)

    rowf = lax.broadcasted_iota(jnp.int32, (N_EXPERTS, tm), 0).astype(F32)
    hits, idxs, ws = [], [], []
    for _ in range(TOP_K):
        _, idx = _first_argmax(masked, rowf, float(N_EXPERTS))
        hit = rowf == idx
        hits.append(hit)
        idxs.append(idx)
        ws.append(jnp.sum(jnp.where(hit, scores, 0.0), axis=0, keepdims=True))
        masked = jnp.where(hit, ninf, masked)
    wsum = ws[0]
    for k in range(1, TOP_K):
        wsum = wsum + ws[k]

    member = hits[0]
    for k in range(1, TOP_K):
        member = member | hits[k]
    member_f = jnp.where(member, 1.0, 0.0)
    before = jnp.dot(member_f.astype(BF16), tri_ref[...], preferred_element_type=F32) + cnt_scr[...]
    for k in range(TOP_K):
        eidx_ref[k:k + 1, :] = idxs[k].astype(jnp.int32)
        wts_ref[k:k + 1, :] = ws[k] / wsum * ROUTED_SCALE
        rank_ref[k:k + 1, :] = jnp.sum(jnp.where(hits[k], before, 0.0), axis=0, keepdims=True).astype(jnp.int32)
    cnt_scr[...] = cnt_scr[...] + jnp.sum(member_f, axis=1, keepdims=True)
    cnt_ref[...] = jnp.broadcast_to(cnt_scr[...], cnt_ref.shape).astype(jnp.int32)


def _post_attn(x2, oa, ob, wgate, woa, wob, wout, g1, b1, rwt, rb, cnt_in, tm):
    t = x2.shape[0]
    tri = jnp.triu(jnp.ones((tm, tm), F32), k=1).astype(BF16)
    row = lambda i: (i, 0)
    col = lambda i: (0, i)
    full = lambda i: (0, 0)
    wspec = lambda arr: pl.BlockSpec(arr.shape, full)
    weights = (wgate, woa, wob, wout, g1, b1, rwt, rb, tri, cnt_in)
    return pl.pallas_call(
        _post_attn_kernel,
        grid=(t // tm,),
        in_specs=[pl.BlockSpec((tm, D_MODEL), row), pl.BlockSpec((tm, 512), row), pl.BlockSpec((tm, 512), row)]
                 + [wspec(w) for w in weights],
        out_specs=[pl.BlockSpec((SC_SPLIT, tm, PIECE), lambda i: (0, i, 0)), pl.BlockSpec((tm, D_MODEL), row),
                   pl.BlockSpec((TOP_K, tm), col), pl.BlockSpec((TOP_K, tm), col), pl.BlockSpec((TOP_K, tm), col),
                   pl.BlockSpec((N_EXPERTS, LANES), full)],
        out_shape=[jax.ShapeDtypeStruct((SC_SPLIT, t, PIECE), jnp.int32), jax.ShapeDtypeStruct((t, D_MODEL), F32),
                   jax.ShapeDtypeStruct((TOP_K, t), jnp.int32), jax.ShapeDtypeStruct((TOP_K, t), F32),
                   jax.ShapeDtypeStruct((TOP_K, t), jnp.int32), jax.ShapeDtypeStruct((N_EXPERTS, LANES), jnp.int32)],
        scratch_shapes=[pltpu.VMEM((N_EXPERTS, 1), F32)],
        compiler_params=_params(("arbitrary",)),
        name="post_attn",
    )(x2, oa, ob, *weights)


def _dest_kernel(eidx_ref, rank_ref, pstart_ref, dest_ref):
    tm = eidx_ref.shape[1]
    rows = lax.broadcasted_iota(jnp.int32, (N_EXPERTS, tm), 0)
    pstart = pstart_ref[...]
    for k in range(TOP_K):
        hit = rows == eidx_ref[k:k + 1, :]
        start = jnp.sum(jnp.where(hit, pstart, 0).astype(F32), axis=0, keepdims=True).astype(jnp.int32)
        dest_ref[k:k + 1, :] = start + rank_ref[k:k + 1, :]


def _dest(eidx, rank, pstart, tm):
    t = eidx.shape[1]
    return pl.pallas_call(
        _dest_kernel,
        grid=(t // tm,),
        in_specs=[pl.BlockSpec((TOP_K, tm), lambda i: (0, i)), pl.BlockSpec((TOP_K, tm), lambda i: (0, i)),
                  pl.BlockSpec((N_EXPERTS, 1), lambda i: (0, 0))],
        out_specs=pl.BlockSpec((TOP_K, tm), lambda i: (0, i)),
        out_shape=jax.ShapeDtypeStruct((TOP_K, t), jnp.int32),
        compiler_params=_params(("arbitrary",)),
        name="dest",
    )(eidx, rank, pstart)


def _sc_mesh():
    return plsc.VectorSubcoreMesh(core_axis_name="core", subcore_axis_name="subcore",
                                  num_cores=SC_CORES, num_subcores=SC_SUBCORES)


def _sc_scatter_rows(xs_groups, dests, n_rows):
    n_groups = len(xs_groups)

    @functools.partial(pl.kernel, out_type=jax.ShapeDtypeStruct((SC_SPLIT, n_rows, PIECE), jnp.int32),
                       mesh=_sc_mesh(), scratch_types=[pltpu.SemaphoreType.DMA])
    def scatter(*refs):
        out_hbm, sem = refs[2 * n_groups], refs[2 * n_groups + 1]
        for g in range(n_groups):
            x_hbm, idx_hbm = refs[2 * g], refs[2 * g + 1]
            for piece in range(SC_SPLIT):
                def window(x_vmem, idx_vmem, piece=piece):
                    copies = [pltpu.async_copy(x_vmem, out_hbm.at[piece].at[idx_vmem.at[k]], sem)
                              for k in range(TOP_K)]
                    for copy in copies:
                        copy.wait()

                pltpu.emit_pipeline(
                    window,
                    grid=(x_hbm.shape[1] // SC_WINDOW,),
                    in_specs=[pl.BlockSpec((SC_WINDOW, PIECE), lambda i: (i, 0)),
                              pl.BlockSpec((TOP_K, SC_WINDOW), lambda i: (0, i))],
                    out_specs=[],
                    core_axis_name=("core", "subcore"),
                    dimension_semantics=(pltpu.PARALLEL,),
                )(x_hbm.at[piece], idx_hbm)

    args = []
    for x, d in zip(xs_groups, dests):
        args += [x, d]
    return scatter(*args)


def _experts_kernel(cb_ref, cnt_ref, xs_ref, wg_ref, wu_ref, wd_ref, ys_ref,
                    wg_scr, wu_scr, wd_scr, xbuf, ybuf, sem_in, sem_out):
    e = pl.program_id(0)
    g0, g1, total = cb_ref[e], cb_ref[e + 1], cb_ref[N_EXPERTS]

    def rows(g):
        return pl.ds(pl.multiple_of(g * EXPERT_ROWS, EXPERT_ROWS), EXPERT_ROWS)

    def x_copy(g):
        s = g % EXPERT_SLOTS
        return pltpu.make_async_copy(xs_ref.at[:, rows(g), :], xbuf.at[s], sem_in.at[s])

    def y_copy(g):
        s = g % EXPERT_SLOTS
        return pltpu.make_async_copy(ybuf.at[s], ys_ref.at[:, rows(g), :], sem_out.at[s])

    @pl.when(e == 0)
    def _():
        for g in range(EXPERT_SLOTS - 1):
            pl.when(g < total)(lambda g=g: x_copy(g).start())

    @pl.when(g1 > g0)
    def _():
        wg_scr[...] = wg_ref[0].astype(BF16)
        wu_scr[...] = wu_ref[0].astype(BF16)
        wd_scr[...] = wd_ref[0].astype(BF16)

    def block(g, carry):
        slot = g % EXPERT_SLOTS

        @pl.when(g + EXPERT_SLOTS - 1 < total)
        def _():
            x_copy(g + EXPERT_SLOTS - 1).start()

        x_copy(g).wait()

        @pl.when(g >= EXPERT_SLOTS)
        def _():
            y_copy(g - EXPERT_SLOTS).wait()

        valid = lax.broadcasted_iota(jnp.int32, (EXPERT_ROWS, PIECE), 0) < cnt_ref[e] - (g - g0) * EXPERT_ROWS
        hg = hu = None
        for piece in range(SC_SPLIT):
            for part, col0 in zip(_unpack_pair(xbuf[slot, piece]), (piece * PIECE, HALF + piece * PIECE)):
                xb = jnp.where(valid, part, 0.0).astype(BF16)
                pg = jnp.dot(xb, wg_scr[col0:col0 + PIECE], preferred_element_type=F32)
                pu = jnp.dot(xb, wu_scr[col0:col0 + PIECE], preferred_element_type=F32)
                hg = pg if hg is None else hg + pg
                hu = pu if hu is None else hu + pu
        h = ((hg * jax.nn.sigmoid(hg)) * hu).astype(BF16)
        y = _pack_pair(jnp.dot(h, wd_scr[:, :HALF], preferred_element_type=F32),
                       jnp.dot(h, wd_scr[:, HALF:], preferred_element_type=F32))
        for piece in range(SC_SPLIT):
            ybuf[slot, piece] = y[:, piece * PIECE:(piece + 1) * PIECE]
        y_copy(g).start()
        return carry

    lax.fori_loop(g0, g1, block, 0)

    @pl.when(e == N_EXPERTS - 1)
    def _():
        for back in range(EXPERT_SLOTS, 0, -1):
            pl.when(total >= back)(lambda back=back: y_copy(total - back).wait())


def _experts(cum_blocks, counts, xs, w_gate, w_up, w_down):
    n_rows = xs.shape[1]
    wmap = lambda e, cb, cnt: (e, 0, 0)
    grid_spec = pltpu.PrefetchScalarGridSpec(
        num_scalar_prefetch=2,
        grid=(N_EXPERTS,),
        in_specs=[pl.BlockSpec(memory_space=pl.ANY),
                  pl.BlockSpec((1, D_MODEL, D_EXPERT), wmap),
                  pl.BlockSpec((1, D_MODEL, D_EXPERT), wmap),
                  pl.BlockSpec((1, D_EXPERT, D_MODEL), wmap)],
        out_specs=pl.BlockSpec(memory_space=pl.ANY),
        scratch_shapes=[pltpu.VMEM((D_MODEL, D_EXPERT), BF16), pltpu.VMEM((D_MODEL, D_EXPERT), BF16),
                        pltpu.VMEM((D_EXPERT, D_MODEL), BF16),
                        pltpu.VMEM((EXPERT_SLOTS, SC_SPLIT, EXPERT_ROWS, PIECE), jnp.int32),
                        pltpu.VMEM((EXPERT_SLOTS, SC_SPLIT, EXPERT_ROWS, PIECE), jnp.int32),
                        pltpu.SemaphoreType.DMA((EXPERT_SLOTS,)), pltpu.SemaphoreType.DMA((EXPERT_SLOTS,))])
    return pl.pallas_call(
        _experts_kernel,
        grid_spec=grid_spec,
        out_shape=jax.ShapeDtypeStruct((SC_SPLIT, n_rows, PIECE), jnp.int32),
        compiler_params=_params(("arbitrary",)),
        name="experts",
    )(cum_blocks, counts, xs, w_gate, w_up, w_down)


def _sc_gather_rows(table, idx):
    n, d = idx.shape[0], table.shape[2]

    @functools.partial(pl.kernel, out_type=jax.ShapeDtypeStruct((SC_SPLIT, n, d), table.dtype), mesh=_sc_mesh(),
                       scratch_types=[])
    def gather(table_hbm, idx_hbm, out_hbm):
        for piece in range(SC_SPLIT):
            def window(idx_vmem, out_vmem, piece=piece):
                pltpu.sync_copy(table_hbm.at[piece].at[idx_vmem.at[0]], out_vmem)

            pltpu.emit_pipeline(
                window,
                grid=(n // SC_WINDOW,),
                in_specs=[pl.BlockSpec((1, SC_WINDOW), lambda i: (0, i))],
                out_specs=[pl.BlockSpec((SC_WINDOW, d), lambda i: (i, 0))],
                core_axis_name=("core", "subcore"),
                dimension_semantics=(pltpu.PARALLEL,),
            )(idx_hbm, out_hbm.at[piece])

    return gather(table, idx.reshape(1, n))


def _combine_kernel(w_ref, x1_ref, g2_ref, b2_ref, wsg_ref, wsu_ref, wsd_ref, rows_ref, o_ref):
    w = w_ref[...]
    x1 = x1_ref[...]
    x1b = x1.astype(BF16)
    hg = jnp.dot(x1b, wsg_ref[...], preferred_element_type=F32)
    hu = jnp.dot(x1b, wsu_ref[...], preferred_element_type=F32)
    hs = (hg * jax.nn.sigmoid(hg)) * hu
    base = DN_ALPHA * x1 + jnp.dot(hs.astype(BF16), wsd_ref[...], preferred_element_type=F32)
    acc = [[base[:, half * HALF + p * PIECE:half * HALF + (p + 1) * PIECE] for p in range(SC_SPLIT)]
           for half in range(2)]
    for k in range(TOP_K):
        wk = w[:, k:k + 1]
        for p in range(SC_SPLIT):
            lo, hi = _unpack_pair(rows_ref[p, k])
            acc[0][p] = acc[0][p] + lo * wk
            acc[1][p] = acc[1][p] + hi * wk
    o_ref[...] = _layer_norm(jnp.concatenate(acc[0] + acc[1], axis=1), g2_ref[...], b2_ref[...])


def _combine(wts_tk, x1, g2, b2, wsg, wsu, wsd, rows, tm):
    t = x1.shape[0]
    full = lambda i: (0, 0)
    return pl.pallas_call(
        _combine_kernel,
        grid=(t // tm,),
        in_specs=[pl.BlockSpec((tm, TOP_K), lambda i: (i, 0)),
                  pl.BlockSpec((tm, D_MODEL), lambda i: (i, 0)),
                  pl.BlockSpec((1, D_MODEL), full),
                  pl.BlockSpec((1, D_MODEL), full),
                  pl.BlockSpec(wsg.shape, full), pl.BlockSpec(wsu.shape, full), pl.BlockSpec(wsd.shape, full),
                  pl.BlockSpec((SC_SPLIT, TOP_K, tm, PIECE), lambda i: (0, 0, i, 0))],
        out_specs=pl.BlockSpec((tm, D_MODEL), lambda i: (i, 0)),
        out_shape=jax.ShapeDtypeStruct((t, D_MODEL), F32),
        compiler_params=_params(("arbitrary",)),
        name="combine",
    )(wts_tk, x1, g2, b2, wsg, wsu, wsd, rows)


def _tile(n, pref):
    t = pref
    while n % t:
        t //= 2
    return t


def _token_mixers(x, layer_idx, wp, cnt_in):
    bsz, seq, _ = x.shape
    t = bsz * seq
    x2 = x.reshape(t, D_MODEL)
    lam_init = 0.8 - 0.6 * math.exp(-0.3 * layer_idx)

    cos_t, sin_t = _rope_tables(seq)
    qa, ka, va, qd, kd, vd = _in_proj(x2, wp["w_qkv"], cos_t, sin_t, seq, _tile(seq, 1024))
    oa = _attn_a(qa, ka, va, wp["sink"], bsz, seq, _tile(seq, 512))
    ob = _attn_b(qd, kd, vd, wp["lam_p"], wp["subln_g"], bsz, seq, _tile(seq, 512), _tile(seq, 1024), lam_init)
    return _post_attn(
        x2, oa, ob, wp["w_gates"], wp["w_o_a"], wp["w_o_b"], wp["w_out"], wp["ln1_g"], wp["ln1_b"],
        wp["router_wt"], wp["router_bias"], cnt_in, _tile(t, 512))


def _encoder_layer(xs_in, layer_idx, wp):
    cnt_in = jnp.zeros((N_EXPERTS, 1), F32)
    routed = []
    for x in xs_in:
        x1p, x1f, eidx, wts, rank, cnt = _token_mixers(x, layer_idx, wp, cnt_in)
        cnt_in = cnt[:, :1].astype(F32)
        routed.append((x1p, x1f, eidx, wts, rank))

    n_assign = sum(x.shape[0] * x.shape[1] for x in xs_in) * TOP_K
    n_blk = -(-(n_assign + N_EXPERTS * (EXPERT_ROWS - 1)) // EXPERT_ROWS)
    counts = cnt[:, 0]
    padded = ((counts + EXPERT_ROWS - 1) // EXPERT_ROWS) * EXPERT_ROWS
    pend = jnp.cumsum(padded)
    pstart = pend - padded
    cum_blocks = jnp.concatenate([jnp.zeros((1,), jnp.int32), (pend // EXPERT_ROWS).astype(jnp.int32)])
    pstart_col = pstart.astype(jnp.int32)[:, None]

    dests = [_dest(eidx, rank, pstart_col, _tile(eidx.shape[1], 512)) for _, _, eidx, _, rank in routed]
    xs = _sc_scatter_rows([r[0] for r in routed], dests, n_blk * EXPERT_ROWS)
    ys = _experts(cum_blocks, counts.astype(jnp.int32), xs, wp["w_gate"], wp["w_up"], wp["w_down"])
    outs = []
    for x, dest, (x1p, x1f, eidx, wts, rank) in zip(xs_in, dests, routed):
        t = x1f.shape[0]
        rows = _sc_gather_rows(ys, dest.reshape(-1)).reshape(SC_SPLIT, TOP_K, t, PIECE)
        y = _combine(wts.T, x1f, wp["ln2_g"], wp["ln2_b"], wp["ws_gate"], wp["ws_up"], wp["ws_down"], rows,
                     _tile(t, 512))
        outs.append(y.reshape(x.shape))
    return outs


def kernel(x_prompt, x_sample, w_in, attn_sink, lambda_q1, lambda_k1, lambda_q2, lambda_k2, subln_g, w_o_a, w_o_b, w_out, ln1_g, ln1_b, router_w, router_bias, w_gate, w_up, w_down, ws_gate, ws_up, ws_down, ln2_g, ln2_b):
    y_prompt, y_sample = x_prompt, x_sample
    for l in range(DEPTH):
        w_qkv, w_gates = _prep_w_in(w_in[l])
        wp = {
            "w_qkv": w_qkv, "w_gates": w_gates,
            "sink": attn_sink[l].astype(F32),
            "lam_p": jnp.stack([lambda_q1[l], lambda_k1[l], lambda_q2[l], lambda_k2[l]]).astype(F32),
            "subln_g": subln_g[l].astype(F32)[None, :],
            "w_o_a": w_o_a[l].astype(BF16), "w_o_b": w_o_b[l].astype(BF16), "w_out": w_out[l].astype(BF16),
            "ln1_g": ln1_g[l].astype(F32)[None, :], "ln1_b": ln1_b[l].astype(F32)[None, :],
            "router_wt": router_w[l].T.astype(BF16), "router_bias": router_bias[l].astype(F32)[:, None],
            "w_gate": w_gate[l], "w_up": w_up[l], "w_down": w_down[l],
            "ws_gate": ws_gate[l].astype(BF16), "ws_up": ws_up[l].astype(BF16), "ws_down": ws_down[l].astype(BF16),
            "ln2_g": ln2_g[l].astype(F32)[None, :], "ln2_b": ln2_b[l].astype(F32)[None, :],
        }
        y_prompt, y_sample = _encoder_layer((y_prompt, y_sample), l, wp)
    return (y_prompt, y_sample)
```

```python
import functools
import math

import jax
import jax.numpy as jnp
from jax import lax
from jax.experimental import pallas as pl
from jax.experimental.pallas import tpu as pltpu
from jax.experimental.pallas import tpu_sc as plsc

D_MODEL = 1024
HEAD_DIM = 64
ROPE_THETA = 10000.0
BLOCK = 128
A_Q_HEADS = 8
A_KV_HEADS = 2
WINDOW = 128
B_HEADS = 4
N_EXPERTS = 256
TOP_K = 8
N_GROUPS = 8
TOPK_GROUPS = 4
GROUP_SIZE = N_EXPERTS // N_GROUPS
D_EXPERT = 256
ROUTED_SCALE = 2.5
EXPERT_ROWS = 512
EXPERT_SLOTS = 6
DEPTH = 1
DN_ALPHA = (2 * DEPTH) ** 0.25
LN_EPS = 1e-5
RMS_EPS = 1e-5
NEG = -1e30
LOG2E = math.log2(math.e)
SOFTMAX_ROWS = 16

LANES = 128
SC_CORES = 2
SC_SUBCORES = 16
SC_WINDOW = 128
SC_SPLIT = 2
PIECE = D_MODEL // 2 // SC_SPLIT
VMEM_LIMIT_BYTES = 56 * 1024 * 1024

F32 = jnp.float32
BF16 = jnp.bfloat16
_NT = (((1,), (1,)), ((), ()))


def _params(sem, vmem=VMEM_LIMIT_BYTES):
    return pltpu.CompilerParams(dimension_semantics=sem, vmem_limit_bytes=vmem)


_C_QA, _C_KA, _C_VA, _C_QD, _C_KD, _C_VD, _C_END = 0, 512, 768, 1024, 1536, 2048, 2560


def _prep_w_in(w_in):
    cuts = [0, 512, 640, 768, 1280, 1792, 2304, 4352]
    qa, ka, va, qd, kd, vd, gates = [w_in[:, cuts[i]:cuts[i + 1]] for i in range(7)]
    dup = lambda w: jnp.concatenate([w[:, :64], w[:, :64], w[:, 64:], w[:, 64:]], axis=1)
    return jnp.concatenate([qa, dup(ka), dup(va), qd, kd, vd], axis=1).astype(BF16), gates.astype(BF16)


def _rope_tables(s):
    half = HEAD_DIM // 2
    inv = 1.0 / (ROPE_THETA ** (jnp.arange(half, dtype=F32) / half))
    ang = jnp.arange(s, dtype=F32)[:, None] * inv[None, :]
    cos, sin = jnp.cos(ang), jnp.sin(ang)
    return (jnp.concatenate([cos, cos, cos, cos], axis=1),
            jnp.concatenate([-sin, sin, -sin, sin], axis=1))


def _in_proj_kernel(x_ref, w_ref, cos_ref, sin_ref, qa_ref, ka_ref, va_ref, qd_ref, kd_ref, vd_ref):
    xb = x_ref[...].astype(BF16)
    cos = cos_ref[...]
    sin = sin_ref[...]
    lane = lax.broadcasted_iota(jnp.int32, cos.shape, 1)
    first_half = (lane & (HEAD_DIM // 2)) == 0

    def proj(c0, c1):
        return jnp.dot(xb, w_ref[:, c0:c1], preferred_element_type=F32)

    def rope_store(u, out_ref, scale):
        for j in range(u.shape[1] // LANES):
            uj = u[:, LANES * j:LANES * (j + 1)]
            rot = jnp.where(first_half, pltpu.roll(uj, LANES - 32, 1), pltpu.roll(uj, 32, 1))
            r = uj * cos + rot * sin
            if scale != 1.0:
                r = r * scale
            out_ref[:, LANES * j:LANES * (j + 1)] = r.astype(out_ref.dtype)

    scale = HEAD_DIM ** -0.5
    rope_store(proj(_C_QA, _C_KA), qa_ref, scale)
    rope_store(proj(_C_KA, _C_VA), ka_ref, 1.0)
    va_ref[...] = proj(_C_VA, _C_QD).astype(va_ref.dtype)
    rope_store(proj(_C_QD, _C_KD), qd_ref, scale * LOG2E)
    rope_store(proj(_C_KD, _C_VD), kd_ref, 1.0)
    vd_ref[...] = proj(_C_VD, _C_END).astype(vd_ref.dtype)


def _in_proj(x2, w_perm, cos_t, sin_t, seq, tm):
    t = x2.shape[0]
    nseq = seq // tm
    row = lambda i: (i, 0)
    widths = (512, 256, 256, 512, 512, 512)
    return pl.pallas_call(
        _in_proj_kernel,
        grid=(t // tm,),
        in_specs=[pl.BlockSpec((tm, D_MODEL), row),
                  pl.BlockSpec((D_MODEL, _C_END), lambda i: (0, 0)),
                  pl.BlockSpec((tm, LANES), lambda i: (i % nseq, 0)),
                  pl.BlockSpec((tm, LANES), lambda i: (i % nseq, 0))],
        out_specs=[pl.BlockSpec((tm, w), row) for w in widths],
        out_shape=[jax.ShapeDtypeStruct((t, w), BF16) for w in widths],
        compiler_params=_params(("arbitrary",)),
        name="in_proj",
    )(x2, w_perm, cos_t, sin_t)


def _attn_a_kernel(sink_ref, q_ref, kp_ref, kc_ref, kn_ref, vp_ref, vc_ref, vn_ref, o_ref, *, seq, tq):
    i = pl.program_id(1)
    kk = jnp.concatenate([kp_ref[...], kc_ref[...], kn_ref[...]], axis=0)
    vv = jnp.concatenate([vp_ref[...], vc_ref[...], vn_ref[...]], axis=0)
    lane = lax.broadcasted_iota(jnp.int32, (kk.shape[0], LANES), 1)
    lo = lane < HEAD_DIM
    zero = jnp.zeros((kk.shape[0], LANES), BF16)
    k_lo = [jnp.where(lo, kk[:, LANES * h:LANES * (h + 1)], zero) for h in range(A_KV_HEADS)]
    k_hi = [jnp.where(lo, zero, kk[:, LANES * h:LANES * (h + 1)]) for h in range(A_KV_HEADS)]
    v_lo = [jnp.where(lo, vv[:, LANES * h:LANES * (h + 1)], zero) for h in range(A_KV_HEADS)]
    v_hi = [jnp.where(lo, zero, vv[:, LANES * h:LANES * (h + 1)]) for h in range(A_KV_HEADS)]

    qi = lax.broadcasted_iota(jnp.int32, (BLOCK, 3 * BLOCK), 0)
    kj = lax.broadcasted_iota(jnp.int32, (BLOCK, 3 * BLOCK), 1)
    band = jnp.abs(kj - BLOCK - qi) <= WINDOW
    head_of_row = lax.broadcasted_iota(jnp.int32, (A_Q_HEADS * BLOCK, 1), 0) // BLOCK
    snk = jnp.zeros((A_Q_HEADS * BLOCK, 1), F32)
    for head in range(A_Q_HEADS):
        snk = jnp.where(head_of_row == head, sink_ref[head], snk)
    for j in range(tq // BLOCK):
        kpos = i * tq + (j - 1) * BLOCK + kj
        mask = band & (kpos >= 0) & (kpos < seq)
        r0, r1 = j * BLOCK, (j + 3) * BLOCK
        pieces = []
        for c in range(A_Q_HEADS // 2):
            q2 = q_ref[j * BLOCK:(j + 1) * BLOCK, LANES * c:LANES * (c + 1)]
            for kx in (k_lo[c // 2], k_hi[c // 2]):
                s = lax.dot_general(q2, kx[r0:r1], _NT, preferred_element_type=F32)
                pieces.append(jnp.where(mask, s, NEG))
        s = jnp.concatenate(pieces, axis=0)
        m = jnp.maximum(jnp.max(s, axis=-1, keepdims=True), snk)
        p = jnp.exp(s - m)
        inv = 1.0 / (jnp.sum(p, axis=-1, keepdims=True) + jnp.exp(snk - m))
        pb = p.astype(BF16)
        for c in range(A_Q_HEADS // 2):
            out = None
            for half, vx in enumerate((v_lo[c // 2], v_hi[c // 2])):
                rows = slice((2 * c + half) * BLOCK, (2 * c + half + 1) * BLOCK)
                pv = jnp.dot(pb[rows], vx[r0:r1], preferred_element_type=F32) * inv[rows]
                out = pv if out is None else out + pv
            o_ref[j * BLOCK:(j + 1) * BLOCK, LANES * c:LANES * (c + 1)] = out.astype(o_ref.dtype)


def _attn_a(qa, ka, va, sink, bsz, seq, tq):
    t = qa.shape[0]
    nq = seq // tq
    nb = seq // BLOCK
    r = tq // BLOCK
    cur = lambda b, i: (b * nq + i, 0)
    prev = lambda b, i: (b * nb + jnp.maximum(i * r - 1, 0), 0)
    nxt = lambda b, i: (b * nb + jnp.minimum((i + 1) * r, nb - 1), 0)
    kv_specs = [pl.BlockSpec((BLOCK, 256), prev), pl.BlockSpec((tq, 256), cur), pl.BlockSpec((BLOCK, 256), nxt)]
    return pl.pallas_call(
        functools.partial(_attn_a_kernel, seq=seq, tq=tq),
        grid=(bsz, nq),
        in_specs=[pl.BlockSpec(memory_space=pltpu.SMEM), pl.BlockSpec((tq, 512), cur)] + kv_specs + kv_specs,
        out_specs=pl.BlockSpec((tq, 512), cur),
        out_shape=jax.ShapeDtypeStruct((t, 512), BF16),
        compiler_params=_params(("arbitrary", "arbitrary")),
        name="attn_a",
    )(sink, qa, ka, ka, ka, va, va, va)


def _attn_b_kernel(lam_ref, g_ref, q_ref, k_ref, v_ref, o_ref, acc_ref, s_ref, p_ref, m_ref, alpha_ref,
                   *, tq, tk, lam_init):
    seq = q_ref.shape[0]
    n_q, n_chunks = seq // tq, seq // tk
    lo = lax.broadcasted_iota(jnp.int32, (tq, LANES), 1) < HEAD_DIM
    zero = jnp.zeros((tq, LANES), BF16)
    ones_col = jnp.where(lax.broadcasted_iota(jnp.int32, (tk, LANES), 1) == 0, 1.0, 0.0).astype(BF16)

    lp = lam_ref[...]
    lam = (jnp.exp(jnp.sum(lp[0:1] * lp[1:2], axis=-1, keepdims=True))
           - jnp.exp(jnp.sum(lp[2:3] * lp[3:4], axis=-1, keepdims=True)) + lam_init)

    def q_tile(qi):
        q = q_ref[pl.ds(pl.multiple_of(qi * tq, tq), tq), :]
        return jnp.concatenate([jnp.where(lo, q, zero), jnp.where(lo, zero, q)], axis=0)

    def scores(q2, kc, slot):
        s_ref[slot] = lax.dot_general(q2, k_ref[kc * tk:(kc + 1) * tk, :], _NT, preferred_element_type=F32)

    def consume(kc, slot):
        v_aug = jnp.concatenate([v_ref[kc * tk:(kc + 1) * tk, :], ones_col], axis=1)
        for r in range(2 * tq // SOFTMAX_ROWS):
            rows = slice(r * SOFTMAX_ROWS, (r + 1) * SOFTMAX_ROWS)
            s = s_ref[slot, rows, :]
            m_old = m_ref[rows, :]
            m_new = jnp.maximum(m_old, jnp.max(s, axis=-1, keepdims=True))
            m_ref[rows, :] = m_new
            alpha_ref[rows, :] = jnp.exp2(m_old - m_new)
            p_ref[rows, :] = jnp.exp2(s - m_new).astype(BF16)
        acc_ref[...] = alpha_ref[...] * acc_ref[...] + jnp.dot(p_ref[...], v_aug, preferred_element_type=F32)

    across_tiles = n_chunks % 2 == 0

    def tile(qi, carry):
        q = q_tile(qi)
        acc_ref[...] = jnp.zeros_like(acc_ref)
        m_ref[...] = jnp.full(m_ref.shape, -jnp.inf, F32)
        if not across_tiles:
            scores(q, 0, 0)
        for kc in range(n_chunks):
            slot = kc % 2
            if kc + 1 < n_chunks:
                scores(q, kc + 1, 1 - slot)
            elif across_tiles:
                scores(q_tile(jnp.minimum(qi + 1, n_q - 1)), 0, 1 - slot)
            consume(kc, slot)
        acc1, acc2 = acc_ref[:tq], acc_ref[tq:]
        l1, l2 = acc1[:, LANES:LANES + 1], acc2[:, LANES:LANES + 1]
        o = acc1[:, :LANES] * (1.0 / l1) - lam * (acc2[:, :LANES] * (1.0 / l2))
        o = o * lax.rsqrt(jnp.mean(o * o, axis=-1, keepdims=True) + RMS_EPS)
        o = o * g_ref[...] * (1.0 - lam_init)
        o_ref[pl.ds(pl.multiple_of(qi * tq, tq), tq), :] = o.astype(o_ref.dtype)
        return carry

    if across_tiles:
        scores(q_tile(0), 0, 0)
    lax.fori_loop(0, n_q, tile, 0)


def _attn_b(qd, kd, vd, lam_p, subln_g, bsz, seq, tq, tk, lam_init):
    t = qd.shape[0]
    seq_block = pl.BlockSpec((seq, LANES), lambda b, h: (b, h))
    return pl.pallas_call(
        functools.partial(_attn_b_kernel, tq=tq, tk=tk, lam_init=lam_init),
        grid=(bsz, B_HEADS),
        in_specs=[pl.BlockSpec((4, HEAD_DIM), lambda b, h: (0, 0)),
                  pl.BlockSpec((1, LANES), lambda b, h: (0, 0)),
                  seq_block, seq_block, seq_block],
        out_specs=seq_block,
        out_shape=jax.ShapeDtypeStruct((t, 512), BF16),
        scratch_shapes=[pltpu.VMEM((2 * tq, 2 * LANES), F32),
                        pltpu.VMEM((2, 2 * tq, tk), F32),
                        pltpu.VMEM((2 * tq, tk), BF16),
                        pltpu.VMEM((2 * tq, 1), F32),
                        pltpu.VMEM((2 * tq, 1), F32)],
        compiler_params=_params(("arbitrary", "arbitrary")),
        name="attn_b",
    )(lam_p, subln_g, qd, kd, vd)


HALF = D_MODEL // 2
_HI_MASK = -65536


def _pack_pair(lo, hi):
    lo_bits = lax.bitcast_convert_type(lo.astype(BF16).astype(F32), jnp.int32)
    hi_bits = lax.bitcast_convert_type(hi.astype(BF16).astype(F32), jnp.int32)
    return lax.shift_right_logical(lo_bits, 16) | hi_bits


def _unpack_pair(w):
    lo = lax.bitcast_convert_type(lax.shift_left(w, 16), F32)
    hi = lax.bitcast_convert_type(w & _HI_MASK, F32)
    return lo, hi


def _layer_norm(y, g, b):
    mu = jnp.mean(y, axis=-1, keepdims=True)
    d = y - mu
    var = jnp.mean(d * d, axis=-1, keepdims=True)
    return d * lax.rsqrt(var + LN_EPS) * g + b


def _first_argmax(vals, rowf, big):
    m = jnp.max(vals, axis=0, keepdims=True)
    idx = jnp.min(jnp.where(vals == m, rowf, big), axis=0, keepdims=True)
    return m, idx


def _post_attn_kernel(x_ref, oa_ref, ob_ref, wgate_ref, woa_ref, wob_ref, wout_ref, g1_ref, b1_ref,
                      rwt_ref, rb_ref, tri_ref, cnt_in_ref,
                      x1_ref, x1f_ref, eidx_ref, wts_ref, rank_ref, cnt_ref, cnt_scr):
    i = pl.program_id(0)
    tm = x_ref.shape[0]

    @pl.when(i == 0)
    def _():
        cnt_scr[...] = cnt_in_ref[...]

    x = x_ref[...]
    gates = jax.nn.sigmoid(jnp.dot(x.astype(BF16), wgate_ref[...], preferred_element_type=F32))
    a = jnp.dot(oa_ref[...], woa_ref[...], preferred_element_type=F32)
    b = jnp.dot(ob_ref[...], wob_ref[...], preferred_element_type=F32)
    merged = gates[:, :D_MODEL] * a + gates[:, D_MODEL:] * b
    mix = jnp.dot(merged.astype(BF16), wout_ref[...], preferred_element_type=F32)
    x1 = _layer_norm(DN_ALPHA * x + mix, g1_ref[...], b1_ref[...])
    x1p = _pack_pair(x1[:, :HALF], x1[:, HALF:])
    for piece in range(SC_SPLIT):
        x1_ref[piece] = x1p[:, piece * PIECE:(piece + 1) * PIECE]
    x1b = x1.astype(BF16)
    x1f_ref[...] = x1

    logits = lax.dot_general(rwt_ref[...], x1b, _NT, preferred_element_type=F32)
    scores = jax.nn.sigmoid(logits)
    choice = scores + rb_ref[...]
    ninf = -jnp.inf
    grow = lax.broadcasted_iota(jnp.int32, (GROUP_SIZE, tm), 0).astype(F32)
    gscore = []
    for g in range(N_GROUPS):
        blk = choice[GROUP_SIZE * g:GROUP_SIZE * (g + 1)]
        m1, i1 = _first_argmax(blk, grow, float(GROUP_SIZE))
        m2 = jnp.max(jnp.where(grow == i1, ninf, blk), axis=0, keepdims=True)
        gscore.append(m1 + m2)
    selected = [jnp.zeros((1, tm), F32) for _ in range(N_GROUPS)]
    work = list(gscore)
    for _ in range(TOPK_GROUPS):
        best = work[0]
        for g in range(1, N_GROUPS):
            best = jnp.maximum(best, work[g])
        taken = jnp.zeros((1, tm), F32)
        for g in range(N_GROUPS):
            hit = jnp.where((work[g] == best) & (taken == 0.0), 1.0, 0.0)
            taken = jnp.maximum(taken, hit)
            selected[g] = jnp.maximum(selected[g], hit)
            work[g] = jnp.where(hit > 0.0, ninf, work[g])
    masked = jnp.concatenate(
        [jnp.where(selected[g] > 0.0, choice[GROUP_SIZE * g:GROUP_SIZE * (g + 1)], ninf) for g in range(N_GROUPS)],
        axis=0)

    rowf = lax.broadcasted_iota(jnp.int32, (N_EXPERTS, tm), 0).astype(F32)
    hits, idxs, ws = [], [], []
    for _ in range(TOP_K):
        _, idx = _first_argmax(masked, rowf, float(N_EXPERTS))
        hit = rowf == idx
        hits.append(hit)
        idxs.append(idx)
        ws.append(jnp.sum(jnp.where(hit, scores, 0.0), axis=0, keepdims=True))
        masked = jnp.where(hit, ninf, masked)
    wsum = ws[0]
    for k in range(1, TOP_K):
        wsum = wsum + ws[k]

    member = hits[0]
    for k in range(1, TOP_K):
        member = member | hits[k]
    member_f = jnp.where(member, 1.0, 0.0)
    before = jnp.dot(member_f.astype(BF16), tri_ref[...], preferred_element_type=F32) + cnt_scr[...]
    for k in range(TOP_K):
        eidx_ref[k:k + 1, :] = idxs[k].astype(jnp.int32)
        wts_ref[k:k + 1, :] = ws[k] / wsum * ROUTED_SCALE
        rank_ref[k:k + 1, :] = jnp.sum(jnp.where(hits[k], before, 0.0), axis=0, keepdims=True).astype(jnp.int32)
    cnt_scr[...] = cnt_scr[...] + jnp.sum(member_f, axis=1, keepdims=True)
    cnt_ref[...] = jnp.broadcast_to(cnt_scr[...], cnt_ref.shape).astype(jnp.int32)


def _post_attn(x2, oa, ob, wgate, woa, wob, wout, g1, b1, rwt, rb, cnt_in, tm):
    t = x2.shape[0]
    tri = jnp.triu(jnp.ones((tm, tm), F32), k=1).astype(BF16)
    row = lambda i: (i, 0)
    col = lambda i: (0, i)
    full = lambda i: (0, 0)
    wspec = lambda arr: pl.BlockSpec(arr.shape, full)
    weights = (wgate, woa, wob, wout, g1, b1, rwt, rb, tri, cnt_in)
    return pl.pallas_call(
        _post_attn_kernel,
        grid=(t // tm,),
        in_specs=[pl.BlockSpec((tm, D_MODEL), row), pl.BlockSpec((tm, 512), row), pl.BlockSpec((tm, 512), row)]
                 + [wspec(w) for w in weights],
        out_specs=[pl.BlockSpec((SC_SPLIT, tm, PIECE), lambda i: (0, i, 0)), pl.BlockSpec((tm, D_MODEL), row),
                   pl.BlockSpec((TOP_K, tm), col), pl.BlockSpec((TOP_K, tm), col), pl.BlockSpec((TOP_K, tm), col),
                   pl.BlockSpec((N_EXPERTS, LANES), full)],
        out_shape=[jax.ShapeDtypeStruct((SC_SPLIT, t, PIECE), jnp.int32), jax.ShapeDtypeStruct((t, D_MODEL), F32),
                   jax.ShapeDtypeStruct((TOP_K, t), jnp.int32), jax.ShapeDtypeStruct((TOP_K, t), F32),
                   jax.ShapeDtypeStruct((TOP_K, t), jnp.int32), jax.ShapeDtypeStruct((N_EXPERTS, LANES), jnp.int32)],
        scratch_shapes=[pltpu.VMEM((N_EXPERTS, 1), F32)],
        compiler_params=_params(("arbitrary",)),
        name="post_attn",
    )(x2, oa, ob, *weights)


def _dest_kernel(eidx_ref, rank_ref, pstart_ref, dest_ref):
    tm = eidx_ref.shape[1]
    rows = lax.broadcasted_iota(jnp.int32, (N_EXPERTS, tm), 0)
    pstart = pstart_ref[...]
    for k in range(TOP_K):
        hit = rows == eidx_ref[k:k + 1, :]
        start = jnp.sum(jnp.where(hit, pstart, 0).astype(F32), axis=0, keepdims=True).astype(jnp.int32)
        dest_ref[k:k + 1, :] = start + rank_ref[k:k + 1, :]


def _dest(eidx, rank, pstart, tm):
    t = eidx.shape[1]
    return pl.pallas_call(
        _dest_kernel,
        grid=(t // tm,),
        in_specs=[pl.BlockSpec((TOP_K, tm), lambda i: (0, i)), pl.BlockSpec((TOP_K, tm), lambda i: (0, i)),
                  pl.BlockSpec((N_EXPERTS, 1), lambda i: (0, 0))],
        out_specs=pl.BlockSpec((TOP_K, tm), lambda i: (0, i)),
        out_shape=jax.ShapeDtypeStruct((TOP_K, t), jnp.int32),
        compiler_params=_params(("arbitrary",)),
        name="dest",
    )(eidx, rank, pstart)


def _sc_mesh():
    return plsc.VectorSubcoreMesh(core_axis_name="core", subcore_axis_name="subcore",
                                  num_cores=SC_CORES, num_subcores=SC_SUBCORES)


def _sc_scatter_rows(xs_groups, dests, n_rows):
    n_groups = len(xs_groups)

    @functools.partial(pl.kernel, out_type=jax.ShapeDtypeStruct((SC_SPLIT, n_rows, PIECE), jnp.int32),
                       mesh=_sc_mesh(), scratch_types=[pltpu.SemaphoreType.DMA])
    def scatter(*refs):
        out_hbm, sem = refs[2 * n_groups], refs[2 * n_groups + 1]
        for g in range(n_groups):
            x_hbm, idx_hbm = refs[2 * g], refs[2 * g + 1]
            for piece in range(SC_SPLIT):
                def window(x_vmem, idx_vmem, piece=piece):
                    copies = [pltpu.async_copy(x_vmem, out_hbm.at[piece].at[idx_vmem.at[k]], sem)
                              for k in range(TOP_K)]
                    for copy in copies:
                        copy.wait()

                pltpu.emit_pipeline(
                    window,
                    grid=(x_hbm.shape[1] // SC_WINDOW,),
                    in_specs=[pl.BlockSpec((SC_WINDOW, PIECE), lambda i: (i, 0)),
                              pl.BlockSpec((TOP_K, SC_WINDOW), lambda i: (0, i))],
                    out_specs=[],
                    core_axis_name=("core", "subcore"),
                    dimension_semantics=(pltpu.PARALLEL,),
                )(x_hbm.at[piece], idx_hbm)

    args = []
    for x, d in zip(xs_groups, dests):
        args += [x, d]
    return scatter(*args)


def _experts_kernel(cb_ref, cnt_ref, xs_ref, wg_ref, wu_ref, wd_ref, ys_ref,
                    wg_scr, wu_scr, wd_scr, xbuf, ybuf, sem_in, sem_out):
    e = pl.program_id(0)
    g0, g1, total = cb_ref[e], cb_ref[e + 1], cb_ref[N_EXPERTS]

    def rows(g):
        return pl.ds(pl.multiple_of(g * EXPERT_ROWS, EXPERT_ROWS), EXPERT_ROWS)

    def x_copy(g):
        s = g % EXPERT_SLOTS
        return pltpu.make_async_copy(xs_ref.at[:, rows(g), :], xbuf.at[s], sem_in.at[s])

    def y_copy(g):
        s = g % EXPERT_SLOTS
        return pltpu.make_async_copy(ybuf.at[s], ys_ref.at[:, rows(g), :], sem_out.at[s])

    @pl.when(e == 0)
    def _():
        for g in range(EXPERT_SLOTS - 1):
            pl.when(g < total)(lambda g=g: x_copy(g).start())

    @pl.when(g1 > g0)
    def _():
        wg_scr[...] = wg_ref[0].astype(BF16)
        wu_scr[...] = wu_ref[0].astype(BF16)
        wd_scr[...] = wd_ref[0].astype(BF16)

    def block(g, carry):
        slot = g % EXPERT_SLOTS

        @pl.when(g + EXPERT_SLOTS - 1 < total)
        def _():
            x_copy(g + EXPERT_SLOTS - 1).start()

        x_copy(g).wait()

        @pl.when(g >= EXPERT_SLOTS)
        def _():
            y_copy(g - EXPERT_SLOTS).wait()

        valid = lax.broadcasted_iota(jnp.int32, (EXPERT_ROWS, PIECE), 0) < cnt_ref[e] - (g - g0) * EXPERT_ROWS
        hg = hu = None
        for piece in range(SC_SPLIT):
            for part, col0 in zip(_unpack_pair(xbuf[slot, piece]), (piece * PIECE, HALF + piece * PIECE)):
                xb = jnp.where(valid, part, 0.0).astype(BF16)
                pg = jnp.dot(xb, wg_scr[col0:col0 + PIECE], preferred_element_type=F32)
                pu = jnp.dot(xb, wu_scr[col0:col0 + PIECE], preferred_element_type=F32)
                hg = pg if hg is None else hg + pg
                hu = pu if hu is None else hu + pu
        h = ((hg * jax.nn.sigmoid(hg)) * hu).astype(BF16)
        y = _pack_pair(jnp.dot(h, wd_scr[:, :HALF], preferred_element_type=F32),
                       jnp.dot(h, wd_scr[:, HALF:], preferred_element_type=F32))
        for piece in range(SC_SPLIT):
            ybuf[slot, piece] = y[:, piece * PIECE:(piece + 1) * PIECE]
        y_copy(g).start()
        return carry

    lax.fori_loop(g0, g1, block, 0)

    @pl.when(e == N_EXPERTS - 1)
    def _():
        for back in range(EXPERT_SLOTS, 0, -1):
            pl.when(total >= back)(lambda back=back: y_copy(total - back).wait())


def _experts(cum_blocks, counts, xs, w_gate, w_up, w_down):
    n_rows = xs.shape[1]
    wmap = lambda e, cb, cnt: (e, 0, 0)
    grid_spec = pltpu.PrefetchScalarGridSpec(
        num_scalar_prefetch=2,
        grid=(N_EXPERTS,),
        in_specs=[pl.BlockSpec(memory_space=pl.ANY),
                  pl.BlockSpec((1, D_MODEL, D_EXPERT), wmap),
                  pl.BlockSpec((1, D_MODEL, D_EXPERT), wmap),
                  pl.BlockSpec((1, D_EXPERT, D_MODEL), wmap)],
        out_specs=pl.BlockSpec(memory_space=pl.ANY),
        scratch_shapes=[pltpu.VMEM((D_MODEL, D_EXPERT), BF16), pltpu.VMEM((D_MODEL, D_EXPERT), BF16),
                        pltpu.VMEM((D_EXPERT, D_MODEL), BF16),
                        pltpu.VMEM((EXPERT_SLOTS, SC_SPLIT, EXPERT_ROWS, PIECE), jnp.int32),
                        pltpu.VMEM((EXPERT_SLOTS, SC_SPLIT, EXPERT_ROWS, PIECE), jnp.int32),
                        pltpu.SemaphoreType.DMA((EXPERT_SLOTS,)), pltpu.SemaphoreType.DMA((EXPERT_SLOTS,))])
    return pl.pallas_call(
        _experts_kernel,
        grid_spec=grid_spec,
        out_shape=jax.ShapeDtypeStruct((SC_SPLIT, n_rows, PIECE), jnp.int32),
        compiler_params=_params(("arbitrary",)),
        name="experts",
    )(cum_blocks, counts, xs, w_gate, w_up, w_down)


def _sc_gather_rows(table, idx):
    n, d = idx.shape[0], table.shape[2]

    @functools.partial(pl.kernel, out_type=jax.ShapeDtypeStruct((SC_SPLIT, n, d), table.dtype), mesh=_sc_mesh(),
                       scratch_types=[])
    def gather(table_hbm, idx_hbm, out_hbm):
        for piece in range(SC_SPLIT):
            def window(idx_vmem, out_vmem, piece=piece):
                pltpu.sync_copy(table_hbm.at[piece].at[idx_vmem.at[0]], out_vmem)

            pltpu.emit_pipeline(
                window,
                grid=(n // SC_WINDOW,),
                in_specs=[pl.BlockSpec((1, SC_WINDOW), lambda i: (0, i))],
                out_specs=[pl.BlockSpec((SC_WINDOW, d), lambda i: (i, 0))],
                core_axis_name=("core", "subcore"),
                dimension_semantics=(pltpu.PARALLEL,),
            )(idx_hbm, out_hbm.at[piece])

    return gather(table, idx.reshape(1, n))


def _combine_kernel(w_ref, x1_ref, g2_ref, b2_ref, wsg_ref, wsu_ref, wsd_ref, rows_ref, o_ref):
    w = w_ref[...]
    x1 = x1_ref[...]
    x1b = x1.astype(BF16)
    hg = jnp.dot(x1b, wsg_ref[...], preferred_element_type=F32)
    hu = jnp.dot(x1b, wsu_ref[...], preferred_element_type=F32)
    hs = (hg * jax.nn.sigmoid(hg)) * hu
    base = DN_ALPHA * x1 + jnp.dot(hs.astype(BF16), wsd_ref[...], preferred_element_type=F32)
    acc = [[base[:, half * HALF + p * PIECE:half * HALF + (p + 1) * PIECE] for p in range(SC_SPLIT)]
           for half in range(2)]
    for k in range(TOP_K):
        wk = w[:, k:k + 1]
        for p in range(SC_SPLIT):
            lo, hi = _unpack_pair(rows_ref[p, k])
            acc[0][p] = acc[0][p] + lo * wk
            acc[1][p] = acc[1][p] + hi * wk
    o_ref[...] = _layer_norm(jnp.concatenate(acc[0] + acc[1], axis=1), g2_ref[...], b2_ref[...])


def _combine(wts_tk, x1, g2, b2, wsg, wsu, wsd, rows, tm):
    t = x1.shape[0]
    full = lambda i: (0, 0)
    return pl.pallas_call(
        _combine_kernel,
        grid=(t // tm,),
        in_specs=[pl.BlockSpec((tm, TOP_K), lambda i: (i, 0)),
                  pl.BlockSpec((tm, D_MODEL), lambda i: (i, 0)),
                  pl.BlockSpec((1, D_MODEL), full),
                  pl.BlockSpec((1, D_MODEL), full),
                  pl.BlockSpec(wsg.shape, full), pl.BlockSpec(wsu.shape, full), pl.BlockSpec(wsd.shape, full),
                  pl.BlockSpec((SC_SPLIT, TOP_K, tm, PIECE), lambda i: (0, 0, i, 0))],
        out_specs=pl.BlockSpec((tm, D_MODEL), lambda i: (i, 0)),
        out_shape=jax.ShapeDtypeStruct((t, D_MODEL), F32),
        compiler_params=_params(("arbitrary",)),
        name="combine",
    )(wts_tk, x1, g2, b2, wsg, wsu, wsd, rows)


def _tile(n, pref):
    t = pref
    while n % t:
        t //= 2
    return t


def _token_mixers(x, layer_idx, wp, cnt_in):
    bsz, seq, _ = x.shape
    t = bsz * seq
    x2 = x.reshape(t, D_MODEL)
    lam_init = 0.8 - 0.6 * math.exp(-0.3 * layer_idx)

    cos_t, sin_t = _rope_tables(seq)
    qa, ka, va, qd, kd, vd = _in_proj(x2, wp["w_qkv"], cos_t, sin_t, seq, _tile(seq, 1024))
    oa = _attn_a(qa, ka, va, wp["sink"], bsz, seq, _tile(seq, 1024))
    ob = _attn_b(qd, kd, vd, wp["lam_p"], wp["subln_g"], bsz, seq, _tile(seq, 512), _tile(seq, 1024), lam_init)
    return _post_attn(
        x2, oa, ob, wp["w_gates"], wp["w_o_a"], wp["w_o_b"], wp["w_out"], wp["ln1_g"], wp["ln1_b"],
        wp["router_wt"], wp["router_bias"], cnt_in, _tile(t, 512))


def _encoder_layer(xs_in, layer_idx, wp):
    cnt_in = jnp.zeros((N_EXPERTS, 1), F32)
    routed = []
    for x in xs_in:
        x1p, x1f, eidx, wts, rank, cnt = _token_mixers(x, layer_idx, wp, cnt_in)
        cnt_in = cnt[:, :1].astype(F32)
        routed.append((x1p, x1f, eidx, wts, rank))

    n_assign = sum(x.shape[0] * x.shape[1] for x in xs_in) * TOP_K
    n_blk = -(-(n_assign + N_EXPERTS * (EXPERT_ROWS - 1)) // EXPERT_ROWS)
    counts = cnt[:, 0]
    padded = ((counts + EXPERT_ROWS - 1) // EXPERT_ROWS) * EXPERT_ROWS
    pend = jnp.cumsum(padded)
    pstart = pend - padded
    cum_blocks = jnp.concatenate([jnp.zeros((1,), jnp.int32), (pend // EXPERT_ROWS).astype(jnp.int32)])
    pstart_col = pstart.astype(jnp.int32)[:, None]

    dests = [_dest(eidx, rank, pstart_col, _tile(eidx.shape[1], 512)) for _, _, eidx, _, rank in routed]
    xs = _sc_scatter_rows([r[0] for r in routed], dests, n_blk * EXPERT_ROWS)
    ys = _experts(cum_blocks, counts.astype(jnp.int32), xs, wp["w_gate"], wp["w_up"], wp["w_down"])
    outs = []
    for x, dest, (x1p, x1f, eidx, wts, rank) in zip(xs_in, dests, routed):
        t = x1f.shape[0]
        rows = _sc_gather_rows(ys, dest.reshape(-1)).reshape(SC_SPLIT, TOP_K, t, PIECE)
        y = _combine(wts.T, x1f, wp["ln2_g"], wp["ln2_b"], wp["ws_gate"], wp["ws_up"], wp["ws_down"], rows,
                     _tile(t, 512))
        outs.append(y.reshape(x.shape))
    return outs


def kernel(x_prompt, x_sample, w_in, attn_sink, lambda_q1, lambda_k1, lambda_q2, lambda_k2, subln_g, w_o_a, w_o_b, w_out, ln1_g, ln1_b, router_w, router_bias, w_gate, w_up, w_down, ws_gate, ws_up, ws_down, ln2_g, ln2_b):
    y_prompt, y_sample = x_prompt, x_sample
    for l in range(DEPTH):
        w_qkv, w_gates = _prep_w_in(w_in[l])
        wp = {
            "w_qkv": w_qkv, "w_gates": w_gates,
            "sink": attn_sink[l].astype(F32),
            "lam_p": jnp.stack([lambda_q1[l], lambda_k1[l], lambda_q2[l], lambda_k2[l]]).astype(F32),
            "subln_g": subln_g[l].astype(F32)[None, :],
            "w_o_a": w_o_a[l].astype(BF16), "w_o_b": w_o_b[l].astype(BF16), "w_out": w_out[l].astype(BF16),
            "ln1_g": ln1_g[l].astype(F32)[None, :], "ln1_b": ln1_b[l].astype(F32)[None, :],
            "router_wt": router_w[l].T.astype(BF16), "router_bias": router_bias[l].astype(F32)[:, None],
            "w_gate": w_gate[l], "w_up": w_up[l], "w_down": w_down[l],
            "ws_gate": ws_gate[l].astype(BF16), "ws_up": ws_up[l].astype(BF16), "ws_down": ws_down[l].astype(BF16),
            "ln2_g": ln2_g[l].astype(F32)[None, :], "ln2_b": ln2_b[l].astype(F32)[None, :],
        }
        y_prompt, y_sample = _encoder_layer((y_prompt, y_sample), l, wp)
    return (y_prompt, y_sample)
```
